```python
import jax
import jax.numpy as jnp
from jax import lax
import numpy as np

D_MODEL = 1024
BATCH = 16
SEQ = 256
DEPTH = 4
DEC_BATCH = 2
DEC_SEQ = 4096
PAST_LEN = 512

GRID_W = 64
N_MIXERS = 2
N_RET_LAYERS = (DEPTH + 1) // 2
N_ATT_LAYERS = DEPTH // 2
N_MOD = 6
EPS = 1e-6
ROPE_BASE = 10000.0
RET_HEADS = 8
RET_DK = D_MODEL // RET_HEADS
RET_DV = 2 * RET_DK
RET_CHUNK = 128
RET_IN = 2 * RET_HEADS * RET_DK + 2 * RET_HEADS * RET_DV
ATT_HEADS = 16
ATT_KV_HEADS = 4
ATT_GROUP = ATT_HEADS // ATT_KV_HEADS
HEAD_DIM = D_MODEL // ATT_HEADS
WINDOW = 128
ATT_BLOCK = 128
ATT_IN = (ATT_HEADS + 2 * ATT_KV_HEADS) * HEAD_DIM
ATT_SCALE = HEAD_DIM ** -0.5
N_EXPERTS = 16
N_GROUPS = 4
EXPERTS_PER_GROUP = N_EXPERTS // N_GROUPS
TOP_K = 2
D_EXPERT = 512

kernel_name = 'hybrid_retention_swa_moe_step'


def rms_norm(x, gain):
    xf = x.astype(jnp.float32)
    y = xf * lax.rsqrt(jnp.mean(xf * xf, axis=-1, keepdims=True) + EPS)
    return (y * gain.astype(jnp.float32)).astype(x.dtype)


def modulation(cond, w_mod, b_mod):
    m = jax.nn.silu(cond) @ w_mod + b_mod
    return [t[:, None, :] for t in jnp.split(m, N_MOD, axis=-1)]


def modulate(x, gain, shift, scale):
    return rms_norm(x, gain) * (1.0 + scale) + shift


def axial_rope(x):
    seq_len, hd = x.shape[1], x.shape[-1]
    rows = seq_len // GRID_W
    row = jnp.repeat(jnp.arange(rows), GRID_W)
    col = jnp.tile(jnp.arange(GRID_W), rows)
    half = hd // 2
    quarter = half // 2
    inv_freq = ROPE_BASE ** (-jnp.arange(quarter, dtype=jnp.float32) / quarter)

    def rotate(xh, pos):
        ang = pos.astype(jnp.float32)[:, None] * inv_freq[None, :]
        cos = jnp.cos(ang)[None, :, None, :].astype(x.dtype)
        sin = jnp.sin(ang)[None, :, None, :].astype(x.dtype)
        x1, x2 = xh[..., :quarter], xh[..., quarter:]
        return jnp.concatenate([x1 * cos - x2 * sin, x2 * cos + x1 * sin], axis=-1)

    return jnp.concatenate([rotate(x[..., :half], row), rotate(x[..., half:], col)], axis=-1)


def retention_scan(q, k, v, log_g, s0):
    bsz, seq_len, nh, dk = q.shape
    dv = v.shape[-1]
    nc = seq_len // RET_CHUNK
    qc = q.reshape(bsz, nc, RET_CHUNK, nh, dk)
    kc = k.reshape(bsz, nc, RET_CHUNK, nh, dk)
    vc = v.reshape(bsz, nc, RET_CHUNK, nh, dv)
    pos = jnp.arange(RET_CHUNK, dtype=jnp.float32)
    diff = pos[:, None] - pos[None, :]
    lower = diff >= 0
    decay = jnp.where(lower[None], jnp.exp(jnp.where(lower, diff, 0.0)[None] * log_g[:, None, None]), 0.0)
    scores = jnp.einsum('bnihd,bnjhd->bnhij', qc, kc) * decay.astype(q.dtype)
    o_intra = jnp.einsum('bnhij,bnjhe->bnihe', scores, vc)
    xi = jnp.exp((pos[:, None] + 1.0) * log_g[None, :])
    zeta = jnp.exp((RET_CHUNK - 1.0 - pos)[:, None] * log_g[None, :])
    chunk_decay = jnp.exp(RET_CHUNK * log_g)[None, :, None, None]
    kv = jnp.einsum('bnjhd,bnjhe->nbhde',
                    kc.astype(jnp.float32) * zeta[:, :, None], vc.astype(jnp.float32))

    def step(s, kv_chunk):
        return chunk_decay * s + kv_chunk, s

    s_final, s_prev = lax.scan(step, s0.astype(jnp.float32), kv)
    o_cross = jnp.einsum('bnihd,nbhde->bnihe', qc.astype(jnp.float32) * xi[:, :, None], s_prev)
    o = o_intra.astype(jnp.float32) + o_cross
    return o.reshape(bsz, seq_len, nh, dv), s_final


def bidir_retention(q, k, v, log_g, s0_fwd, s0_bwd):
    o_f, s_f = retention_scan(q, k, v, log_g[0], s0_fwd)
    o_b, s_b = retention_scan(q[:, ::-1], k[:, ::-1], v[:, ::-1], log_g[1], s0_bwd)
    return o_f + o_b[:, ::-1], s_f, s_b


def retention_project(h, w_in):
    bsz, seq_len, _ = h.shape
    qk_w = RET_HEADS * RET_DK
    v_w = RET_HEADS * RET_DV
    proj = h @ w_in
    q = proj[..., :qk_w].reshape(bsz, seq_len, RET_HEADS, RET_DK)
    k = proj[..., qk_w:2 * qk_w].reshape(bsz, seq_len, RET_HEADS, RET_DK) * (RET_DK ** -0.5)
    v = proj[..., 2 * qk_w:2 * qk_w + v_w].reshape(bsz, seq_len, RET_HEADS, RET_DV)
    gate = proj[..., 2 * qk_w + v_w:]
    return q, k, v, gate


def retention_output(o, gate, w_out):
    mu = jnp.mean(o, axis=-1, keepdims=True)
    var = jnp.mean(jnp.square(o - mu), axis=-1, keepdims=True)
    o = ((o - mu) * lax.rsqrt(var + EPS)).reshape(gate.shape)
    return (jax.nn.silu(gate) * o.astype(gate.dtype)) @ w_out


def attention_project(h, w_in, q_gain, k_gain):
    bsz, seq_len, _ = h.shape
    qw = ATT_HEADS * HEAD_DIM
    kw = ATT_KV_HEADS * HEAD_DIM
    proj = h @ w_in
    q = rms_norm(proj[..., :qw].reshape(bsz, seq_len, ATT_HEADS, HEAD_DIM), q_gain)
    k = rms_norm(proj[..., qw:qw + kw].reshape(bsz, seq_len, ATT_KV_HEADS, HEAD_DIM), k_gain)
    v = proj[..., qw + kw:].reshape(bsz, seq_len, ATT_KV_HEADS, HEAD_DIM)
    return q, k, v


def sink_softmax(score_list, sink):
    m = sink
    for s in score_list:
        m = jnp.maximum(m, jnp.max(s, axis=-1, keepdims=True))
    exps = [jnp.exp(s - m) for s in score_list]
    denom = jnp.exp(sink - m)
    for e in exps:
        denom = denom + jnp.sum(e, axis=-1, keepdims=True)
    return [e / denom for e in exps]


def context_attention(q, k_ctx, v_ctx, sink):
    bsz, seq_len = q.shape[:2]
    nb = seq_len // ATT_BLOCK
    qb = jnp.moveaxis(q.reshape(bsz, nb, ATT_BLOCK, ATT_KV_HEADS, ATT_GROUP, HEAD_DIM), 1, 0)
    sink_b = sink.astype(jnp.float32).reshape(1, ATT_KV_HEADS, ATT_GROUP, 1, 1)

    def one_block(q_blk):
        s = jnp.einsum('bqkgd,bksd->bkgqs', q_blk, k_ctx).astype(jnp.float32) * ATT_SCALE
        (p,) = sink_softmax([s], sink_b)
        return jnp.einsum('bkgqs,bksd->bqkgd', p.astype(v_ctx.dtype), v_ctx)

    o = lax.map(one_block, qb)
    return jnp.moveaxis(o, 0, 1).reshape(bsz, seq_len, ATT_HEADS * HEAD_DIM)


def latent_attention(q, k, v, k_ctx, v_ctx, sink):
    bsz, seq_len = q.shape[:2]
    nb = seq_len // ATT_BLOCK
    blk = ATT_BLOCK
    qb = q.reshape(bsz, nb, blk, ATT_KV_HEADS, ATT_GROUP, HEAD_DIM)

    def band(t):
        tb = jnp.pad(t.reshape(bsz, nb, blk, ATT_KV_HEADS, HEAD_DIM), ((0, 0), (1, 1), (0, 0), (0, 0), (0, 0)))
        return jnp.concatenate([tb[:, :-2], tb[:, 1:-1], tb[:, 2:]], axis=2)

    kw, vw = band(k), band(v)
    qpos = jnp.arange(nb)[:, None] * blk + jnp.arange(blk)[None, :]
    kpos = jnp.arange(nb)[:, None] * blk - blk + jnp.arange(3 * blk)[None, :]
    valid = ((jnp.abs(qpos[:, :, None] - kpos[:, None, :]) <= WINDOW)
             & (kpos >= 0)[:, None, :] & (kpos < seq_len)[:, None, :])
    s_loc = jnp.einsum('bnqkgd,bnskd->bnkgqs', qb, kw).astype(jnp.float32) * ATT_SCALE
    s_loc = jnp.where(valid[None, :, None, None], s_loc, -jnp.inf)
    s_ctx = jnp.einsum('bnqkgd,bksd->bnkgqs', qb, k_ctx).astype(jnp.float32) * ATT_SCALE
    sink_b = sink.astype(jnp.float32).reshape(1, 1, ATT_KV_HEADS, ATT_GROUP, 1, 1)
    p_loc, p_ctx = sink_softmax([s_loc, s_ctx], sink_b)
    o = (jnp.einsum('bnkgqs,bnskd->bnqkgd', p_loc.astype(v.dtype), vw)
         + jnp.einsum('bnkgqs,bksd->bnqkgd', p_ctx.astype(v_ctx.dtype), v_ctx))
    return o.reshape(bsz, seq_len, ATT_HEADS * HEAD_DIM)


def moe(h, w_router, router_bias, w_up, w_down):
    bsz, seq_len, d = h.shape
    t = h.reshape(-1, d)
    scores = jax.nn.sigmoid((t @ w_router).astype(jnp.float32))
    sel = scores + router_bias.astype(jnp.float32)
    grp_score = lax.top_k(sel.reshape(-1, N_GROUPS, EXPERTS_PER_GROUP), TOP_K)[0].sum(-1)
    best_group = jnp.argmax(grp_score, axis=-1)
    in_group = (jnp.arange(N_EXPERTS) // EXPERTS_PER_GROUP)[None, :] == best_group[:, None]
    _, idx = lax.top_k(jnp.where(in_group, sel, -jnp.inf), TOP_K)
    w = jnp.take_along_axis(scores, idx, axis=-1)
    w = w / jnp.sum(w, axis=-1, keepdims=True)
    gate = jnp.sum(jax.nn.one_hot(idx, N_EXPERTS, dtype=jnp.float32) * w[..., None], axis=1)
    hu = jnp.einsum('td,edf->tef', t, w_up)
    a = jax.nn.silu(hu[..., :D_EXPERT]) * hu[..., D_EXPERT:]
    y = jnp.einsum('tef,efd->td', a * gate[:, :, None].astype(a.dtype), w_down)
    return y.reshape(bsz, seq_len, d)


def setup_inputs(seed: int = 0) -> dict:
    key = jax.random.key(seed)
    ks = jax.random.split(key, 23)

    def normal(k, shape, scale):
        return jax.random.normal(k, shape, jnp.float32) * scale

    decay_base = jnp.log(2.0 ** (5.0 + jnp.arange(RET_HEADS, dtype=jnp.float32)) - 1.0)
    return {
        'x_prompt': normal(ks[0], (BATCH, SEQ, D_MODEL), 1.0),
        'x_sample': normal(ks[1], (DEC_BATCH, DEC_SEQ, D_MODEL), 1.0),
        'state_ret': normal(ks[2], (DEC_BATCH, N_RET_LAYERS, 2, RET_HEADS, RET_DK, RET_DV), 1.0),
        'cache_k': normal(ks[3], (DEC_BATCH, N_ATT_LAYERS, ATT_KV_HEADS, PAST_LEN, HEAD_DIM), 1.0),
        'cache_v': normal(ks[4], (DEC_BATCH, N_ATT_LAYERS, ATT_KV_HEADS, PAST_LEN, HEAD_DIM), 1.0),
        'c': normal(ks[5], (DEC_BATCH, D_MODEL), 1.0),
        'c_ctx': normal(ks[6], (D_MODEL,), 1.0),
        'w_mod': normal(ks[7], (DEPTH, D_MODEL, N_MOD * D_MODEL), 0.5 * D_MODEL ** -0.5),
        'b_mod': normal(ks[8], (DEPTH, N_MOD * D_MODEL), 0.01),
        'norm_mix': 1.0 + normal(ks[9], (DEPTH, D_MODEL), 0.05),
        'norm_moe': 1.0 + normal(ks[10], (DEPTH, D_MODEL), 0.05),
        'ret_w_in': normal(ks[11], (N_RET_LAYERS, D_MODEL, RET_IN), D_MODEL ** -0.5),
        'ret_decay': decay_base + normal(ks[12], (N_RET_LAYERS, 2, RET_HEADS), 0.1),
        'ret_w_out': normal(ks[13], (N_RET_LAYERS, RET_HEADS * RET_DV, D_MODEL), (RET_HEADS * RET_DV) ** -0.5),
        'att_w_in': normal(ks[14], (N_ATT_LAYERS, D_MODEL, ATT_IN), D_MODEL ** -0.5),
        'att_q_norm': 1.0 + normal(ks[15], (N_ATT_LAYERS, HEAD_DIM), 0.05),
        'att_k_norm': 1.0 + normal(ks[16], (N_ATT_LAYERS, HEAD_DIM), 0.05),
        'att_sink': normal(ks[17], (N_ATT_LAYERS, ATT_HEADS), 0.5),
        'att_w_out': normal(ks[18], (N_ATT_LAYERS, ATT_HEADS * HEAD_DIM, D_MODEL), (ATT_HEADS * HEAD_DIM) ** -0.5),
        'w_router': normal(ks[19], (D_MODEL, N_EXPERTS), D_MODEL ** -0.5),
        'router_bias': normal(ks[20], (N_EXPERTS,), 0.01),
        'moe_w_up': normal(ks[21], (DEPTH, N_EXPERTS, D_MODEL, 2 * D_EXPERT), D_MODEL ** -0.5),
        'moe_w_down': normal(ks[22], (DEPTH, N_EXPERTS, D_EXPERT, D_MODEL), D_EXPERT ** -0.5),
    }


def reference(x_prompt, x_sample, state_ret, cache_k, cache_v, c, c_ctx, w_mod, b_mod, norm_mix,
              norm_moe, ret_w_in, ret_decay, ret_w_out, att_w_in, att_q_norm, att_k_norm, att_sink,
              att_w_out, w_router, router_bias, moe_w_up, moe_w_down):
    ctx, lat = x_prompt, x_sample
    bsz_ctx = x_prompt.shape[0]
    new_ret, new_k, new_v = [], [], []
    for layer in range(DEPTH):
        sh_a_c, sc_a_c, g_a_c, sh_m_c, sc_m_c, g_m_c = modulation(c_ctx[None, :], w_mod[layer], b_mod[layer])
        sh_a_l, sc_a_l, g_a_l, sh_m_l, sc_m_l, g_m_l = modulation(c, w_mod[layer], b_mod[layer])
        h_ctx = modulate(ctx, norm_mix[layer], sh_a_c, sc_a_c)
        h_lat = modulate(lat, norm_mix[layer], sh_a_l, sc_a_l)
        j = layer // N_MIXERS
        if layer % N_MIXERS == 0:
            log_g = jax.nn.log_sigmoid(ret_decay[j].astype(jnp.float32))
            q, k, v, gate = retention_project(h_ctx, ret_w_in[j])
            zeros = jnp.zeros((bsz_ctx, RET_HEADS, RET_DK, RET_DV), jnp.float32)
            o, s_f, s_b = bidir_retention(q, k, v, log_g, zeros, zeros)
            mix_ctx = retention_output(o, gate, ret_w_out[j])
            new_ret.append(jnp.stack([s_f, s_b], axis=1))
            q, k, v, gate = retention_project(h_lat, ret_w_in[j])
            o, _, _ = bidir_retention(axial_rope(q), axial_rope(k), v, log_g,
                                      state_ret[:, j, 0], state_ret[:, j, 1])
            mix_lat = retention_output(o, gate, ret_w_out[j])
        else:
            q, k, v = attention_project(h_ctx, att_w_in[j], att_q_norm[j], att_k_norm[j])
            k_c = jnp.swapaxes(k, 1, 2)
            v_c = jnp.swapaxes(v, 1, 2)
            new_k.append(k_c)
            new_v.append(v_c)
            mix_ctx = context_attention(q, k_c, v_c, att_sink[j]) @ att_w_out[j]
            q, k, v = attention_project(h_lat, att_w_in[j], att_q_norm[j], att_k_norm[j])
            mix_lat = latent_attention(axial_rope(q), axial_rope(k), v, cache_k[:, j], cache_v[:, j],
                                       att_sink[j]) @ att_w_out[j]
        ctx = ctx + g_a_c * mix_ctx
        lat = lat + g_a_l * mix_lat
        ctx = ctx + g_m_c * moe(modulate(ctx, norm_moe[layer], sh_m_c, sc_m_c), w_router, router_bias,
                                moe_w_up[layer], moe_w_down[layer])
        lat = lat + g_m_l * moe(modulate(lat, norm_moe[layer], sh_m_l, sc_m_l), w_router, router_bias,
                                moe_w_up[layer], moe_w_down[layer])
    new_state_ret = jnp.stack(new_ret, axis=1)
    new_cache_k = jnp.stack(new_k, axis=1)
    new_cache_v = jnp.stack(new_v, axis=1)
    return (ctx, lat, new_state_ret, new_cache_k, new_cache_v)
```

```python
import functools

import jax
import jax.numpy as jnp
from jax import lax
from jax.experimental import pallas as pl
from jax.experimental.pallas import tpu as pltpu

F32 = jnp.float32
BF16 = jnp.bfloat16

D_MODEL = 1024
BATCH = 16
SEQ = 256
DEPTH = 4
DEC_BATCH = 2
DEC_SEQ = 4096
PAST_LEN = 512
GRID_W = 64
N_MOD = 6
EPS = 1e-6
ROPE_BASE = 10000.0
RET_HEADS = 8
RET_DK = 128
RET_DV = 256
RET_CHUNK = 128
RET_IN = 2 * RET_HEADS * RET_DK + 2 * RET_HEADS * RET_DV
ATT_HEADS = 16
ATT_KV_HEADS = 4
ATT_GROUP = 4
HEAD_DIM = 64
WINDOW = 128
ATT_BLOCK = 128
ATT_IN = (ATT_HEADS + 2 * ATT_KV_HEADS) * HEAD_DIM
ATT_SCALE = HEAD_DIM ** -0.5
N_EXPERTS = 16
N_GROUPS = 4
EXPERTS_PER_GROUP = 4
D_EXPERT = 512

CTX_ROWS = BATCH * SEQ
LAT_ROWS = DEC_BATCH * DEC_SEQ
T_ROWS = CTX_ROWS + LAT_ROWS
N_COND = 1 + DEC_BATCH

LANES = 128
TM = 512
N_CTX_TILES = CTX_ROWS // TM
N_LAT_TILES = DEC_SEQ // TM
TM_MOE = 512
N_MOE_TILES = T_ROWS // TM_MOE + N_GROUPS
P_ROWS = N_MOE_TILES * TM_MOE
VMEM_LIMIT = 52 * 1024 * 1024

NT_DIMS = (((1,), (1,)), ((), ()))


def _params(n_axes, vmem=VMEM_LIMIT):
    return pltpu.CompilerParams(dimension_semantics=("arbitrary",) * n_axes, vmem_limit_bytes=vmem)


def _cond_row(i):
    return (i * TM) // DEC_SEQ


def _rope_row(i):
    return jnp.where(i < N_CTX_TILES, 0, 1 + (i - N_CTX_TILES) % N_LAT_TILES)


def _norm_mod(x, gain, shift, scale):
    ms = jnp.mean(x * x, axis=-1, keepdims=True)
    y = x * lax.rsqrt(ms + EPS) * gain
    return y * (1.0 + scale) + shift


def _silu(x):
    return x * jax.nn.sigmoid(x)


def _mod_kernel(c_ref, w_ref, b_ref, o_ref):
    c = c_ref[...]
    s = _silu(c).astype(BF16)
    o_ref[0] = jnp.dot(s, w_ref[0].astype(BF16), preferred_element_type=F32) + b_ref[0]


def _modulation_all(cond8, w_mod, b_mod):
    tn = 1536
    n = N_MOD * D_MODEL
    return pl.pallas_call(
        _mod_kernel,
        grid=(DEPTH, n // tn),
        in_specs=[
            pl.BlockSpec((8, D_MODEL), lambda l, j: (0, 0)),
            pl.BlockSpec((1, D_MODEL, tn), lambda l, j: (l, 0, j)),
            pl.BlockSpec((1, 1, tn), lambda l, j: (l, 0, j)),
        ],
        out_specs=pl.BlockSpec((1, 8, tn), lambda l, j: (l, 0, j)),
        out_shape=jax.ShapeDtypeStruct((DEPTH, 8, n), F32),
        compiler_params=_params(2),
        name="modulation",
    )(cond8, w_mod, b_mod.reshape(DEPTH, 1, n))


def _rope_tables(head_dim):
    half = head_dim // 2
    quarter = half // 2
    t = jnp.arange(DEC_SEQ)
    row = (t // GRID_W).astype(F32)
    col = (t % GRID_W).astype(F32)
    inv_freq = ROPE_BASE ** (-jnp.arange(quarter, dtype=F32) / quarter)
    lane = jnp.arange(LANES)
    d = lane % head_dim
    w = d % half
    f = w % quarter
    pos = jnp.where((d // half)[None, :] == 0, row[:, None], col[:, None])
    ang = pos * inv_freq[f][None, :]
    cos = jnp.cos(ang)
    sin = jnp.where((w < quarter)[None, :], -jnp.sin(ang), jnp.sin(ang))
    cos = jnp.concatenate([jnp.ones((TM, LANES), F32), cos], axis=0)
    sin = jnp.concatenate([jnp.zeros((TM, LANES), F32), sin], axis=0)
    return cos, sin


def _rope_chunk(xc, cos, sin, quarter):
    lane = lax.broadcasted_iota(jnp.int32, xc.shape, 1)
    first = (lane % (2 * quarter)) < quarter
    partner = jnp.where(first, pltpu.roll(xc, LANES - quarter, 1), pltpu.roll(xc, quarter, 1))
    return xc * cos + partner * sin


RET_TN = 1024
RET_Q_TILES = RET_HEADS * RET_DK // RET_TN
RET_QK_TILES = 2 * RET_Q_TILES


def _ret_proj_kernel(x_ref, gain_ref, sh_ref, sc_ref, w_ref, cos_ref, sin_ref, o_ref, wbf_ref):
    j = pl.program_id(0)
    i = pl.program_id(1)

    @pl.when(i == 0)
    def _():
        wbf_ref[...] = w_ref[...].astype(BF16)

    h = _norm_mod(x_ref[...], gain_ref[...], sh_ref[0], sc_ref[0]).astype(BF16)
    acc = jnp.dot(h, wbf_ref[...], preferred_element_type=F32)

    @pl.when(j < RET_QK_TILES)
    def _():
        scale = jnp.where(j < RET_Q_TILES, 1.0, RET_DK ** -0.5).astype(F32)
        cos = cos_ref[...]
        sin = sin_ref[...]
        for c in range(RET_TN // LANES):
            xc = acc[:, c * LANES:(c + 1) * LANES]
            o_ref[:, c * LANES:(c + 1) * LANES] = (_rope_chunk(xc, cos, sin, RET_DK // 4) * scale).astype(BF16)

    @pl.when(j >= RET_QK_TILES)
    def _():
        o_ref[...] = acc.astype(BF16)


def _ret_proj(x, gain, shift, scale, w_in, cos, sin):
    n_j = RET_IN // RET_TN
    return pl.pallas_call(
        _ret_proj_kernel,
        grid=(n_j, T_ROWS // TM),
        in_specs=[
            pl.BlockSpec((TM, D_MODEL), lambda j, i: (i, 0)),
            pl.BlockSpec((1, D_MODEL), lambda j, i: (0, 0)),
            pl.BlockSpec((1, 1, D_MODEL), lambda j, i: (_cond_row(i), 0, 0)),
            pl.BlockSpec((1, 1, D_MODEL), lambda j, i: (_cond_row(i), 0, 0)),
            pl.BlockSpec((D_MODEL, RET_TN), lambda j, i: (0, j)),
            pl.BlockSpec((TM, LANES), lambda j, i: (_rope_row(i), 0)),
            pl.BlockSpec((TM, LANES), lambda j, i: (_rope_row(i), 0)),
        ],
        out_specs=pl.BlockSpec((TM, RET_TN), lambda j, i: (i, j)),
        out_shape=jax.ShapeDtypeStruct((T_ROWS, RET_IN), BF16),
        scratch_shapes=[pltpu.VMEM((D_MODEL, RET_TN), BF16)],
        compiler_params=_params(2),
        name="ret_proj",
    )(x, gain, shift, scale, w_in, cos, sin)


def _log_sigmoid(x):
    return -(jnp.maximum(-x, 0.0) + jnp.log(1.0 + jnp.exp(-jnp.abs(x))))


def _ret_core_kernel(*refs, seq_len, has_init, emit_state):
    it = iter(refs)
    decay_ref = next(it)
    q_ref = next(it)
    k_ref = next(it)
    v_ref = next(it)
    g_ref = next(it)
    s0_ref = next(it) if has_init else None
    o_ref = next(it)
    st_ref = next(it) if emit_state else None
    sf_ref = next(it)
    sb_ref = next(it)
    of_ref = next(it)
    ob_ref = next(it)

    c = RET_CHUNK
    nc = seq_len // c
    head = pl.program_id(1)

    lg_all = _log_sigmoid(decay_ref[...])
    pick = lax.broadcasted_iota(jnp.int32, lg_all.shape, 1) == head
    lg = jnp.sum(jnp.where(pick, lg_all, 0.0), axis=1, keepdims=True)
    lg_f = lg[0:1, :]
    lg_b = lg[1:2, :]

    ri = lax.broadcasted_iota(jnp.int32, (c, c), 0).astype(F32)
    ci = lax.broadcasted_iota(jnp.int32, (c, c), 1).astype(F32)
    diff = ri - ci
    decay_f = jnp.where(diff >= 0, jnp.exp(jnp.where(diff >= 0, diff, 0.0) * lg_f), 0.0)
    decay_b = jnp.where(diff <= 0, jnp.exp(jnp.where(diff <= 0, -diff, 0.0) * lg_b), 0.0)
    pos = lax.broadcasted_iota(jnp.int32, (c, 1), 0).astype(F32)
    xi_f = jnp.exp((pos + 1.0) * lg_f)
    zeta_f = jnp.exp((c - 1.0 - pos) * lg_f)
    cd_f = jnp.exp(c * lg_f)
    xi_b = jnp.exp((c - pos) * lg_b)
    zeta_b = jnp.exp(pos * lg_b)
    cd_b = jnp.exp(c * lg_b)

    if has_init:
        sf_ref[...] = s0_ref[0, 0, 0, 0]
        sb_ref[...] = s0_ref[0, 0, 1, 0]
    else:
        sf_ref[...] = jnp.zeros_like(sf_ref)
        sb_ref[...] = jnp.zeros_like(sb_ref)

    def chunk(n, decay, xi, zeta, cd, s_ref, out_ref):
        r0 = pl.multiple_of(n * c, c)
        qn = q_ref[pl.ds(r0, c), :]
        kn = k_ref[pl.ds(r0, c), :]
        vn = v_ref[pl.ds(r0, c), :]
        s = lax.dot_general(qn, kn, NT_DIMS, preferred_element_type=F32) * decay
        o = jnp.dot(s.astype(BF16), vn, preferred_element_type=F32)
        state = s_ref[...]
        o = o + jnp.dot((qn.astype(F32) * xi).astype(BF16), state.astype(BF16), preferred_element_type=F32)
        out_ref[pl.ds(r0, c), :] = o
        kz_t = (kn.astype(F32) * zeta).T.astype(BF16)
        s_ref[...] = cd * state + jnp.dot(kz_t, vn, preferred_element_type=F32)

    def both(n, carry):
        chunk(n, decay_f, xi_f, zeta_f, cd_f, sf_ref, of_ref)
        chunk(nc - 1 - n, decay_b, xi_b, zeta_b, cd_b, sb_ref, ob_ref)
        return carry

    lax.fori_loop(0, nc, both, 0)

    def finalize(n, carry):
        r0 = pl.multiple_of(n * c, c)
        o = of_ref[pl.ds(r0, c), :] + ob_ref[pl.ds(r0, c), :]
        mu = jnp.mean(o, axis=-1, keepdims=True)
        var = jnp.mean(jnp.square(o - mu), axis=-1, keepdims=True)
        on = (o - mu) * lax.rsqrt(var + EPS)
        gate = g_ref[pl.ds(r0, c), :].astype(F32)
        o_ref[pl.ds(r0, c), :] = (_silu(gate) * on).astype(BF16)
        return carry

    lax.fori_loop(0, nc, finalize, 0)

    if emit_state:
        st_ref[0, 0, 0] = sf_ref[...]
        st_ref[0, 1, 0] = sb_ref[...]


def _ret_core(proj, ret_decay_j, state_ret, layer_j, *, latent):
    if latent:
        nb, seq_len, row0 = DEC_BATCH, DEC_SEQ, CTX_ROWS // DEC_SEQ
    else:
        nb, seq_len, row0 = BATCH, SEQ, 0
    kcol = RET_HEADS * RET_DK // RET_DK
    vcol = 2 * RET_HEADS * RET_DK // RET_DV
    gcol = vcol + RET_HEADS
    in_specs = [
        pl.BlockSpec((2, RET_HEADS), lambda b, h: (0, 0)),
        pl.BlockSpec((seq_len, RET_DK), lambda b, h: (row0 + b, h)),
        pl.BlockSpec((seq_len, RET_DK), lambda b, h: (row0 + b, kcol + h)),
        pl.BlockSpec((seq_len, RET_DV), lambda b, h: (row0 + b, vcol + h)),
        pl.BlockSpec((seq_len, RET_DV), lambda b, h: (row0 + b, gcol + h)),
    ]
    args = [ret_decay_j, proj, proj, proj, proj]
    if latent:
        in_specs.append(pl.BlockSpec((1, 1, 2, 1, RET_DK, RET_DV), lambda b, h: (b, layer_j, 0, h, 0, 0)))
        args.append(state_ret)
    out_specs = [pl.BlockSpec((seq_len, RET_DV), lambda b, h: (b, h))]
    out_shape = [jax.ShapeDtypeStruct((nb * seq_len, RET_HEADS * RET_DV), BF16)]
    if not latent:
        out_specs.append(pl.BlockSpec((1, 2, 1, RET_DK, RET_DV), lambda b, h: (b, 0, h, 0, 0)))
        out_shape.append(jax.ShapeDtypeStruct((nb, 2, RET_HEADS, RET_DK, RET_DV), F32))
    return pl.pallas_call(
        functools.partial(_ret_core_kernel, seq_len=seq_len, has_init=latent, emit_state=not latent),
        grid=(nb, RET_HEADS),
        in_specs=in_specs,
        out_specs=out_specs,
        out_shape=out_shape,
        scratch_shapes=[
            pltpu.VMEM((RET_DK, RET_DV), F32),
            pltpu.VMEM((RET_DK, RET_DV), F32),
            pltpu.VMEM((seq_len, RET_DV), F32),
            pltpu.VMEM((seq_len, RET_DV), F32),
        ],
        compiler_params=_params(2),
        name="ret_core_lat" if latent else "ret_core_ctx",
    )(*args)


def _out_proj_kernel(a_ref, w_ref, x_ref, g_ref, o_ref, wbf_ref):
    @pl.when(pl.program_id(0) == 0)
    def _():
        wbf_ref[...] = w_ref[...].astype(BF16)

    y = jnp.dot(a_ref[...], wbf_ref[...], preferred_element_type=F32)
    o_ref[...] = x_ref[...] + g_ref[0] * y


def _out_proj(a, w_out, x, gate):
    k = a.shape[1]
    return pl.pallas_call(
        _out_proj_kernel,
        grid=(T_ROWS // TM,),
        in_specs=[
            pl.BlockSpec((TM, k), lambda i: (i, 0)),
            pl.BlockSpec((k, D_MODEL), lambda i: (0, 0)),
            pl.BlockSpec((TM, D_MODEL), lambda i: (i, 0)),
            pl.BlockSpec((1, 1, D_MODEL), lambda i: (_cond_row(i), 0, 0)),
        ],
        out_specs=pl.BlockSpec((TM, D_MODEL), lambda i: (i, 0)),
        out_shape=jax.ShapeDtypeStruct((T_ROWS, D_MODEL), F32),
        scratch_shapes=[pltpu.VMEM((k, D_MODEL), BF16)],
        compiler_params=_params(1),
        name="out_proj",
    )(a, w_out, x, gate)


ATT_Q_W = ATT_HEADS * HEAD_DIM
ATT_KV_W = ATT_KV_HEADS * HEAD_DIM


def _group_mean_sq(xc, gmat):
    sq = xc * xc
    hi = sq.astype(BF16)
    lo = (sq - hi.astype(F32)).astype(BF16)
    return jnp.dot(hi, gmat, preferred_element_type=F32) + jnp.dot(lo, gmat, preferred_element_type=F32)


def _att_proj_kernel(x_ref, gain_ref, sh_ref, sc_ref, w_ref, qg_ref, kg_ref, cos_ref, sin_ref,
                     o_ref, kv_ref, wbf_ref):
    i = pl.program_id(0)

    @pl.when(i == 0)
    def _():
        wbf_ref[...] = w_ref[...].astype(BF16)

    h = _norm_mod(x_ref[...], gain_ref[...], sh_ref[0], sc_ref[0]).astype(BF16)
    acc = jnp.dot(h, wbf_ref[...], preferred_element_type=F32)

    r = lax.broadcasted_iota(jnp.int32, (LANES, LANES), 0) // HEAD_DIM
    c = lax.broadcasted_iota(jnp.int32, (LANES, LANES), 1) // HEAD_DIM
    gmat = jnp.where(r == c, 1.0 / HEAD_DIM, 0.0).astype(BF16)
    cos = cos_ref[...]
    sin = sin_ref[...]
    n_qk = (ATT_Q_W + ATT_KV_W) // LANES
    for cidx in range(n_qk):
        xc = acc[:, cidx * LANES:(cidx + 1) * LANES]
        gain = qg_ref[...] if cidx < ATT_Q_W // LANES else kg_ref[...]
        normed = xc * lax.rsqrt(_group_mean_sq(xc, gmat) + EPS) * gain
        if cidx >= ATT_Q_W // LANES:
            kc = cidx - ATT_Q_W // LANES
            kv_ref[:, kc * LANES:(kc + 1) * LANES] = normed
        o_ref[:, cidx * LANES:(cidx + 1) * LANES] = _rope_chunk(normed, cos, sin, HEAD_DIM // 4).astype(BF16)
    v = acc[:, ATT_Q_W + ATT_KV_W:]
    kv_ref[:, ATT_KV_W:] = v
    o_ref[:, ATT_Q_W + ATT_KV_W:] = v.astype(BF16)


def _att_proj(x, gain, shift, scale, w_in, q_gain, k_gain, cos, sin):
    kv_block = lambda i: (jnp.minimum(i, N_CTX_TILES), 0)
    return pl.pallas_call(
        _att_proj_kernel,
        grid=(T_ROWS // TM,),
        in_specs=[
            pl.BlockSpec((TM, D_MODEL), lambda i: (i, 0)),
            pl.BlockSpec((1, D_MODEL), lambda i: (0, 0)),
            pl.BlockSpec((1, 1, D_MODEL), lambda i: (_cond_row(i), 0, 0)),
            pl.BlockSpec((1, 1, D_MODEL), lambda i: (_cond_row(i), 0, 0)),
            pl.BlockSpec((D_MODEL, ATT_IN), lambda i: (0, 0)),
            pl.BlockSpec((1, LANES), lambda i: (0, 0)),
            pl.BlockSpec((1, LANES), lambda i: (0, 0)),
            pl.BlockSpec((TM, LANES), lambda i: (_rope_row(i), 0)),
            pl.BlockSpec((TM, LANES), lambda i: (_rope_row(i), 0)),
        ],
        out_specs=[
            pl.BlockSpec((TM, ATT_IN), lambda i: (i, 0)),
            pl.BlockSpec((TM, 2 * ATT_KV_W), kv_block),
        ],
        out_shape=[
            jax.ShapeDtypeStruct((T_ROWS, ATT_IN), BF16),
            jax.ShapeDtypeStruct((CTX_ROWS + TM, 2 * ATT_KV_W), F32),
        ],
        scratch_shapes=[pltpu.VMEM((D_MODEL, ATT_IN), BF16)],
        compiler_params=_params(1),
        name="att_proj",
    )(x, gain, shift, scale, w_in, q_gain, k_gain, cos, sin)


def _sink_softmax_pv(qs, sink_col, blocks):
    scores = []
    m = sink_col
    for k, _, mask in blocks:
        s = lax.dot_general(qs, k, NT_DIMS, preferred_element_type=F32)
        if mask is not None:
            s = jnp.where(mask, s, -jnp.inf)
        scores.append(s)
        m = jnp.maximum(m, jnp.max(s, axis=-1, keepdims=True))
    den = jnp.exp(sink_col - m)
    o = None
    for s, (_, v, _) in zip(scores, blocks):
        e = jnp.exp(s - m)
        den = den + jnp.sum(e, axis=-1, keepdims=True)
        pv = jnp.dot(e.astype(BF16), v, preferred_element_type=F32)
        o = pv if o is None else o + pv
    return o / den


def _stack_heads(q, kvh, rows):
    parts = []
    for g in range(ATT_GROUP):
        hd = kvh * ATT_GROUP + g
        parts.append(q[:, hd * HEAD_DIM:(hd + 1) * HEAD_DIM])
    return jnp.concatenate(parts, axis=0) * jnp.asarray(ATT_SCALE, BF16)


def _sink_column(sink_ref, kvh, rows):
    parts = [jnp.full((rows, 1), sink_ref[kvh * ATT_GROUP + g], F32) for g in range(ATT_GROUP)]
    return jnp.concatenate(parts, axis=0)


def _store_heads(o_ref, o, kvh, rows):
    for g in range(ATT_GROUP):
        hd = kvh * ATT_GROUP + g
        o_ref[:, hd * HEAD_DIM:(hd + 1) * HEAD_DIM] = o[g * rows:(g + 1) * rows, :].astype(BF16)


def _ctx_att_kernel(sink_ref, q_ref, k_ref, v_ref, o_ref):
    q = q_ref[...]
    for kvh in range(ATT_KV_HEADS):
        qs = _stack_heads(q, kvh, SEQ)
        k = k_ref[:, kvh * HEAD_DIM:(kvh + 1) * HEAD_DIM]
        v = v_ref[:, kvh * HEAD_DIM:(kvh + 1) * HEAD_DIM]
        o = _sink_softmax_pv(qs, _sink_column(sink_ref, kvh, SEQ), [(k, v, None)])
        _store_heads(o_ref, o, kvh, SEQ)


def _lat_att_kernel(sink_ref, q_ref, k_ref, v_ref, ck_ref, cv_ref, o_ref):
    n = pl.program_id(1)
    nb = DEC_SEQ // ATT_BLOCK
    blk = ATT_BLOCK
    rows = ATT_GROUP * blk
    prev0 = pl.multiple_of(jnp.maximum(n - 1, 0) * blk, blk)
    cur0 = pl.multiple_of(n * blk, blk)
    next0 = pl.multiple_of(jnp.minimum(n + 1, nb - 1) * blk, blk)

    qi = lax.broadcasted_iota(jnp.int32, (rows, 3 * blk), 0) % blk
    kj = lax.broadcasted_iota(jnp.int32, (rows, 3 * blk), 1) - blk
    kabs = n * blk + kj
    valid = (jnp.abs(qi - kj) <= WINDOW) & (kabs >= 0) & (kabs < DEC_SEQ)

    q = q_ref[...]
    for kvh in range(ATT_KV_HEADS):
        cols = slice(kvh * HEAD_DIM, (kvh + 1) * HEAD_DIM)
        qs = _stack_heads(q, kvh, blk)
        k_loc = jnp.concatenate([k_ref[pl.ds(prev0, blk), cols], k_ref[pl.ds(cur0, blk), cols],
                                 k_ref[pl.ds(next0, blk), cols]], axis=0)
        v_loc = jnp.concatenate([v_ref[pl.ds(prev0, blk), cols], v_ref[pl.ds(cur0, blk), cols],
                                 v_ref[pl.ds(next0, blk), cols]], axis=0)
        k_ctx = ck_ref[0, 0, kvh].astype(BF16)
        v_ctx = cv_ref[0, 0, kvh].astype(BF16)
        o = _sink_softmax_pv(qs, _sink_column(sink_ref, kvh, blk),
                             [(k_loc, v_loc, valid), (k_ctx, v_ctx, None)])
        _store_heads(o_ref, o, kvh, blk)


def _ctx_attention(qkv, sink):
    kcol = ATT_Q_W // ATT_KV_W
    return pl.pallas_call(
        _ctx_att_kernel,
        grid_spec=pltpu.PrefetchScalarGridSpec(
            num_scalar_prefetch=1,
            grid=(BATCH,),
            in_specs=[
                pl.BlockSpec((SEQ, ATT_Q_W), lambda b, s: (b, 0)),
                pl.BlockSpec((SEQ, ATT_KV_W), lambda b, s: (b, kcol)),
                pl.BlockSpec((SEQ, ATT_KV_W), lambda b, s: (b, kcol + 1)),
            ],
            out_specs=pl.BlockSpec((SEQ, ATT_Q_W), lambda b, s: (b, 0)),
        ),
        out_shape=jax.ShapeDtypeStruct((CTX_ROWS, ATT_Q_W), BF16),
        compiler_params=_params(1),
        name="ctx_attention",
    )(sink, qkv, qkv, qkv)


def _lat_attention(qkv, sink, cache_k, cache_v, layer_j):
    kcol = ATT_Q_W // ATT_KV_W
    nb = DEC_SEQ // ATT_BLOCK
    q0 = CTX_ROWS // ATT_BLOCK
    s0 = CTX_ROWS // DEC_SEQ
    cache_spec = pl.BlockSpec((1, 1, ATT_KV_HEADS, PAST_LEN, HEAD_DIM), lambda b, n, s: (b, layer_j, 0, 0, 0))
    return pl.pallas_call(
        _lat_att_kernel,
        grid_spec=pltpu.PrefetchScalarGridSpec(
            num_scalar_prefetch=1,
            grid=(DEC_BATCH, nb),
            in_specs=[
                pl.BlockSpec((ATT_BLOCK, ATT_Q_W), lambda b, n, s: (q0 + b * nb + n, 0)),
                pl.BlockSpec((DEC_SEQ, ATT_KV_W), lambda b, n, s: (s0 + b, kcol)),
                pl.BlockSpec((DEC_SEQ, ATT_KV_W), lambda b, n, s: (s0 + b, kcol + 1)),
                cache_spec,
                cache_spec,
            ],
            out_specs=pl.BlockSpec((ATT_BLOCK, ATT_Q_W), lambda b, n, s: (b * nb + n, 0)),
        ),
        out_shape=jax.ShapeDtypeStruct((LAT_ROWS, ATT_Q_W), BF16),
        compiler_params=_params(2),
        name="lat_attention",
    )(sink, qkv, qkv, qkv, cache_k, cache_v)


def _router_kernel(x_ref, gain_ref, sh_ref, sc_ref, wrh_ref, wrl_ref, bias_ref,
                   h_ref, info_ref, cnt_ref, carry_ref):
    i = pl.program_id(0)

    @pl.when(i == 0)
    def _():
        carry_ref[...] = jnp.zeros_like(carry_ref)

    hf = _norm_mod(x_ref[...], gain_ref[...], sh_ref[0], sc_ref[0])
    hb = hf.astype(BF16)
    h_ref[...] = hb
    hl = (hf - hb.astype(F32)).astype(BF16)
    wrh = wrh_ref[...]
    logits = (jnp.dot(hb, wrh, preferred_element_type=F32) + jnp.dot(hl, wrh, preferred_element_type=F32)
              + jnp.dot(hb, wrl_ref[...], preferred_element_type=F32))
    lt = logits.T[0:N_EXPERTS, :]
    scores = jax.nn.sigmoid(lt)
    sel = scores + bias_ref[...]
    ng = N_GROUPS
    xs = [sel[k * ng:(k + 1) * ng, :] for k in range(EXPERTS_PER_GROUP)]
    sc = [scores[k * ng:(k + 1) * ng, :] for k in range(EXPERTS_PER_GROUP)]
    a, b, c, d = xs
    gs = jnp.maximum(jnp.maximum(jnp.maximum(a + b, a + c), jnp.maximum(a + d, b + c)),
                     jnp.maximum(b + d, c + d))
    bv = gs[0:1, :]
    bg = jnp.zeros(bv.shape, jnp.int32)
    for g in range(1, ng):
        better = gs[g:g + 1, :] > bv
        bg = jnp.where(better, g, bg)
        bv = jnp.where(better, gs[g:g + 1, :], bv)
    giota = lax.broadcasted_iota(jnp.int32, (ng, TM), 0)
    onehot = giota == bg
    wk = []
    for k in range(EXPERTS_PER_GROUP):
        rank = jnp.zeros((ng, TM), F32)
        for j in range(EXPERTS_PER_GROUP):
            if j < k:
                rank = rank + (xs[j] >= xs[k]).astype(F32)
            elif j > k:
                rank = rank + (xs[j] > xs[k]).astype(F32)
        chosen = (rank < 2.0) & onehot
        wk.append(jnp.sum(jnp.where(chosen, sc[k], 0.0), axis=0, keepdims=True))
    den = wk[0] + wk[1] + wk[2] + wk[3]
    gates = [w / den for w in wk]

    oh = jnp.where(onehot, 1.0, 0.0)
    r = lax.broadcasted_iota(jnp.int32, (TM, TM), 0)
    cc = lax.broadcasted_iota(jnp.int32, (TM, TM), 1)
    tri = jnp.where(r < cc, 1.0, 0.0).astype(BF16)
    carry = carry_ref[0:ng, 0:1]
    before = jnp.dot(oh.astype(BF16), tri, preferred_element_type=F32) + carry
    rank_in_group = jnp.sum(jnp.where(onehot, before, 0.0), axis=0, keepdims=True)
    new_carry = carry + jnp.sum(oh, axis=1, keepdims=True)
    carry_ref[0:ng, 0:1] = new_carry

    info_ref[...] = jnp.concatenate(gates + [bg.astype(F32), rank_in_group, jnp.zeros((2, TM), F32)], axis=0)
    cnt_ref[...] = jnp.broadcast_to(jnp.concatenate([new_carry, jnp.zeros((8 - ng, 1), F32)], axis=0), (8, LANES))


def _router(x, gain, shift, scale, wr_hi, wr_lo, bias_col):
    return pl.pallas_call(
        _router_kernel,
        grid=(T_ROWS // TM,),
        in_specs=[
            pl.BlockSpec((TM, D_MODEL), lambda i: (i, 0)),
            pl.BlockSpec((1, D_MODEL), lambda i: (0, 0)),
            pl.BlockSpec((1, 1, D_MODEL), lambda i: (_cond_row(i), 0, 0)),
            pl.BlockSpec((1, 1, D_MODEL), lambda i: (_cond_row(i), 0, 0)),
            pl.BlockSpec((D_MODEL, LANES), lambda i: (0, 0)),
            pl.BlockSpec((D_MODEL, LANES), lambda i: (0, 0)),
            pl.BlockSpec((N_EXPERTS, 1), lambda i: (0, 0)),
        ],
        out_specs=[
            pl.BlockSpec((TM, D_MODEL), lambda i: (i, 0)),
            pl.BlockSpec((8, TM), lambda i: (0, i)),
            pl.BlockSpec((8, LANES), lambda i: (0, 0)),
        ],
        out_shape=[
            jax.ShapeDtypeStruct((T_ROWS, D_MODEL), BF16),
            jax.ShapeDtypeStruct((8, T_ROWS), F32),
            jax.ShapeDtypeStruct((8, LANES), F32),
        ],
        scratch_shapes=[pltpu.VMEM((8, LANES), F32)],
        compiler_params=_params(1),
        name="moe_router",
    )(x, gain, shift, scale, wr_hi, wr_lo, bias_col)


def _moe_up_kernel(tg_ref, tv_ref, xs_ref, gs_ref, w_ref, o_ref, wbf_ref):
    k = pl.program_id(0)
    i = pl.program_id(1)
    new_w = (i == 0) | (tg_ref[i] != tg_ref[jnp.maximum(i - 1, 0)])

    @pl.when(new_w)
    def _():
        wbf_ref[...] = w_ref[0].astype(BF16)

    @pl.when(tv_ref[i] == 1)
    def _():
        hu = jnp.dot(xs_ref[...], wbf_ref[...], preferred_element_type=F32)
        gs = gs_ref[...]
        lane = lax.broadcasted_iota(jnp.int32, gs.shape, 1)
        gcol = jnp.sum(jnp.where(lane == k, gs, 0.0), axis=1, keepdims=True)
        a = _silu(hu[:, :D_EXPERT]) * hu[:, D_EXPERT:] * gcol
        o_ref[...] = a.astype(BF16)

    @pl.when(tv_ref[i] == 0)
    def _():
        o_ref[...] = jnp.zeros_like(o_ref)


def _moe_up(tile_group, tile_valid, xs, gs, w_up):
    return pl.pallas_call(
        _moe_up_kernel,
        grid_spec=pltpu.PrefetchScalarGridSpec(
            num_scalar_prefetch=2,
            grid=(EXPERTS_PER_GROUP, N_MOE_TILES),
            in_specs=[
                pl.BlockSpec((TM_MOE, D_MODEL), lambda k, i, tg, tv: (i, 0)),
                pl.BlockSpec((TM_MOE, EXPERTS_PER_GROUP), lambda k, i, tg, tv: (i, 0)),
                pl.BlockSpec((1, D_MODEL, 2 * D_EXPERT),
                             lambda k, i, tg, tv: (tg[i] * EXPERTS_PER_GROUP + k, 0, 0)),
            ],
            out_specs=pl.BlockSpec((TM_MOE, D_EXPERT), lambda k, i, tg, tv: (i, k)),
            scratch_shapes=[pltpu.VMEM((D_MODEL, 2 * D_EXPERT), BF16)],
        ),
        out_shape=jax.ShapeDtypeStruct((P_ROWS, EXPERTS_PER_GROUP * D_EXPERT), BF16),
        compiler_params=_params(2),
        name="moe_up",
    )(tile_group, tile_valid, xs, gs, w_up)


def _moe_down_kernel(tg_ref, tv_ref, a_ref, w_ref, o_ref, wbf_ref):
    i = pl.program_id(0)
    new_w = (i == 0) | (tg_ref[i] != tg_ref[jnp.maximum(i - 1, 0)])

    @pl.when(new_w)
    def _():
        wbf_ref[...] = w_ref[0].astype(BF16)

    o_ref[...] = jnp.dot(a_ref[...], wbf_ref[...], preferred_element_type=F32).astype(BF16)


def _moe_down(tile_group, tile_valid, a, w_down_grouped):
    kdim = EXPERTS_PER_GROUP * D_EXPERT
    return pl.pallas_call(
        _moe_down_kernel,
        grid_spec=pltpu.PrefetchScalarGridSpec(
            num_scalar_prefetch=2,
            grid=(N_MOE_TILES,),
            in_specs=[
                pl.BlockSpec((TM_MOE, kdim), lambda i, tg, tv: (i, 0)),
                pl.BlockSpec((1, kdim, D_MODEL), lambda i, tg, tv: (tg[i], 0, 0)),
            ],
            out_specs=pl.BlockSpec((TM_MOE, D_MODEL), lambda i, tg, tv: (i, 0)),
            scratch_shapes=[pltpu.VMEM((kdim, D_MODEL), BF16)],
        ),
        out_shape=jax.ShapeDtypeStruct((P_ROWS, D_MODEL), BF16),
        compiler_params=_params(1),
        name="moe_down",
    )(tile_group, tile_valid, a, w_down_grouped)


def _residual_kernel(x_ref, y_ref, g_ref, o_ref):
    o_ref[...] = x_ref[...] + g_ref[0] * y_ref[...].astype(F32)


def _residual(x, y, gate):
    return pl.pallas_call(
        _residual_kernel,
        grid=(T_ROWS // TM,),
        in_specs=[
            pl.BlockSpec((TM, D_MODEL), lambda i: (i, 0)),
            pl.BlockSpec((TM, D_MODEL), lambda i: (i, 0)),
            pl.BlockSpec((1, 1, D_MODEL), lambda i: (_cond_row(i), 0, 0)),
        ],
        out_specs=pl.BlockSpec((TM, D_MODEL), lambda i: (i, 0)),
        out_shape=jax.ShapeDtypeStruct((T_ROWS, D_MODEL), F32),
        compiler_params=_params(1),
        name="moe_residual",
    )(x, y, gate)


def _moe_layer(x, gain, shift, scale, gate, router_w, w_up, w_down):
    wr_hi, wr_lo, bias_col = router_w
    h, info, cnt = _router(x, gain, shift, scale, wr_hi, wr_lo, bias_col)
    gate4 = info[0:EXPERTS_PER_GROUP].T
    grp = info[4].astype(jnp.int32)
    rank = info[5].astype(jnp.int32)
    counts = cnt[0:N_GROUPS, 0].astype(jnp.int32)
    padded = ((counts + TM_MOE - 1) // TM_MOE) * TM_MOE
    ends = jnp.cumsum(padded)
    pos = (ends - padded)[grp] + rank
    tile_start = jnp.arange(N_MOE_TILES, dtype=jnp.int32) * TM_MOE
    tile_group = jnp.minimum(jnp.sum(tile_start[:, None] >= ends[None, :], axis=1), N_GROUPS - 1).astype(jnp.int32)
    tile_valid = (tile_start < ends[N_GROUPS - 1]).astype(jnp.int32)
    xs = jnp.zeros((P_ROWS, D_MODEL), BF16).at[pos].set(h)
    gsorted = jnp.zeros((P_ROWS, EXPERTS_PER_GROUP), F32).at[pos].set(gate4)
    a = _moe_up(tile_group, tile_valid, xs, gsorted, w_up)
    w_down_grouped = w_down.reshape(N_GROUPS, EXPERTS_PER_GROUP * D_EXPERT, D_MODEL)
    ys = _moe_down(tile_group, tile_valid, a, w_down_grouped)
    y = ys[pos]
    return _residual(x, y, gate)


def kernel(x_prompt, x_sample, state_ret, cache_k, cache_v, c, c_ctx, w_mod, b_mod, norm_mix,
           norm_moe, ret_w_in, ret_decay, ret_w_out, att_w_in, att_q_norm, att_k_norm, att_sink,
           att_w_out, w_router, router_bias, moe_w_up, moe_w_down):
    x = jnp.concatenate([x_prompt.reshape(CTX_ROWS, D_MODEL), x_sample.reshape(LAT_ROWS, D_MODEL)], axis=0)
    cond8 = jnp.zeros((8, D_MODEL), F32).at[0].set(c_ctx).at[1:N_COND].set(c)
    mods = _modulation_all(cond8, w_mod, b_mod)
    mods = mods[:, :N_COND].reshape(DEPTH, N_COND, N_MOD, 1, D_MODEL)

    ret_cos, ret_sin = _rope_tables(RET_DK)
    att_cos, att_sin = _rope_tables(HEAD_DIM)

    perm = jnp.arange(N_EXPERTS).reshape(N_GROUPS, EXPERTS_PER_GROUP).T.reshape(-1)
    wr = jnp.zeros((D_MODEL, LANES), F32).at[:, :N_EXPERTS].set(w_router[:, perm])
    wr_hi = wr.astype(BF16)
    wr_lo = (wr - wr_hi.astype(F32)).astype(BF16)
    bias_col = router_bias[perm].astype(F32).reshape(N_EXPERTS, 1)
    router_w = (wr_hi, wr_lo, bias_col)

    new_ret, new_k, new_v = [], [], []
    for layer in range(DEPTH):
        sh_a, sc_a, g_a, sh_m, sc_m, g_m = [mods[layer, :, t] for t in range(N_MOD)]
        gain_mix = norm_mix[layer].reshape(1, D_MODEL)
        gain_moe = norm_moe[layer].reshape(1, D_MODEL)
        j = layer // 2
        if layer % 2 == 0:
            proj = _ret_proj(x, gain_mix, sh_a, sc_a, ret_w_in[j], ret_cos, ret_sin)
            o_ctx, st = _ret_core(proj, ret_decay[j], state_ret, j, latent=False)
            (o_lat,) = _ret_core(proj, ret_decay[j], state_ret, j, latent=True)
            new_ret.append(st)
            mixed = jnp.concatenate([o_ctx, o_lat], axis=0)
            x = _out_proj(mixed, ret_w_out[j], x, g_a)
        else:
            q_gain = jnp.tile(att_q_norm[j], LANES // HEAD_DIM).reshape(1, LANES)
            k_gain = jnp.tile(att_k_norm[j], LANES // HEAD_DIM).reshape(1, LANES)
            qkv, kv32 = _att_proj(x, gain_mix, sh_a, sc_a, att_w_in[j], q_gain, k_gain, att_cos, att_sin)
            kv = kv32[:CTX_ROWS].reshape(BATCH, SEQ, 2, ATT_KV_HEADS, HEAD_DIM)
            new_k.append(jnp.transpose(kv[:, :, 0], (0, 2, 1, 3)))
            new_v.append(jnp.transpose(kv[:, :, 1], (0, 2, 1, 3)))
            sink = att_sink[j].astype(F32)
            o_ctx = _ctx_attention(qkv, sink)
            o_lat = _lat_attention(qkv, sink, cache_k, cache_v, j)
            mixed = jnp.concatenate([o_ctx, o_lat], axis=0)
            x = _out_proj(mixed, att_w_out[j], x, g_a)
        x = _moe_layer(x, gain_moe, sh_m, sc_m, g_m, router_w, moe_w_up[layer], moe_w_down[layer])

    y_prompt = x[:CTX_ROWS].reshape(BATCH, SEQ, D_MODEL)
    y_sample = x[CTX_ROWS:].reshape(DEC_BATCH, DEC_SEQ, D_MODEL)
    return (y_prompt, y_sample, jnp.stack(new_ret, axis=1), jnp.stack(new_k, axis=1), jnp.stack(new_v, axis=1))
```

```python
import functools

import jax
import jax.numpy as jnp
from jax import lax
from jax.experimental import pallas as pl
from jax.experimental.pallas import tpu as pltpu

F32 = jnp.float32
BF16 = jnp.bfloat16

D_MODEL = 1024
BATCH = 16
SEQ = 256
DEPTH = 4
DEC_BATCH = 2
DEC_SEQ = 4096
PAST_LEN = 512
GRID_W = 64
N_MOD = 6
EPS = 1e-6
ROPE_BASE = 10000.0
RET_HEADS = 8
RET_DK = 128
RET_DV = 256
RET_CHUNK = 128
RET_IN = 2 * RET_HEADS * RET_DK + 2 * RET_HEADS * RET_DV
ATT_HEADS = 16
ATT_KV_HEADS = 4
ATT_GROUP = 4
HEAD_DIM = 64
WINDOW = 128
ATT_BLOCK = 128
ATT_IN = (ATT_HEADS + 2 * ATT_KV_HEADS) * HEAD_DIM
ATT_SCALE = HEAD_DIM ** -0.5
N_EXPERTS = 16
N_GROUPS = 4
EXPERTS_PER_GROUP = 4
D_EXPERT = 512

CTX_ROWS = BATCH * SEQ
LAT_ROWS = DEC_BATCH * DEC_SEQ
T_ROWS = CTX_ROWS + LAT_ROWS
N_COND = 1 + DEC_BATCH

LANES = 128
TM = 512
N_CTX_TILES = CTX_ROWS // TM
N_LAT_TILES = DEC_SEQ // TM
TM_MOE = 512
N_MOE_TILES = T_ROWS // TM_MOE + N_GROUPS
P_ROWS = N_MOE_TILES * TM_MOE
VMEM_LIMIT = 52 * 1024 * 1024

NT_DIMS = (((1,), (1,)), ((), ()))


def _params(n_axes, vmem=VMEM_LIMIT):
    return pltpu.CompilerParams(dimension_semantics=("arbitrary",) * n_axes, vmem_limit_bytes=vmem)


def _cond_row(i):
    return (i * TM) // DEC_SEQ


def _rope_row(i):
    return jnp.where(i < N_CTX_TILES, 0, 1 + (i - N_CTX_TILES) % N_LAT_TILES)


def _norm_mod(x, gain, shift, scale):
    ms = jnp.mean(x * x, axis=-1, keepdims=True)
    y = x * lax.rsqrt(ms + EPS) * gain
    return y * (1.0 + scale) + shift


def _silu(x):
    return x * jax.nn.sigmoid(x)


def _mod_kernel(c_ref, w_ref, b_ref, o_ref):
    c = c_ref[...]
    s = _silu(c).astype(BF16)
    o_ref[0] = jnp.dot(s, w_ref[0].astype(BF16), preferred_element_type=F32) + b_ref[0]


def _modulation_all(cond8, w_mod, b_mod):
    tn = 1536
    n = N_MOD * D_MODEL
    return pl.pallas_call(
        _mod_kernel,
        grid=(DEPTH, n // tn),
        in_specs=[
            pl.BlockSpec((8, D_MODEL), lambda l, j: (0, 0)),
            pl.BlockSpec((1, D_MODEL, tn), lambda l, j: (l, 0, j)),
            pl.BlockSpec((1, 1, tn), lambda l, j: (l, 0, j)),
        ],
        out_specs=pl.BlockSpec((1, 8, tn), lambda l, j: (l, 0, j)),
        out_shape=jax.ShapeDtypeStruct((DEPTH, 8, n), F32),
        compiler_params=_params(2),
        name="modulation",
    )(cond8, w_mod, b_mod.reshape(DEPTH, 1, n))


def _rope_tables(head_dim):
    half = head_dim // 2
    quarter = half // 2
    t = jnp.arange(DEC_SEQ)
    row = (t // GRID_W).astype(F32)
    col = (t % GRID_W).astype(F32)
    inv_freq = ROPE_BASE ** (-jnp.arange(quarter, dtype=F32) / quarter)
    lane = jnp.arange(LANES)
    d = lane % head_dim
    w = d % half
    f = w % quarter
    pos = jnp.where((d // half)[None, :] == 0, row[:, None], col[:, None])
    ang = pos * inv_freq[f][None, :]
    cos = jnp.cos(ang)
    sin = jnp.where((w < quarter)[None, :], -jnp.sin(ang), jnp.sin(ang))
    cos = jnp.concatenate([jnp.ones((TM, LANES), F32), cos], axis=0)
    sin = jnp.concatenate([jnp.zeros((TM, LANES), F32), sin], axis=0)
    return cos, sin


def _rope_chunk(xc, cos, sin, quarter):
    lane = lax.broadcasted_iota(jnp.int32, xc.shape, 1)
    first = (lane % (2 * quarter)) < quarter
    partner = jnp.where(first, pltpu.roll(xc, LANES - quarter, 1), pltpu.roll(xc, quarter, 1))
    return xc * cos + partner * sin


RET_TN = 1024
RET_Q_TILES = RET_HEADS * RET_DK // RET_TN
RET_QK_TILES = 2 * RET_Q_TILES


def _ret_proj_kernel(x_ref, gain_ref, sh_ref, sc_ref, w_ref, cos_ref, sin_ref, o_ref, wbf_ref):
    j = pl.program_id(0)
    i = pl.program_id(1)

    @pl.when(i == 0)
    def _():
        wbf_ref[...] = w_ref[...].astype(BF16)

    h = _norm_mod(x_ref[...], gain_ref[...], sh_ref[0], sc_ref[0]).astype(BF16)
    acc = jnp.dot(h, wbf_ref[...], preferred_element_type=F32)

    @pl.when(j < RET_QK_TILES)
    def _():
        scale = jnp.where(j < RET_Q_TILES, 1.0, RET_DK ** -0.5).astype(F32)
        cos = cos_ref[...]
        sin = sin_ref[...]
        for c in range(RET_TN // LANES):
            xc = acc[:, c * LANES:(c + 1) * LANES]
            o_ref[:, c * LANES:(c + 1) * LANES] = (_rope_chunk(xc, cos, sin, RET_DK // 4) * scale).astype(BF16)

    @pl.when(j >= RET_QK_TILES)
    def _():
        o_ref[...] = acc.astype(BF16)


def _ret_proj(x, gain, shift, scale, w_in, cos, sin):
    n_j = RET_IN // RET_TN
    return pl.pallas_call(
        _ret_proj_kernel,
        grid=(n_j, T_ROWS // TM),
        in_specs=[
            pl.BlockSpec((TM, D_MODEL), lambda j, i: (i, 0)),
            pl.BlockSpec((1, D_MODEL), lambda j, i: (0, 0)),
            pl.BlockSpec((1, 1, D_MODEL), lambda j, i: (_cond_row(i), 0, 0)),
            pl.BlockSpec((1, 1, D_MODEL), lambda j, i: (_cond_row(i), 0, 0)),
            pl.BlockSpec((D_MODEL, RET_TN), lambda j, i: (0, j)),
            pl.BlockSpec((TM, LANES), lambda j, i: (_rope_row(i), 0)),
            pl.BlockSpec((TM, LANES), lambda j, i: (_rope_row(i), 0)),
        ],
        out_specs=pl.BlockSpec((TM, RET_TN), lambda j, i: (i, j)),
        out_shape=jax.ShapeDtypeStruct((T_ROWS, RET_IN), BF16),
        scratch_shapes=[pltpu.VMEM((D_MODEL, RET_TN), BF16)],
        compiler_params=_params(2),
        name="ret_proj",
    )(x, gain, shift, scale, w_in, cos, sin)


def _log_sigmoid(x):
    return -(jnp.maximum(-x, 0.0) + jnp.log(1.0 + jnp.exp(-jnp.abs(x))))


def _ret_core_kernel(*refs, seq_len, n_seq, has_init, emit_state):
    it = iter(refs)
    decay_ref = next(it)
    q_ref = next(it)
    k_ref = next(it)
    v_ref = next(it)
    g_ref = next(it)
    s0_ref = next(it) if has_init else None
    o_ref = next(it)
    st_ref = next(it) if emit_state else None
    acc_ref = next(it)
    kv_ref = next(it)

    c = RET_CHUNK
    nc = seq_len // c
    n_chunks = n_seq * nc
    unroll = min(RET_UNROLL, n_chunks)
    head = pl.program_id(1)

    lg_all = _log_sigmoid(decay_ref[...])
    pick = lax.broadcasted_iota(jnp.int32, lg_all.shape, 1) == head
    lg = jnp.sum(jnp.where(pick, lg_all, 0.0), axis=1, keepdims=True)
    lg_f = lg[0:1, :]
    lg_b = lg[1:2, :]

    ri = lax.broadcasted_iota(jnp.int32, (c, c), 0).astype(F32)
    ci = lax.broadcasted_iota(jnp.int32, (c, c), 1).astype(F32)
    diff = ri - ci
    decay = (jnp.where(diff >= 0, jnp.exp(jnp.where(diff >= 0, diff, 0.0) * lg_f), 0.0)
             + jnp.where(diff <= 0, jnp.exp(jnp.where(diff <= 0, -diff, 0.0) * lg_b), 0.0))
    pos_col = lax.broadcasted_iota(jnp.int32, (c, 1), 0).astype(F32)
    pos_row = lax.broadcasted_iota(jnp.int32, (1, c), 1).astype(F32)
    xi_f = jnp.exp((pos_col + 1.0) * lg_f)
    xi_b = jnp.exp((c - pos_col) * lg_b)
    zeta_f = jnp.exp((c - 1.0 - pos_row) * lg_f)
    zeta_b = jnp.exp(pos_row * lg_b)
    cd_f = jnp.exp(c * lg_f)
    cd_b = jnp.exp(c * lg_b)

    def intra(n):
        r0 = pl.multiple_of(n * c, c)
        qn = q_ref[pl.ds(r0, c), :]
        kn = k_ref[pl.ds(r0, c), :]
        vn = v_ref[pl.ds(r0, c), :]
        s = lax.dot_general(qn, kn, NT_DIMS, preferred_element_type=F32) * decay
        acc_ref[pl.ds(r0, c), :] = jnp.dot(s.astype(BF16), vn, preferred_element_type=F32)
        kt = kn.astype(F32).T
        kz = jnp.concatenate([(kt * zeta_f).astype(BF16), (kt * zeta_b).astype(BF16)], axis=0)
        kv_ref[n] = jnp.dot(kz, vn, preferred_element_type=F32)

    def cross(n):
        r0 = pl.multiple_of(n * c, c)
        qn = q_ref[pl.ds(r0, c), :].astype(F32)
        qx = jnp.concatenate([(qn * xi_f).astype(BF16), (qn * xi_b).astype(BF16)], axis=1)
        o = acc_ref[pl.ds(r0, c), :] + jnp.dot(qx, kv_ref[n].astype(BF16), preferred_element_type=F32)
        mu = jnp.mean(o, axis=-1, keepdims=True)
        var = jnp.mean(jnp.square(o - mu), axis=-1, keepdims=True)
        on = (o - mu) * lax.rsqrt(var + EPS)
        gate = g_ref[pl.ds(r0, c), :].astype(F32)
        o_ref[pl.ds(r0, c), :] = (_silu(gate) * on).astype(BF16)

    def over_chunks(fn):
        def body(step, carry):
            for u in range(unroll):
                fn(step * unroll + u)
            return carry

        lax.fori_loop(0, n_chunks // unroll, body, 0)

    over_chunks(intra)

    for s in range(n_seq):
        if has_init:
            init_f = s0_ref[s, 0, 0, 0]
            init_b = s0_ref[s, 0, 1, 0]
        else:
            init_f = jnp.zeros((RET_DK, RET_DV), F32)
            init_b = init_f

        def fwd(n, state, s=s):
            kv = kv_ref[s * nc + n, 0:RET_DK, :]
            kv_ref[s * nc + n, 0:RET_DK, :] = state
            return cd_f * state + kv

        def bwd(n, state, s=s):
            m = s * nc + nc - 1 - n
            kv = kv_ref[m, RET_DK:, :]
            kv_ref[m, RET_DK:, :] = state
            return cd_b * state + kv

        final_f = lax.fori_loop(0, nc, fwd, init_f)
        final_b = lax.fori_loop(0, nc, bwd, init_b)
        if emit_state:
            st_ref[s, 0, 0] = final_f
            st_ref[s, 1, 0] = final_b

    over_chunks(cross)


RET_UNROLL = 4
RET_CTX_SEQS = 4


def _ret_core(proj, ret_decay_j, state_ret, layer_j, *, latent):
    if latent:
        nb, seq_len, n_seq, row0 = DEC_BATCH, DEC_SEQ, 1, CTX_ROWS // DEC_SEQ
    else:
        nb, seq_len, n_seq, row0 = BATCH // RET_CTX_SEQS, SEQ, RET_CTX_SEQS, 0
    rows = n_seq * seq_len
    kcol = RET_HEADS * RET_DK // RET_DK
    vcol = 2 * RET_HEADS * RET_DK // RET_DV
    gcol = vcol + RET_HEADS
    in_specs = [
        pl.BlockSpec((2, RET_HEADS), lambda b, h: (0, 0)),
        pl.BlockSpec((rows, RET_DK), lambda b, h: (row0 + b, h)),
        pl.BlockSpec((rows, RET_DK), lambda b, h: (row0 + b, kcol + h)),
        pl.BlockSpec((rows, RET_DV), lambda b, h: (row0 + b, vcol + h)),
        pl.BlockSpec((rows, RET_DV), lambda b, h: (row0 + b, gcol + h)),
    ]
    args = [ret_decay_j, proj, proj, proj, proj]
    if latent:
        in_specs.append(pl.BlockSpec((1, 1, 2, 1, RET_DK, RET_DV), lambda b, h: (b, layer_j, 0, h, 0, 0)))
        args.append(state_ret)
    out_specs = [pl.BlockSpec((rows, RET_DV), lambda b, h: (b, h))]
    out_shape = [jax.ShapeDtypeStruct((nb * rows, RET_HEADS * RET_DV), BF16)]
    if not latent:
        out_specs.append(pl.BlockSpec((n_seq, 2, 1, RET_DK, RET_DV), lambda b, h: (b, 0, h, 0, 0)))
        out_shape.append(jax.ShapeDtypeStruct((BATCH, 2, RET_HEADS, RET_DK, RET_DV), F32))
    return pl.pallas_call(
        functools.partial(_ret_core_kernel, seq_len=seq_len, n_seq=n_seq, has_init=latent,
                          emit_state=not latent),
        grid=(nb, RET_HEADS),
        in_specs=in_specs,
        out_specs=out_specs,
        out_shape=out_shape,
        scratch_shapes=[
            pltpu.VMEM((rows, RET_DV), F32),
            pltpu.VMEM((rows // RET_CHUNK, 2 * RET_DK, RET_DV), F32),
        ],
        compiler_params=_params(2),
        name="ret_core_lat" if latent else "ret_core_ctx",
    )(*args)


def _out_proj_kernel(ac_ref, al_ref, w_ref, x_ref, g_ref, o_ref, wbf_ref):
    i = pl.program_id(0)

    @pl.when(i == 0)
    def _():
        wbf_ref[...] = w_ref[...].astype(BF16)

    def emit(a_ref):
        y = jnp.dot(a_ref[...], wbf_ref[...], preferred_element_type=F32)
        o_ref[...] = x_ref[...] + g_ref[0] * y

    @pl.when(i < N_CTX_TILES)
    def _():
        emit(ac_ref)

    @pl.when(i >= N_CTX_TILES)
    def _():
        emit(al_ref)


def _out_proj(a_ctx, a_lat, w_out, x, gate):
    k = a_ctx.shape[1]
    return pl.pallas_call(
        _out_proj_kernel,
        grid=(T_ROWS // TM,),
        in_specs=[
            pl.BlockSpec((TM, k), lambda i: (jnp.minimum(i, N_CTX_TILES - 1), 0)),
            pl.BlockSpec((TM, k), lambda i: (jnp.maximum(i - N_CTX_TILES, 0), 0)),
            pl.BlockSpec((k, D_MODEL), lambda i: (0, 0)),
            pl.BlockSpec((TM, D_MODEL), lambda i: (i, 0)),
            pl.BlockSpec((1, 1, D_MODEL), lambda i: (_cond_row(i), 0, 0)),
        ],
        out_specs=pl.BlockSpec((TM, D_MODEL), lambda i: (i, 0)),
        out_shape=jax.ShapeDtypeStruct((T_ROWS, D_MODEL), F32),
        scratch_shapes=[pltpu.VMEM((k, D_MODEL), BF16)],
        compiler_params=_params(1),
        name="out_proj",
    )(a_ctx, a_lat, w_out, x, gate)


ATT_Q_W = ATT_HEADS * HEAD_DIM
ATT_KV_W = ATT_KV_HEADS * HEAD_DIM


def _group_mean_sq(xc, gmat):
    sq = xc * xc
    hi = sq.astype(BF16)
    lo = (sq - hi.astype(F32)).astype(BF16)
    return jnp.dot(hi, gmat, preferred_element_type=F32) + jnp.dot(lo, gmat, preferred_element_type=F32)


def _att_proj_kernel(x_ref, gain_ref, sh_ref, sc_ref, w_ref, qg_ref, kg_ref, cos_ref, sin_ref,
                     o_ref, kv_ref, wbf_ref):
    i = pl.program_id(0)

    @pl.when(i == 0)
    def _():
        wbf_ref[...] = w_ref[...].astype(BF16)

    h = _norm_mod(x_ref[...], gain_ref[...], sh_ref[0], sc_ref[0]).astype(BF16)
    acc = jnp.dot(h, wbf_ref[...], preferred_element_type=F32)

    r = lax.broadcasted_iota(jnp.int32, (LANES, LANES), 0) // HEAD_DIM
    c = lax.broadcasted_iota(jnp.int32, (LANES, LANES), 1) // HEAD_DIM
    gmat = jnp.where(r == c, 1.0 / HEAD_DIM, 0.0).astype(BF16)
    cos = cos_ref[...]
    sin = sin_ref[...]
    n_qk = (ATT_Q_W + ATT_KV_W) // LANES
    for cidx in range(n_qk):
        xc = acc[:, cidx * LANES:(cidx + 1) * LANES]
        gain = qg_ref[...] if cidx < ATT_Q_W // LANES else kg_ref[...]
        normed = xc * lax.rsqrt(_group_mean_sq(xc, gmat) + EPS) * gain
        if cidx >= ATT_Q_W // LANES:
            kc = cidx - ATT_Q_W // LANES
            kv_ref[:, kc * LANES:(kc + 1) * LANES] = normed
        o_ref[:, cidx * LANES:(cidx + 1) * LANES] = _rope_chunk(normed, cos, sin, HEAD_DIM // 4).astype(BF16)
    v = acc[:, ATT_Q_W + ATT_KV_W:]
    kv_ref[:, ATT_KV_W:] = v
    o_ref[:, ATT_Q_W + ATT_KV_W:] = v.astype(BF16)


def _att_proj(x, gain, shift, scale, w_in, q_gain, k_gain, cos, sin):
    kv_block = lambda i: (jnp.minimum(i, N_CTX_TILES), 0)
    return pl.pallas_call(
        _att_proj_kernel,
        grid=(T_ROWS // TM,),
        in_specs=[
            pl.BlockSpec((TM, D_MODEL), lambda i: (i, 0)),
            pl.BlockSpec((1, D_MODEL), lambda i: (0, 0)),
            pl.BlockSpec((1, 1, D_MODEL), lambda i: (_cond_row(i), 0, 0)),
            pl.BlockSpec((1, 1, D_MODEL), lambda i: (_cond_row(i), 0, 0)),
            pl.BlockSpec((D_MODEL, ATT_IN), lambda i: (0, 0)),
            pl.BlockSpec((1, LANES), lambda i: (0, 0)),
            pl.BlockSpec((1, LANES), lambda i: (0, 0)),
            pl.BlockSpec((TM, LANES), lambda i: (_rope_row(i), 0)),
            pl.BlockSpec((TM, LANES), lambda i: (_rope_row(i), 0)),
        ],
        out_specs=[
            pl.BlockSpec((TM, ATT_IN), lambda i: (i, 0)),
            pl.BlockSpec((TM, 2 * ATT_KV_W), kv_block),
        ],
        out_shape=[
            jax.ShapeDtypeStruct((T_ROWS, ATT_IN), BF16),
            jax.ShapeDtypeStruct((CTX_ROWS + TM, 2 * ATT_KV_W), F32),
        ],
        scratch_shapes=[pltpu.VMEM((D_MODEL, ATT_IN), BF16)],
        compiler_params=_params(1),
        name="att_proj",
    )(x, gain, shift, scale, w_in, q_gain, k_gain, cos, sin)


def _sink_softmax_pv(qs, sink_col, blocks):
    scores = []
    m = sink_col
    for k, _, mask in blocks:
        s = lax.dot_general(qs, k, NT_DIMS, preferred_element_type=F32)
        if mask is not None:
            s = jnp.where(mask, s, -jnp.inf)
        scores.append(s)
        m = jnp.maximum(m, jnp.max(s, axis=-1, keepdims=True))
    den = jnp.exp(sink_col - m)
    o = None
    for s, (_, v, _) in zip(scores, blocks):
        e = jnp.exp(s - m)
        den = den + jnp.sum(e, axis=-1, keepdims=True)
        pv = jnp.dot(e.astype(BF16), v, preferred_element_type=F32)
        o = pv if o is None else o + pv
    return o / den


def _stack_heads(q, kvh, rows):
    parts = []
    for g in range(ATT_GROUP):
        hd = kvh * ATT_GROUP + g
        parts.append(q[:, hd * HEAD_DIM:(hd + 1) * HEAD_DIM])
    return jnp.concatenate(parts, axis=0) * jnp.asarray(ATT_SCALE, BF16)


def _sink_column(sink_ref, kvh, rows):
    parts = [jnp.full((rows, 1), sink_ref[kvh * ATT_GROUP + g], F32) for g in range(ATT_GROUP)]
    return jnp.concatenate(parts, axis=0)


def _store_heads(o_ref, o, kvh, rows):
    for g in range(ATT_GROUP):
        hd = kvh * ATT_GROUP + g
        o_ref[:, hd * HEAD_DIM:(hd + 1) * HEAD_DIM] = o[g * rows:(g + 1) * rows, :].astype(BF16)


def _ctx_att_kernel(sink_ref, q_ref, k_ref, v_ref, o_ref):
    q = q_ref[...]
    for kvh in range(ATT_KV_HEADS):
        qs = _stack_heads(q, kvh, SEQ)
        k = k_ref[:, kvh * HEAD_DIM:(kvh + 1) * HEAD_DIM]
        v = v_ref[:, kvh * HEAD_DIM:(kvh + 1) * HEAD_DIM]
        o = _sink_softmax_pv(qs, _sink_column(sink_ref, kvh, SEQ), [(k, v, None)])
        _store_heads(o_ref, o, kvh, SEQ)


def _lat_att_kernel(sink_ref, q_ref, k_ref, v_ref, ck_ref, cv_ref, o_ref):
    n = pl.program_id(1)
    nb = DEC_SEQ // ATT_BLOCK
    blk = ATT_BLOCK
    rows = ATT_GROUP * blk
    prev0 = pl.multiple_of(jnp.maximum(n - 1, 0) * blk, blk)
    cur0 = pl.multiple_of(n * blk, blk)
    next0 = pl.multiple_of(jnp.minimum(n + 1, nb - 1) * blk, blk)

    qi = lax.broadcasted_iota(jnp.int32, (rows, 3 * blk), 0) % blk
    kj = lax.broadcasted_iota(jnp.int32, (rows, 3 * blk), 1) - blk
    kabs = n * blk + kj
    valid = (jnp.abs(qi - kj) <= WINDOW) & (kabs >= 0) & (kabs < DEC_SEQ)

    q = q_ref[...]
    for kvh in range(ATT_KV_HEADS):
        cols = slice(kvh * HEAD_DIM, (kvh + 1) * HEAD_DIM)
        qs = _stack_heads(q, kvh, blk)
        k_loc = jnp.concatenate([k_ref[pl.ds(prev0, blk), cols], k_ref[pl.ds(cur0, blk), cols],
                                 k_ref[pl.ds(next0, blk), cols]], axis=0)
        v_loc = jnp.concatenate([v_ref[pl.ds(prev0, blk), cols], v_ref[pl.ds(cur0, blk), cols],
                                 v_ref[pl.ds(next0, blk), cols]], axis=0)
        k_ctx = ck_ref[0, 0, kvh].astype(BF16)
        v_ctx = cv_ref[0, 0, kvh].astype(BF16)
        o = _sink_softmax_pv(qs, _sink_column(sink_ref, kvh, blk),
                             [(k_loc, v_loc, valid), (k_ctx, v_ctx, None)])
        _store_heads(o_ref, o, kvh, blk)


def _ctx_attention(qkv, sink):
    kcol = ATT_Q_W // ATT_KV_W
    return pl.pallas_call(
        _ctx_att_kernel,
        grid_spec=pltpu.PrefetchScalarGridSpec(
            num_scalar_prefetch=1,
            grid=(BATCH,),
            in_specs=[
                pl.BlockSpec((SEQ, ATT_Q_W), lambda b, s: (b, 0)),
                pl.BlockSpec((SEQ, ATT_KV_W), lambda b, s: (b, kcol)),
                pl.BlockSpec((SEQ, ATT_KV_W), lambda b, s: (b, kcol + 1)),
            ],
            out_specs=pl.BlockSpec((SEQ, ATT_Q_W), lambda b, s: (b, 0)),
        ),
        out_shape=jax.ShapeDtypeStruct((CTX_ROWS, ATT_Q_W), BF16),
        compiler_params=_params(1),
        name="ctx_attention",
    )(sink, qkv, qkv, qkv)


def _lat_attention(qkv, sink, cache_k, cache_v, layer_j):
    kcol = ATT_Q_W // ATT_KV_W
    nb = DEC_SEQ // ATT_BLOCK
    q0 = CTX_ROWS // ATT_BLOCK
    s0 = CTX_ROWS // DEC_SEQ
    cache_spec = pl.BlockSpec((1, 1, ATT_KV_HEADS, PAST_LEN, HEAD_DIM), lambda b, n, s: (b, layer_j, 0, 0, 0))
    return pl.pallas_call(
        _lat_att_kernel,
        grid_spec=pltpu.PrefetchScalarGridSpec(
            num_scalar_prefetch=1,
            grid=(DEC_BATCH, nb),
            in_specs=[
                pl.BlockSpec((ATT_BLOCK, ATT_Q_W), lambda b, n, s: (q0 + b * nb + n, 0)),
                pl.BlockSpec((DEC_SEQ, ATT_KV_W), lambda b, n, s: (s0 + b, kcol)),
                pl.BlockSpec((DEC_SEQ, ATT_KV_W), lambda b, n, s: (s0 + b, kcol + 1)),
                cache_spec,
                cache_spec,
            ],
            out_specs=pl.BlockSpec((ATT_BLOCK, ATT_Q_W), lambda b, n, s: (b * nb + n, 0)),
        ),
        out_shape=jax.ShapeDtypeStruct((LAT_ROWS, ATT_Q_W), BF16),
        compiler_params=_params(2),
        name="lat_attention",
    )(sink, qkv, qkv, qkv, cache_k, cache_v)


HALF_D = D_MODEL // 2
U32 = jnp.uint32


def _pack_rows(x):
    bits = lax.bitcast_convert_type(x.astype(BF16).astype(F32), U32)
    return (bits[:, :HALF_D] >> 16) | bits[:, HALF_D:]


def _unpack_rows(w):
    lo = lax.bitcast_convert_type(w << 16, F32)
    hi = lax.bitcast_convert_type(w & jnp.uint32(0xFFFF0000), F32)
    return jnp.concatenate([lo, hi], axis=1)


def _router_kernel(x_ref, gain_ref, sh_ref, sc_ref, wrh_ref, wrl_ref, bias_ref,
                   h_ref, row_ref, info_ref, cnt_ref, carry_ref):
    i = pl.program_id(0)

    @pl.when(i == 0)
    def _():
        carry_ref[...] = jnp.zeros_like(carry_ref)

    hf = _norm_mod(x_ref[...], gain_ref[...], sh_ref[0], sc_ref[0])
    hb = hf.astype(BF16)
    h_ref[...] = _pack_rows(hf)
    hl = (hf - hb.astype(F32)).astype(BF16)
    wrh = wrh_ref[...]
    logits = (jnp.dot(hb, wrh, preferred_element_type=F32) + jnp.dot(hl, wrh, preferred_element_type=F32)
              + jnp.dot(hb, wrl_ref[...], preferred_element_type=F32))
    lt = logits.T[0:N_EXPERTS, :]
    scores = jax.nn.sigmoid(lt)
    sel = scores + bias_ref[...]
    ng = N_GROUPS
    xs = [sel[k * ng:(k + 1) * ng, :] for k in range(EXPERTS_PER_GROUP)]
    sc = [scores[k * ng:(k + 1) * ng, :] for k in range(EXPERTS_PER_GROUP)]
    a, b, c, d = xs
    gs = jnp.maximum(jnp.maximum(jnp.maximum(a + b, a + c), jnp.maximum(a + d, b + c)),
                     jnp.maximum(b + d, c + d))
    bv = gs[0:1, :]
    bg = jnp.zeros(bv.shape, jnp.int32)
    for g in range(1, ng):
        better = gs[g:g + 1, :] > bv
        bg = jnp.where(better, g, bg)
        bv = jnp.where(better, gs[g:g + 1, :], bv)
    giota = lax.broadcasted_iota(jnp.int32, (ng, TM), 0)
    onehot = giota == bg
    wk = []
    for k in range(EXPERTS_PER_GROUP):
        rank = jnp.zeros((ng, TM), F32)
        for j in range(EXPERTS_PER_GROUP):
            if j < k:
                rank = rank + (xs[j] >= xs[k]).astype(F32)
            elif j > k:
                rank = rank + (xs[j] > xs[k]).astype(F32)
        chosen = (rank < 2.0) & onehot
        wk.append(jnp.sum(jnp.where(chosen, sc[k], 0.0), axis=0, keepdims=True))
    den = wk[0] + wk[1] + wk[2] + wk[3]
    gates = [w / den for w in wk]

    oh = jnp.where(onehot, 1.0, 0.0)
    r = lax.broadcasted_iota(jnp.int32, (TM, TM), 0)
    cc = lax.broadcasted_iota(jnp.int32, (TM, TM), 1)
    tri = jnp.where(r < cc, 1.0, 0.0).astype(BF16)
    carry = carry_ref[0:ng, 0:1]
    before = jnp.dot(oh.astype(BF16), tri, preferred_element_type=F32) + carry
    rank_in_group = jnp.sum(jnp.where(onehot, before, 0.0), axis=0, keepdims=True)
    new_carry = carry + jnp.sum(oh, axis=1, keepdims=True)
    carry_ref[0:ng, 0:1] = new_carry

    info = jnp.concatenate(gates + [bg.astype(F32), rank_in_group, jnp.zeros((2, TM), F32)], axis=0)
    info_ref[...] = info
    row_ref[...] = jnp.concatenate([info, jnp.zeros((LANES - 8, TM), F32)], axis=0).T
    cnt_ref[...] = jnp.broadcast_to(jnp.concatenate([new_carry, jnp.zeros((8 - ng, 1), F32)], axis=0), (8, LANES))


def _router(x, gain, shift, scale, wr_hi, wr_lo, bias_col):
    return pl.pallas_call(
        _router_kernel,
        grid=(T_ROWS // TM,),
        in_specs=[
            pl.BlockSpec((TM, D_MODEL), lambda i: (i, 0)),
            pl.BlockSpec((1, D_MODEL), lambda i: (0, 0)),
            pl.BlockSpec((1, 1, D_MODEL), lambda i: (_cond_row(i), 0, 0)),
            pl.BlockSpec((1, 1, D_MODEL), lambda i: (_cond_row(i), 0, 0)),
            pl.BlockSpec((D_MODEL, LANES), lambda i: (0, 0)),
            pl.BlockSpec((D_MODEL, LANES), lambda i: (0, 0)),
            pl.BlockSpec((N_EXPERTS, 1), lambda i: (0, 0)),
        ],
        out_specs=[
            pl.BlockSpec((TM, HALF_D), lambda i: (i, 0)),
            pl.BlockSpec((TM, LANES), lambda i: (i, 0)),
            pl.BlockSpec((8, TM), lambda i: (0, i)),
            pl.BlockSpec((8, LANES), lambda i: (0, 0)),
        ],
        out_shape=[
            jax.ShapeDtypeStruct((T_ROWS, HALF_D), U32),
            jax.ShapeDtypeStruct((T_ROWS, LANES), F32),
            jax.ShapeDtypeStruct((8, T_ROWS), F32),
            jax.ShapeDtypeStruct((8, LANES), F32),
        ],
        scratch_shapes=[pltpu.VMEM((8, LANES), F32)],
        compiler_params=_params(1),
        name="moe_router",
    )(x, gain, shift, scale, wr_hi, wr_lo, bias_col)


def _moe_up_kernel(tg_ref, tv_ref, xs_ref, gs_ref, w_ref, o_ref, wbf_ref):
    k = pl.program_id(0)
    i = pl.program_id(1)
    new_w = (i == 0) | (tg_ref[i] != tg_ref[jnp.maximum(i - 1, 0)])

    @pl.when(new_w)
    def _():
        wbf_ref[...] = w_ref[0].astype(BF16)

    @pl.when(tv_ref[i] == 1)
    def _():
        xs = _unpack_rows(xs_ref[...]).astype(BF16)
        hu = jnp.dot(xs, wbf_ref[...], preferred_element_type=F32)
        gs = gs_ref[...]
        lane = lax.broadcasted_iota(jnp.int32, gs.shape, 1)
        gcol = jnp.sum(jnp.where(lane == k, gs, 0.0), axis=1, keepdims=True)
        a = _silu(hu[:, :D_EXPERT]) * hu[:, D_EXPERT:] * gcol
        o_ref[...] = a.astype(BF16)

    @pl.when(tv_ref[i] == 0)
    def _():
        o_ref[...] = jnp.zeros_like(o_ref)


def _moe_up(tile_group, tile_valid, xs, gs, w_up):
    return pl.pallas_call(
        _moe_up_kernel,
        grid_spec=pltpu.PrefetchScalarGridSpec(
            num_scalar_prefetch=2,
            grid=(EXPERTS_PER_GROUP, N_MOE_TILES),
            in_specs=[
                pl.BlockSpec((TM_MOE, HALF_D), lambda k, i, tg, tv: (i, 0)),
                pl.BlockSpec((TM_MOE, LANES), lambda k, i, tg, tv: (i, 0)),
                pl.BlockSpec((1, D_MODEL, 2 * D_EXPERT),
                             lambda k, i, tg, tv: (tg[i] * EXPERTS_PER_GROUP + k, 0, 0)),
            ],
            out_specs=pl.BlockSpec((TM_MOE, D_EXPERT), lambda k, i, tg, tv: (i, k)),
            scratch_shapes=[pltpu.VMEM((D_MODEL, 2 * D_EXPERT), BF16)],
        ),
        out_shape=jax.ShapeDtypeStruct((P_ROWS, EXPERTS_PER_GROUP * D_EXPERT), BF16),
        compiler_params=_params(2),
        name="moe_up",
    )(tile_group, tile_valid, xs, gs, w_up)


def _moe_down_kernel(tg_ref, tv_ref, a_ref, w_ref, o_ref, wbf_ref):
    i = pl.program_id(0)
    new_w = (i == 0) | (tg_ref[i] != tg_ref[jnp.maximum(i - 1, 0)])

    @pl.when(new_w)
    def _():
        wbf_ref[...] = w_ref[0].astype(BF16)

    o_ref[...] = _pack_rows(jnp.dot(a_ref[...], wbf_ref[...], preferred_element_type=F32))


def _moe_down(tile_group, tile_valid, a, w_down_grouped):
    kdim = EXPERTS_PER_GROUP * D_EXPERT
    return pl.pallas_call(
        _moe_down_kernel,
        grid_spec=pltpu.PrefetchScalarGridSpec(
            num_scalar_prefetch=2,
            grid=(N_MOE_TILES,),
            in_specs=[
                pl.BlockSpec((TM_MOE, kdim), lambda i, tg, tv: (i, 0)),
                pl.BlockSpec((1, kdim, D_MODEL), lambda i, tg, tv: (tg[i], 0, 0)),
            ],
            out_specs=pl.BlockSpec((TM_MOE, HALF_D), lambda i, tg, tv: (i, 0)),
            scratch_shapes=[pltpu.VMEM((kdim, D_MODEL), BF16)],
        ),
        out_shape=jax.ShapeDtypeStruct((P_ROWS, HALF_D), U32),
        compiler_params=_params(1),
        name="moe_down",
    )(tile_group, tile_valid, a, w_down_grouped)


ROW_UNROLL = 8


def _moe_scatter_kernel(pos_ref, h_ref, row_ref, xs_hbm, gs_hbm, xs_vm, gs_vm, sem):
    i = pl.program_id(0)

    @pl.when(i == 0)
    def _():
        def zero(c, carry):
            r0 = pl.multiple_of(c * TM_MOE, TM_MOE)
            xs_vm[pl.ds(r0, TM_MOE), :] = jnp.zeros((TM_MOE, HALF_D), U32)
            gs_vm[pl.ds(r0, TM_MOE), :] = jnp.zeros((TM_MOE, LANES), F32)
            return carry

        lax.fori_loop(0, N_MOE_TILES, zero, 0)

    base = i * TM

    def move(r, carry):
        p = pos_ref[base + r]
        xs_vm[pl.ds(p, 1), :] = h_ref[pl.ds(r, 1), :]
        gs_vm[pl.ds(p, 1), :] = row_ref[pl.ds(r, 1), :]
        return carry

    lax.fori_loop(0, TM, move, 0, unroll=ROW_UNROLL)

    @pl.when(i == pl.num_programs(0) - 1)
    def _():
        copy_x = pltpu.make_async_copy(xs_vm, xs_hbm, sem.at[0])
        copy_g = pltpu.make_async_copy(gs_vm, gs_hbm, sem.at[1])
        copy_x.start()
        copy_g.start()
        copy_x.wait()
        copy_g.wait()


def _moe_scatter(pos, h_packed, rowinfo):
    return pl.pallas_call(
        _moe_scatter_kernel,
        grid_spec=pltpu.PrefetchScalarGridSpec(
            num_scalar_prefetch=1,
            grid=(T_ROWS // TM,),
            in_specs=[
                pl.BlockSpec((TM, HALF_D), lambda i, pos: (i, 0)),
                pl.BlockSpec((TM, LANES), lambda i, pos: (i, 0)),
            ],
            out_specs=[pl.BlockSpec(memory_space=pl.ANY), pl.BlockSpec(memory_space=pl.ANY)],
            scratch_shapes=[
                pltpu.VMEM((P_ROWS, HALF_D), U32),
                pltpu.VMEM((P_ROWS, LANES), F32),
                pltpu.SemaphoreType.DMA((2,)),
            ],
        ),
        out_shape=[
            jax.ShapeDtypeStruct((P_ROWS, HALF_D), U32),
            jax.ShapeDtypeStruct((P_ROWS, LANES), F32),
        ],
        compiler_params=_params(1),
        name="moe_scatter",
    )(pos, h_packed, rowinfo)


def _moe_combine_kernel(pos_ref, ys_ref, x_ref, g_ref, o_ref, yt_ref):
    base = pl.program_id(0) * TM

    def move(r, carry):
        p = pos_ref[base + r]
        yt_ref[pl.ds(r, 1), :] = ys_ref[pl.ds(p, 1), :]
        return carry

    lax.fori_loop(0, TM, move, 0, unroll=ROW_UNROLL)
    o_ref[...] = x_ref[...] + g_ref[0] * _unpack_rows(yt_ref[...])


def _moe_combine(pos, ys, x, gate):
    return pl.pallas_call(
        _moe_combine_kernel,
        grid_spec=pltpu.PrefetchScalarGridSpec(
            num_scalar_prefetch=1,
            grid=(T_ROWS // TM,),
            in_specs=[
                pl.BlockSpec((P_ROWS, HALF_D), lambda i, pos: (0, 0), pipeline_mode=pl.Buffered(1)),
                pl.BlockSpec((TM, D_MODEL), lambda i, pos: (i, 0)),
                pl.BlockSpec((1, 1, D_MODEL), lambda i, pos: (_cond_row(i), 0, 0)),
            ],
            out_specs=pl.BlockSpec((TM, D_MODEL), lambda i, pos: (i, 0)),
            scratch_shapes=[pltpu.VMEM((TM, HALF_D), U32)],
        ),
        out_shape=jax.ShapeDtypeStruct((T_ROWS, D_MODEL), F32),
        compiler_params=_params(1),
        name="moe_combine",
    )(pos, ys, x, gate)


def _moe_layer(x, gain, shift, scale, gate, router_w, w_up, w_down):
    wr_hi, wr_lo, bias_col = router_w
    h, rowinfo, info, cnt = _router(x, gain, shift, scale, wr_hi, wr_lo, bias_col)
    grp = info[4].astype(jnp.int32)
    rank = info[5].astype(jnp.int32)
    counts = cnt[0:N_GROUPS, 0].astype(jnp.int32)
    padded = ((counts + TM_MOE - 1) // TM_MOE) * TM_MOE
    ends = jnp.cumsum(padded)
    pos = (ends - padded)[grp] + rank
    tile_start = jnp.arange(N_MOE_TILES, dtype=jnp.int32) * TM_MOE
    tile_group = jnp.minimum(jnp.sum(tile_start[:, None] >= ends[None, :], axis=1), N_GROUPS - 1).astype(jnp.int32)
    tile_valid = (tile_start < ends[N_GROUPS - 1]).astype(jnp.int32)
    xs, gsorted = _moe_scatter(pos, h, rowinfo)
    a = _moe_up(tile_group, tile_valid, xs, gsorted, w_up)
    w_down_grouped = w_down.reshape(N_GROUPS, EXPERTS_PER_GROUP * D_EXPERT, D_MODEL)
    ys = _moe_down(tile_group, tile_valid, a, w_down_grouped)
    return _moe_combine(pos, ys, x, gate)


def kernel(x_prompt, x_sample, state_ret, cache_k, cache_v, c, c_ctx, w_mod, b_mod, norm_mix,
           norm_moe, ret_w_in, ret_decay, ret_w_out, att_w_in, att_q_norm, att_k_norm, att_sink,
           att_w_out, w_router, router_bias, moe_w_up, moe_w_down):
    x = jnp.concatenate([x_prompt.reshape(CTX_ROWS, D_MODEL), x_sample.reshape(LAT_ROWS, D_MODEL)], axis=0)
    cond8 = jnp.zeros((8, D_MODEL), F32).at[0].set(c_ctx).at[1:N_COND].set(c)
    mods = _modulation_all(cond8, w_mod, b_mod)
    mods = mods[:, :N_COND].reshape(DEPTH, N_COND, N_MOD, 1, D_MODEL)

    ret_cos, ret_sin = _rope_tables(RET_DK)
    att_cos, att_sin = _rope_tables(HEAD_DIM)

    perm = jnp.arange(N_EXPERTS).reshape(N_GROUPS, EXPERTS_PER_GROUP).T.reshape(-1)
    wr = jnp.zeros((D_MODEL, LANES), F32).at[:, :N_EXPERTS].set(w_router[:, perm])
    wr_hi = wr.astype(BF16)
    wr_lo = (wr - wr_hi.astype(F32)).astype(BF16)
    bias_col = router_bias[perm].astype(F32).reshape(N_EXPERTS, 1)
    router_w = (wr_hi, wr_lo, bias_col)

    new_ret, new_k, new_v = [], [], []
    for layer in range(DEPTH):
        sh_a, sc_a, g_a, sh_m, sc_m, g_m = [mods[layer, :, t] for t in range(N_MOD)]
        gain_mix = norm_mix[layer].reshape(1, D_MODEL)
        gain_moe = norm_moe[layer].reshape(1, D_MODEL)
        j = layer // 2
        if layer % 2 == 0:
            proj = _ret_proj(x, gain_mix, sh_a, sc_a, ret_w_in[j], ret_cos, ret_sin)
            o_ctx, st = _ret_core(proj, ret_decay[j], state_ret, j, latent=False)
            (o_lat,) = _ret_core(proj, ret_decay[j], state_ret, j, latent=True)
            new_ret.append(st)
            x = _out_proj(o_ctx, o_lat, ret_w_out[j], x, g_a)
        else:
            q_gain = jnp.tile(att_q_norm[j], LANES // HEAD_DIM).reshape(1, LANES)
            k_gain = jnp.tile(att_k_norm[j], LANES // HEAD_DIM).reshape(1, LANES)
            qkv, kv32 = _att_proj(x, gain_mix, sh_a, sc_a, att_w_in[j], q_gain, k_gain, att_cos, att_sin)
            kv = kv32[:CTX_ROWS].reshape(BATCH, SEQ, 2, ATT_KV_HEADS, HEAD_DIM)
            new_k.append(jnp.transpose(kv[:, :, 0], (0, 2, 1, 3)))
            new_v.append(jnp.transpose(kv[:, :, 1], (0, 2, 1, 3)))
            sink = att_sink[j].astype(F32)
            o_ctx = _ctx_attention(qkv, sink)
            o_lat = _lat_attention(qkv, sink, cache_k, cache_v, j)
            x = _out_proj(o_ctx, o_lat, att_w_out[j], x, g_a)
        x = _moe_layer(x, gain_moe, sh_m, sc_m, g_m, router_w, moe_w_up[layer], moe_w_down[layer])

    y_prompt = x[:CTX_ROWS].reshape(BATCH, SEQ, D_MODEL)
    y_sample = x[CTX_ROWS:].reshape(DEC_BATCH, DEC_SEQ, D_MODEL)
    return (y_prompt, y_sample, jnp.stack(new_ret, axis=1), jnp.stack(new_k, axis=1), jnp.stack(new_v, axis=1))
```

```python
import functools

import jax
import jax.numpy as jnp
from jax import lax
from jax.experimental import pallas as pl
from jax.experimental.pallas import tpu as pltpu

F32 = jnp.float32
BF16 = jnp.bfloat16

D_MODEL = 1024
BATCH = 16
SEQ = 256
DEPTH = 4
DEC_BATCH = 2
DEC_SEQ = 4096
PAST_LEN = 512
GRID_W = 64
N_MOD = 6
EPS = 1e-6
ROPE_BASE = 10000.0
RET_HEADS = 8
RET_DK = 128
RET_DV = 256
RET_CHUNK = 128
RET_IN = 2 * RET_HEADS * RET_DK + 2 * RET_HEADS * RET_DV
ATT_HEADS = 16
ATT_KV_HEADS = 4
ATT_GROUP = 4
HEAD_DIM = 64
WINDOW = 128
ATT_BLOCK = 128
ATT_IN = (ATT_HEADS + 2 * ATT_KV_HEADS) * HEAD_DIM
ATT_SCALE = HEAD_DIM ** -0.5
N_EXPERTS = 16
N_GROUPS = 4
EXPERTS_PER_GROUP = 4
D_EXPERT = 512

CTX_ROWS = BATCH * SEQ
LAT_ROWS = DEC_BATCH * DEC_SEQ
T_ROWS = CTX_ROWS + LAT_ROWS
N_COND = 1 + DEC_BATCH

LANES = 128
TM = 512
N_CTX_TILES = CTX_ROWS // TM
N_LAT_TILES = DEC_SEQ // TM
TM_MOE = 512
N_MOE_TILES = T_ROWS // TM_MOE + N_GROUPS
P_ROWS = N_MOE_TILES * TM_MOE
VMEM_LIMIT = 52 * 1024 * 1024

NT_DIMS = (((1,), (1,)), ((), ()))


def _params(n_axes, vmem=VMEM_LIMIT):
    return pltpu.CompilerParams(dimension_semantics=("arbitrary",) * n_axes, vmem_limit_bytes=vmem)


def _cond_row(i):
    return (i * TM) // DEC_SEQ


def _rope_row(i):
    return jnp.where(i < N_CTX_TILES, 0, 1 + (i - N_CTX_TILES) % N_LAT_TILES)


def _norm_mod(x, gain, shift, scale):
    ms = jnp.mean(x * x, axis=-1, keepdims=True)
    y = x * lax.rsqrt(ms + EPS) * gain
    return y * (1.0 + scale) + shift


def _silu(x):
    return x * jax.nn.sigmoid(x)


def _mod_kernel(c_ref, w_ref, b_ref, o_ref):
    c = c_ref[...]
    s = _silu(c).astype(BF16)
    o_ref[0] = jnp.dot(s, w_ref[0].astype(BF16), preferred_element_type=F32) + b_ref[0]


def _modulation_all(cond8, w_mod, b_mod):
    tn = 1536
    n = N_MOD * D_MODEL
    return pl.pallas_call(
        _mod_kernel,
        grid=(DEPTH, n // tn),
        in_specs=[
            pl.BlockSpec((8, D_MODEL), lambda l, j: (0, 0)),
            pl.BlockSpec((1, D_MODEL, tn), lambda l, j: (l, 0, j)),
            pl.BlockSpec((1, 1, tn), lambda l, j: (l, 0, j)),
        ],
        out_specs=pl.BlockSpec((1, 8, tn), lambda l, j: (l, 0, j)),
        out_shape=jax.ShapeDtypeStruct((DEPTH, 8, n), F32),
        compiler_params=_params(2),
        name="modulation",
    )(cond8, w_mod, b_mod.reshape(DEPTH, 1, n))


def _rope_tables(head_dim):
    half = head_dim // 2
    quarter = half // 2
    t = jnp.arange(DEC_SEQ)
    row = (t // GRID_W).astype(F32)
    col = (t % GRID_W).astype(F32)
    inv_freq = ROPE_BASE ** (-jnp.arange(quarter, dtype=F32) / quarter)
    lane = jnp.arange(LANES)
    d = lane % head_dim
    w = d % half
    f = w % quarter
    pos = jnp.where((d // half)[None, :] == 0, row[:, None], col[:, None])
    ang = pos * inv_freq[f][None, :]
    cos = jnp.cos(ang)
    sin = jnp.where((w < quarter)[None, :], -jnp.sin(ang), jnp.sin(ang))
    cos = jnp.concatenate([jnp.ones((TM, LANES), F32), cos], axis=0)
    sin = jnp.concatenate([jnp.zeros((TM, LANES), F32), sin], axis=0)
    return cos, sin


def _rope_chunk(xc, cos, sin, quarter):
    lane = lax.broadcasted_iota(jnp.int32, xc.shape, 1)
    first = (lane % (2 * quarter)) < quarter
    partner = jnp.where(first, pltpu.roll(xc, LANES - quarter, 1), pltpu.roll(xc, quarter, 1))
    return xc * cos + partner * sin


RET_TN = 1024
RET_Q_TILES = RET_HEADS * RET_DK // RET_TN
RET_QK_TILES = 2 * RET_Q_TILES


def _ret_proj_kernel(x_ref, gain_ref, sh_ref, sc_ref, w_ref, cos_ref, sin_ref, o_ref, wbf_ref):
    j = pl.program_id(0)
    i = pl.program_id(1)

    @pl.when(i == 0)
    def _():
        wbf_ref[...] = w_ref[0].astype(BF16)

    h = _norm_mod(x_ref[...], gain_ref[...], sh_ref[0], sc_ref[0]).astype(BF16)
    acc = jnp.dot(h, wbf_ref[...], preferred_element_type=F32)

    @pl.when(j < RET_QK_TILES)
    def _():
        scale = jnp.where(j < RET_Q_TILES, 1.0, RET_DK ** -0.5).astype(F32)
        cos = cos_ref[...]
        sin = sin_ref[...]
        for c in range(RET_TN // LANES):
            xc = acc[:, c * LANES:(c + 1) * LANES]
            o_ref[:, c * LANES:(c + 1) * LANES] = (_rope_chunk(xc, cos, sin, RET_DK // 4) * scale).astype(BF16)

    @pl.when(j >= RET_QK_TILES)
    def _():
        o_ref[...] = acc.astype(BF16)


def _ret_proj(x, gain, shift, scale, w_in, layer_j, cos, sin):
    n_j = RET_IN // RET_TN
    return pl.pallas_call(
        _ret_proj_kernel,
        grid=(n_j, T_ROWS // TM),
        in_specs=[
            pl.BlockSpec((TM, D_MODEL), lambda j, i: (i, 0)),
            pl.BlockSpec((1, D_MODEL), lambda j, i: (0, 0)),
            pl.BlockSpec((1, 1, D_MODEL), lambda j, i: (_cond_row(i), 0, 0)),
            pl.BlockSpec((1, 1, D_MODEL), lambda j, i: (_cond_row(i), 0, 0)),
            pl.BlockSpec((1, D_MODEL, RET_TN), lambda j, i: (layer_j, 0, j)),
            pl.BlockSpec((TM, LANES), lambda j, i: (_rope_row(i), 0)),
            pl.BlockSpec((TM, LANES), lambda j, i: (_rope_row(i), 0)),
        ],
        out_specs=pl.BlockSpec((TM, RET_TN), lambda j, i: (i, j)),
        out_shape=jax.ShapeDtypeStruct((T_ROWS, RET_IN), BF16),
        scratch_shapes=[pltpu.VMEM((D_MODEL, RET_TN), BF16)],
        compiler_params=_params(2),
        name="ret_proj",
    )(x, gain, shift, scale, w_in, cos, sin)


def _log_sigmoid(x):
    return -(jnp.maximum(-x, 0.0) + jnp.log(1.0 + jnp.exp(-jnp.abs(x))))


def _ret_core_kernel(*refs, seq_len, n_seq, has_init, emit_state, n_alias):
    it = iter(refs)
    decay_ref = next(it)
    q_ref = next(it)
    k_ref = next(it)
    v_ref = next(it)
    g_ref = next(it)
    s0_ref = next(it) if has_init else None
    for _ in range(n_alias):
        next(it)
    o_ref = next(it)
    st_ref = next(it) if emit_state else None
    acc_ref = next(it)
    kv_ref = next(it)

    c = RET_CHUNK
    nc = seq_len // c
    n_chunks = n_seq * nc
    unroll = min(RET_UNROLL, n_chunks)
    head = pl.program_id(1)

    lg_all = _log_sigmoid(decay_ref[...])
    pick = lax.broadcasted_iota(jnp.int32, lg_all.shape, 1) == head
    lg = jnp.sum(jnp.where(pick, lg_all, 0.0), axis=1, keepdims=True)
    lg_f = lg[0:1, :]
    lg_b = lg[1:2, :]

    ri = lax.broadcasted_iota(jnp.int32, (c, c), 0).astype(F32)
    ci = lax.broadcasted_iota(jnp.int32, (c, c), 1).astype(F32)
    diff = ri - ci
    decay = (jnp.where(diff >= 0, jnp.exp(jnp.where(diff >= 0, diff, 0.0) * lg_f), 0.0)
             + jnp.where(diff <= 0, jnp.exp(jnp.where(diff <= 0, -diff, 0.0) * lg_b), 0.0))
    pos_col = lax.broadcasted_iota(jnp.int32, (c, 1), 0).astype(F32)
    pos_row = lax.broadcasted_iota(jnp.int32, (1, c), 1).astype(F32)
    xi_f = jnp.exp((pos_col + 1.0) * lg_f)
    xi_b = jnp.exp((c - pos_col) * lg_b)
    zeta_f = jnp.exp((c - 1.0 - pos_row) * lg_f)
    zeta_b = jnp.exp(pos_row * lg_b)
    cd_f = jnp.exp(c * lg_f)
    cd_b = jnp.exp(c * lg_b)

    def intra(n):
        r0 = pl.multiple_of(n * c, c)
        qn = q_ref[pl.ds(r0, c), :]
        kn = k_ref[pl.ds(r0, c), :]
        vn = v_ref[pl.ds(r0, c), :]
        s = lax.dot_general(qn, kn, NT_DIMS, preferred_element_type=F32) * decay
        acc_ref[pl.ds(r0, c), :] = jnp.dot(s.astype(BF16), vn, preferred_element_type=F32)
        kt = kn.astype(F32).T
        kz = jnp.concatenate([(kt * zeta_f).astype(BF16), (kt * zeta_b).astype(BF16)], axis=0)
        kv_ref[n] = jnp.dot(kz, vn, preferred_element_type=F32)

    def cross(n):
        r0 = pl.multiple_of(n * c, c)
        qn = q_ref[pl.ds(r0, c), :].astype(F32)
        qx = jnp.concatenate([(qn * xi_f).astype(BF16), (qn * xi_b).astype(BF16)], axis=1)
        o = acc_ref[pl.ds(r0, c), :] + jnp.dot(qx, kv_ref[n].astype(BF16), preferred_element_type=F32)
        mu = jnp.mean(o, axis=-1, keepdims=True)
        var = jnp.mean(jnp.square(o - mu), axis=-1, keepdims=True)
        on = (o - mu) * lax.rsqrt(var + EPS)
        gate = g_ref[pl.ds(r0, c), :].astype(F32)
        o_ref[pl.ds(r0, c), :] = (_silu(gate) * on).astype(BF16)

    def over_chunks(fn):
        def body(step, carry):
            for u in range(unroll):
                fn(step * unroll + u)
            return carry

        lax.fori_loop(0, n_chunks // unroll, body, 0)

    over_chunks(intra)

    for s in range(n_seq):
        if has_init:
            init_f = s0_ref[s, 0, 0, 0]
            init_b = s0_ref[s, 0, 1, 0]
        else:
            init_f = jnp.zeros((RET_DK, RET_DV), F32)
            init_b = init_f

        def fwd(n, state, s=s):
            kv = kv_ref[s * nc + n, 0:RET_DK, :]
            kv_ref[s * nc + n, 0:RET_DK, :] = state
            return cd_f * state + kv

        def bwd(n, state, s=s):
            m = s * nc + nc - 1 - n
            kv = kv_ref[m, RET_DK:, :]
            kv_ref[m, RET_DK:, :] = state
            return cd_b * state + kv

        final_f = lax.fori_loop(0, nc, fwd, init_f)
        final_b = lax.fori_loop(0, nc, bwd, init_b)
        if emit_state:
            st_ref[s, 0, 0, 0] = final_f
            st_ref[s, 0, 1, 0] = final_b

    over_chunks(cross)


RET_UNROLL = 4
RET_CTX_SEQS = 4


def _ret_core(proj, ret_decay_j, state_ret, layer_j, *, latent, new_state=None):
    if latent:
        nb, seq_len, n_seq, row0 = DEC_BATCH, DEC_SEQ, 1, CTX_ROWS // DEC_SEQ
    else:
        nb, seq_len, n_seq, row0 = BATCH // RET_CTX_SEQS, SEQ, RET_CTX_SEQS, 0
    rows = n_seq * seq_len
    kcol = RET_HEADS * RET_DK // RET_DK
    vcol = 2 * RET_HEADS * RET_DK // RET_DV
    gcol = vcol + RET_HEADS
    in_specs = [
        pl.BlockSpec((2, RET_HEADS), lambda b, h: (0, 0)),
        pl.BlockSpec((rows, RET_DK), lambda b, h: (row0 + b, h)),
        pl.BlockSpec((rows, RET_DK), lambda b, h: (row0 + b, kcol + h)),
        pl.BlockSpec((rows, RET_DV), lambda b, h: (row0 + b, vcol + h)),
        pl.BlockSpec((rows, RET_DV), lambda b, h: (row0 + b, gcol + h)),
    ]
    args = [ret_decay_j, proj, proj, proj, proj]
    if latent:
        in_specs.append(pl.BlockSpec((1, 1, 2, 1, RET_DK, RET_DV), lambda b, h: (b, layer_j, 0, h, 0, 0)))
        args.append(state_ret)
    aliases = {}
    if new_state is not None:
        aliases = {len(args): 1}
        in_specs.append(pl.BlockSpec(memory_space=pl.ANY))
        args.append(new_state)
    out_specs = [pl.BlockSpec((rows, RET_DV), lambda b, h: (b, h))]
    out_shape = [jax.ShapeDtypeStruct((nb * rows, RET_HEADS * RET_DV), BF16)]
    if not latent:
        out_specs.append(pl.BlockSpec((n_seq, 1, 2, 1, RET_DK, RET_DV), lambda b, h: (b, layer_j, 0, h, 0, 0)))
        out_shape.append(jax.ShapeDtypeStruct((BATCH, (DEPTH + 1) // 2, 2, RET_HEADS, RET_DK, RET_DV), F32))
    return pl.pallas_call(
        functools.partial(_ret_core_kernel, seq_len=seq_len, n_seq=n_seq, has_init=latent,
                          emit_state=not latent, n_alias=len(aliases)),
        grid=(nb, RET_HEADS),
        in_specs=in_specs,
        out_specs=out_specs,
        out_shape=out_shape,
        input_output_aliases=aliases,
        scratch_shapes=[
            pltpu.VMEM((rows, RET_DV), F32),
            pltpu.VMEM((rows // RET_CHUNK, 2 * RET_DK, RET_DV), F32),
        ],
        compiler_params=_params(2),
        name="ret_core_lat" if latent else "ret_core_ctx",
    )(*args)


def _out_proj_kernel(ac_ref, al_ref, w_ref, x_ref, g_ref, o_ref, wbf_ref):
    i = pl.program_id(0)

    @pl.when(i == 0)
    def _():
        wbf_ref[...] = w_ref[0].astype(BF16)

    def emit(a_ref):
        y = jnp.dot(a_ref[...], wbf_ref[...], preferred_element_type=F32)
        o_ref[...] = x_ref[...] + g_ref[0] * y

    @pl.when(i < N_CTX_TILES)
    def _():
        emit(ac_ref)

    @pl.when(i >= N_CTX_TILES)
    def _():
        emit(al_ref)


def _out_proj(a_ctx, a_lat, w_out, layer_j, x, gate):
    k = a_ctx.shape[1]
    return pl.pallas_call(
        _out_proj_kernel,
        grid=(T_ROWS // TM,),
        in_specs=[
            pl.BlockSpec((TM, k), lambda i: (jnp.minimum(i, N_CTX_TILES - 1), 0)),
            pl.BlockSpec((TM, k), lambda i: (jnp.maximum(i - N_CTX_TILES, 0), 0)),
            pl.BlockSpec((1, k, D_MODEL), lambda i: (layer_j, 0, 0)),
            pl.BlockSpec((TM, D_MODEL), lambda i: (i, 0)),
            pl.BlockSpec((1, 1, D_MODEL), lambda i: (_cond_row(i), 0, 0)),
        ],
        out_specs=pl.BlockSpec((TM, D_MODEL), lambda i: (i, 0)),
        out_shape=jax.ShapeDtypeStruct((T_ROWS, D_MODEL), F32),
        scratch_shapes=[pltpu.VMEM((k, D_MODEL), BF16)],
        compiler_params=_params(1),
        name="out_proj",
    )(a_ctx, a_lat, w_out, x, gate)


ATT_Q_W = ATT_HEADS * HEAD_DIM
ATT_KV_W = ATT_KV_HEADS * HEAD_DIM


def _group_mean_sq(xc, gmat):
    sq = xc * xc
    hi = sq.astype(BF16)
    lo = (sq - hi.astype(F32)).astype(BF16)
    return jnp.dot(hi, gmat, preferred_element_type=F32) + jnp.dot(lo, gmat, preferred_element_type=F32)


SEQS_PER_TILE = TM // SEQ


def _att_proj_kernel(*refs, n_alias):
    (x_ref, gain_ref, sh_ref, sc_ref, w_ref, qg_ref, kg_ref, cos_ref, sin_ref) = refs[:9]
    o_ref, kc_ref, vc_ref, wbf_ref = refs[9 + n_alias:]
    i = pl.program_id(0)

    @pl.when(i == 0)
    def _():
        wbf_ref[...] = w_ref[0].astype(BF16)

    h = _norm_mod(x_ref[...], gain_ref[...], sh_ref[0], sc_ref[0]).astype(BF16)
    acc = jnp.dot(h, wbf_ref[...], preferred_element_type=F32)

    def to_cache(cache_ref, chunk, first_head):
        @pl.when(i < N_CTX_TILES)
        def _():
            for s in range(SEQS_PER_TILE):
                for hh in range(LANES // HEAD_DIM):
                    cache_ref[s, 0, first_head + hh] = chunk[s * SEQ:(s + 1) * SEQ, hh * HEAD_DIM:(hh + 1) * HEAD_DIM]

    r = lax.broadcasted_iota(jnp.int32, (LANES, LANES), 0) // HEAD_DIM
    c = lax.broadcasted_iota(jnp.int32, (LANES, LANES), 1) // HEAD_DIM
    gmat = jnp.where(r == c, 1.0 / HEAD_DIM, 0.0).astype(BF16)
    cos = cos_ref[...]
    sin = sin_ref[...]
    n_q = ATT_Q_W // LANES
    n_kv = ATT_KV_W // LANES
    heads_per_chunk = LANES // HEAD_DIM
    for cidx in range(n_q + n_kv):
        xc = acc[:, cidx * LANES:(cidx + 1) * LANES]
        gain = qg_ref[...] if cidx < n_q else kg_ref[...]
        normed = xc * lax.rsqrt(_group_mean_sq(xc, gmat) + EPS) * gain
        if cidx >= n_q:
            to_cache(kc_ref, normed, (cidx - n_q) * heads_per_chunk)
        o_ref[:, cidx * LANES:(cidx + 1) * LANES] = _rope_chunk(normed, cos, sin, HEAD_DIM // 4).astype(BF16)
    for vidx in range(n_kv):
        v = acc[:, ATT_Q_W + ATT_KV_W + vidx * LANES:ATT_Q_W + ATT_KV_W + (vidx + 1) * LANES]
        to_cache(vc_ref, v, vidx * heads_per_chunk)
    o_ref[:, ATT_Q_W + ATT_KV_W:] = acc[:, ATT_Q_W + ATT_KV_W:].astype(BF16)


def _att_proj(x, gain, shift, scale, w_in, layer_j, q_gain, k_gain, cos, sin, caches):
    n_alias = 0 if caches is None else 2
    cache_shape = (BATCH, DEPTH // 2, ATT_KV_HEADS, SEQ, HEAD_DIM)
    cache_spec = pl.BlockSpec((SEQS_PER_TILE, 1, ATT_KV_HEADS, SEQ, HEAD_DIM),
                              lambda i: (jnp.minimum(i, N_CTX_TILES - 1), layer_j, 0, 0, 0))
    in_specs = [
        pl.BlockSpec((TM, D_MODEL), lambda i: (i, 0)),
        pl.BlockSpec((1, D_MODEL), lambda i: (0, 0)),
        pl.BlockSpec((1, 1, D_MODEL), lambda i: (_cond_row(i), 0, 0)),
        pl.BlockSpec((1, 1, D_MODEL), lambda i: (_cond_row(i), 0, 0)),
        pl.BlockSpec((1, D_MODEL, ATT_IN), lambda i: (layer_j, 0, 0)),
        pl.BlockSpec((1, LANES), lambda i: (0, 0)),
        pl.BlockSpec((1, LANES), lambda i: (0, 0)),
        pl.BlockSpec((TM, LANES), lambda i: (_rope_row(i), 0)),
        pl.BlockSpec((TM, LANES), lambda i: (_rope_row(i), 0)),
    ] + [pl.BlockSpec(memory_space=pl.ANY)] * n_alias
    args = [x, gain, shift, scale, w_in, q_gain, k_gain, cos, sin] + ([] if caches is None else list(caches))
    return pl.pallas_call(
        functools.partial(_att_proj_kernel, n_alias=n_alias),
        grid=(T_ROWS // TM,),
        in_specs=in_specs,
        out_specs=[pl.BlockSpec((TM, ATT_IN), lambda i: (i, 0)), cache_spec, cache_spec],
        out_shape=[
            jax.ShapeDtypeStruct((T_ROWS, ATT_IN), BF16),
            jax.ShapeDtypeStruct(cache_shape, F32),
            jax.ShapeDtypeStruct(cache_shape, F32),
        ],
        input_output_aliases={9: 1, 10: 2} if caches is not None else {},
        scratch_shapes=[pltpu.VMEM((D_MODEL, ATT_IN), BF16)],
        compiler_params=_params(1),
        name="att_proj",
    )(*args)


SINK_ROWS = 16
TN_DIMS = (((0,), (0,)), ((), ()))


def _ones_column(n):
    lane = lax.broadcasted_iota(jnp.int32, (n, HEAD_DIM), 1)
    return jnp.where(lane == 0, 1.0, 0.0).astype(BF16)


def _sink_softmax_pv(qs, sink_row, blocks):
    r = qs.shape[0]
    scores = []
    values = []
    for k, v, bias in blocks:
        s = lax.dot_general(k, qs, NT_DIMS, preferred_element_type=F32)
        scores.append(s if bias is None else s + bias)
        values.append(jnp.concatenate([v, _ones_column(v.shape[0])], axis=1))
    row = lax.broadcasted_iota(jnp.int32, (SINK_ROWS, r), 0)
    scores.append(jnp.where(row == 0, sink_row, -jnp.inf))
    values.append(jnp.concatenate([jnp.zeros((SINK_ROWS, HEAD_DIM), BF16), _ones_column(SINK_ROWS)], axis=1))
    st = jnp.concatenate(scores, axis=0)
    m = jnp.max(st, axis=0, keepdims=True)
    pt = jnp.exp(st - m).astype(BF16)
    ov = lax.dot_general(pt, jnp.concatenate(values, axis=0), TN_DIMS, preferred_element_type=F32)
    return ov[:, :HEAD_DIM] / ov[:, HEAD_DIM:HEAD_DIM + 1]


def _stack_heads(q, kvh, rows):
    parts = []
    for g in range(ATT_GROUP):
        hd = kvh * ATT_GROUP + g
        parts.append(q[:, hd * HEAD_DIM:(hd + 1) * HEAD_DIM])
    return jnp.concatenate(parts, axis=0) * jnp.asarray(ATT_SCALE, BF16)


def _sink_row(sink_ref, kvh, rows):
    head = lax.broadcasted_iota(jnp.int32, (1, ATT_GROUP * rows), 1) // rows
    out = jnp.full((1, ATT_GROUP * rows), sink_ref[kvh * ATT_GROUP], F32)
    for g in range(1, ATT_GROUP):
        out = jnp.where(head == g, sink_ref[kvh * ATT_GROUP + g], out)
    return out


def _store_heads(o_ref, o, kvh, rows):
    for g in range(ATT_GROUP):
        hd = kvh * ATT_GROUP + g
        o_ref[:, hd * HEAD_DIM:(hd + 1) * HEAD_DIM] = o[g * rows:(g + 1) * rows, :].astype(BF16)


def _ctx_att_kernel(sink_ref, q_ref, k_ref, v_ref, o_ref):
    q = q_ref[...]
    for kvh in range(ATT_KV_HEADS):
        qs = _stack_heads(q, kvh, SEQ)
        k = k_ref[:, kvh * HEAD_DIM:(kvh + 1) * HEAD_DIM]
        v = v_ref[:, kvh * HEAD_DIM:(kvh + 1) * HEAD_DIM]
        o = _sink_softmax_pv(qs, _sink_row(sink_ref, kvh, SEQ), [(k, v, None)])
        _store_heads(o_ref, o, kvh, SEQ)


def _lat_att_kernel(sink_ref, q_ref, k_ref, v_ref, ck_ref, cv_ref, o_ref):
    n = pl.program_id(1)
    nb = DEC_SEQ // ATT_BLOCK
    blk = ATT_BLOCK
    rows = ATT_GROUP * blk
    prev0 = pl.multiple_of(jnp.maximum(n - 1, 0) * blk, blk)
    cur0 = pl.multiple_of(n * blk, blk)
    next0 = pl.multiple_of(jnp.minimum(n + 1, nb - 1) * blk, blk)

    kj = lax.broadcasted_iota(jnp.int32, (3 * blk, rows), 0) - blk
    qi = lax.broadcasted_iota(jnp.int32, (3 * blk, rows), 1) % blk
    kabs = n * blk + kj
    valid = (jnp.abs(qi - kj) <= WINDOW) & (kabs >= 0) & (kabs < DEC_SEQ)
    bias = jnp.where(valid, 0.0, -jnp.inf)

    q = q_ref[...]
    for kvh in range(ATT_KV_HEADS):
        cols = slice(kvh * HEAD_DIM, (kvh + 1) * HEAD_DIM)
        qs = _stack_heads(q, kvh, blk)
        k_loc = jnp.concatenate([k_ref[pl.ds(prev0, blk), cols], k_ref[pl.ds(cur0, blk), cols],
                                 k_ref[pl.ds(next0, blk), cols]], axis=0)
        v_loc = jnp.concatenate([v_ref[pl.ds(prev0, blk), cols], v_ref[pl.ds(cur0, blk), cols],
                                 v_ref[pl.ds(next0, blk), cols]], axis=0)
        k_ctx = ck_ref[0, 0, kvh].astype(BF16)
        v_ctx = cv_ref[0, 0, kvh].astype(BF16)
        o = _sink_softmax_pv(qs, _sink_row(sink_ref, kvh, blk),
                             [(k_loc, v_loc, bias), (k_ctx, v_ctx, None)])
        _store_heads(o_ref, o, kvh, blk)


def _ctx_attention(qkv, sink):
    kcol = ATT_Q_W // ATT_KV_W
    return pl.pallas_call(
        _ctx_att_kernel,
        grid_spec=pltpu.PrefetchScalarGridSpec(
            num_scalar_prefetch=1,
            grid=(BATCH,),
            in_specs=[
                pl.BlockSpec((SEQ, ATT_Q_W), lambda b, s: (b, 0)),
                pl.BlockSpec((SEQ, ATT_KV_W), lambda b, s: (b, kcol)),
                pl.BlockSpec((SEQ, ATT_KV_W), lambda b, s: (b, kcol + 1)),
            ],
            out_specs=pl.BlockSpec((SEQ, ATT_Q_W), lambda b, s: (b, 0)),
        ),
        out_shape=jax.ShapeDtypeStruct((CTX_ROWS, ATT_Q_W), BF16),
        compiler_params=_params(1),
        name="ctx_attention",
    )(sink, qkv, qkv, qkv)


def _lat_attention(qkv, sink, cache_k, cache_v, layer_j):
    kcol = ATT_Q_W // ATT_KV_W
    nb = DEC_SEQ // ATT_BLOCK
    q0 = CTX_ROWS // ATT_BLOCK
    s0 = CTX_ROWS // DEC_SEQ
    cache_spec = pl.BlockSpec((1, 1, ATT_KV_HEADS, PAST_LEN, HEAD_DIM), lambda b, n, s: (b, layer_j, 0, 0, 0))
    return pl.pallas_call(
        _lat_att_kernel,
        grid_spec=pltpu.PrefetchScalarGridSpec(
            num_scalar_prefetch=1,
            grid=(DEC_BATCH, nb),
            in_specs=[
                pl.BlockSpec((ATT_BLOCK, ATT_Q_W), lambda b, n, s: (q0 + b * nb + n, 0)),
                pl.BlockSpec((DEC_SEQ, ATT_KV_W), lambda b, n, s: (s0 + b, kcol)),
                pl.BlockSpec((DEC_SEQ, ATT_KV_W), lambda b, n, s: (s0 + b, kcol + 1)),
                cache_spec,
                cache_spec,
            ],
            out_specs=pl.BlockSpec((ATT_BLOCK, ATT_Q_W), lambda b, n, s: (b * nb + n, 0)),
        ),
        out_shape=jax.ShapeDtypeStruct((LAT_ROWS, ATT_Q_W), BF16),
        compiler_params=_params(2),
        name="lat_attention",
    )(sink, qkv, qkv, qkv, cache_k, cache_v)


HALF_D = D_MODEL // 2
U32 = jnp.uint32


def _pack_rows(x):
    bits = lax.bitcast_convert_type(x.astype(BF16).astype(F32), U32)
    return (bits[:, :HALF_D] >> 16) | bits[:, HALF_D:]


def _unpack_rows(w):
    lo = lax.bitcast_convert_type(w << 16, F32)
    hi = lax.bitcast_convert_type(w & jnp.uint32(0xFFFF0000), F32)
    return jnp.concatenate([lo, hi], axis=1)


def _router_kernel(x_ref, gain_ref, sh_ref, sc_ref, wrh_ref, wrl_ref, bias_ref,
                   h_ref, row_ref, info_ref, cnt_ref, carry_ref):
    i = pl.program_id(0)

    @pl.when(i == 0)
    def _():
        carry_ref[...] = jnp.zeros_like(carry_ref)

    hf = _norm_mod(x_ref[...], gain_ref[...], sh_ref[0], sc_ref[0])
    hb = hf.astype(BF16)
    h_ref[...] = _pack_rows(hf)
    hl = (hf - hb.astype(F32)).astype(BF16)
    wrh = wrh_ref[...]
    logits = (jnp.dot(hb, wrh, preferred_element_type=F32) + jnp.dot(hl, wrh, preferred_element_type=F32)
              + jnp.dot(hb, wrl_ref[...], preferred_element_type=F32))
    lt = logits.T[0:N_EXPERTS, :]
    scores = jax.nn.sigmoid(lt)
    sel = scores + bias_ref[...]
    ng = N_GROUPS
    xs = [sel[k * ng:(k + 1) * ng, :] for k in range(EXPERTS_PER_GROUP)]
    sc = [scores[k * ng:(k + 1) * ng, :] for k in range(EXPERTS_PER_GROUP)]
    a, b, c, d = xs
    gs = jnp.maximum(jnp.maximum(jnp.maximum(a + b, a + c), jnp.maximum(a + d, b + c)),
                     jnp.maximum(b + d, c + d))
    bv = gs[0:1, :]
    bg = jnp.zeros(bv.shape, jnp.int32)
    for g in range(1, ng):
        better = gs[g:g + 1, :] > bv
        bg = jnp.where(better, g, bg)
        bv = jnp.where(better, gs[g:g + 1, :], bv)
    giota = lax.broadcasted_iota(jnp.int32, (ng, TM), 0)
    onehot = giota == bg
    wk = []
    for k in range(EXPERTS_PER_GROUP):
        rank = jnp.zeros((ng, TM), F32)
        for j in range(EXPERTS_PER_GROUP):
            if j < k:
                rank = rank + (xs[j] >= xs[k]).astype(F32)
            elif j > k:
                rank = rank + (xs[j] > xs[k]).astype(F32)
        chosen = (rank < 2.0) & onehot
        wk.append(jnp.sum(jnp.where(chosen, sc[k], 0.0), axis=0, keepdims=True))
    den = wk[0] + wk[1] + wk[2] + wk[3]
    gates = [w / den for w in wk]

    oh = jnp.where(onehot, 1.0, 0.0)
    r = lax.broadcasted_iota(jnp.int32, (TM, TM), 0)
    cc = lax.broadcasted_iota(jnp.int32, (TM, TM), 1)
    tri = jnp.where(r < cc, 1.0, 0.0).astype(BF16)
    carry = carry_ref[0:ng, 0:1]
    before = jnp.dot(oh.astype(BF16), tri, preferred_element_type=F32) + carry
    rank_in_group = jnp.sum(jnp.where(onehot, before, 0.0), axis=0, keepdims=True)
    new_carry = carry + jnp.sum(oh, axis=1, keepdims=True)
    carry_ref[0:ng, 0:1] = new_carry

    info = jnp.concatenate(gates + [bg.astype(F32), rank_in_group, jnp.zeros((2, TM), F32)], axis=0)
    info_ref[...] = info
    row_ref[...] = jnp.concatenate([info, jnp.zeros((LANES - 8, TM), F32)], axis=0).T
    cnt_ref[...] = jnp.broadcast_to(jnp.concatenate([new_carry, jnp.zeros((8 - ng, 1), F32)], axis=0), (8, LANES))


def _router(x, gain, shift, scale, wr_hi, wr_lo, bias_col):
    return pl.pallas_call(
        _router_kernel,
        grid=(T_ROWS // TM,),
        in_specs=[
            pl.BlockSpec((TM, D_MODEL), lambda i: (i, 0)),
            pl.BlockSpec((1, D_MODEL), lambda i: (0, 0)),
            pl.BlockSpec((1, 1, D_MODEL), lambda i: (_cond_row(i), 0, 0)),
            pl.BlockSpec((1, 1, D_MODEL), lambda i: (_cond_row(i), 0, 0)),
            pl.BlockSpec((D_MODEL, LANES), lambda i: (0, 0)),
            pl.BlockSpec((D_MODEL, LANES), lambda i: (0, 0)),
            pl.BlockSpec((N_EXPERTS, 1), lambda i: (0, 0)),
        ],
        out_specs=[
            pl.BlockSpec((TM, HALF_D), lambda i: (i, 0)),
            pl.BlockSpec((TM, LANES), lambda i: (i, 0)),
            pl.BlockSpec((8, TM), lambda i: (0, i)),
            pl.BlockSpec((8, LANES), lambda i: (0, 0)),
        ],
        out_shape=[
            jax.ShapeDtypeStruct((T_ROWS, HALF_D), U32),
            jax.ShapeDtypeStruct((T_ROWS, LANES), F32),
            jax.ShapeDtypeStruct((8, T_ROWS), F32),
            jax.ShapeDtypeStruct((8, LANES), F32),
        ],
        scratch_shapes=[pltpu.VMEM((8, LANES), F32)],
        compiler_params=_params(1),
        name="moe_router",
    )(x, gain, shift, scale, wr_hi, wr_lo, bias_col)


def _moe_up_kernel(tg_ref, tv_ref, xs_ref, gs_ref, w_ref, o_ref, wbf_ref):
    k = pl.program_id(0)
    i = pl.program_id(1)
    new_w = (i == 0) | (tg_ref[i] != tg_ref[jnp.maximum(i - 1, 0)])

    @pl.when(new_w)
    def _():
        wbf_ref[...] = w_ref[0].astype(BF16)

    @pl.when(tv_ref[i] == 1)
    def _():
        xs = _unpack_rows(xs_ref[...]).astype(BF16)
        hu = jnp.dot(xs, wbf_ref[...], preferred_element_type=F32)
        gs = gs_ref[...]
        lane = lax.broadcasted_iota(jnp.int32, gs.shape, 1)
        gcol = jnp.sum(jnp.where(lane == k, gs, 0.0), axis=1, keepdims=True)
        a = _silu(hu[:, :D_EXPERT]) * hu[:, D_EXPERT:] * gcol
        o_ref[...] = a.astype(BF16)

    @pl.when(tv_ref[i] == 0)
    def _():
        o_ref[...] = jnp.zeros_like(o_ref)


def _moe_up(tile_group, tile_valid, xs, gs, w_up, layer):
    return pl.pallas_call(
        _moe_up_kernel,
        grid_spec=pltpu.PrefetchScalarGridSpec(
            num_scalar_prefetch=2,
            grid=(EXPERTS_PER_GROUP, N_MOE_TILES),
            in_specs=[
                pl.BlockSpec((TM_MOE, HALF_D), lambda k, i, tg, tv: (i, 0)),
                pl.BlockSpec((TM_MOE, LANES), lambda k, i, tg, tv: (i, 0)),
                pl.BlockSpec((1, D_MODEL, 2 * D_EXPERT),
                             lambda k, i, tg, tv: (layer * N_EXPERTS + tg[i] * EXPERTS_PER_GROUP + k, 0, 0)),
            ],
            out_specs=pl.BlockSpec((TM_MOE, D_EXPERT), lambda k, i, tg, tv: (i, k)),
            scratch_shapes=[pltpu.VMEM((D_MODEL, 2 * D_EXPERT), BF16)],
        ),
        out_shape=jax.ShapeDtypeStruct((P_ROWS, EXPERTS_PER_GROUP * D_EXPERT), BF16),
        compiler_params=_params(2),
        name="moe_up",
    )(tile_group, tile_valid, xs, gs, w_up)


def _moe_down_kernel(tg_ref, tv_ref, a_ref, w_ref, o_ref, wbf_ref):
    i = pl.program_id(0)
    new_w = (i == 0) | (tg_ref[i] != tg_ref[jnp.maximum(i - 1, 0)])

    @pl.when(new_w)
    def _():
        wbf_ref[...] = w_ref[0].astype(BF16)

    o_ref[...] = _pack_rows(jnp.dot(a_ref[...], wbf_ref[...], preferred_element_type=F32))


def _moe_down(tile_group, tile_valid, a, w_down_grouped, layer):
    kdim = EXPERTS_PER_GROUP * D_EXPERT
    return pl.pallas_call(
        _moe_down_kernel,
        grid_spec=pltpu.PrefetchScalarGridSpec(
            num_scalar_prefetch=2,
            grid=(N_MOE_TILES,),
            in_specs=[
                pl.BlockSpec((TM_MOE, kdim), lambda i, tg, tv: (i, 0)),
                pl.BlockSpec((1, kdim, D_MODEL), lambda i, tg, tv: (layer * N_GROUPS + tg[i], 0, 0)),
            ],
            out_specs=pl.BlockSpec((TM_MOE, HALF_D), lambda i, tg, tv: (i, 0)),
            scratch_shapes=[pltpu.VMEM((kdim, D_MODEL), BF16)],
        ),
        out_shape=jax.ShapeDtypeStruct((P_ROWS, HALF_D), U32),
        compiler_params=_params(1),
        name="moe_down",
    )(tile_group, tile_valid, a, w_down_grouped)


ROW_UNROLL = 8


def _moe_scatter_kernel(pos_ref, h_ref, row_ref, xs_hbm, gs_hbm, xs_vm, gs_vm, sem):
    i = pl.program_id(0)

    @pl.when(i == 0)
    def _():
        def zero(c, carry):
            r0 = pl.multiple_of(c * TM_MOE, TM_MOE)
            xs_vm[pl.ds(r0, TM_MOE), :] = jnp.zeros((TM_MOE, HALF_D), U32)
            gs_vm[pl.ds(r0, TM_MOE), :] = jnp.zeros((TM_MOE, LANES), F32)
            return carry

        lax.fori_loop(0, N_MOE_TILES, zero, 0)

    base = i * TM

    def move(r, carry):
        p = pos_ref[base + r]
        xs_vm[pl.ds(p, 1), :] = h_ref[pl.ds(r, 1), :]
        gs_vm[pl.ds(p, 1), :] = row_ref[pl.ds(r, 1), :]
        return carry

    lax.fori_loop(0, TM, move, 0, unroll=ROW_UNROLL)

    @pl.when(i == pl.num_programs(0) - 1)
    def _():
        copy_x = pltpu.make_async_copy(xs_vm, xs_hbm, sem.at[0])
        copy_g = pltpu.make_async_copy(gs_vm, gs_hbm, sem.at[1])
        copy_x.start()
        copy_g.start()
        copy_x.wait()
        copy_g.wait()


def _moe_scatter(pos, h_packed, rowinfo):
    return pl.pallas_call(
        _moe_scatter_kernel,
        grid_spec=pltpu.PrefetchScalarGridSpec(
            num_scalar_prefetch=1,
            grid=(T_ROWS // TM,),
            in_specs=[
                pl.BlockSpec((TM, HALF_D), lambda i, pos: (i, 0)),
                pl.BlockSpec((TM, LANES), lambda i, pos: (i, 0)),
            ],
            out_specs=[pl.BlockSpec(memory_space=pl.ANY), pl.BlockSpec(memory_space=pl.ANY)],
            scratch_shapes=[
                pltpu.VMEM((P_ROWS, HALF_D), U32),
                pltpu.VMEM((P_ROWS, LANES), F32),
                pltpu.SemaphoreType.DMA((2,)),
            ],
        ),
        out_shape=[
            jax.ShapeDtypeStruct((P_ROWS, HALF_D), U32),
            jax.ShapeDtypeStruct((P_ROWS, LANES), F32),
        ],
        compiler_params=_params(1),
        name="moe_scatter",
    )(pos, h_packed, rowinfo)


def _moe_combine_kernel(pos_ref, ys_ref, x_ref, g_ref, o_ref, yt_ref):
    base = pl.program_id(0) * TM

    def move(r, carry):
        p = pos_ref[base + r]
        yt_ref[pl.ds(r, 1), :] = ys_ref[pl.ds(p, 1), :]
        return carry

    lax.fori_loop(0, TM, move, 0, unroll=ROW_UNROLL)
    o_ref[...] = x_ref[...] + g_ref[0] * _unpack_rows(yt_ref[...])


def _moe_combine(pos, ys, x, gate):
    return pl.pallas_call(
        _moe_combine_kernel,
        grid_spec=pltpu.PrefetchScalarGridSpec(
            num_scalar_prefetch=1,
            grid=(T_ROWS // TM,),
            in_specs=[
                pl.BlockSpec((P_ROWS, HALF_D), lambda i, pos: (0, 0), pipeline_mode=pl.Buffered(1)),
                pl.BlockSpec((TM, D_MODEL), lambda i, pos: (i, 0)),
                pl.BlockSpec((1, 1, D_MODEL), lambda i, pos: (_cond_row(i), 0, 0)),
            ],
            out_specs=pl.BlockSpec((TM, D_MODEL), lambda i, pos: (i, 0)),
            scratch_shapes=[pltpu.VMEM((TM, HALF_D), U32)],
        ),
        out_shape=jax.ShapeDtypeStruct((T_ROWS, D_MODEL), F32),
        compiler_params=_params(1),
        name="moe_combine",
    )(pos, ys, x, gate)


def _moe_layer(x, gain, shift, scale, gate, router_w, w_up, w_down_grouped, layer):
    wr_hi, wr_lo, bias_col = router_w
    h, rowinfo, info, cnt = _router(x, gain, shift, scale, wr_hi, wr_lo, bias_col)
    grp = info[4].astype(jnp.int32)
    rank = info[5].astype(jnp.int32)
    counts = cnt[0:N_GROUPS, 0].astype(jnp.int32)
    padded = ((counts + TM_MOE - 1) // TM_MOE) * TM_MOE
    ends = jnp.cumsum(padded)
    pos = (ends - padded)[grp] + rank
    tile_start = jnp.arange(N_MOE_TILES, dtype=jnp.int32) * TM_MOE
    tile_group = jnp.minimum(jnp.sum(tile_start[:, None] >= ends[None, :], axis=1), N_GROUPS - 1).astype(jnp.int32)
    tile_valid = (tile_start < ends[N_GROUPS - 1]).astype(jnp.int32)
    xs, gsorted = _moe_scatter(pos, h, rowinfo)
    a = _moe_up(tile_group, tile_valid, xs, gsorted, w_up, layer)
    ys = _moe_down(tile_group, tile_valid, a, w_down_grouped, layer)
    return _moe_combine(pos, ys, x, gate)


def kernel(x_prompt, x_sample, state_ret, cache_k, cache_v, c, c_ctx, w_mod, b_mod, norm_mix,
           norm_moe, ret_w_in, ret_decay, ret_w_out, att_w_in, att_q_norm, att_k_norm, att_sink,
           att_w_out, w_router, router_bias, moe_w_up, moe_w_down):
    x = jnp.concatenate([x_prompt.reshape(CTX_ROWS, D_MODEL), x_sample.reshape(LAT_ROWS, D_MODEL)], axis=0)
    cond8 = jnp.zeros((8, D_MODEL), F32).at[0].set(c_ctx).at[1:N_COND].set(c)
    mods = _modulation_all(cond8, w_mod, b_mod)
    mods = mods[:, :N_COND].reshape(DEPTH, N_COND, N_MOD, 1, D_MODEL)

    ret_cos, ret_sin = _rope_tables(RET_DK)
    att_cos, att_sin = _rope_tables(HEAD_DIM)

    perm = jnp.arange(N_EXPERTS).reshape(N_GROUPS, EXPERTS_PER_GROUP).T.reshape(-1)
    wr = jnp.zeros((D_MODEL, LANES), F32).at[:, :N_EXPERTS].set(w_router[:, perm])
    wr_hi = wr.astype(BF16)
    wr_lo = (wr - wr_hi.astype(F32)).astype(BF16)
    bias_col = router_bias[perm].astype(F32).reshape(N_EXPERTS, 1)
    router_w = (wr_hi, wr_lo, bias_col)

    w_up_all = moe_w_up.reshape(DEPTH * N_EXPERTS, D_MODEL, 2 * D_EXPERT)
    w_down_all = moe_w_down.reshape(DEPTH * N_GROUPS, EXPERTS_PER_GROUP * D_EXPERT, D_MODEL)
    new_state, new_kv = None, None
    for layer in range(DEPTH):
        sh_a, sc_a, g_a, sh_m, sc_m, g_m = [mods[layer, :, t] for t in range(N_MOD)]
        gain_mix = norm_mix[layer].reshape(1, D_MODEL)
        gain_moe = norm_moe[layer].reshape(1, D_MODEL)
        j = layer // 2
        if layer % 2 == 0:
            proj = _ret_proj(x, gain_mix, sh_a, sc_a, ret_w_in, j, ret_cos, ret_sin)
            o_ctx, new_state = _ret_core(proj, ret_decay[j], state_ret, j, latent=False, new_state=new_state)
            (o_lat,) = _ret_core(proj, ret_decay[j], state_ret, j, latent=True)
            x = _out_proj(o_ctx, o_lat, ret_w_out, j, x, g_a)
        else:
            q_gain = jnp.tile(att_q_norm[j], LANES // HEAD_DIM).reshape(1, LANES)
            k_gain = jnp.tile(att_k_norm[j], LANES // HEAD_DIM).reshape(1, LANES)
            qkv, new_k, new_v = _att_proj(x, gain_mix, sh_a, sc_a, att_w_in, j, q_gain, k_gain,
                                          att_cos, att_sin, new_kv)
            new_kv = (new_k, new_v)
            sink = att_sink[j].astype(F32)
            o_ctx = _ctx_attention(qkv, sink)
            o_lat = _lat_attention(qkv, sink, cache_k, cache_v, j)
            x = _out_proj(o_ctx, o_lat, att_w_out, j, x, g_a)
        x = _moe_layer(x, gain_moe, sh_m, sc_m, g_m, router_w, w_up_all, w_down_all, layer)

    y_prompt = x[:CTX_ROWS].reshape(BATCH, SEQ, D_MODEL)
    y_sample = x[CTX_ROWS:].reshape(DEC_BATCH, DEC_SEQ, D_MODEL)
    return (y_prompt, y_sample, new_state, new_kv[0], new_kv[1])
```

```python
import functools

import jax
import jax.numpy as jnp
from jax import lax
from jax.experimental import pallas as pl
from jax.experimental.pallas import tpu as pltpu

F32 = jnp.float32
BF16 = jnp.bfloat16

D_MODEL = 1024
BATCH = 16
SEQ = 256
DEPTH = 4
DEC_BATCH = 2
DEC_SEQ = 4096
PAST_LEN = 512
GRID_W = 64
N_MOD = 6
EPS = 1e-6
ROPE_BASE = 10000.0
RET_HEADS = 8
RET_DK = 128
RET_DV = 256
RET_CHUNK = 128
RET_IN = 2 * RET_HEADS * RET_DK + 2 * RET_HEADS * RET_DV
ATT_HEADS = 16
ATT_KV_HEADS = 4
ATT_GROUP = 4
HEAD_DIM = 64
WINDOW = 128
ATT_BLOCK = 128
ATT_IN = (ATT_HEADS + 2 * ATT_KV_HEADS) * HEAD_DIM
ATT_SCALE = HEAD_DIM ** -0.5
N_EXPERTS = 16
N_GROUPS = 4
EXPERTS_PER_GROUP = 4
D_EXPERT = 512

CTX_ROWS = BATCH * SEQ
LAT_ROWS = DEC_BATCH * DEC_SEQ
T_ROWS = CTX_ROWS + LAT_ROWS
N_COND = 1 + DEC_BATCH

LANES = 128
TM = 512
N_CTX_TILES = CTX_ROWS // TM
N_LAT_TILES = DEC_SEQ // TM
TM_MOE = 512
N_MOE_TILES = 36
VMEM_LIMIT = 52 * 1024 * 1024

NT_DIMS = (((1,), (1,)), ((), ()))


def _params(n_axes, vmem=VMEM_LIMIT):
    return pltpu.CompilerParams(dimension_semantics=("arbitrary",) * n_axes, vmem_limit_bytes=vmem)


def _cond_row(i):
    return (i * TM) // DEC_SEQ


def _rope_row(i):
    return jnp.where(i < N_CTX_TILES, 0, 1 + (i - N_CTX_TILES) % N_LAT_TILES)


def _norm_mod(x, gain, shift, scale):
    ms = jnp.mean(x * x, axis=-1, keepdims=True)
    y = x * lax.rsqrt(ms + EPS) * gain
    return y * (1.0 + scale) + shift


def _silu(x):
    return x * jax.nn.sigmoid(x)


def _mod_kernel(c_ref, w_ref, b_ref, o_ref):
    c = c_ref[...]
    s = _silu(c).astype(BF16)
    o_ref[0] = jnp.dot(s, w_ref[0].astype(BF16), preferred_element_type=F32) + b_ref[0]


def _modulation_all(cond8, w_mod, b_mod):
    tn = 1536
    n = N_MOD * D_MODEL
    return pl.pallas_call(
        _mod_kernel,
        grid=(DEPTH, n // tn),
        in_specs=[
            pl.BlockSpec((8, D_MODEL), lambda l, j: (0, 0)),
            pl.BlockSpec((1, D_MODEL, tn), lambda l, j: (l, 0, j)),
            pl.BlockSpec((1, 1, tn), lambda l, j: (l, 0, j)),
        ],
        out_specs=pl.BlockSpec((1, 8, tn), lambda l, j: (l, 0, j)),
        out_shape=jax.ShapeDtypeStruct((DEPTH, 8, n), F32),
        compiler_params=_params(2),
        name="modulation",
    )(cond8, w_mod, b_mod.reshape(DEPTH, 1, n))


def _rope_tables(head_dim):
    half = head_dim // 2
    quarter = half // 2
    t = jnp.arange(DEC_SEQ)
    row = (t // GRID_W).astype(F32)
    col = (t % GRID_W).astype(F32)
    inv_freq = ROPE_BASE ** (-jnp.arange(quarter, dtype=F32) / quarter)
    lane = jnp.arange(LANES)
    d = lane % head_dim
    w = d % half
    f = w % quarter
    pos = jnp.where((d // half)[None, :] == 0, row[:, None], col[:, None])
    ang = pos * inv_freq[f][None, :]
    cos = jnp.cos(ang)
    sin = jnp.where((w < quarter)[None, :], -jnp.sin(ang), jnp.sin(ang))
    cos = jnp.concatenate([jnp.ones((TM, LANES), F32), cos], axis=0)
    sin = jnp.concatenate([jnp.zeros((TM, LANES), F32), sin], axis=0)
    return cos, sin


def _rope_chunk(xc, cos, sin, quarter):
    lane = lax.broadcasted_iota(jnp.int32, xc.shape, 1)
    first = (lane % (2 * quarter)) < quarter
    partner = jnp.where(first, pltpu.roll(xc, LANES - quarter, 1), pltpu.roll(xc, quarter, 1))
    return xc * cos + partner * sin


RET_TN = 1024
RET_Q_TILES = RET_HEADS * RET_DK // RET_TN
RET_QK_TILES = 2 * RET_Q_TILES


def _ret_proj_kernel(x_ref, gain_ref, sh_ref, sc_ref, w_ref, cos_ref, sin_ref, o_ref, wbf_ref):
    j = pl.program_id(0)
    i = pl.program_id(1)

    @pl.when(i == 0)
    def _():
        wbf_ref[...] = w_ref[0].astype(BF16)

    h = _norm_mod(x_ref[...], gain_ref[...], sh_ref[0], sc_ref[0]).astype(BF16)
    acc = jnp.dot(h, wbf_ref[...], preferred_element_type=F32)

    @pl.when(j < RET_QK_TILES)
    def _():
        scale = jnp.where(j < RET_Q_TILES, 1.0, RET_DK ** -0.5).astype(F32)
        cos = cos_ref[...]
        sin = sin_ref[...]
        for c in range(RET_TN // LANES):
            xc = acc[:, c * LANES:(c + 1) * LANES]
            o_ref[:, c * LANES:(c + 1) * LANES] = (_rope_chunk(xc, cos, sin, RET_DK // 4) * scale).astype(BF16)

    @pl.when(j >= RET_QK_TILES)
    def _():
        o_ref[...] = acc.astype(BF16)


def _ret_proj(x, gain, shift, scale, w_in, layer_j, cos, sin):
    n_j = RET_IN // RET_TN
    return pl.pallas_call(
        _ret_proj_kernel,
        grid=(n_j, T_ROWS // TM),
        in_specs=[
            pl.BlockSpec((TM, D_MODEL), lambda j, i: (i, 0)),
            pl.BlockSpec((1, D_MODEL), lambda j, i: (0, 0)),
            pl.BlockSpec((1, 1, D_MODEL), lambda j, i: (_cond_row(i), 0, 0)),
            pl.BlockSpec((1, 1, D_MODEL), lambda j, i: (_cond_row(i), 0, 0)),
            pl.BlockSpec((1, D_MODEL, RET_TN), lambda j, i: (layer_j, 0, j)),
            pl.BlockSpec((TM, LANES), lambda j, i: (_rope_row(i), 0)),
            pl.BlockSpec((TM, LANES), lambda j, i: (_rope_row(i), 0)),
        ],
        out_specs=pl.BlockSpec((TM, RET_TN), lambda j, i: (i, j)),
        out_shape=jax.ShapeDtypeStruct((T_ROWS, RET_IN), BF16),
        scratch_shapes=[pltpu.VMEM((D_MODEL, RET_TN), BF16)],
        compiler_params=_params(2),
        name="ret_proj",
    )(x, gain, shift, scale, w_in, cos, sin)


def _log_sigmoid(x):
    return -(jnp.maximum(-x, 0.0) + jnp.log(1.0 + jnp.exp(-jnp.abs(x))))


def _ret_core_kernel(*refs, seq_len, n_seq, has_init, emit_state, n_alias):
    it = iter(refs)
    decay_ref = next(it)
    q_ref = next(it)
    k_ref = next(it)
    v_ref = next(it)
    g_ref = next(it)
    s0_ref = next(it) if has_init else None
    for _ in range(n_alias):
        next(it)
    o_ref = next(it)
    st_ref = next(it) if emit_state else None
    acc_ref = next(it)
    kv_ref = next(it)

    c = RET_CHUNK
    nc = seq_len // c
    n_chunks = n_seq * nc
    unroll = min(RET_UNROLL, n_chunks)
    head = pl.program_id(1)

    lg_all = _log_sigmoid(decay_ref[...])
    pick = lax.broadcasted_iota(jnp.int32, lg_all.shape, 1) == head
    lg = jnp.sum(jnp.where(pick, lg_all, 0.0), axis=1, keepdims=True)
    lg_f = lg[0:1, :]
    lg_b = lg[1:2, :]

    ri = lax.broadcasted_iota(jnp.int32, (c, c), 0).astype(F32)
    ci = lax.broadcasted_iota(jnp.int32, (c, c), 1).astype(F32)
    diff = ri - ci
    decay = (jnp.where(diff >= 0, jnp.exp(jnp.where(diff >= 0, diff, 0.0) * lg_f), 0.0)
             + jnp.where(diff <= 0, jnp.exp(jnp.where(diff <= 0, -diff, 0.0) * lg_b), 0.0))
    pos_col = lax.broadcasted_iota(jnp.int32, (c, 1), 0).astype(F32)
    pos_row = lax.broadcasted_iota(jnp.int32, (1, c), 1).astype(F32)
    xi_f = jnp.exp((pos_col + 1.0) * lg_f)
    xi_b = jnp.exp((c - pos_col) * lg_b)
    zeta_f = jnp.exp((c - 1.0 - pos_row) * lg_f)
    zeta_b = jnp.exp(pos_row * lg_b)
    cd_f = jnp.exp(c * lg_f)
    cd_b = jnp.exp(c * lg_b)

    def intra(n):
        r0 = pl.multiple_of(n * c, c)
        qn = q_ref[pl.ds(r0, c), :]
        kn = k_ref[pl.ds(r0, c), :]
        vn = v_ref[pl.ds(r0, c), :]
        s = lax.dot_general(qn, kn, NT_DIMS, preferred_element_type=F32) * decay
        acc_ref[pl.ds(r0, c), :] = jnp.dot(s.astype(BF16), vn, preferred_element_type=F32)
        kt = kn.astype(F32).T
        kz = jnp.concatenate([(kt * zeta_f).astype(BF16), (kt * zeta_b).astype(BF16)], axis=0)
        kv_ref[n] = jnp.dot(kz, vn, preferred_element_type=F32)

    def cross(n):
        r0 = pl.multiple_of(n * c, c)
        qn = q_ref[pl.ds(r0, c), :].astype(F32)
        qx = jnp.concatenate([(qn * xi_f).astype(BF16), (qn * xi_b).astype(BF16)], axis=1)
        o = acc_ref[pl.ds(r0, c), :] + jnp.dot(qx, kv_ref[n].astype(BF16), preferred_element_type=F32)
        mu = jnp.mean(o, axis=-1, keepdims=True)
        var = jnp.mean(jnp.square(o - mu), axis=-1, keepdims=True)
        on = (o - mu) * lax.rsqrt(var + EPS)
        gate = g_ref[pl.ds(r0, c), :].astype(F32)
        o_ref[pl.ds(r0, c), :] = (_silu(gate) * on).astype(BF16)

    def over_chunks(fn):
        def body(step, carry):
            for u in range(unroll):
                fn(step * unroll + u)
            return carry

        lax.fori_loop(0, n_chunks // unroll, body, 0)

    over_chunks(intra)

    for s in range(n_seq):
        if has_init:
            init_f = s0_ref[s, 0, 0, 0]
            init_b = s0_ref[s, 0, 1, 0]
        else:
            init_f = jnp.zeros((RET_DK, RET_DV), F32)
            init_b = init_f

        def fwd(n, state, s=s):
            kv = kv_ref[s * nc + n, 0:RET_DK, :]
            kv_ref[s * nc + n, 0:RET_DK, :] = state
            return cd_f * state + kv

        def bwd(n, state, s=s):
            m = s * nc + nc - 1 - n
            kv = kv_ref[m, RET_DK:, :]
            kv_ref[m, RET_DK:, :] = state
            return cd_b * state + kv

        final_f = lax.fori_loop(0, nc, fwd, init_f)
        final_b = lax.fori_loop(0, nc, bwd, init_b)
        if emit_state:
            st_ref[s, 0, 0, 0] = final_f
            st_ref[s, 0, 1, 0] = final_b

    over_chunks(cross)


RET_UNROLL = 4
RET_CTX_SEQS = 4


def _ret_core(proj, ret_decay_j, state_ret, layer_j, *, latent, new_state=None):
    if latent:
        nb, seq_len, n_seq, row0 = DEC_BATCH, DEC_SEQ, 1, CTX_ROWS // DEC_SEQ
    else:
        nb, seq_len, n_seq, row0 = BATCH // RET_CTX_SEQS, SEQ, RET_CTX_SEQS, 0
    rows = n_seq * seq_len
    kcol = RET_HEADS * RET_DK // RET_DK
    vcol = 2 * RET_HEADS * RET_DK // RET_DV
    gcol = vcol + RET_HEADS
    in_specs = [
        pl.BlockSpec((2, RET_HEADS), lambda b, h: (0, 0)),
        pl.BlockSpec((rows, RET_DK), lambda b, h: (row0 + b, h)),
        pl.BlockSpec((rows, RET_DK), lambda b, h: (row0 + b, kcol + h)),
        pl.BlockSpec((rows, RET_DV), lambda b, h: (row0 + b, vcol + h)),
        pl.BlockSpec((rows, RET_DV), lambda b, h: (row0 + b, gcol + h)),
    ]
    args = [ret_decay_j, proj, proj, proj, proj]
    if latent:
        in_specs.append(pl.BlockSpec((1, 1, 2, 1, RET_DK, RET_DV), lambda b, h: (b, layer_j, 0, h, 0, 0)))
        args.append(state_ret)
    aliases = {}
    if new_state is not None:
        aliases = {len(args): 1}
        in_specs.append(pl.BlockSpec(memory_space=pl.ANY))
        args.append(new_state)
    out_specs = [pl.BlockSpec((rows, RET_DV), lambda b, h: (b, h))]
    out_shape = [jax.ShapeDtypeStruct((nb * rows, RET_HEADS * RET_DV), BF16)]
    if not latent:
        out_specs.append(pl.BlockSpec((n_seq, 1, 2, 1, RET_DK, RET_DV), lambda b, h: (b, layer_j, 0, h, 0, 0)))
        out_shape.append(jax.ShapeDtypeStruct((BATCH, (DEPTH + 1) // 2, 2, RET_HEADS, RET_DK, RET_DV), F32))
    return pl.pallas_call(
        functools.partial(_ret_core_kernel, seq_len=seq_len, n_seq=n_seq, has_init=latent,
                          emit_state=not latent, n_alias=len(aliases)),
        grid=(nb, RET_HEADS),
        in_specs=in_specs,
        out_specs=out_specs,
        out_shape=out_shape,
        input_output_aliases=aliases,
        scratch_shapes=[
            pltpu.VMEM((rows, RET_DV), F32),
            pltpu.VMEM((rows // RET_CHUNK, 2 * RET_DK, RET_DV), F32),
        ],
        compiler_params=_params(2),
        name="ret_core_lat" if latent else "ret_core_ctx",
    )(*args)


def _out_proj_kernel(ac_ref, al_ref, w_ref, x_ref, g_ref, o_ref, wbf_ref):
    i = pl.program_id(0)

    @pl.when(i == 0)
    def _():
        wbf_ref[...] = w_ref[0].astype(BF16)

    def emit(a_ref):
        y = jnp.dot(a_ref[...], wbf_ref[...], preferred_element_type=F32)
        o_ref[...] = x_ref[...] + g_ref[0] * y

    @pl.when(i < N_CTX_TILES)
    def _():
        emit(ac_ref)

    @pl.when(i >= N_CTX_TILES)
    def _():
        emit(al_ref)


def _out_proj(a_ctx, a_lat, w_out, layer_j, x, gate):
    k = a_ctx.shape[1]
    return pl.pallas_call(
        _out_proj_kernel,
        grid=(T_ROWS // TM,),
        in_specs=[
            pl.BlockSpec((TM, k), lambda i: (jnp.minimum(i, N_CTX_TILES - 1), 0)),
            pl.BlockSpec((TM, k), lambda i: (jnp.maximum(i - N_CTX_TILES, 0), 0)),
            pl.BlockSpec((1, k, D_MODEL), lambda i: (layer_j, 0, 0)),
            pl.BlockSpec((TM, D_MODEL), lambda i: (i, 0)),
            pl.BlockSpec((1, 1, D_MODEL), lambda i: (_cond_row(i), 0, 0)),
        ],
        out_specs=pl.BlockSpec((TM, D_MODEL), lambda i: (i, 0)),
        out_shape=jax.ShapeDtypeStruct((T_ROWS, D_MODEL), F32),
        scratch_shapes=[pltpu.VMEM((k, D_MODEL), BF16)],
        compiler_params=_params(1),
        name="out_proj",
    )(a_ctx, a_lat, w_out, x, gate)


ATT_Q_W = ATT_HEADS * HEAD_DIM
ATT_KV_W = ATT_KV_HEADS * HEAD_DIM


def _group_mean_sq(xc, gmat):
    sq = xc * xc
    hi = sq.astype(BF16)
    lo = (sq - hi.astype(F32)).astype(BF16)
    return jnp.dot(hi, gmat, preferred_element_type=F32) + jnp.dot(lo, gmat, preferred_element_type=F32)


SEQS_PER_TILE = TM // SEQ


def _att_proj_kernel(*refs, n_alias):
    (x_ref, gain_ref, sh_ref, sc_ref, w_ref, qg_ref, kg_ref, cos_ref, sin_ref) = refs[:9]
    o_ref, kc_ref, vc_ref, wbf_ref = refs[9 + n_alias:]
    i = pl.program_id(0)

    @pl.when(i == 0)
    def _():
        wbf_ref[...] = w_ref[0].astype(BF16)

    h = _norm_mod(x_ref[...], gain_ref[...], sh_ref[0], sc_ref[0]).astype(BF16)
    acc = jnp.dot(h, wbf_ref[...], preferred_element_type=F32)

    def to_cache(cache_ref, chunk, first_head):
        @pl.when(i < N_CTX_TILES)
        def _():
            for s in range(SEQS_PER_TILE):
                for hh in range(LANES // HEAD_DIM):
                    cache_ref[s, 0, first_head + hh] = chunk[s * SEQ:(s + 1) * SEQ, hh * HEAD_DIM:(hh + 1) * HEAD_DIM]

    r = lax.broadcasted_iota(jnp.int32, (LANES, LANES), 0) // HEAD_DIM
    c = lax.broadcasted_iota(jnp.int32, (LANES, LANES), 1) // HEAD_DIM
    gmat = jnp.where(r == c, 1.0 / HEAD_DIM, 0.0).astype(BF16)
    cos = cos_ref[...]
    sin = sin_ref[...]
    n_q = ATT_Q_W // LANES
    n_kv = ATT_KV_W // LANES
    heads_per_chunk = LANES // HEAD_DIM
    for cidx in range(n_q + n_kv):
        xc = acc[:, cidx * LANES:(cidx + 1) * LANES]
        gain = qg_ref[...] if cidx < n_q else kg_ref[...]
        normed = xc * lax.rsqrt(_group_mean_sq(xc, gmat) + EPS) * gain
        if cidx >= n_q:
            to_cache(kc_ref, normed, (cidx - n_q) * heads_per_chunk)
        o_ref[:, cidx * LANES:(cidx + 1) * LANES] = _rope_chunk(normed, cos, sin, HEAD_DIM // 4).astype(BF16)
    for vidx in range(n_kv):
        v = acc[:, ATT_Q_W + ATT_KV_W + vidx * LANES:ATT_Q_W + ATT_KV_W + (vidx + 1) * LANES]
        to_cache(vc_ref, v, vidx * heads_per_chunk)
    o_ref[:, ATT_Q_W + ATT_KV_W:] = acc[:, ATT_Q_W + ATT_KV_W:].astype(BF16)


def _att_proj(x, gain, shift, scale, w_in, layer_j, q_gain, k_gain, cos, sin, caches):
    n_alias = len(caches)
    cache_shape = caches[0].shape
    cache_spec = pl.BlockSpec((SEQS_PER_TILE, 1, ATT_KV_HEADS, SEQ, HEAD_DIM),
                              lambda i: (jnp.minimum(i, N_CTX_TILES - 1), layer_j, 0, 0, 0))
    in_specs = [
        pl.BlockSpec((TM, D_MODEL), lambda i: (i, 0)),
        pl.BlockSpec((1, D_MODEL), lambda i: (0, 0)),
        pl.BlockSpec((1, 1, D_MODEL), lambda i: (_cond_row(i), 0, 0)),
        pl.BlockSpec((1, 1, D_MODEL), lambda i: (_cond_row(i), 0, 0)),
        pl.BlockSpec((1, D_MODEL, ATT_IN), lambda i: (layer_j, 0, 0)),
        pl.BlockSpec((1, LANES), lambda i: (0, 0)),
        pl.BlockSpec((1, LANES), lambda i: (0, 0)),
        pl.BlockSpec((TM, LANES), lambda i: (_rope_row(i), 0)),
        pl.BlockSpec((TM, LANES), lambda i: (_rope_row(i), 0)),
    ] + [pl.BlockSpec(memory_space=pl.ANY)] * n_alias
    args = [x, gain, shift, scale, w_in, q_gain, k_gain, cos, sin] + list(caches)
    return pl.pallas_call(
        functools.partial(_att_proj_kernel, n_alias=n_alias),
        grid=(T_ROWS // TM,),
        in_specs=in_specs,
        out_specs=[pl.BlockSpec((TM, ATT_IN), lambda i: (i, 0)), cache_spec, cache_spec],
        out_shape=[
            jax.ShapeDtypeStruct((T_ROWS, ATT_IN), BF16),
            jax.ShapeDtypeStruct(cache_shape, F32),
            jax.ShapeDtypeStruct(cache_shape, F32),
        ],
        input_output_aliases={9: 1, 10: 2},
        scratch_shapes=[pltpu.VMEM((D_MODEL, ATT_IN), BF16)],
        compiler_params=_params(1),
        name="att_proj",
    )(*args)


SINK_ROWS = 16
TN_DIMS = (((0,), (0,)), ((), ()))


def _ones_column(n):
    lane = lax.broadcasted_iota(jnp.int32, (n, HEAD_DIM), 1)
    return jnp.where(lane == 0, 1.0, 0.0).astype(BF16)


def _sink_softmax_pv(qs, sink_row, blocks):
    r = qs.shape[0]
    scores = []
    values = []
    for k, v, bias in blocks:
        s = lax.dot_general(k, qs, NT_DIMS, preferred_element_type=F32)
        scores.append(s if bias is None else s + bias)
        values.append(jnp.concatenate([v, _ones_column(v.shape[0])], axis=1))
    row = lax.broadcasted_iota(jnp.int32, (SINK_ROWS, r), 0)
    scores.append(jnp.where(row == 0, sink_row, -jnp.inf))
    values.append(jnp.concatenate([jnp.zeros((SINK_ROWS, HEAD_DIM), BF16), _ones_column(SINK_ROWS)], axis=1))
    st = jnp.concatenate(scores, axis=0)
    m = jnp.max(st, axis=0, keepdims=True)
    pt = jnp.exp(st - m).astype(BF16)
    ov = lax.dot_general(pt, jnp.concatenate(values, axis=0), TN_DIMS, preferred_element_type=F32)
    return ov[:, :HEAD_DIM] / ov[:, HEAD_DIM:HEAD_DIM + 1]


def _stack_heads(q, kvh, rows):
    parts = []
    for g in range(ATT_GROUP):
        hd = kvh * ATT_GROUP + g
        parts.append(q[:, hd * HEAD_DIM:(hd + 1) * HEAD_DIM])
    return jnp.concatenate(parts, axis=0) * jnp.asarray(ATT_SCALE, BF16)


def _sink_row(sink_ref, kvh, rows):
    head = lax.broadcasted_iota(jnp.int32, (1, ATT_GROUP * rows), 1) // rows
    out = jnp.full((1, ATT_GROUP * rows), sink_ref[kvh * ATT_GROUP], F32)
    for g in range(1, ATT_GROUP):
        out = jnp.where(head == g, sink_ref[kvh * ATT_GROUP + g], out)
    return out


def _store_heads(o_ref, o, kvh, rows):
    for g in range(ATT_GROUP):
        hd = kvh * ATT_GROUP + g
        o_ref[:, hd * HEAD_DIM:(hd + 1) * HEAD_DIM] = o[g * rows:(g + 1) * rows, :].astype(BF16)


def _ctx_att_kernel(sink_ref, q_ref, k_ref, v_ref, o_ref):
    q = q_ref[...]
    for kvh in range(ATT_KV_HEADS):
        qs = _stack_heads(q, kvh, SEQ)
        k = k_ref[:, kvh * HEAD_DIM:(kvh + 1) * HEAD_DIM]
        v = v_ref[:, kvh * HEAD_DIM:(kvh + 1) * HEAD_DIM]
        o = _sink_softmax_pv(qs, _sink_row(sink_ref, kvh, SEQ), [(k, v, None)])
        _store_heads(o_ref, o, kvh, SEQ)


def _lat_att_kernel(sink_ref, q_ref, k_ref, v_ref, ck_ref, cv_ref, o_ref):
    n = pl.program_id(1)
    nb = DEC_SEQ // ATT_BLOCK
    blk = ATT_BLOCK
    rows = ATT_GROUP * blk
    prev0 = pl.multiple_of(jnp.maximum(n - 1, 0) * blk, blk)
    cur0 = pl.multiple_of(n * blk, blk)
    next0 = pl.multiple_of(jnp.minimum(n + 1, nb - 1) * blk, blk)

    kj = lax.broadcasted_iota(jnp.int32, (3 * blk, rows), 0) - blk
    qi = lax.broadcasted_iota(jnp.int32, (3 * blk, rows), 1) % blk
    kabs = n * blk + kj
    valid = (jnp.abs(qi - kj) <= WINDOW) & (kabs >= 0) & (kabs < DEC_SEQ)
    bias = jnp.where(valid, 0.0, -jnp.inf)

    q = q_ref[...]
    for kvh in range(ATT_KV_HEADS):
        cols = slice(kvh * HEAD_DIM, (kvh + 1) * HEAD_DIM)
        qs = _stack_heads(q, kvh, blk)
        k_loc = jnp.concatenate([k_ref[pl.ds(prev0, blk), cols], k_ref[pl.ds(cur0, blk), cols],
                                 k_ref[pl.ds(next0, blk), cols]], axis=0)
        v_loc = jnp.concatenate([v_ref[pl.ds(prev0, blk), cols], v_ref[pl.ds(cur0, blk), cols],
                                 v_ref[pl.ds(next0, blk), cols]], axis=0)
        k_ctx = ck_ref[0, 0, kvh].astype(BF16)
        v_ctx = cv_ref[0, 0, kvh].astype(BF16)
        o = _sink_softmax_pv(qs, _sink_row(sink_ref, kvh, blk),
                             [(k_loc, v_loc, bias), (k_ctx, v_ctx, None)])
        _store_heads(o_ref, o, kvh, blk)


def _ctx_attention(qkv, sink):
    kcol = ATT_Q_W // ATT_KV_W
    return pl.pallas_call(
        _ctx_att_kernel,
        grid_spec=pltpu.PrefetchScalarGridSpec(
            num_scalar_prefetch=1,
            grid=(BATCH,),
            in_specs=[
                pl.BlockSpec((SEQ, ATT_Q_W), lambda b, s: (b, 0)),
                pl.BlockSpec((SEQ, ATT_KV_W), lambda b, s: (b, kcol)),
                pl.BlockSpec((SEQ, ATT_KV_W), lambda b, s: (b, kcol + 1)),
            ],
            out_specs=pl.BlockSpec((SEQ, ATT_Q_W), lambda b, s: (b, 0)),
        ),
        out_shape=jax.ShapeDtypeStruct((CTX_ROWS, ATT_Q_W), BF16),
        compiler_params=_params(1),
        name="ctx_attention",
    )(sink, qkv, qkv, qkv)


def _lat_attention(qkv, sink, cache_k, cache_v, layer_j):
    kcol = ATT_Q_W // ATT_KV_W
    nb = DEC_SEQ // ATT_BLOCK
    q0 = CTX_ROWS // ATT_BLOCK
    s0 = CTX_ROWS // DEC_SEQ
    cache_spec = pl.BlockSpec((1, 1, ATT_KV_HEADS, PAST_LEN, HEAD_DIM), lambda b, n, s: (b, layer_j, 0, 0, 0))
    return pl.pallas_call(
        _lat_att_kernel,
        grid_spec=pltpu.PrefetchScalarGridSpec(
            num_scalar_prefetch=1,
            grid=(DEC_BATCH, nb),
            in_specs=[
                pl.BlockSpec((ATT_BLOCK, ATT_Q_W), lambda b, n, s: (q0 + b * nb + n, 0)),
                pl.BlockSpec((DEC_SEQ, ATT_KV_W), lambda b, n, s: (s0 + b, kcol)),
                pl.BlockSpec((DEC_SEQ, ATT_KV_W), lambda b, n, s: (s0 + b, kcol + 1)),
                cache_spec,
                cache_spec,
            ],
            out_specs=pl.BlockSpec((ATT_BLOCK, ATT_Q_W), lambda b, n, s: (b * nb + n, 0)),
        ),
        out_shape=jax.ShapeDtypeStruct((LAT_ROWS, ATT_Q_W), BF16),
        compiler_params=_params(2),
        name="lat_attention",
    )(sink, qkv, qkv, qkv, cache_k, cache_v)


ROW_ALIGN = 16
S_SLOTS = 576
BUF_ROWS = S_SLOTS + TM_MOE
XW = D_MODEL + LANES
SLOT_LANE = 6
GATE_LANE0 = 8
REGION_TILES = 28
REGION_ROWS = REGION_TILES * TM_MOE
TAB_W = 16
N_TOK_TILES = T_ROWS // TM


def _router_kernel(x_ref, gain_ref, sh_ref, sc_ref, wrh_ref, wrl_ref, bias_ref,
                   xs_hbm, row_ref, tab_ref, buf_ref, off_ref, sem):
    i = pl.program_id(0)
    ng = N_GROUPS
    last = pl.num_programs(0) - 1

    @pl.when(i == 0)
    def _():
        buf_ref[...] = jnp.zeros_like(buf_ref)
        for g in range(ng):
            off_ref[g] = 0

        def clear(r, carry):
            for c in range(TAB_W):
                tab_ref[r, c] = 0
            return carry

        lax.fori_loop(0, N_TOK_TILES + 1, clear, 0)

    hf = _norm_mod(x_ref[...], gain_ref[...], sh_ref[0], sc_ref[0])
    hb = hf.astype(BF16)
    hl = (hf - hb.astype(F32)).astype(BF16)
    wrh = wrh_ref[...]
    logits = (jnp.dot(hb, wrh, preferred_element_type=F32) + jnp.dot(hl, wrh, preferred_element_type=F32)
              + jnp.dot(hb, wrl_ref[...], preferred_element_type=F32))
    lt = logits.T[0:N_EXPERTS, :]
    scores = jax.nn.sigmoid(lt)
    sel = scores + bias_ref[...]
    xs = [sel[k * ng:(k + 1) * ng, :] for k in range(EXPERTS_PER_GROUP)]
    sc = [scores[k * ng:(k + 1) * ng, :] for k in range(EXPERTS_PER_GROUP)]
    a, b, c, d = xs
    gs = jnp.maximum(jnp.maximum(jnp.maximum(a + b, a + c), jnp.maximum(a + d, b + c)),
                     jnp.maximum(b + d, c + d))
    bv = gs[0:1, :]
    bg = jnp.zeros(bv.shape, jnp.int32)
    for g in range(1, ng):
        better = gs[g:g + 1, :] > bv
        bg = jnp.where(better, g, bg)
        bv = jnp.where(better, gs[g:g + 1, :], bv)
    giota = lax.broadcasted_iota(jnp.int32, (ng, TM), 0)
    onehot = giota == bg
    wk = []
    for k in range(EXPERTS_PER_GROUP):
        rank = jnp.zeros((ng, TM), F32)
        for j in range(EXPERTS_PER_GROUP):
            if j < k:
                rank = rank + (xs[j] >= xs[k]).astype(F32)
            elif j > k:
                rank = rank + (xs[j] > xs[k]).astype(F32)
        chosen = (rank < 2.0) & onehot
        wk.append(jnp.sum(jnp.where(chosen, sc[k], 0.0), axis=0, keepdims=True))
    den = wk[0] + wk[1] + wk[2] + wk[3]
    gates = [w / den for w in wk]

    oh = jnp.where(onehot, 1.0, 0.0)
    r = lax.broadcasted_iota(jnp.int32, (TM, TM), 0)
    cc = lax.broadcasted_iota(jnp.int32, (TM, TM), 1)
    tri = jnp.where(r < cc, 1.0, 0.0).astype(BF16)
    before = jnp.dot(oh.astype(BF16), tri, preferred_element_type=F32)
    rank_local = jnp.sum(jnp.where(onehot, before, 0.0), axis=0, keepdims=True).astype(jnp.int32)
    counts = [jnp.sum(jnp.where(bg == g, 1, 0)) for g in range(ng)]
    pads = [((cnt + ROW_ALIGN - 1) // ROW_ALIGN) * ROW_ALIGN for cnt in counts]
    segs = [0]
    for g in range(1, ng):
        segs.append(segs[-1] + pads[g - 1])
    total = segs[-1] + pads[-1]
    seg_of = jnp.zeros(bg.shape, jnp.int32)
    for g in range(1, ng):
        seg_of = jnp.where(bg == g, segs[g], seg_of)
    slot = seg_of + rank_local

    parts = []
    for gt in gates:
        hi = gt.astype(BF16).astype(F32)
        rest = gt - hi
        mid = rest.astype(BF16).astype(F32)
        parts += [hi, mid, (rest - mid).astype(BF16).astype(F32)]
    zero_row = jnp.zeros((1, TM), F32)
    info = jnp.concatenate(gates + [bg.astype(F32), zero_row, slot.astype(F32), zero_row] + parts
                           + [jnp.zeros((LANES - GATE_LANE0 - len(parts), TM), F32)], axis=0)
    rowinfo = info.T
    row_ref[...] = rowinfo

    h_ext = jnp.concatenate([hb, rowinfo.astype(BF16)], axis=1)
    srow = lax.broadcasted_iota(jnp.int32, (S_SLOTS, TM), 0)
    pick = jnp.where(srow == slot, 1.0, 0.0).astype(BF16)
    par = i % 2
    buf_ref[par, 0:S_SLOTS, :] = jnp.dot(pick, h_ext, preferred_element_type=F32).astype(BF16)

    def window_copy(g, src_row, dst_row):
        return pltpu.make_async_copy(
            buf_ref.at[par, pl.ds(pl.multiple_of(src_row, ROW_ALIGN), TM_MOE), :],
            xs_hbm.at[pl.ds(pl.multiple_of(dst_row, ROW_ALIGN), TM_MOE), :],
            sem.at[g])

    def wait_windows():
        for g in range(ng):
            window_copy(g, 0, 0).wait()

    @pl.when(i > 0)
    def _():
        wait_windows()

    for g in range(ng):
        off = off_ref[g]
        window_copy(g, segs[g], g * REGION_ROWS + off).start()
        tab_ref[i, g] = segs[g]
        tab_ref[i, ng + g] = off
        off_ref[g] = off + pads[g]
    tab_ref[i, 2 * ng] = total

    @pl.when(i == last)
    def _():
        wait_windows()
        for g in range(ng):
            window_copy(g, S_SLOTS, g * REGION_ROWS + off_ref[g]).start()
            tab_ref[last + 1, g] = off_ref[g]
        wait_windows()


def _router(x, gain, shift, scale, wr_hi, wr_lo, bias_col):
    return pl.pallas_call(
        _router_kernel,
        grid=(N_TOK_TILES,),
        in_specs=[
            pl.BlockSpec((TM, D_MODEL), lambda i: (i, 0)),
            pl.BlockSpec((1, D_MODEL), lambda i: (0, 0)),
            pl.BlockSpec((1, 1, D_MODEL), lambda i: (_cond_row(i), 0, 0)),
            pl.BlockSpec((1, 1, D_MODEL), lambda i: (_cond_row(i), 0, 0)),
            pl.BlockSpec((D_MODEL, LANES), lambda i: (0, 0)),
            pl.BlockSpec((D_MODEL, LANES), lambda i: (0, 0)),
            pl.BlockSpec((N_EXPERTS, 1), lambda i: (0, 0)),
        ],
        out_specs=[
            pl.BlockSpec(memory_space=pl.ANY),
            pl.BlockSpec((TM, LANES), lambda i: (i, 0)),
            pl.BlockSpec(memory_space=pltpu.SMEM),
        ],
        out_shape=[
            jax.ShapeDtypeStruct((N_GROUPS * REGION_ROWS, XW), BF16),
            jax.ShapeDtypeStruct((T_ROWS, LANES), F32),
            jax.ShapeDtypeStruct((N_TOK_TILES + 1, TAB_W), jnp.int32),
        ],
        scratch_shapes=[
            pltpu.VMEM((2, BUF_ROWS, XW), BF16),
            pltpu.SMEM((N_GROUPS,), jnp.int32),
            pltpu.SemaphoreType.DMA((N_GROUPS,)),
        ],
        compiler_params=_params(1),
        name="moe_router",
    )(x, gain, shift, scale, wr_hi, wr_lo, bias_col)


def _moe_up_kernel(tb_ref, tg_ref, tv_ref, xs_ref, w_ref, o_ref, wbf_ref):
    k = pl.program_id(0)
    i = pl.program_id(1)
    new_w = (i == 0) | (tg_ref[i] != tg_ref[jnp.maximum(i - 1, 0)])

    @pl.when(new_w)
    def _():
        wbf_ref[...] = w_ref[0].astype(BF16)

    @pl.when(tv_ref[i] == 1)
    def _():
        hu = jnp.dot(xs_ref[:, :D_MODEL], wbf_ref[...], preferred_element_type=F32)
        extra = xs_ref[:, D_MODEL:].astype(F32)
        lane = lax.broadcasted_iota(jnp.int32, extra.shape, 1)
        lane0 = GATE_LANE0 + 3 * k
        gcol = jnp.sum(jnp.where((lane >= lane0) & (lane < lane0 + 3), extra, 0.0), axis=1, keepdims=True)
        a = _silu(hu[:, :D_EXPERT]) * hu[:, D_EXPERT:] * gcol
        o_ref[...] = a.astype(BF16)

    @pl.when(tv_ref[i] != 1)
    def _():
        o_ref[...] = jnp.zeros_like(o_ref)


def _moe_up(tile_block, tile_group, tile_valid, xs, w_up, layer):
    return pl.pallas_call(
        _moe_up_kernel,
        grid_spec=pltpu.PrefetchScalarGridSpec(
            num_scalar_prefetch=3,
            grid=(EXPERTS_PER_GROUP, N_MOE_TILES),
            in_specs=[
                pl.BlockSpec((TM_MOE, XW), lambda k, i, tb, tg, tv: (tb[i], 0)),
                pl.BlockSpec((1, D_MODEL, 2 * D_EXPERT),
                             lambda k, i, tb, tg, tv: (layer * N_EXPERTS + tg[i] * EXPERTS_PER_GROUP + k, 0, 0)),
            ],
            out_specs=pl.BlockSpec((TM_MOE, D_EXPERT), lambda k, i, tb, tg, tv: (tb[i], k)),
            scratch_shapes=[pltpu.VMEM((D_MODEL, 2 * D_EXPERT), BF16)],
        ),
        out_shape=jax.ShapeDtypeStruct((N_GROUPS * REGION_ROWS, EXPERTS_PER_GROUP * D_EXPERT), BF16),
        compiler_params=_params(2),
        name="moe_up",
    )(tile_block, tile_group, tile_valid, xs, w_up)


def _moe_down_kernel(tb_ref, tg_ref, tv_ref, a_ref, w_ref, o_ref, wbf_ref):
    i = pl.program_id(0)
    new_w = (i == 0) | (tg_ref[i] != tg_ref[jnp.maximum(i - 1, 0)])

    @pl.when(new_w)
    def _():
        wbf_ref[...] = w_ref[0].astype(BF16)

    @pl.when(tv_ref[i] == 1)
    def _():
        o_ref[...] = jnp.dot(a_ref[...], wbf_ref[...], preferred_element_type=F32).astype(BF16)

    @pl.when(tv_ref[i] != 1)
    def _():
        o_ref[...] = jnp.zeros_like(o_ref)


def _moe_down(tile_block, tile_group, tile_valid, a, w_down_grouped, layer):
    kdim = EXPERTS_PER_GROUP * D_EXPERT
    return pl.pallas_call(
        _moe_down_kernel,
        grid_spec=pltpu.PrefetchScalarGridSpec(
            num_scalar_prefetch=3,
            grid=(N_MOE_TILES,),
            in_specs=[
                pl.BlockSpec((TM_MOE, kdim), lambda i, tb, tg, tv: (tb[i], 0)),
                pl.BlockSpec((1, kdim, D_MODEL), lambda i, tb, tg, tv: (layer * N_GROUPS + tg[i], 0, 0)),
            ],
            out_specs=pl.BlockSpec((TM_MOE, D_MODEL), lambda i, tb, tg, tv: (tb[i], 0)),
            scratch_shapes=[pltpu.VMEM((kdim, D_MODEL), BF16)],
        ),
        out_shape=jax.ShapeDtypeStruct((N_GROUPS * REGION_ROWS, D_MODEL), BF16),
        compiler_params=_params(1),
        name="moe_down",
    )(tile_block, tile_group, tile_valid, a, w_down_grouped)


def _moe_combine_kernel(tab_ref, w0_ref, w1_ref, w2_ref, w3_ref, row_ref, x_ref, g_ref, o_ref, buf_ref):
    i = pl.program_id(0)

    @pl.when(i == 0)
    def _():
        buf_ref[...] = jnp.zeros_like(buf_ref)

    for g, w_ref in enumerate((w0_ref, w1_ref, w2_ref, w3_ref)):
        start = pl.multiple_of(tab_ref[i * TAB_W + g], ROW_ALIGN)
        buf_ref[pl.ds(start, TM_MOE), :] = w_ref[...]
    slot = row_ref[:, SLOT_LANE:SLOT_LANE + 1].astype(jnp.int32)
    scol = lax.broadcasted_iota(jnp.int32, (TM, S_SLOTS), 1)
    pick = jnp.where(scol == slot, 1.0, 0.0).astype(BF16)
    y = jnp.dot(pick, buf_ref[0:S_SLOTS, :], preferred_element_type=F32)
    o_ref[...] = x_ref[...] + g_ref[0] * y


def _moe_combine(tab_flat, ys, rowinfo, x, gate):
    def window_spec(g):
        return pl.BlockSpec((pl.Element(TM_MOE), pl.Element(D_MODEL)),
                            lambda i, tab: (pl.multiple_of(tab[i * TAB_W + N_GROUPS + g] + g * REGION_ROWS,
                                                           ROW_ALIGN), 0))

    return pl.pallas_call(
        _moe_combine_kernel,
        grid_spec=pltpu.PrefetchScalarGridSpec(
            num_scalar_prefetch=1,
            grid=(N_TOK_TILES,),
            in_specs=[window_spec(g) for g in range(N_GROUPS)] + [
                pl.BlockSpec((TM, LANES), lambda i, tab: (i, 0)),
                pl.BlockSpec((TM, D_MODEL), lambda i, tab: (i, 0)),
                pl.BlockSpec((1, 1, D_MODEL), lambda i, tab: (_cond_row(i), 0, 0)),
            ],
            out_specs=pl.BlockSpec((TM, D_MODEL), lambda i, tab: (i, 0)),
            scratch_shapes=[pltpu.VMEM((BUF_ROWS, D_MODEL), BF16)],
        ),
        out_shape=jax.ShapeDtypeStruct((T_ROWS, D_MODEL), F32),
        compiler_params=_params(1),
        name="moe_combine",
    )(tab_flat, ys, ys, ys, ys, rowinfo, x, gate)


def _moe_layer(x, gain, shift, scale, gate, router_w, w_up, w_down_grouped, layer):
    wr_hi, wr_lo, bias_col = router_w
    xs, rowinfo, tab = _router(x, gain, shift, scale, wr_hi, wr_lo, bias_col)
    n_full = (tab[N_TOK_TILES, :N_GROUPS] + TM_MOE - 1) // TM_MOE
    ends = jnp.cumsum(n_full + 1)
    starts = ends - (n_full + 1)
    entry = jnp.arange(N_MOE_TILES, dtype=jnp.int32)
    grp = jnp.minimum(jnp.sum(entry[:, None] >= ends[None, :], axis=1), N_GROUPS - 1).astype(jnp.int32)
    idx = entry - starts[grp]
    listed = entry < ends[N_GROUPS - 1]
    closing = (N_GROUPS - 1) * REGION_TILES + n_full[N_GROUPS - 1]
    tile_block = jnp.where(listed, grp * REGION_TILES + idx, closing).astype(jnp.int32)
    tile_valid = (listed & (idx < n_full[grp])).astype(jnp.int32)
    a = _moe_up(tile_block, grp, tile_valid, xs, w_up, layer)
    ys = _moe_down(tile_block, grp, tile_valid, a, w_down_grouped, layer)
    return _moe_combine(tab.reshape(-1), ys, rowinfo, x, gate)


def kernel(x_prompt, x_sample, state_ret, cache_k, cache_v, c, c_ctx, w_mod, b_mod, norm_mix,
           norm_moe, ret_w_in, ret_decay, ret_w_out, att_w_in, att_q_norm, att_k_norm, att_sink,
           att_w_out, w_router, router_bias, moe_w_up, moe_w_down):
    x = jnp.concatenate([x_prompt.reshape(CTX_ROWS, D_MODEL), x_sample.reshape(LAT_ROWS, D_MODEL)], axis=0)
    cond8 = jnp.zeros((8, D_MODEL), F32).at[0].set(c_ctx).at[1:N_COND].set(c)
    mods = _modulation_all(cond8, w_mod, b_mod)
    mods = mods[:, :N_COND].reshape(DEPTH, N_COND, N_MOD, 1, D_MODEL)

    ret_cos, ret_sin = _rope_tables(RET_DK)
    att_cos, att_sin = _rope_tables(HEAD_DIM)

    perm = jnp.arange(N_EXPERTS).reshape(N_GROUPS, EXPERTS_PER_GROUP).T.reshape(-1)
    wr = jnp.zeros((D_MODEL, LANES), F32).at[:, :N_EXPERTS].set(w_router[:, perm])
    wr_hi = wr.astype(BF16)
    wr_lo = (wr - wr_hi.astype(F32)).astype(BF16)
    bias_col = router_bias[perm].astype(F32).reshape(N_EXPERTS, 1)
    router_w = (wr_hi, wr_lo, bias_col)

    w_up_all = moe_w_up.reshape(DEPTH * N_EXPERTS, D_MODEL, 2 * D_EXPERT)
    w_down_all = moe_w_down.reshape(DEPTH * N_GROUPS, EXPERTS_PER_GROUP * D_EXPERT, D_MODEL)
    new_state = jnp.zeros((BATCH, (DEPTH + 1) // 2, 2, RET_HEADS, RET_DK, RET_DV), F32)
    cache_shape = (BATCH, DEPTH // 2, ATT_KV_HEADS, SEQ, HEAD_DIM)
    new_kv = (jnp.zeros(cache_shape, F32), jnp.zeros(cache_shape, F32))
    for layer in range(DEPTH):
        sh_a, sc_a, g_a, sh_m, sc_m, g_m = [mods[layer, :, t] for t in range(N_MOD)]
        gain_mix = norm_mix[layer].reshape(1, D_MODEL)
        gain_moe = norm_moe[layer].reshape(1, D_MODEL)
        j = layer // 2
        if layer % 2 == 0:
            proj = _ret_proj(x, gain_mix, sh_a, sc_a, ret_w_in, j, ret_cos, ret_sin)
            o_ctx, new_state = _ret_core(proj, ret_decay[j], state_ret, j, latent=False, new_state=new_state)
            (o_lat,) = _ret_core(proj, ret_decay[j], state_ret, j, latent=True)
            x = _out_proj(o_ctx, o_lat, ret_w_out, j, x, g_a)
        else:
            q_gain = jnp.tile(att_q_norm[j], LANES // HEAD_DIM).reshape(1, LANES)
            k_gain = jnp.tile(att_k_norm[j], LANES // HEAD_DIM).reshape(1, LANES)
            qkv, new_k, new_v = _att_proj(x, gain_mix, sh_a, sc_a, att_w_in, j, q_gain, k_gain,
                                          att_cos, att_sin, new_kv)
            new_kv = (new_k, new_v)
            sink = att_sink[j].astype(F32)
            o_ctx = _ctx_attention(qkv, sink)
            o_lat = _lat_attention(qkv, sink, cache_k, cache_v, j)
            x = _out_proj(o_ctx, o_lat, att_w_out, j, x, g_a)
        x = _moe_layer(x, gain_moe, sh_m, sc_m, g_m, router_w, w_up_all, w_down_all, layer)

    y_prompt = x[:CTX_ROWS].reshape(BATCH, SEQ, D_MODEL)
    y_sample = x[CTX_ROWS:].reshape(DEC_BATCH, DEC_SEQ, D_MODEL)
    return (y_prompt, y_sample, new_state, new_kv[0], new_kv[1])
```

```python
import functools

import jax
import jax.numpy as jnp
from jax import lax
from jax.experimental import pallas as pl
from jax.experimental.pallas import tpu as pltpu

F32 = jnp.float32
BF16 = jnp.bfloat16

D_MODEL = 1024
BATCH = 16
SEQ = 256
DEPTH = 4
DEC_BATCH = 2
DEC_SEQ = 4096
PAST_LEN = 512
GRID_W = 64
N_MOD = 6
EPS = 1e-6
ROPE_BASE = 10000.0
RET_HEADS = 8
RET_DK = 128
RET_DV = 256
RET_CHUNK = 128
RET_IN = 2 * RET_HEADS * RET_DK + 2 * RET_HEADS * RET_DV
ATT_HEADS = 16
ATT_KV_HEADS = 4
ATT_GROUP = 4
HEAD_DIM = 64
WINDOW = 128
ATT_BLOCK = 128
ATT_IN = (ATT_HEADS + 2 * ATT_KV_HEADS) * HEAD_DIM
ATT_SCALE = HEAD_DIM ** -0.5
N_EXPERTS = 16
N_GROUPS = 4
EXPERTS_PER_GROUP = 4
D_EXPERT = 512

CTX_ROWS = BATCH * SEQ
LAT_ROWS = DEC_BATCH * DEC_SEQ
T_ROWS = CTX_ROWS + LAT_ROWS
N_COND = 1 + DEC_BATCH

LANES = 128
TM = 512
N_CTX_TILES = CTX_ROWS // TM
N_LAT_TILES = DEC_SEQ // TM
TM_MOE = 512
N_MOE_TILES = 36
VMEM_LIMIT = 52 * 1024 * 1024

NT_DIMS = (((1,), (1,)), ((), ()))


def _params(n_axes, vmem=VMEM_LIMIT):
    return pltpu.CompilerParams(dimension_semantics=("arbitrary",) * n_axes, vmem_limit_bytes=vmem)


def _cond_row(i):
    return (i * TM) // DEC_SEQ


def _rope_row(i):
    return jnp.where(i < N_CTX_TILES, 0, 1 + (i - N_CTX_TILES) % N_LAT_TILES)


def _norm_mod(x, gain, shift, scale):
    ms = jnp.mean(x * x, axis=-1, keepdims=True)
    y = x * lax.rsqrt(ms + EPS) * gain
    return y * (1.0 + scale) + shift


def _silu(x):
    return x * jax.nn.sigmoid(x)


def _mod_kernel(c_ref, w_ref, b_ref, o_ref):
    c = c_ref[...]
    s = _silu(c).astype(BF16)
    o_ref[0] = jnp.dot(s, w_ref[0].astype(BF16), preferred_element_type=F32) + b_ref[0]


def _modulation_all(cond8, w_mod, b_mod):
    tn = 1536
    n = N_MOD * D_MODEL
    return pl.pallas_call(
        _mod_kernel,
        grid=(DEPTH, n // tn),
        in_specs=[
            pl.BlockSpec((8, D_MODEL), lambda l, j: (0, 0)),
            pl.BlockSpec((1, D_MODEL, tn), lambda l, j: (l, 0, j)),
            pl.BlockSpec((1, 1, tn), lambda l, j: (l, 0, j)),
        ],
        out_specs=pl.BlockSpec((1, 8, tn), lambda l, j: (l, 0, j)),
        out_shape=jax.ShapeDtypeStruct((DEPTH, 8, n), F32),
        compiler_params=_params(2),
        name="modulation",
    )(cond8, w_mod, b_mod.reshape(DEPTH, 1, n))


def _rope_tables(head_dim):
    half = head_dim // 2
    quarter = half // 2
    t = jnp.arange(DEC_SEQ)
    row = (t // GRID_W).astype(F32)
    col = (t % GRID_W).astype(F32)
    inv_freq = ROPE_BASE ** (-jnp.arange(quarter, dtype=F32) / quarter)
    lane = jnp.arange(LANES)
    d = lane % head_dim
    w = d % half
    f = w % quarter
    pos = jnp.where((d // half)[None, :] == 0, row[:, None], col[:, None])
    ang = pos * inv_freq[f][None, :]
    cos = jnp.cos(ang)
    sin = jnp.where((w < quarter)[None, :], -jnp.sin(ang), jnp.sin(ang))
    cos = jnp.concatenate([jnp.ones((TM, LANES), F32), cos], axis=0)
    sin = jnp.concatenate([jnp.zeros((TM, LANES), F32), sin], axis=0)
    return cos, sin


def _rope_chunk(xc, cos, sin, quarter):
    lane = lax.broadcasted_iota(jnp.int32, xc.shape, 1)
    first = (lane % (2 * quarter)) < quarter
    partner = jnp.where(first, pltpu.roll(xc, LANES - quarter, 1), pltpu.roll(xc, quarter, 1))
    return xc * cos + partner * sin


RET_TN = 1024
RET_Q_TILES = RET_HEADS * RET_DK // RET_TN
RET_QK_TILES = 2 * RET_Q_TILES


def _ret_proj_kernel(x_ref, gain_ref, sh_ref, sc_ref, w_ref, cos_ref, sin_ref, o_ref):
    h = _norm_mod(x_ref[...], gain_ref[...], sh_ref[0], sc_ref[0]).astype(BF16)
    cos = cos_ref[...]
    sin = sin_ref[...]
    for j in range(RET_IN // RET_TN):
        cols = slice(j * RET_TN, (j + 1) * RET_TN)
        acc = jnp.dot(h, w_ref[0, :, cols], preferred_element_type=F32)
        if j < RET_QK_TILES:
            scale = 1.0 if j < RET_Q_TILES else RET_DK ** -0.5
            for c in range(RET_TN // LANES):
                xc = acc[:, c * LANES:(c + 1) * LANES]
                lanes = slice(j * RET_TN + c * LANES, j * RET_TN + (c + 1) * LANES)
                o_ref[:, lanes] = (_rope_chunk(xc, cos, sin, RET_DK // 4) * scale).astype(BF16)
        else:
            o_ref[:, cols] = acc.astype(BF16)


def _ret_proj(x, gain, shift, scale, w_in_bf16, layer_j, cos, sin):
    return pl.pallas_call(
        _ret_proj_kernel,
        grid=(T_ROWS // TM,),
        in_specs=[
            pl.BlockSpec((TM, D_MODEL), lambda i: (i, 0)),
            pl.BlockSpec((1, D_MODEL), lambda i: (0, 0)),
            pl.BlockSpec((1, 1, D_MODEL), lambda i: (_cond_row(i), 0, 0)),
            pl.BlockSpec((1, 1, D_MODEL), lambda i: (_cond_row(i), 0, 0)),
            pl.BlockSpec((1, D_MODEL, RET_IN), lambda i: (layer_j, 0, 0), pipeline_mode=pl.Buffered(1)),
            pl.BlockSpec((TM, LANES), lambda i: (_rope_row(i), 0)),
            pl.BlockSpec((TM, LANES), lambda i: (_rope_row(i), 0)),
        ],
        out_specs=pl.BlockSpec((TM, RET_IN), lambda i: (i, 0)),
        out_shape=jax.ShapeDtypeStruct((T_ROWS, RET_IN), BF16),
        compiler_params=_params(1),
        name="ret_proj",
    )(x, gain, shift, scale, w_in_bf16, cos, sin)


def _log_sigmoid(x):
    return -(jnp.maximum(-x, 0.0) + jnp.log(1.0 + jnp.exp(-jnp.abs(x))))


def _ret_core_kernel(*refs, seq_len, n_seq, has_init, emit_state, n_alias):
    it = iter(refs)
    decay_ref = next(it)
    q_ref = next(it)
    k_ref = next(it)
    v_ref = next(it)
    g_ref = next(it)
    s0_ref = next(it) if has_init else None
    for _ in range(n_alias):
        next(it)
    o_ref = next(it)
    st_ref = next(it) if emit_state else None
    acc_ref = next(it)
    kv_ref = next(it)

    c = RET_CHUNK
    nc = seq_len // c
    n_chunks = n_seq * nc
    unroll = min(RET_UNROLL, n_chunks)
    head = pl.program_id(1)

    lg_all = _log_sigmoid(decay_ref[...])
    pick = lax.broadcasted_iota(jnp.int32, lg_all.shape, 1) == head
    lg = jnp.sum(jnp.where(pick, lg_all, 0.0), axis=1, keepdims=True)
    lg_f = lg[0:1, :]
    lg_b = lg[1:2, :]

    ri = lax.broadcasted_iota(jnp.int32, (c, c), 0).astype(F32)
    ci = lax.broadcasted_iota(jnp.int32, (c, c), 1).astype(F32)
    diff = ri - ci
    decay = (jnp.where(diff >= 0, jnp.exp(jnp.where(diff >= 0, diff, 0.0) * lg_f), 0.0)
             + jnp.where(diff <= 0, jnp.exp(jnp.where(diff <= 0, -diff, 0.0) * lg_b), 0.0))
    pos_col = lax.broadcasted_iota(jnp.int32, (c, 1), 0).astype(F32)
    pos_row = lax.broadcasted_iota(jnp.int32, (1, c), 1).astype(F32)
    xi_f = jnp.exp((pos_col + 1.0) * lg_f)
    xi_b = jnp.exp((c - pos_col) * lg_b)
    zeta_f = jnp.exp((c - 1.0 - pos_row) * lg_f)
    zeta_b = jnp.exp(pos_row * lg_b)
    cd_f = jnp.exp(c * lg_f)
    cd_b = jnp.exp(c * lg_b)

    def intra(n):
        r0 = pl.multiple_of(n * c, c)
        qn = q_ref[pl.ds(r0, c), :]
        kn = k_ref[pl.ds(r0, c), :]
        vn = v_ref[pl.ds(r0, c), :]
        s = lax.dot_general(qn, kn, NT_DIMS, preferred_element_type=F32) * decay
        acc_ref[pl.ds(r0, c), :] = jnp.dot(s.astype(BF16), vn, preferred_element_type=F32)
        kt = kn.astype(F32).T
        kz = jnp.concatenate([(kt * zeta_f).astype(BF16), (kt * zeta_b).astype(BF16)], axis=0)
        kv_ref[n] = jnp.dot(kz, vn, preferred_element_type=F32)

    def cross(n):
        r0 = pl.multiple_of(n * c, c)
        qn = q_ref[pl.ds(r0, c), :].astype(F32)
        qx = jnp.concatenate([(qn * xi_f).astype(BF16), (qn * xi_b).astype(BF16)], axis=1)
        o = acc_ref[pl.ds(r0, c), :] + jnp.dot(qx, kv_ref[n].astype(BF16), preferred_element_type=F32)
        mu = jnp.mean(o, axis=-1, keepdims=True)
        var = jnp.mean(jnp.square(o - mu), axis=-1, keepdims=True)
        on = (o - mu) * lax.rsqrt(var + EPS)
        gate = g_ref[pl.ds(r0, c), :].astype(F32)
        o_ref[pl.ds(r0, c), :] = (_silu(gate) * on).astype(BF16)

    def over_chunks(fn):
        def body(step, carry):
            for u in range(unroll):
                fn(step * unroll + u)
            return carry

        lax.fori_loop(0, n_chunks // unroll, body, 0)

    over_chunks(intra)

    for s in range(n_seq):
        if has_init:
            init_f = s0_ref[s, 0, 0, 0]
            init_b = s0_ref[s, 0, 1, 0]
        else:
            init_f = jnp.zeros((RET_DK, RET_DV), F32)
            init_b = init_f

        def fwd(n, state, s=s):
            kv = kv_ref[s * nc + n, 0:RET_DK, :]
            kv_ref[s * nc + n, 0:RET_DK, :] = state
            return cd_f * state + kv

        def bwd(n, state, s=s):
            m = s * nc + nc - 1 - n
            kv = kv_ref[m, RET_DK:, :]
            kv_ref[m, RET_DK:, :] = state
            return cd_b * state + kv

        final_f = lax.fori_loop(0, nc, fwd, init_f)
        final_b = lax.fori_loop(0, nc, bwd, init_b)
        if emit_state:
            st_ref[s, 0, 0, 0] = final_f
            st_ref[s, 0, 1, 0] = final_b

    over_chunks(cross)


RET_UNROLL = 4
RET_CTX_SEQS = 4


def _ret_core(proj, ret_decay_j, state_ret, layer_j, *, latent, new_state=None):
    if latent:
        nb, seq_len, n_seq, row0 = DEC_BATCH, DEC_SEQ, 1, CTX_ROWS // DEC_SEQ
    else:
        nb, seq_len, n_seq, row0 = BATCH // RET_CTX_SEQS, SEQ, RET_CTX_SEQS, 0
    rows = n_seq * seq_len
    kcol = RET_HEADS * RET_DK // RET_DK
    vcol = 2 * RET_HEADS * RET_DK // RET_DV
    gcol = vcol + RET_HEADS
    in_specs = [
        pl.BlockSpec((2, RET_HEADS), lambda b, h: (0, 0)),
        pl.BlockSpec((rows, RET_DK), lambda b, h: (row0 + b, h)),
        pl.BlockSpec((rows, RET_DK), lambda b, h: (row0 + b, kcol + h)),
        pl.BlockSpec((rows, RET_DV), lambda b, h: (row0 + b, vcol + h)),
        pl.BlockSpec((rows, RET_DV), lambda b, h: (row0 + b, gcol + h)),
    ]
    args = [ret_decay_j, proj, proj, proj, proj]
    if latent:
        in_specs.append(pl.BlockSpec((1, 1, 2, 1, RET_DK, RET_DV), lambda b, h: (b, layer_j, 0, h, 0, 0)))
        args.append(state_ret)
    aliases = {}
    if new_state is not None:
        aliases = {len(args): 1}
        in_specs.append(pl.BlockSpec(memory_space=pl.ANY))
        args.append(new_state)
    out_specs = [pl.BlockSpec((rows, RET_DV), lambda b, h: (b, h))]
    out_shape = [jax.ShapeDtypeStruct((nb * rows, RET_HEADS * RET_DV), BF16)]
    if not latent:
        out_specs.append(pl.BlockSpec((n_seq, 1, 2, 1, RET_DK, RET_DV), lambda b, h: (b, layer_j, 0, h, 0, 0)))
        out_shape.append(jax.ShapeDtypeStruct((BATCH, (DEPTH + 1) // 2, 2, RET_HEADS, RET_DK, RET_DV), F32))
    return pl.pallas_call(
        functools.partial(_ret_core_kernel, seq_len=seq_len, n_seq=n_seq, has_init=latent,
                          emit_state=not latent, n_alias=len(aliases)),
        grid=(nb, RET_HEADS),
        in_specs=in_specs,
        out_specs=out_specs,
        out_shape=out_shape,
        input_output_aliases=aliases,
        scratch_shapes=[
            pltpu.VMEM((rows, RET_DV), F32),
            pltpu.VMEM((rows // RET_CHUNK, 2 * RET_DK, RET_DV), F32),
        ],
        compiler_params=_params(2),
        name="ret_core_lat" if latent else "ret_core_ctx",
    )(*args)


def _out_proj_kernel(ac_ref, al_ref, w_ref, x_ref, g_ref, o_ref, wbf_ref):
    i = pl.program_id(0)

    @pl.when(i == 0)
    def _():
        wbf_ref[...] = w_ref[0].astype(BF16)

    def emit(a_ref):
        y = jnp.dot(a_ref[...], wbf_ref[...], preferred_element_type=F32)
        o_ref[...] = x_ref[...] + g_ref[0] * y

    @pl.when(i < N_CTX_TILES)
    def _():
        emit(ac_ref)

    @pl.when(i >= N_CTX_TILES)
    def _():
        emit(al_ref)


def _out_proj(a_ctx, a_lat, w_out, layer_j, x, gate):
    k = a_ctx.shape[1]
    return pl.pallas_call(
        _out_proj_kernel,
        grid=(T_ROWS // TM,),
        in_specs=[
            pl.BlockSpec((TM, k), lambda i: (jnp.minimum(i, N_CTX_TILES - 1), 0)),
            pl.BlockSpec((TM, k), lambda i: (jnp.maximum(i - N_CTX_TILES, 0), 0)),
            pl.BlockSpec((1, k, D_MODEL), lambda i: (layer_j, 0, 0)),
            pl.BlockSpec((TM, D_MODEL), lambda i: (i, 0)),
            pl.BlockSpec((1, 1, D_MODEL), lambda i: (_cond_row(i), 0, 0)),
        ],
        out_specs=pl.BlockSpec((TM, D_MODEL), lambda i: (i, 0)),
        out_shape=jax.ShapeDtypeStruct((T_ROWS, D_MODEL), F32),
        scratch_shapes=[pltpu.VMEM((k, D_MODEL), BF16)],
        compiler_params=_params(1),
        name="out_proj",
    )(a_ctx, a_lat, w_out, x, gate)


ATT_Q_W = ATT_HEADS * HEAD_DIM
ATT_KV_W = ATT_KV_HEADS * HEAD_DIM


def _group_mean_sq(xc, gmat):
    sq = xc * xc
    hi = sq.astype(BF16)
    lo = (sq - hi.astype(F32)).astype(BF16)
    return jnp.dot(hi, gmat, preferred_element_type=F32) + jnp.dot(lo, gmat, preferred_element_type=F32)


SEQS_PER_TILE = TM // SEQ


def _att_proj_kernel(*refs, n_alias):
    (x_ref, gain_ref, sh_ref, sc_ref, w_ref, qg_ref, kg_ref, cos_ref, sin_ref) = refs[:9]
    o_ref, kc_ref, vc_ref, wbf_ref = refs[9 + n_alias:]
    i = pl.program_id(0)

    @pl.when(i == 0)
    def _():
        wbf_ref[...] = w_ref[0].astype(BF16)

    h = _norm_mod(x_ref[...], gain_ref[...], sh_ref[0], sc_ref[0]).astype(BF16)
    acc = jnp.dot(h, wbf_ref[...], preferred_element_type=F32)

    def to_cache(cache_ref, chunk, first_head):
        @pl.when(i < N_CTX_TILES)
        def _():
            for s in range(SEQS_PER_TILE):
                for hh in range(LANES // HEAD_DIM):
                    cache_ref[s, 0, first_head + hh] = chunk[s * SEQ:(s + 1) * SEQ, hh * HEAD_DIM:(hh + 1) * HEAD_DIM]

    r = lax.broadcasted_iota(jnp.int32, (LANES, LANES), 0) // HEAD_DIM
    c = lax.broadcasted_iota(jnp.int32, (LANES, LANES), 1) // HEAD_DIM
    gmat = jnp.where(r == c, 1.0 / HEAD_DIM, 0.0).astype(BF16)
    cos = cos_ref[...]
    sin = sin_ref[...]
    n_q = ATT_Q_W // LANES
    n_kv = ATT_KV_W // LANES
    heads_per_chunk = LANES // HEAD_DIM
    for cidx in range(n_q + n_kv):
        xc = acc[:, cidx * LANES:(cidx + 1) * LANES]
        gain = qg_ref[...] if cidx < n_q else kg_ref[...]
        normed = xc * lax.rsqrt(_group_mean_sq(xc, gmat) + EPS) * gain
        if cidx >= n_q:
            to_cache(kc_ref, normed, (cidx - n_q) * heads_per_chunk)
        o_ref[:, cidx * LANES:(cidx + 1) * LANES] = _rope_chunk(normed, cos, sin, HEAD_DIM // 4).astype(BF16)
    for vidx in range(n_kv):
        v = acc[:, ATT_Q_W + ATT_KV_W + vidx * LANES:ATT_Q_W + ATT_KV_W + (vidx + 1) * LANES]
        to_cache(vc_ref, v, vidx * heads_per_chunk)
    o_ref[:, ATT_Q_W + ATT_KV_W:] = acc[:, ATT_Q_W + ATT_KV_W:].astype(BF16)


def _att_proj(x, gain, shift, scale, w_in, layer_j, q_gain, k_gain, cos, sin, caches):
    n_alias = len(caches)
    cache_shape = caches[0].shape
    cache_spec = pl.BlockSpec((SEQS_PER_TILE, 1, ATT_KV_HEADS, SEQ, HEAD_DIM),
                              lambda i: (jnp.minimum(i, N_CTX_TILES - 1), layer_j, 0, 0, 0))
    in_specs = [
        pl.BlockSpec((TM, D_MODEL), lambda i: (i, 0)),
        pl.BlockSpec((1, D_MODEL), lambda i: (0, 0)),
        pl.BlockSpec((1, 1, D_MODEL), lambda i: (_cond_row(i), 0, 0)),
        pl.BlockSpec((1, 1, D_MODEL), lambda i: (_cond_row(i), 0, 0)),
        pl.BlockSpec((1, D_MODEL, ATT_IN), lambda i: (layer_j, 0, 0)),
        pl.BlockSpec((1, LANES), lambda i: (0, 0)),
        pl.BlockSpec((1, LANES), lambda i: (0, 0)),
        pl.BlockSpec((TM, LANES), lambda i: (_rope_row(i), 0)),
        pl.BlockSpec((TM, LANES), lambda i: (_rope_row(i), 0)),
    ] + [pl.BlockSpec(memory_space=pl.ANY)] * n_alias
    args = [x, gain, shift, scale, w_in, q_gain, k_gain, cos, sin] + list(caches)
    return pl.pallas_call(
        functools.partial(_att_proj_kernel, n_alias=n_alias),
        grid=(T_ROWS // TM,),
        in_specs=in_specs,
        out_specs=[pl.BlockSpec((TM, ATT_IN), lambda i: (i, 0)), cache_spec, cache_spec],
        out_shape=[
            jax.ShapeDtypeStruct((T_ROWS, ATT_IN), BF16),
            jax.ShapeDtypeStruct(cache_shape, F32),
            jax.ShapeDtypeStruct(cache_shape, F32),
        ],
        input_output_aliases={9: 1, 10: 2},
        scratch_shapes=[pltpu.VMEM((D_MODEL, ATT_IN), BF16)],
        compiler_params=_params(1),
        name="att_proj",
    )(*args)


SINK_ROWS = 16
TN_DIMS = (((0,), (0,)), ((), ()))


def _ones_column(n):
    lane = lax.broadcasted_iota(jnp.int32, (n, HEAD_DIM), 1)
    return jnp.where(lane == 0, 1.0, 0.0).astype(BF16)


def _sink_softmax_pv(qs, sink_row, blocks):
    r = qs.shape[0]
    scores = []
    values = []
    for k, v, bias in blocks:
        s = lax.dot_general(k, qs, NT_DIMS, preferred_element_type=F32)
        scores.append(s if bias is None else s + bias)
        values.append(jnp.concatenate([v, _ones_column(v.shape[0])], axis=1))
    row = lax.broadcasted_iota(jnp.int32, (SINK_ROWS, r), 0)
    scores.append(jnp.where(row == 0, sink_row, -jnp.inf))
    values.append(jnp.concatenate([jnp.zeros((SINK_ROWS, HEAD_DIM), BF16), _ones_column(SINK_ROWS)], axis=1))
    st = jnp.concatenate(scores, axis=0)
    m = jnp.max(st, axis=0, keepdims=True)
    pt = jnp.exp(st - m).astype(BF16)
    ov = lax.dot_general(pt, jnp.concatenate(values, axis=0), TN_DIMS, preferred_element_type=F32)
    return ov[:, :HEAD_DIM] / ov[:, HEAD_DIM:HEAD_DIM + 1]


def _stack_heads(q, kvh, rows):
    parts = []
    for g in range(ATT_GROUP):
        hd = kvh * ATT_GROUP + g
        parts.append(q[:, hd * HEAD_DIM:(hd + 1) * HEAD_DIM])
    return jnp.concatenate(parts, axis=0) * jnp.asarray(ATT_SCALE, BF16)


def _sink_row(sink_ref, kvh, rows):
    head = lax.broadcasted_iota(jnp.int32, (1, ATT_GROUP * rows), 1) // rows
    out = jnp.full((1, ATT_GROUP * rows), sink_ref[kvh * ATT_GROUP], F32)
    for g in range(1, ATT_GROUP):
        out = jnp.where(head == g, sink_ref[kvh * ATT_GROUP + g], out)
    return out


def _store_heads(o_ref, o, kvh, rows):
    for g in range(ATT_GROUP):
        hd = kvh * ATT_GROUP + g
        o_ref[:, hd * HEAD_DIM:(hd + 1) * HEAD_DIM] = o[g * rows:(g + 1) * rows, :].astype(BF16)


def _ctx_att_kernel(sink_ref, q_ref, k_ref, v_ref, o_ref):
    q = q_ref[...]
    for kvh in range(ATT_KV_HEADS):
        qs = _stack_heads(q, kvh, SEQ)
        k = k_ref[:, kvh * HEAD_DIM:(kvh + 1) * HEAD_DIM]
        v = v_ref[:, kvh * HEAD_DIM:(kvh + 1) * HEAD_DIM]
        o = _sink_softmax_pv(qs, _sink_row(sink_ref, kvh, SEQ), [(k, v, None)])
        _store_heads(o_ref, o, kvh, SEQ)


def _lat_att_kernel(sink_ref, q_ref, k_ref, v_ref, ck_ref, cv_ref, o_ref):
    n = pl.program_id(1)
    nb = DEC_SEQ // ATT_BLOCK
    blk = ATT_BLOCK
    rows = ATT_GROUP * blk
    prev0 = pl.multiple_of(jnp.maximum(n - 1, 0) * blk, blk)
    cur0 = pl.multiple_of(n * blk, blk)
    next0 = pl.multiple_of(jnp.minimum(n + 1, nb - 1) * blk, blk)

    kj = lax.broadcasted_iota(jnp.int32, (3 * blk, rows), 0) - blk
    qi = lax.broadcasted_iota(jnp.int32, (3 * blk, rows), 1) % blk
    kabs = n * blk + kj
    valid = (jnp.abs(qi - kj) <= WINDOW) & (kabs >= 0) & (kabs < DEC_SEQ)
    bias = jnp.where(valid, 0.0, -jnp.inf)

    q = q_ref[...]
    for kvh in range(ATT_KV_HEADS):
        cols = slice(kvh * HEAD_DIM, (kvh + 1) * HEAD_DIM)
        qs = _stack_heads(q, kvh, blk)
        k_loc = jnp.concatenate([k_ref[pl.ds(prev0, blk), cols], k_ref[pl.ds(cur0, blk), cols],
                                 k_ref[pl.ds(next0, blk), cols]], axis=0)
        v_loc = jnp.concatenate([v_ref[pl.ds(prev0, blk), cols], v_ref[pl.ds(cur0, blk), cols],
                                 v_ref[pl.ds(next0, blk), cols]], axis=0)
        k_ctx = ck_ref[0, 0, kvh].astype(BF16)
        v_ctx = cv_ref[0, 0, kvh].astype(BF16)
        o = _sink_softmax_pv(qs, _sink_row(sink_ref, kvh, blk),
                             [(k_loc, v_loc, bias), (k_ctx, v_ctx, None)])
        _store_heads(o_ref, o, kvh, blk)


def _ctx_attention(qkv, sink):
    kcol = ATT_Q_W // ATT_KV_W
    return pl.pallas_call(
        _ctx_att_kernel,
        grid_spec=pltpu.PrefetchScalarGridSpec(
            num_scalar_prefetch=1,
            grid=(BATCH,),
            in_specs=[
                pl.BlockSpec((SEQ, ATT_Q_W), lambda b, s: (b, 0)),
                pl.BlockSpec((SEQ, ATT_KV_W), lambda b, s: (b, kcol)),
                pl.BlockSpec((SEQ, ATT_KV_W), lambda b, s: (b, kcol + 1)),
            ],
            out_specs=pl.BlockSpec((SEQ, ATT_Q_W), lambda b, s: (b, 0)),
        ),
        out_shape=jax.ShapeDtypeStruct((CTX_ROWS, ATT_Q_W), BF16),
        compiler_params=_params(1),
        name="ctx_attention",
    )(sink, qkv, qkv, qkv)


def _lat_attention(qkv, sink, cache_k, cache_v, layer_j):
    kcol = ATT_Q_W // ATT_KV_W
    nb = DEC_SEQ // ATT_BLOCK
    q0 = CTX_ROWS // ATT_BLOCK
    s0 = CTX_ROWS // DEC_SEQ
    cache_spec = pl.BlockSpec((1, 1, ATT_KV_HEADS, PAST_LEN, HEAD_DIM), lambda b, n, s: (b, layer_j, 0, 0, 0))
    return pl.pallas_call(
        _lat_att_kernel,
        grid_spec=pltpu.PrefetchScalarGridSpec(
            num_scalar_prefetch=1,
            grid=(DEC_BATCH, nb),
            in_specs=[
                pl.BlockSpec((ATT_BLOCK, ATT_Q_W), lambda b, n, s: (q0 + b * nb + n, 0)),
                pl.BlockSpec((DEC_SEQ, ATT_KV_W), lambda b, n, s: (s0 + b, kcol)),
                pl.BlockSpec((DEC_SEQ, ATT_KV_W), lambda b, n, s: (s0 + b, kcol + 1)),
                cache_spec,
                cache_spec,
            ],
            out_specs=pl.BlockSpec((ATT_BLOCK, ATT_Q_W), lambda b, n, s: (b * nb + n, 0)),
        ),
        out_shape=jax.ShapeDtypeStruct((LAT_ROWS, ATT_Q_W), BF16),
        compiler_params=_params(2),
        name="lat_attention",
    )(sink, qkv, qkv, qkv, cache_k, cache_v)


ROW_ALIGN = 16
S_SLOTS = 576
BUF_ROWS = S_SLOTS + TM_MOE
XW = D_MODEL + LANES
SLOT_LANE = 6
GATE_LANE0 = 8
REGION_TILES = 28
REGION_ROWS = REGION_TILES * TM_MOE
TAB_W = 16
HALF_WIN = TM_MOE // 2
N_TOK_TILES = T_ROWS // TM


def _router_kernel(x_ref, gain_ref, sh_ref, sc_ref, wrh_ref, wrl_ref, bias_ref,
                   xs_hbm, row_ref, tab_ref, buf_ref, off_ref, wide_ref, sem):
    i = pl.program_id(0)
    ng = N_GROUPS
    last = pl.num_programs(0) - 1

    @pl.when(i == 0)
    def _():
        buf_ref[...] = jnp.zeros_like(buf_ref)
        for g in range(ng):
            off_ref[g] = 0

        def clear(r, carry):
            for c in range(TAB_W):
                tab_ref[r, c] = 0
            return carry

        lax.fori_loop(0, N_TOK_TILES + 1, clear, 0)

    hf = _norm_mod(x_ref[...], gain_ref[...], sh_ref[0], sc_ref[0])
    hb = hf.astype(BF16)
    hl = (hf - hb.astype(F32)).astype(BF16)
    wrh = wrh_ref[...]
    logits = (jnp.dot(hb, wrh, preferred_element_type=F32) + jnp.dot(hl, wrh, preferred_element_type=F32)
              + jnp.dot(hb, wrl_ref[...], preferred_element_type=F32))
    lt = logits.T[0:N_EXPERTS, :]
    scores = jax.nn.sigmoid(lt)
    sel = scores + bias_ref[...]
    xs = [sel[k * ng:(k + 1) * ng, :] for k in range(EXPERTS_PER_GROUP)]
    sc = [scores[k * ng:(k + 1) * ng, :] for k in range(EXPERTS_PER_GROUP)]
    a, b, c, d = xs
    gs = jnp.maximum(jnp.maximum(jnp.maximum(a + b, a + c), jnp.maximum(a + d, b + c)),
                     jnp.maximum(b + d, c + d))
    bv = gs[0:1, :]
    bg = jnp.zeros(bv.shape, jnp.int32)
    for g in range(1, ng):
        better = gs[g:g + 1, :] > bv
        bg = jnp.where(better, g, bg)
        bv = jnp.where(better, gs[g:g + 1, :], bv)
    giota = lax.broadcasted_iota(jnp.int32, (ng, TM), 0)
    onehot = giota == bg
    wk = []
    for k in range(EXPERTS_PER_GROUP):
        rank = jnp.zeros((ng, TM), F32)
        for j in range(EXPERTS_PER_GROUP):
            if j < k:
                rank = rank + (xs[j] >= xs[k]).astype(F32)
            elif j > k:
                rank = rank + (xs[j] > xs[k]).astype(F32)
        chosen = (rank < 2.0) & onehot
        wk.append(jnp.sum(jnp.where(chosen, sc[k], 0.0), axis=0, keepdims=True))
    den = wk[0] + wk[1] + wk[2] + wk[3]
    gates = [w / den for w in wk]

    oh = jnp.where(onehot, 1.0, 0.0)
    r = lax.broadcasted_iota(jnp.int32, (TM, TM), 0)
    cc = lax.broadcasted_iota(jnp.int32, (TM, TM), 1)
    tri = jnp.where(r < cc, 1.0, 0.0).astype(BF16)
    before = jnp.dot(oh.astype(BF16), tri, preferred_element_type=F32)
    rank_local = jnp.sum(jnp.where(onehot, before, 0.0), axis=0, keepdims=True).astype(jnp.int32)
    counts = [jnp.sum(jnp.where(bg == g, 1, 0)) for g in range(ng)]
    pads = [((cnt + ROW_ALIGN - 1) // ROW_ALIGN) * ROW_ALIGN for cnt in counts]
    segs = [0]
    for g in range(1, ng):
        segs.append(segs[-1] + pads[g - 1])
    total = segs[-1] + pads[-1]
    seg_of = jnp.zeros(bg.shape, jnp.int32)
    for g in range(1, ng):
        seg_of = jnp.where(bg == g, segs[g], seg_of)
    slot = seg_of + rank_local

    parts = []
    for gt in gates:
        hi = gt.astype(BF16).astype(F32)
        rest = gt - hi
        mid = rest.astype(BF16).astype(F32)
        parts += [hi, mid, (rest - mid).astype(BF16).astype(F32)]
    zero_row = jnp.zeros((1, TM), F32)
    info = jnp.concatenate(gates + [bg.astype(F32), zero_row, slot.astype(F32), zero_row] + parts
                           + [jnp.zeros((LANES - GATE_LANE0 - len(parts), TM), F32)], axis=0)
    rowinfo = info.T
    row_ref[...] = rowinfo

    h_ext = jnp.concatenate([hb, rowinfo.astype(BF16)], axis=1)
    srow = lax.broadcasted_iota(jnp.int32, (S_SLOTS, TM), 0)
    pick = jnp.where(srow == slot, 1.0, 0.0).astype(BF16)
    par = i % 2
    buf_ref[par, 0:S_SLOTS, :] = jnp.dot(pick, h_ext, preferred_element_type=F32).astype(BF16)

    def window_copy(g, src_row, dst_row, rows):
        return pltpu.make_async_copy(
            buf_ref.at[par, pl.ds(pl.multiple_of(src_row, ROW_ALIGN), rows), :],
            xs_hbm.at[pl.ds(pl.multiple_of(dst_row, ROW_ALIGN), rows), :],
            sem.at[g])

    def wait_windows(rows):
        for g in range(ng):
            window_copy(g, 0, 0, rows).wait()

    def wait_previous():
        @pl.when(wide_ref[0] == 1)
        def _():
            wait_windows(TM_MOE)

        @pl.when(wide_ref[0] == 0)
        def _():
            wait_windows(HALF_WIN)

    @pl.when(i > 0)
    def _():
        wait_previous()

    wide = jnp.maximum(jnp.maximum(pads[0], pads[1]), jnp.maximum(pads[2], pads[3])) > HALF_WIN
    offs = [off_ref[g] for g in range(ng)]

    def issue(rows):
        for g in range(ng):
            window_copy(g, segs[g], g * REGION_ROWS + offs[g], rows).start()

    @pl.when(wide)
    def _():
        issue(TM_MOE)

    @pl.when(jnp.logical_not(wide))
    def _():
        issue(HALF_WIN)

    wide_ref[0] = wide.astype(jnp.int32)
    for g in range(ng):
        tab_ref[i, g] = segs[g]
        tab_ref[i, ng + g] = offs[g]
        tab_ref[i, 2 * ng + 1 + g] = pads[g]
        off_ref[g] = offs[g] + pads[g]
    tab_ref[i, 2 * ng] = total

    @pl.when(i == last)
    def _():
        wait_previous()
        for g in range(ng):
            window_copy(g, S_SLOTS, g * REGION_ROWS + off_ref[g], TM_MOE).start()
            tab_ref[last + 1, g] = off_ref[g]
        wait_windows(TM_MOE)


def _router(x, gain, shift, scale, wr_hi, wr_lo, bias_col):
    return pl.pallas_call(
        _router_kernel,
        grid=(N_TOK_TILES,),
        in_specs=[
            pl.BlockSpec((TM, D_MODEL), lambda i: (i, 0)),
            pl.BlockSpec((1, D_MODEL), lambda i: (0, 0)),
            pl.BlockSpec((1, 1, D_MODEL), lambda i: (_cond_row(i), 0, 0)),
            pl.BlockSpec((1, 1, D_MODEL), lambda i: (_cond_row(i), 0, 0)),
            pl.BlockSpec((D_MODEL, LANES), lambda i: (0, 0)),
            pl.BlockSpec((D_MODEL, LANES), lambda i: (0, 0)),
            pl.BlockSpec((N_EXPERTS, 1), lambda i: (0, 0)),
        ],
        out_specs=[
            pl.BlockSpec(memory_space=pl.ANY),
            pl.BlockSpec((TM, LANES), lambda i: (i, 0)),
            pl.BlockSpec(memory_space=pltpu.SMEM),
        ],
        out_shape=[
            jax.ShapeDtypeStruct((N_GROUPS * REGION_ROWS, XW), BF16),
            jax.ShapeDtypeStruct((T_ROWS, LANES), F32),
            jax.ShapeDtypeStruct((N_TOK_TILES + 1, TAB_W), jnp.int32),
        ],
        scratch_shapes=[
            pltpu.VMEM((2, BUF_ROWS, XW), BF16),
            pltpu.SMEM((N_GROUPS,), jnp.int32),
            pltpu.SMEM((1,), jnp.int32),
            pltpu.SemaphoreType.DMA((N_GROUPS,)),
        ],
        compiler_params=_params(1),
        name="moe_router",
    )(x, gain, shift, scale, wr_hi, wr_lo, bias_col)


def _moe_up_kernel(tb_ref, tg_ref, tv_ref, xs_ref, w_ref, o_ref, wbf_ref):
    k = pl.program_id(0)
    i = pl.program_id(1)
    new_w = (i == 0) | (tg_ref[i] != tg_ref[jnp.maximum(i - 1, 0)])

    @pl.when(new_w)
    def _():
        wbf_ref[...] = w_ref[0].astype(BF16)

    @pl.when(tv_ref[i] == 1)
    def _():
        hu = jnp.dot(xs_ref[:, :D_MODEL], wbf_ref[...], preferred_element_type=F32)
        extra = xs_ref[:, D_MODEL:].astype(F32)
        lane = lax.broadcasted_iota(jnp.int32, extra.shape, 1)
        lane0 = GATE_LANE0 + 3 * k
        gcol = jnp.sum(jnp.where((lane >= lane0) & (lane < lane0 + 3), extra, 0.0), axis=1, keepdims=True)
        a = _silu(hu[:, :D_EXPERT]) * hu[:, D_EXPERT:] * gcol
        o_ref[...] = a.astype(BF16)

    @pl.when(tv_ref[i] != 1)
    def _():
        o_ref[...] = jnp.zeros_like(o_ref)


def _moe_up(tile_block, tile_group, tile_valid, xs, w_up, layer):
    return pl.pallas_call(
        _moe_up_kernel,
        grid_spec=pltpu.PrefetchScalarGridSpec(
            num_scalar_prefetch=3,
            grid=(EXPERTS_PER_GROUP, N_MOE_TILES),
            in_specs=[
                pl.BlockSpec((TM_MOE, XW), lambda k, i, tb, tg, tv: (tb[i], 0)),
                pl.BlockSpec((1, D_MODEL, 2 * D_EXPERT),
                             lambda k, i, tb, tg, tv: (layer * N_EXPERTS + tg[i] * EXPERTS_PER_GROUP + k, 0, 0)),
            ],
            out_specs=pl.BlockSpec((TM_MOE, D_EXPERT), lambda k, i, tb, tg, tv: (tb[i], k)),
            scratch_shapes=[pltpu.VMEM((D_MODEL, 2 * D_EXPERT), BF16)],
        ),
        out_shape=jax.ShapeDtypeStruct((N_GROUPS * REGION_ROWS, EXPERTS_PER_GROUP * D_EXPERT), BF16),
        compiler_params=_params(2),
        name="moe_up",
    )(tile_block, tile_group, tile_valid, xs, w_up)


def _moe_down_kernel(tb_ref, tg_ref, tv_ref, a_ref, w_ref, o_ref, wbf_ref):
    i = pl.program_id(0)
    new_w = (i == 0) | (tg_ref[i] != tg_ref[jnp.maximum(i - 1, 0)])

    @pl.when(new_w)
    def _():
        wbf_ref[...] = w_ref[0].astype(BF16)

    @pl.when(tv_ref[i] == 1)
    def _():
        o_ref[...] = jnp.dot(a_ref[...], wbf_ref[...], preferred_element_type=F32).astype(BF16)

    @pl.when(tv_ref[i] != 1)
    def _():
        o_ref[...] = jnp.zeros_like(o_ref)


def _moe_down(tile_block, tile_group, tile_valid, a, w_down_grouped, layer):
    kdim = EXPERTS_PER_GROUP * D_EXPERT
    return pl.pallas_call(
        _moe_down_kernel,
        grid_spec=pltpu.PrefetchScalarGridSpec(
            num_scalar_prefetch=3,
            grid=(N_MOE_TILES,),
            in_specs=[
                pl.BlockSpec((TM_MOE, kdim), lambda i, tb, tg, tv: (tb[i], 0)),
                pl.BlockSpec((1, kdim, D_MODEL), lambda i, tb, tg, tv: (layer * N_GROUPS + tg[i], 0, 0)),
            ],
            out_specs=pl.BlockSpec((TM_MOE, D_MODEL), lambda i, tb, tg, tv: (tb[i], 0)),
            scratch_shapes=[pltpu.VMEM((kdim, D_MODEL), BF16)],
        ),
        out_shape=jax.ShapeDtypeStruct((N_GROUPS * REGION_ROWS, D_MODEL), BF16),
        compiler_params=_params(1),
        name="moe_down",
    )(tile_block, tile_group, tile_valid, a, w_down_grouped)


def _moe_combine_kernel(tab_ref, *refs):
    first = refs[:N_GROUPS]
    second = refs[N_GROUPS:2 * N_GROUPS]
    row_ref, x_ref, g_ref, o_ref, buf_ref = refs[2 * N_GROUPS:]
    i = pl.program_id(0)

    @pl.when(i == 0)
    def _():
        buf_ref[...] = jnp.zeros_like(buf_ref)

    for g in range(N_GROUPS):
        start = pl.multiple_of(tab_ref[i * TAB_W + g], ROW_ALIGN)
        buf_ref[pl.ds(start, HALF_WIN), :] = first[g][...]

        @pl.when(tab_ref[i * TAB_W + 2 * N_GROUPS + 1 + g] > HALF_WIN)
        def _(g=g, start=start):
            buf_ref[pl.ds(start + HALF_WIN, HALF_WIN), :] = second[g][...]

    slot = row_ref[:, SLOT_LANE:SLOT_LANE + 1].astype(jnp.int32)
    scol = lax.broadcasted_iota(jnp.int32, (TM, S_SLOTS), 1)
    pick = jnp.where(scol == slot, 1.0, 0.0).astype(BF16)
    y = jnp.dot(pick, buf_ref[0:S_SLOTS, :], preferred_element_type=F32)
    o_ref[...] = x_ref[...] + g_ref[0] * y


def _moe_combine(tab_flat, ys, rowinfo, x, gate):
    def window_spec(g, second):
        def index(i, tab):
            off = tab[i * TAB_W + N_GROUPS + g]
            if second:
                off = jnp.where(tab[i * TAB_W + 2 * N_GROUPS + 1 + g] > HALF_WIN, off + HALF_WIN, 0)
            return pl.multiple_of(off + g * REGION_ROWS, ROW_ALIGN), 0

        return pl.BlockSpec((pl.Element(HALF_WIN), pl.Element(D_MODEL)), index)

    windows = [window_spec(g, False) for g in range(N_GROUPS)] + [window_spec(g, True) for g in range(N_GROUPS)]
    return pl.pallas_call(
        _moe_combine_kernel,
        grid_spec=pltpu.PrefetchScalarGridSpec(
            num_scalar_prefetch=1,
            grid=(N_TOK_TILES,),
            in_specs=windows + [
                pl.BlockSpec((TM, LANES), lambda i, tab: (i, 0)),
                pl.BlockSpec((TM, D_MODEL), lambda i, tab: (i, 0)),
                pl.BlockSpec((1, 1, D_MODEL), lambda i, tab: (_cond_row(i), 0, 0)),
            ],
            out_specs=pl.BlockSpec((TM, D_MODEL), lambda i, tab: (i, 0)),
            scratch_shapes=[pltpu.VMEM((BUF_ROWS, D_MODEL), BF16)],
        ),
        out_shape=jax.ShapeDtypeStruct((T_ROWS, D_MODEL), F32),
        compiler_params=_params(1),
        name="moe_combine",
    )(tab_flat, *([ys] * (2 * N_GROUPS)), rowinfo, x, gate)


def _moe_layer(x, gain, shift, scale, gate, router_w, w_up, w_down_grouped, layer):
    wr_hi, wr_lo, bias_col = router_w
    xs, rowinfo, tab = _router(x, gain, shift, scale, wr_hi, wr_lo, bias_col)
    n_full = (tab[N_TOK_TILES, :N_GROUPS] + TM_MOE - 1) // TM_MOE
    ends = jnp.cumsum(n_full + 1)
    starts = ends - (n_full + 1)
    entry = jnp.arange(N_MOE_TILES, dtype=jnp.int32)
    grp = jnp.minimum(jnp.sum(entry[:, None] >= ends[None, :], axis=1), N_GROUPS - 1).astype(jnp.int32)
    idx = entry - starts[grp]
    listed = entry < ends[N_GROUPS - 1]
    closing = (N_GROUPS - 1) * REGION_TILES + n_full[N_GROUPS - 1]
    tile_block = jnp.where(listed, grp * REGION_TILES + idx, closing).astype(jnp.int32)
    tile_valid = (listed & (idx < n_full[grp])).astype(jnp.int32)
    a = _moe_up(tile_block, grp, tile_valid, xs, w_up, layer)
    ys = _moe_down(tile_block, grp, tile_valid, a, w_down_grouped, layer)
    return _moe_combine(tab.reshape(-1), ys, rowinfo, x, gate)


def kernel(x_prompt, x_sample, state_ret, cache_k, cache_v, c, c_ctx, w_mod, b_mod, norm_mix,
           norm_moe, ret_w_in, ret_decay, ret_w_out, att_w_in, att_q_norm, att_k_norm, att_sink,
           att_w_out, w_router, router_bias, moe_w_up, moe_w_down):
    x = jnp.concatenate([x_prompt.reshape(CTX_ROWS, D_MODEL), x_sample.reshape(LAT_ROWS, D_MODEL)], axis=0)
    cond8 = jnp.zeros((8, D_MODEL), F32).at[0].set(c_ctx).at[1:N_COND].set(c)
    mods = _modulation_all(cond8, w_mod, b_mod)
    mods = mods[:, :N_COND].reshape(DEPTH, N_COND, N_MOD, 1, D_MODEL)

    ret_cos, ret_sin = _rope_tables(RET_DK)
    att_cos, att_sin = _rope_tables(HEAD_DIM)

    perm = jnp.arange(N_EXPERTS).reshape(N_GROUPS, EXPERTS_PER_GROUP).T.reshape(-1)
    wr = jnp.zeros((D_MODEL, LANES), F32).at[:, :N_EXPERTS].set(w_router[:, perm])
    wr_hi = wr.astype(BF16)
    wr_lo = (wr - wr_hi.astype(F32)).astype(BF16)
    bias_col = router_bias[perm].astype(F32).reshape(N_EXPERTS, 1)
    router_w = (wr_hi, wr_lo, bias_col)

    w_up_all = moe_w_up.reshape(DEPTH * N_EXPERTS, D_MODEL, 2 * D_EXPERT)
    w_down_all = moe_w_down.reshape(DEPTH * N_GROUPS, EXPERTS_PER_GROUP * D_EXPERT, D_MODEL)
    ret_w_in_bf16 = ret_w_in.astype(BF16)
    new_state = jnp.zeros((BATCH, (DEPTH + 1) // 2, 2, RET_HEADS, RET_DK, RET_DV), F32)
    cache_shape = (BATCH, DEPTH // 2, ATT_KV_HEADS, SEQ, HEAD_DIM)
    new_kv = (jnp.zeros(cache_shape, F32), jnp.zeros(cache_shape, F32))
    for layer in range(DEPTH):
        sh_a, sc_a, g_a, sh_m, sc_m, g_m = [mods[layer, :, t] for t in range(N_MOD)]
        gain_mix = norm_mix[layer].reshape(1, D_MODEL)
        gain_moe = norm_moe[layer].reshape(1, D_MODEL)
        j = layer // 2
        if layer % 2 == 0:
            proj = _ret_proj(x, gain_mix, sh_a, sc_a, ret_w_in_bf16, j, ret_cos, ret_sin)
            o_ctx, new_state = _ret_core(proj, ret_decay[j], state_ret, j, latent=False, new_state=new_state)
            (o_lat,) = _ret_core(proj, ret_decay[j], state_ret, j, latent=True)
            x = _out_proj(o_ctx, o_lat, ret_w_out, j, x, g_a)
        else:
            q_gain = jnp.tile(att_q_norm[j], LANES // HEAD_DIM).reshape(1, LANES)
            k_gain = jnp.tile(att_k_norm[j], LANES // HEAD_DIM).reshape(1, LANES)
            qkv, new_k, new_v = _att_proj(x, gain_mix, sh_a, sc_a, att_w_in, j, q_gain, k_gain,
                                          att_cos, att_sin, new_kv)
            new_kv = (new_k, new_v)
            sink = att_sink[j].astype(F32)
            o_ctx = _ctx_attention(qkv, sink)
            o_lat = _lat_attention(qkv, sink, cache_k, cache_v, j)
            x = _out_proj(o_ctx, o_lat, att_w_out, j, x, g_a)
        x = _moe_layer(x, gain_moe, sh_m, sc_m, g_m, router_w, w_up_all, w_down_all, layer)

    y_prompt = x[:CTX_ROWS].reshape(BATCH, SEQ, D_MODEL)
    y_sample = x[CTX_ROWS:].reshape(DEC_BATCH, DEC_SEQ, D_MODEL)
    return (y_prompt, y_sample, new_state, new_kv[0], new_kv[1])
```

```python
import functools

import jax
import jax.numpy as jnp
from jax import lax
from jax.experimental import pallas as pl
from jax.experimental.pallas import tpu as pltpu

F32 = jnp.float32
BF16 = jnp.bfloat16

D_MODEL = 1024
BATCH = 16
SEQ = 256
DEPTH = 4
DEC_BATCH = 2
DEC_SEQ = 4096
PAST_LEN = 512
GRID_W = 64
N_MOD = 6
EPS = 1e-6
ROPE_BASE = 10000.0
RET_HEADS = 8
RET_DK = 128
RET_DV = 256
RET_CHUNK = 128
RET_IN = 2 * RET_HEADS * RET_DK + 2 * RET_HEADS * RET_DV
ATT_HEADS = 16
ATT_KV_HEADS = 4
ATT_GROUP = 4
HEAD_DIM = 64
WINDOW = 128
ATT_BLOCK = 128
ATT_IN = (ATT_HEADS + 2 * ATT_KV_HEADS) * HEAD_DIM
ATT_SCALE = HEAD_DIM ** -0.5
N_EXPERTS = 16
N_GROUPS = 4
EXPERTS_PER_GROUP = 4
D_EXPERT = 512

CTX_ROWS = BATCH * SEQ
LAT_ROWS = DEC_BATCH * DEC_SEQ
T_ROWS = CTX_ROWS + LAT_ROWS
N_COND = 1 + DEC_BATCH

LANES = 128
TM = 512
N_CTX_TILES = CTX_ROWS // TM
N_LAT_TILES = DEC_SEQ // TM
TM_MOE = 512
N_MOE_TILES = 36
VMEM_LIMIT = 52 * 1024 * 1024

NT_DIMS = (((1,), (1,)), ((), ()))


def _params(n_axes, vmem=VMEM_LIMIT):
    return pltpu.CompilerParams(dimension_semantics=("arbitrary",) * n_axes, vmem_limit_bytes=vmem)


def _cond_row(i):
    return (i * TM) // DEC_SEQ


def _rope_row(i):
    return jnp.where(i < N_CTX_TILES, 0, 1 + (i - N_CTX_TILES) % N_LAT_TILES)


def _norm_mod(x, gain, shift, scale):
    ms = jnp.mean(x * x, axis=-1, keepdims=True)
    y = x * lax.rsqrt(ms + EPS) * gain
    return y * (1.0 + scale) + shift


def _silu(x):
    return x * jax.nn.sigmoid(x)


def _mod_kernel(c_ref, w_ref, b_ref, o_ref):
    c = c_ref[...]
    s = _silu(c).astype(BF16)
    o_ref[0] = jnp.dot(s, w_ref[0].astype(BF16), preferred_element_type=F32) + b_ref[0]


def _modulation_all(cond8, w_mod, b_mod):
    tn = 1536
    n = N_MOD * D_MODEL
    return pl.pallas_call(
        _mod_kernel,
        grid=(DEPTH, n // tn),
        in_specs=[
            pl.BlockSpec((8, D_MODEL), lambda l, j: (0, 0)),
            pl.BlockSpec((1, D_MODEL, tn), lambda l, j: (l, 0, j)),
            pl.BlockSpec((1, 1, tn), lambda l, j: (l, 0, j)),
        ],
        out_specs=pl.BlockSpec((1, 8, tn), lambda l, j: (l, 0, j)),
        out_shape=jax.ShapeDtypeStruct((DEPTH, 8, n), F32),
        compiler_params=_params(2),
        name="modulation",
    )(cond8, w_mod, b_mod.reshape(DEPTH, 1, n))


def _rope_tables(head_dim):
    half = head_dim // 2
    quarter = half // 2
    t = jnp.arange(DEC_SEQ)
    row = (t // GRID_W).astype(F32)
    col = (t % GRID_W).astype(F32)
    inv_freq = ROPE_BASE ** (-jnp.arange(quarter, dtype=F32) / quarter)
    lane = jnp.arange(LANES)
    d = lane % head_dim
    w = d % half
    f = w % quarter
    pos = jnp.where((d // half)[None, :] == 0, row[:, None], col[:, None])
    ang = pos * inv_freq[f][None, :]
    cos = jnp.cos(ang)
    sin = jnp.where((w < quarter)[None, :], -jnp.sin(ang), jnp.sin(ang))
    cos = jnp.concatenate([jnp.ones((TM, LANES), F32), cos], axis=0)
    sin = jnp.concatenate([jnp.zeros((TM, LANES), F32), sin], axis=0)
    return cos, sin


def _rope_chunk(xc, cos, sin, quarter):
    lane = lax.broadcasted_iota(jnp.int32, xc.shape, 1)
    first = (lane % (2 * quarter)) < quarter
    partner = jnp.where(first, pltpu.roll(xc, LANES - quarter, 1), pltpu.roll(xc, quarter, 1))
    return xc * cos + partner * sin


RET_TN = 1024
RET_Q_TILES = RET_HEADS * RET_DK // RET_TN
RET_QK_TILES = 2 * RET_Q_TILES


def _ret_proj_kernel(x_ref, gain_ref, sh_ref, sc_ref, w_ref, cos_ref, sin_ref, o_ref):
    h = _norm_mod(x_ref[...], gain_ref[...], sh_ref[0], sc_ref[0]).astype(BF16)
    cos = cos_ref[...]
    sin = sin_ref[...]
    for j in range(RET_IN // RET_TN):
        cols = slice(j * RET_TN, (j + 1) * RET_TN)
        acc = jnp.dot(h, w_ref[0, :, cols], preferred_element_type=F32)
        if j < RET_QK_TILES:
            scale = 1.0 if j < RET_Q_TILES else RET_DK ** -0.5
            for c in range(RET_TN // LANES):
                xc = acc[:, c * LANES:(c + 1) * LANES]
                lanes = slice(j * RET_TN + c * LANES, j * RET_TN + (c + 1) * LANES)
                o_ref[:, lanes] = (_rope_chunk(xc, cos, sin, RET_DK // 4) * scale).astype(BF16)
        else:
            o_ref[:, cols] = acc.astype(BF16)


def _ret_proj(x, gain, shift, scale, w_in_bf16, layer_j, cos, sin):
    return pl.pallas_call(
        _ret_proj_kernel,
        grid=(T_ROWS // TM,),
        in_specs=[
            pl.BlockSpec((TM, D_MODEL), lambda i: (i, 0)),
            pl.BlockSpec((1, D_MODEL), lambda i: (0, 0)),
            pl.BlockSpec((1, 1, D_MODEL), lambda i: (_cond_row(i), 0, 0)),
            pl.BlockSpec((1, 1, D_MODEL), lambda i: (_cond_row(i), 0, 0)),
            pl.BlockSpec((1, D_MODEL, RET_IN), lambda i: (layer_j, 0, 0), pipeline_mode=pl.Buffered(1)),
            pl.BlockSpec((TM, LANES), lambda i: (_rope_row(i), 0)),
            pl.BlockSpec((TM, LANES), lambda i: (_rope_row(i), 0)),
        ],
        out_specs=pl.BlockSpec((TM, RET_IN), lambda i: (i, 0)),
        out_shape=jax.ShapeDtypeStruct((T_ROWS, RET_IN), BF16),
        compiler_params=_params(1),
        name="ret_proj",
    )(x, gain, shift, scale, w_in_bf16, cos, sin)


def _log_sigmoid(x):
    return -(jnp.maximum(-x, 0.0) + jnp.log(1.0 + jnp.exp(-jnp.abs(x))))


def _ret_core_kernel(*refs, seq_len, n_seq, has_init, emit_state, n_alias):
    it = iter(refs)
    decay_ref = next(it)
    q_ref = next(it)
    k_ref = next(it)
    v_ref = next(it)
    g_ref = next(it)
    s0_ref = next(it) if has_init else None
    for _ in range(n_alias):
        next(it)
    o_ref = next(it)
    st_ref = next(it) if emit_state else None
    acc_ref = next(it)
    kv_ref = next(it)

    c = RET_CHUNK
    nc = seq_len // c
    n_chunks = n_seq * nc
    unroll = min(RET_UNROLL, n_chunks)
    head = pl.program_id(1)

    lg_all = _log_sigmoid(decay_ref[...])
    pick = lax.broadcasted_iota(jnp.int32, lg_all.shape, 1) == head
    lg = jnp.sum(jnp.where(pick, lg_all, 0.0), axis=1, keepdims=True)
    lg_f = lg[0:1, :]
    lg_b = lg[1:2, :]

    ri = lax.broadcasted_iota(jnp.int32, (c, c), 0).astype(F32)
    ci = lax.broadcasted_iota(jnp.int32, (c, c), 1).astype(F32)
    diff = ri - ci
    decay = (jnp.where(diff >= 0, jnp.exp(jnp.where(diff >= 0, diff, 0.0) * lg_f), 0.0)
             + jnp.where(diff <= 0, jnp.exp(jnp.where(diff <= 0, -diff, 0.0) * lg_b), 0.0))
    pos_col = lax.broadcasted_iota(jnp.int32, (c, 1), 0).astype(F32)
    pos_row = lax.broadcasted_iota(jnp.int32, (1, c), 1).astype(F32)
    xi_f = jnp.exp((pos_col + 1.0) * lg_f)
    xi_b = jnp.exp((c - pos_col) * lg_b)
    zeta_f = jnp.exp((c - 1.0 - pos_row) * lg_f)
    zeta_b = jnp.exp(pos_row * lg_b)
    cd_f = jnp.exp(c * lg_f)
    cd_b = jnp.exp(c * lg_b)

    def intra(n):
        r0 = pl.multiple_of(n * c, c)
        qn = q_ref[pl.ds(r0, c), :]
        kn = k_ref[pl.ds(r0, c), :]
        vn = v_ref[pl.ds(r0, c), :]
        s = lax.dot_general(qn, kn, NT_DIMS, preferred_element_type=F32) * decay
        acc_ref[pl.ds(r0, c), :] = jnp.dot(s.astype(BF16), vn, preferred_element_type=F32)
        kt = kn.astype(F32).T
        kz = jnp.concatenate([(kt * zeta_f).astype(BF16), (kt * zeta_b).astype(BF16)], axis=0)
        kv_ref[n] = jnp.dot(kz, vn, preferred_element_type=F32)

    def cross(n):
        r0 = pl.multiple_of(n * c, c)
        qn = q_ref[pl.ds(r0, c), :].astype(F32)
        qx = jnp.concatenate([(qn * xi_f).astype(BF16), (qn * xi_b).astype(BF16)], axis=1)
        o = acc_ref[pl.ds(r0, c), :] + jnp.dot(qx, kv_ref[n].astype(BF16), preferred_element_type=F32)
        mu = jnp.mean(o, axis=-1, keepdims=True)
        var = jnp.mean(jnp.square(o - mu), axis=-1, keepdims=True)
        on = (o - mu) * lax.rsqrt(var + EPS)
        gate = g_ref[pl.ds(r0, c), :].astype(F32)
        o_ref[pl.ds(r0, c), :] = (_silu(gate) * on).astype(BF16)

    def over_chunks(fn):
        def body(step, carry):
            for u in range(unroll):
                fn(step * unroll + u)
            return carry

        lax.fori_loop(0, n_chunks // unroll, body, 0)

    over_chunks(intra)

    for s in range(n_seq):
        if has_init:
            init_f = s0_ref[s, 0, 0, 0]
            init_b = s0_ref[s, 0, 1, 0]
        else:
            init_f = jnp.zeros((RET_DK, RET_DV), F32)
            init_b = init_f

        def fwd(n, state, s=s):
            kv = kv_ref[s * nc + n, 0:RET_DK, :]
            kv_ref[s * nc + n, 0:RET_DK, :] = state
            return cd_f * state + kv

        def bwd(n, state, s=s):
            m = s * nc + nc - 1 - n
            kv = kv_ref[m, RET_DK:, :]
            kv_ref[m, RET_DK:, :] = state
            return cd_b * state + kv

        final_f = lax.fori_loop(0, nc, fwd, init_f)
        final_b = lax.fori_loop(0, nc, bwd, init_b)
        if emit_state:
            st_ref[s, 0, 0, 0] = final_f
            st_ref[s, 0, 1, 0] = final_b

    over_chunks(cross)


RET_UNROLL = 16
RET_CTX_SEQS = 4


def _ret_core(proj, ret_decay_j, state_ret, layer_j, *, latent, new_state=None):
    if latent:
        nb, seq_len, n_seq, row0 = DEC_BATCH, DEC_SEQ, 1, CTX_ROWS // DEC_SEQ
    else:
        nb, seq_len, n_seq, row0 = BATCH // RET_CTX_SEQS, SEQ, RET_CTX_SEQS, 0
    rows = n_seq * seq_len
    kcol = RET_HEADS * RET_DK // RET_DK
    vcol = 2 * RET_HEADS * RET_DK // RET_DV
    gcol = vcol + RET_HEADS
    in_specs = [
        pl.BlockSpec((2, RET_HEADS), lambda b, h: (0, 0)),
        pl.BlockSpec((rows, RET_DK), lambda b, h: (row0 + b, h)),
        pl.BlockSpec((rows, RET_DK), lambda b, h: (row0 + b, kcol + h)),
        pl.BlockSpec((rows, RET_DV), lambda b, h: (row0 + b, vcol + h)),
        pl.BlockSpec((rows, RET_DV), lambda b, h: (row0 + b, gcol + h)),
    ]
    args = [ret_decay_j, proj, proj, proj, proj]
    if latent:
        in_specs.append(pl.BlockSpec((1, 1, 2, 1, RET_DK, RET_DV), lambda b, h: (b, layer_j, 0, h, 0, 0)))
        args.append(state_ret)
    aliases = {}
    if new_state is not None:
        aliases = {len(args): 1}
        in_specs.append(pl.BlockSpec(memory_space=pl.ANY))
        args.append(new_state)
    out_specs = [pl.BlockSpec((rows, RET_DV), lambda b, h: (b, h))]
    out_shape = [jax.ShapeDtypeStruct((nb * rows, RET_HEADS * RET_DV), BF16)]
    if not latent:
        out_specs.append(pl.BlockSpec((n_seq, 1, 2, 1, RET_DK, RET_DV), lambda b, h: (b, layer_j, 0, h, 0, 0)))
        out_shape.append(jax.ShapeDtypeStruct((BATCH, (DEPTH + 1) // 2, 2, RET_HEADS, RET_DK, RET_DV), F32))
    return pl.pallas_call(
        functools.partial(_ret_core_kernel, seq_len=seq_len, n_seq=n_seq, has_init=latent,
                          emit_state=not latent, n_alias=len(aliases)),
        grid=(nb, RET_HEADS),
        in_specs=in_specs,
        out_specs=out_specs,
        out_shape=out_shape,
        input_output_aliases=aliases,
        scratch_shapes=[
            pltpu.VMEM((rows, RET_DV), F32),
            pltpu.VMEM((rows // RET_CHUNK, 2 * RET_DK, RET_DV), F32),
        ],
        compiler_params=_params(2),
        name="ret_core_lat" if latent else "ret_core_ctx",
    )(*args)


def _out_proj_kernel(ac_ref, al_ref, w_ref, x_ref, g_ref, o_ref, wbf_ref):
    i = pl.program_id(0)

    @pl.when(i == 0)
    def _():
        wbf_ref[...] = w_ref[0].astype(BF16)

    def emit(a_ref):
        y = jnp.dot(a_ref[...], wbf_ref[...], preferred_element_type=F32)
        o_ref[...] = x_ref[...] + g_ref[0] * y

    @pl.when(i < N_CTX_TILES)
    def _():
        emit(ac_ref)

    @pl.when(i >= N_CTX_TILES)
    def _():
        emit(al_ref)


def _out_proj(a_ctx, a_lat, w_out, layer_j, x, gate):
    k = a_ctx.shape[1]
    return pl.pallas_call(
        _out_proj_kernel,
        grid=(T_ROWS // TM,),
        in_specs=[
            pl.BlockSpec((TM, k), lambda i: (jnp.minimum(i, N_CTX_TILES - 1), 0)),
            pl.BlockSpec((TM, k), lambda i: (jnp.maximum(i - N_CTX_TILES, 0), 0)),
            pl.BlockSpec((1, k, D_MODEL), lambda i: (layer_j, 0, 0)),
            pl.BlockSpec((TM, D_MODEL), lambda i: (i, 0)),
            pl.BlockSpec((1, 1, D_MODEL), lambda i: (_cond_row(i), 0, 0)),
        ],
        out_specs=pl.BlockSpec((TM, D_MODEL), lambda i: (i, 0)),
        out_shape=jax.ShapeDtypeStruct((T_ROWS, D_MODEL), F32),
        scratch_shapes=[pltpu.VMEM((k, D_MODEL), BF16)],
        compiler_params=_params(1),
        name="out_proj",
    )(a_ctx, a_lat, w_out, x, gate)


ATT_Q_W = ATT_HEADS * HEAD_DIM
ATT_KV_W = ATT_KV_HEADS * HEAD_DIM


def _group_mean_sq(xc, gmat):
    sq = xc * xc
    hi = sq.astype(BF16)
    lo = (sq - hi.astype(F32)).astype(BF16)
    return jnp.dot(hi, gmat, preferred_element_type=F32) + jnp.dot(lo, gmat, preferred_element_type=F32)


SEQS_PER_TILE = TM // SEQ


def _att_proj_kernel(*refs, n_alias):
    (x_ref, gain_ref, sh_ref, sc_ref, w_ref, qg_ref, kg_ref, cos_ref, sin_ref) = refs[:9]
    o_ref, kc_ref, vc_ref, wbf_ref = refs[9 + n_alias:]
    i = pl.program_id(0)

    @pl.when(i == 0)
    def _():
        wbf_ref[...] = w_ref[0].astype(BF16)

    h = _norm_mod(x_ref[...], gain_ref[...], sh_ref[0], sc_ref[0]).astype(BF16)
    acc = jnp.dot(h, wbf_ref[...], preferred_element_type=F32)

    def to_cache(cache_ref, chunk, first_head):
        @pl.when(i < N_CTX_TILES)
        def _():
            for s in range(SEQS_PER_TILE):
                for hh in range(LANES // HEAD_DIM):
                    cache_ref[s, 0, first_head + hh] = chunk[s * SEQ:(s + 1) * SEQ, hh * HEAD_DIM:(hh + 1) * HEAD_DIM]

    r = lax.broadcasted_iota(jnp.int32, (LANES, LANES), 0) // HEAD_DIM
    c = lax.broadcasted_iota(jnp.int32, (LANES, LANES), 1) // HEAD_DIM
    gmat = jnp.where(r == c, 1.0 / HEAD_DIM, 0.0).astype(BF16)
    cos = cos_ref[...]
    sin = sin_ref[...]
    n_q = ATT_Q_W // LANES
    n_kv = ATT_KV_W // LANES
    heads_per_chunk = LANES // HEAD_DIM
    for cidx in range(n_q + n_kv):
        xc = acc[:, cidx * LANES:(cidx + 1) * LANES]
        gain = qg_ref[...] if cidx < n_q else kg_ref[...]
        normed = xc * lax.rsqrt(_group_mean_sq(xc, gmat) + EPS) * gain
        if cidx >= n_q:
            to_cache(kc_ref, normed, (cidx - n_q) * heads_per_chunk)
        o_ref[:, cidx * LANES:(cidx + 1) * LANES] = _rope_chunk(normed, cos, sin, HEAD_DIM // 4).astype(BF16)
    for vidx in range(n_kv):
        v = acc[:, ATT_Q_W + ATT_KV_W + vidx * LANES:ATT_Q_W + ATT_KV_W + (vidx + 1) * LANES]
        to_cache(vc_ref, v, vidx * heads_per_chunk)
    o_ref[:, ATT_Q_W + ATT_KV_W:] = acc[:, ATT_Q_W + ATT_KV_W:].astype(BF16)


def _att_proj(x, gain, shift, scale, w_in, layer_j, q_gain, k_gain, cos, sin, caches):
    n_alias = len(caches)
    cache_shape = caches[0].shape
    cache_spec = pl.BlockSpec((SEQS_PER_TILE, 1, ATT_KV_HEADS, SEQ, HEAD_DIM),
                              lambda i: (jnp.minimum(i, N_CTX_TILES - 1), layer_j, 0, 0, 0))
    in_specs = [
        pl.BlockSpec((TM, D_MODEL), lambda i: (i, 0)),
        pl.BlockSpec((1, D_MODEL), lambda i: (0, 0)),
        pl.BlockSpec((1, 1, D_MODEL), lambda i: (_cond_row(i), 0, 0)),
        pl.BlockSpec((1, 1, D_MODEL), lambda i: (_cond_row(i), 0, 0)),
        pl.BlockSpec((1, D_MODEL, ATT_IN), lambda i: (layer_j, 0, 0)),
        pl.BlockSpec((1, LANES), lambda i: (0, 0)),
        pl.BlockSpec((1, LANES), lambda i: (0, 0)),
        pl.BlockSpec((TM, LANES), lambda i: (_rope_row(i), 0)),
        pl.BlockSpec((TM, LANES), lambda i: (_rope_row(i), 0)),
    ] + [pl.BlockSpec(memory_space=pl.ANY)] * n_alias
    args = [x, gain, shift, scale, w_in, q_gain, k_gain, cos, sin] + list(caches)
    return pl.pallas_call(
        functools.partial(_att_proj_kernel, n_alias=n_alias),
        grid=(T_ROWS // TM,),
        in_specs=in_specs,
        out_specs=[pl.BlockSpec((TM, ATT_IN), lambda i: (i, 0)), cache_spec, cache_spec],
        out_shape=[
            jax.ShapeDtypeStruct((T_ROWS, ATT_IN), BF16),
            jax.ShapeDtypeStruct(cache_shape, F32),
            jax.ShapeDtypeStruct(cache_shape, F32),
        ],
        input_output_aliases={9: 1, 10: 2},
        scratch_shapes=[pltpu.VMEM((D_MODEL, ATT_IN), BF16)],
        compiler_params=_params(1),
        name="att_proj",
    )(*args)


SINK_ROWS = 16
TN_DIMS = (((0,), (0,)), ((), ()))


def _ones_column(n):
    lane = lax.broadcasted_iota(jnp.int32, (n, HEAD_DIM), 1)
    return jnp.where(lane == 0, 1.0, 0.0).astype(BF16)


def _sink_softmax_pv(qs, sink_row, blocks):
    r = qs.shape[0]
    scores = []
    values = []
    for k, v, bias in blocks:
        s = lax.dot_general(k, qs, NT_DIMS, preferred_element_type=F32)
        scores.append(s if bias is None else s + bias)
        values.append(jnp.concatenate([v, _ones_column(v.shape[0])], axis=1))
    row = lax.broadcasted_iota(jnp.int32, (SINK_ROWS, r), 0)
    scores.append(jnp.where(row == 0, sink_row, -jnp.inf))
    values.append(jnp.concatenate([jnp.zeros((SINK_ROWS, HEAD_DIM), BF16), _ones_column(SINK_ROWS)], axis=1))
    st = jnp.concatenate(scores, axis=0)
    m = jnp.max(st, axis=0, keepdims=True)
    pt = jnp.exp(st - m).astype(BF16)
    ov = lax.dot_general(pt, jnp.concatenate(values, axis=0), TN_DIMS, preferred_element_type=F32)
    return ov[:, :HEAD_DIM] / ov[:, HEAD_DIM:HEAD_DIM + 1]


def _stack_heads(q, kvh, rows):
    parts = []
    for g in range(ATT_GROUP):
        hd = kvh * ATT_GROUP + g
        parts.append(q[:, hd * HEAD_DIM:(hd + 1) * HEAD_DIM])
    return jnp.concatenate(parts, axis=0) * jnp.asarray(ATT_SCALE, BF16)


def _sink_row(sink_ref, kvh, rows):
    head = lax.broadcasted_iota(jnp.int32, (1, ATT_GROUP * rows), 1) // rows
    out = jnp.full((1, ATT_GROUP * rows), sink_ref[kvh * ATT_GROUP], F32)
    for g in range(1, ATT_GROUP):
        out = jnp.where(head == g, sink_ref[kvh * ATT_GROUP + g], out)
    return out


def _store_heads(o_ref, o, kvh, rows):
    for g in range(ATT_GROUP):
        hd = kvh * ATT_GROUP + g
        o_ref[:, hd * HEAD_DIM:(hd + 1) * HEAD_DIM] = o[g * rows:(g + 1) * rows, :].astype(BF16)


def _ctx_att_kernel(sink_ref, q_ref, k_ref, v_ref, o_ref):
    q = q_ref[...]
    for kvh in range(ATT_KV_HEADS):
        qs = _stack_heads(q, kvh, SEQ)
        k = k_ref[:, kvh * HEAD_DIM:(kvh + 1) * HEAD_DIM]
        v = v_ref[:, kvh * HEAD_DIM:(kvh + 1) * HEAD_DIM]
        o = _sink_softmax_pv(qs, _sink_row(sink_ref, kvh, SEQ), [(k, v, None)])
        _store_heads(o_ref, o, kvh, SEQ)


def _lat_att_kernel(sink_ref, q_ref, k_ref, v_ref, ck_ref, cv_ref, o_ref):
    n = pl.program_id(1)
    nb = DEC_SEQ // ATT_BLOCK
    blk = ATT_BLOCK
    rows = ATT_GROUP * blk
    prev0 = pl.multiple_of(jnp.maximum(n - 1, 0) * blk, blk)
    cur0 = pl.multiple_of(n * blk, blk)
    next0 = pl.multiple_of(jnp.minimum(n + 1, nb - 1) * blk, blk)

    kj = lax.broadcasted_iota(jnp.int32, (3 * blk, rows), 0) - blk
    qi = lax.broadcasted_iota(jnp.int32, (3 * blk, rows), 1) % blk
    kabs = n * blk + kj
    valid = (jnp.abs(qi - kj) <= WINDOW) & (kabs >= 0) & (kabs < DEC_SEQ)
    bias = jnp.where(valid, 0.0, -jnp.inf)

    q = q_ref[...]
    for kvh in range(ATT_KV_HEADS):
        cols = slice(kvh * HEAD_DIM, (kvh + 1) * HEAD_DIM)
        qs = _stack_heads(q, kvh, blk)
        k_loc = jnp.concatenate([k_ref[pl.ds(prev0, blk), cols], k_ref[pl.ds(cur0, blk), cols],
                                 k_ref[pl.ds(next0, blk), cols]], axis=0)
        v_loc = jnp.concatenate([v_ref[pl.ds(prev0, blk), cols], v_ref[pl.ds(cur0, blk), cols],
                                 v_ref[pl.ds(next0, blk), cols]], axis=0)
        k_ctx = ck_ref[0, 0, kvh].astype(BF16)
        v_ctx = cv_ref[0, 0, kvh].astype(BF16)
        o = _sink_softmax_pv(qs, _sink_row(sink_ref, kvh, blk),
                             [(k_loc, v_loc, bias), (k_ctx, v_ctx, None)])
        _store_heads(o_ref, o, kvh, blk)


def _ctx_attention(qkv, sink):
    kcol = ATT_Q_W // ATT_KV_W
    return pl.pallas_call(
        _ctx_att_kernel,
        grid_spec=pltpu.PrefetchScalarGridSpec(
            num_scalar_prefetch=1,
            grid=(BATCH,),
            in_specs=[
                pl.BlockSpec((SEQ, ATT_Q_W), lambda b, s: (b, 0)),
                pl.BlockSpec((SEQ, ATT_KV_W), lambda b, s: (b, kcol)),
                pl.BlockSpec((SEQ, ATT_KV_W), lambda b, s: (b, kcol + 1)),
            ],
            out_specs=pl.BlockSpec((SEQ, ATT_Q_W), lambda b, s: (b, 0)),
        ),
        out_shape=jax.ShapeDtypeStruct((CTX_ROWS, ATT_Q_W), BF16),
        compiler_params=_params(1),
        name="ctx_attention",
    )(sink, qkv, qkv, qkv)


def _lat_attention(qkv, sink, cache_k, cache_v, layer_j):
    kcol = ATT_Q_W // ATT_KV_W
    nb = DEC_SEQ // ATT_BLOCK
    q0 = CTX_ROWS // ATT_BLOCK
    s0 = CTX_ROWS // DEC_SEQ
    cache_spec = pl.BlockSpec((1, 1, ATT_KV_HEADS, PAST_LEN, HEAD_DIM), lambda b, n, s: (b, layer_j, 0, 0, 0))
    return pl.pallas_call(
        _lat_att_kernel,
        grid_spec=pltpu.PrefetchScalarGridSpec(
            num_scalar_prefetch=1,
            grid=(DEC_BATCH, nb),
            in_specs=[
                pl.BlockSpec((ATT_BLOCK, ATT_Q_W), lambda b, n, s: (q0 + b * nb + n, 0)),
                pl.BlockSpec((DEC_SEQ, ATT_KV_W), lambda b, n, s: (s0 + b, kcol)),
                pl.BlockSpec((DEC_SEQ, ATT_KV_W), lambda b, n, s: (s0 + b, kcol + 1)),
                cache_spec,
                cache_spec,
            ],
            out_specs=pl.BlockSpec((ATT_BLOCK, ATT_Q_W), lambda b, n, s: (b * nb + n, 0)),
        ),
        out_shape=jax.ShapeDtypeStruct((LAT_ROWS, ATT_Q_W), BF16),
        compiler_params=_params(2),
        name="lat_attention",
    )(sink, qkv, qkv, qkv, cache_k, cache_v)


ROW_ALIGN = 16
S_SLOTS = 576
BUF_ROWS = S_SLOTS + TM_MOE
XW = D_MODEL + LANES
SLOT_LANE = 6
GATE_LANE0 = 8
REGION_TILES = 28
REGION_ROWS = REGION_TILES * TM_MOE
TAB_W = 16
HALF_WIN = TM_MOE // 2
N_TOK_TILES = T_ROWS // TM


def _router_kernel(x_ref, gain_ref, sh_ref, sc_ref, wr_ref, bias_ref,
                   xs_hbm, row_ref, tab_ref, buf_ref, off_ref, wide_ref, sem):
    i = pl.program_id(0)
    ng = N_GROUPS
    last = pl.num_programs(0) - 1

    @pl.when(i == 0)
    def _():
        buf_ref[...] = jnp.zeros_like(buf_ref)
        for g in range(ng):
            off_ref[g] = 0

        def clear(r, carry):
            for c in range(TAB_W):
                tab_ref[r, c] = 0
            return carry

        lax.fori_loop(0, N_TOK_TILES + 1, clear, 0)

    hf = _norm_mod(x_ref[...], gain_ref[...], sh_ref[0], sc_ref[0])
    hb = hf.astype(BF16)
    hl = (hf - hb.astype(F32)).astype(BF16)
    both = jnp.dot(hb, wr_ref[...], preferred_element_type=F32)
    logits = (both[:, :LANES] + both[:, LANES:]
              + jnp.dot(hl, wr_ref[:, :LANES], preferred_element_type=F32))
    lt = logits.T[0:N_EXPERTS, :]
    scores = jax.nn.sigmoid(lt)
    sel = scores + bias_ref[...]
    xs = [sel[k * ng:(k + 1) * ng, :] for k in range(EXPERTS_PER_GROUP)]
    sc = [scores[k * ng:(k + 1) * ng, :] for k in range(EXPERTS_PER_GROUP)]
    a, b, c, d = xs
    gs = jnp.maximum(jnp.maximum(jnp.maximum(a + b, a + c), jnp.maximum(a + d, b + c)),
                     jnp.maximum(b + d, c + d))
    bv = gs[0:1, :]
    bg = jnp.zeros(bv.shape, jnp.int32)
    for g in range(1, ng):
        better = gs[g:g + 1, :] > bv
        bg = jnp.where(better, g, bg)
        bv = jnp.where(better, gs[g:g + 1, :], bv)
    giota = lax.broadcasted_iota(jnp.int32, (ng, TM), 0)
    onehot = giota == bg
    wk = []
    for k in range(EXPERTS_PER_GROUP):
        rank = jnp.zeros((ng, TM), F32)
        for j in range(EXPERTS_PER_GROUP):
            if j < k:
                rank = rank + (xs[j] >= xs[k]).astype(F32)
            elif j > k:
                rank = rank + (xs[j] > xs[k]).astype(F32)
        chosen = (rank < 2.0) & onehot
        wk.append(jnp.sum(jnp.where(chosen, sc[k], 0.0), axis=0, keepdims=True))
    den = wk[0] + wk[1] + wk[2] + wk[3]
    gates = [w / den for w in wk]

    oh = jnp.where(onehot, 1.0, 0.0)
    r = lax.broadcasted_iota(jnp.int32, (TM, TM), 0)
    cc = lax.broadcasted_iota(jnp.int32, (TM, TM), 1)
    tri = jnp.where(r < cc, 1.0, 0.0).astype(BF16)
    before = jnp.dot(oh.astype(BF16), tri, preferred_element_type=F32)
    rank_local = jnp.sum(jnp.where(onehot, before, 0.0), axis=0, keepdims=True).astype(jnp.int32)
    counts = [jnp.sum(jnp.where(bg == g, 1, 0)) for g in range(ng)]
    pads = [((cnt + ROW_ALIGN - 1) // ROW_ALIGN) * ROW_ALIGN for cnt in counts]
    segs = [0]
    for g in range(1, ng):
        segs.append(segs[-1] + pads[g - 1])
    total = segs[-1] + pads[-1]
    seg_of = jnp.zeros(bg.shape, jnp.int32)
    for g in range(1, ng):
        seg_of = jnp.where(bg == g, segs[g], seg_of)
    slot = seg_of + rank_local

    parts = []
    for gt in gates:
        hi = gt.astype(BF16).astype(F32)
        rest = gt - hi
        mid = rest.astype(BF16).astype(F32)
        parts += [hi, mid, (rest - mid).astype(BF16).astype(F32)]
    zero_row = jnp.zeros((1, TM), F32)
    info = jnp.concatenate(gates + [bg.astype(F32), zero_row, slot.astype(F32), zero_row] + parts
                           + [jnp.zeros((LANES - GATE_LANE0 - len(parts), TM), F32)], axis=0)
    rowinfo = info.T
    row_ref[...] = rowinfo

    h_ext = jnp.concatenate([hb, rowinfo.astype(BF16)], axis=1)
    srow = lax.broadcasted_iota(jnp.int32, (S_SLOTS, TM), 0)
    pick = jnp.where(srow == slot, 1.0, 0.0).astype(BF16)
    par = i % 2
    buf_ref[par, 0:S_SLOTS, :] = jnp.dot(pick, h_ext, preferred_element_type=F32).astype(BF16)

    def window_copy(g, src_row, dst_row, rows):
        return pltpu.make_async_copy(
            buf_ref.at[par, pl.ds(pl.multiple_of(src_row, ROW_ALIGN), rows), :],
            xs_hbm.at[pl.ds(pl.multiple_of(dst_row, ROW_ALIGN), rows), :],
            sem.at[g])

    def wait_windows(rows):
        for g in range(ng):
            window_copy(g, 0, 0, rows).wait()

    def wait_previous():
        @pl.when(wide_ref[0] == 1)
        def _():
            wait_windows(TM_MOE)

        @pl.when(wide_ref[0] == 0)
        def _():
            wait_windows(HALF_WIN)

    @pl.when(i > 0)
    def _():
        wait_previous()

    wide = jnp.maximum(jnp.maximum(pads[0], pads[1]), jnp.maximum(pads[2], pads[3])) > HALF_WIN
    offs = [off_ref[g] for g in range(ng)]

    def issue(rows):
        for g in range(ng):
            window_copy(g, segs[g], g * REGION_ROWS + offs[g], rows).start()

    @pl.when(wide)
    def _():
        issue(TM_MOE)

    @pl.when(jnp.logical_not(wide))
    def _():
        issue(HALF_WIN)

    wide_ref[0] = wide.astype(jnp.int32)
    for g in range(ng):
        tab_ref[i, g] = segs[g]
        tab_ref[i, ng + g] = offs[g]
        tab_ref[i, 2 * ng + 1 + g] = pads[g]
        off_ref[g] = offs[g] + pads[g]
    tab_ref[i, 2 * ng] = total

    @pl.when(i == last)
    def _():
        wait_previous()
        for g in range(ng):
            window_copy(g, S_SLOTS, g * REGION_ROWS + off_ref[g], TM_MOE).start()
            tab_ref[last + 1, g] = off_ref[g]
        wait_windows(TM_MOE)


def _router(x, gain, shift, scale, wr_parts, bias_col):
    return pl.pallas_call(
        _router_kernel,
        grid=(N_TOK_TILES,),
        in_specs=[
            pl.BlockSpec((TM, D_MODEL), lambda i: (i, 0)),
            pl.BlockSpec((1, D_MODEL), lambda i: (0, 0)),
            pl.BlockSpec((1, 1, D_MODEL), lambda i: (_cond_row(i), 0, 0)),
            pl.BlockSpec((1, 1, D_MODEL), lambda i: (_cond_row(i), 0, 0)),
            pl.BlockSpec((D_MODEL, 2 * LANES), lambda i: (0, 0)),
            pl.BlockSpec((N_EXPERTS, 1), lambda i: (0, 0)),
        ],
        out_specs=[
            pl.BlockSpec(memory_space=pl.ANY),
            pl.BlockSpec((TM, LANES), lambda i: (i, 0)),
            pl.BlockSpec(memory_space=pltpu.SMEM),
        ],
        out_shape=[
            jax.ShapeDtypeStruct((N_GROUPS * REGION_ROWS, XW), BF16),
            jax.ShapeDtypeStruct((T_ROWS, LANES), F32),
            jax.ShapeDtypeStruct((N_TOK_TILES + 1, TAB_W), jnp.int32),
        ],
        scratch_shapes=[
            pltpu.VMEM((2, BUF_ROWS, XW), BF16),
            pltpu.SMEM((N_GROUPS,), jnp.int32),
            pltpu.SMEM((1,), jnp.int32),
            pltpu.SemaphoreType.DMA((N_GROUPS,)),
        ],
        compiler_params=_params(1),
        name="moe_router",
    )(x, gain, shift, scale, wr_parts, bias_col)


UP_EXPERTS = 2


def _moe_up_kernel(tb_ref, tg_ref, tv_ref, xs_ref, w_ref, o_ref, wbf_ref):
    kk = pl.program_id(0)
    i = pl.program_id(1)
    new_w = (i == 0) | (tg_ref[i] != tg_ref[jnp.maximum(i - 1, 0)])

    @pl.when(new_w)
    def _():
        wbf_ref[...] = w_ref[...].astype(BF16)

    @pl.when(tv_ref[i] == 1)
    def _():
        xs = xs_ref[:, :D_MODEL]
        extra = xs_ref[:, D_MODEL:].astype(F32)
        lane = lax.broadcasted_iota(jnp.int32, extra.shape, 1)
        for e in range(UP_EXPERTS):
            hu = jnp.dot(xs, wbf_ref[e], preferred_element_type=F32)
            lane0 = GATE_LANE0 + 3 * (kk * UP_EXPERTS + e)
            gcol = jnp.sum(jnp.where((lane >= lane0) & (lane < lane0 + 3), extra, 0.0), axis=1, keepdims=True)
            a = _silu(hu[:, :D_EXPERT]) * hu[:, D_EXPERT:] * gcol
            o_ref[:, e * D_EXPERT:(e + 1) * D_EXPERT] = a.astype(BF16)

    @pl.when(tv_ref[i] != 1)
    def _():
        o_ref[...] = jnp.zeros_like(o_ref)


def _moe_up(tile_block, tile_group, tile_valid, xs, w_up, layer):
    steps = EXPERTS_PER_GROUP // UP_EXPERTS
    return pl.pallas_call(
        _moe_up_kernel,
        grid_spec=pltpu.PrefetchScalarGridSpec(
            num_scalar_prefetch=3,
            grid=(steps, N_MOE_TILES),
            in_specs=[
                pl.BlockSpec((TM_MOE, XW), lambda k, i, tb, tg, tv: (tb[i], 0)),
                pl.BlockSpec((UP_EXPERTS, D_MODEL, 2 * D_EXPERT),
                             lambda k, i, tb, tg, tv: ((layer * N_GROUPS + tg[i]) * steps + k, 0, 0)),
            ],
            out_specs=pl.BlockSpec((TM_MOE, UP_EXPERTS * D_EXPERT), lambda k, i, tb, tg, tv: (tb[i], k)),
            scratch_shapes=[pltpu.VMEM((UP_EXPERTS, D_MODEL, 2 * D_EXPERT), BF16)],
        ),
        out_shape=jax.ShapeDtypeStruct((N_GROUPS * REGION_ROWS, EXPERTS_PER_GROUP * D_EXPERT), BF16),
        compiler_params=_params(2),
        name="moe_up",
    )(tile_block, tile_group, tile_valid, xs, w_up)


def _moe_down_kernel(tb_ref, tg_ref, tv_ref, a_ref, w_ref, o_ref, wbf_ref):
    i = pl.program_id(0)
    new_w = (i == 0) | (tg_ref[i] != tg_ref[jnp.maximum(i - 1, 0)])

    @pl.when(new_w)
    def _():
        wbf_ref[...] = w_ref[0].astype(BF16)

    @pl.when(tv_ref[i] == 1)
    def _():
        o_ref[...] = jnp.dot(a_ref[...], wbf_ref[...], preferred_element_type=F32).astype(BF16)

    @pl.when(tv_ref[i] != 1)
    def _():
        o_ref[...] = jnp.zeros_like(o_ref)


def _moe_down(tile_block, tile_group, tile_valid, a, w_down_grouped, layer):
    kdim = EXPERTS_PER_GROUP * D_EXPERT
    return pl.pallas_call(
        _moe_down_kernel,
        grid_spec=pltpu.PrefetchScalarGridSpec(
            num_scalar_prefetch=3,
            grid=(N_MOE_TILES,),
            in_specs=[
                pl.BlockSpec((TM_MOE, kdim), lambda i, tb, tg, tv: (tb[i], 0)),
                pl.BlockSpec((1, kdim, D_MODEL), lambda i, tb, tg, tv: (layer * N_GROUPS + tg[i], 0, 0)),
            ],
            out_specs=pl.BlockSpec((TM_MOE, D_MODEL), lambda i, tb, tg, tv: (tb[i], 0)),
            scratch_shapes=[pltpu.VMEM((kdim, D_MODEL), BF16)],
        ),
        out_shape=jax.ShapeDtypeStruct((N_GROUPS * REGION_ROWS, D_MODEL), BF16),
        compiler_params=_params(1),
        name="moe_down",
    )(tile_block, tile_group, tile_valid, a, w_down_grouped)


def _moe_combine_kernel(tab_ref, *refs, split):
    first = refs[:N_GROUPS]
    second = refs[N_GROUPS:2 * N_GROUPS]
    row_ref, x_ref, g_ref = refs[2 * N_GROUPS:2 * N_GROUPS + 3]
    outs = refs[2 * N_GROUPS + 3:-1]
    buf_ref = refs[-1]
    i = pl.program_id(0)

    @pl.when(i == 0)
    def _():
        buf_ref[...] = jnp.zeros_like(buf_ref)

    for g in range(N_GROUPS):
        start = pl.multiple_of(tab_ref[i * TAB_W + g], ROW_ALIGN)
        buf_ref[pl.ds(start, HALF_WIN), :] = first[g][...]

        @pl.when(tab_ref[i * TAB_W + 2 * N_GROUPS + 1 + g] > HALF_WIN)
        def _(g=g, start=start):
            buf_ref[pl.ds(start + HALF_WIN, HALF_WIN), :] = second[g][...]

    slot = row_ref[:, SLOT_LANE:SLOT_LANE + 1].astype(jnp.int32)
    scol = lax.broadcasted_iota(jnp.int32, (TM, S_SLOTS), 1)
    pick = jnp.where(scol == slot, 1.0, 0.0).astype(BF16)
    y = jnp.dot(pick, buf_ref[0:S_SLOTS, :], preferred_element_type=F32)
    new_x = x_ref[...] + g_ref[0] * y
    if split:
        @pl.when(i < N_CTX_TILES)
        def _():
            outs[0][...] = new_x

        @pl.when(i >= N_CTX_TILES)
        def _():
            outs[1][...] = new_x
    else:
        outs[0][...] = new_x


def _moe_combine(tab_flat, ys, rowinfo, x, gate, split=False):
    if split:
        out_specs = [pl.BlockSpec((TM, D_MODEL), lambda i, tab: (jnp.minimum(i, N_CTX_TILES - 1), 0)),
                     pl.BlockSpec((TM, D_MODEL), lambda i, tab: (jnp.maximum(i - N_CTX_TILES, 0), 0))]
        out_shape = [jax.ShapeDtypeStruct((CTX_ROWS, D_MODEL), F32), jax.ShapeDtypeStruct((LAT_ROWS, D_MODEL), F32)]
    else:
        out_specs = [pl.BlockSpec((TM, D_MODEL), lambda i, tab: (i, 0))]
        out_shape = [jax.ShapeDtypeStruct((T_ROWS, D_MODEL), F32)]

    def window_spec(g, second):
        def index(i, tab):
            off = tab[i * TAB_W + N_GROUPS + g]
            if second:
                off = jnp.where(tab[i * TAB_W + 2 * N_GROUPS + 1 + g] > HALF_WIN, off + HALF_WIN, 0)
            return pl.multiple_of(off + g * REGION_ROWS, ROW_ALIGN), 0

        return pl.BlockSpec((pl.Element(HALF_WIN), pl.Element(D_MODEL)), index)

    windows = [window_spec(g, False) for g in range(N_GROUPS)] + [window_spec(g, True) for g in range(N_GROUPS)]
    outs = pl.pallas_call(
        functools.partial(_moe_combine_kernel, split=split),
        grid_spec=pltpu.PrefetchScalarGridSpec(
            num_scalar_prefetch=1,
            grid=(N_TOK_TILES,),
            in_specs=windows + [
                pl.BlockSpec((TM, LANES), lambda i, tab: (i, 0)),
                pl.BlockSpec((TM, D_MODEL), lambda i, tab: (i, 0)),
                pl.BlockSpec((1, 1, D_MODEL), lambda i, tab: (_cond_row(i), 0, 0)),
            ],
            out_specs=out_specs,
            scratch_shapes=[pltpu.VMEM((BUF_ROWS, D_MODEL), BF16)],
        ),
        out_shape=out_shape,
        compiler_params=_params(1),
        name="moe_combine",
    )(tab_flat, *([ys] * (2 * N_GROUPS)), rowinfo, x, gate)
    return tuple(outs) if split else outs[0]


def _moe_layer(x, gain, shift, scale, gate, router_w, w_up, w_down_grouped, layer):
    wr_parts, bias_col = router_w
    xs, rowinfo, tab = _router(x, gain, shift, scale, wr_parts, bias_col)
    n_full = (tab[N_TOK_TILES, :N_GROUPS] + TM_MOE - 1) // TM_MOE
    ends = jnp.cumsum(n_full + 1)
    starts = ends - (n_full + 1)
    entry = jnp.arange(N_MOE_TILES, dtype=jnp.int32)
    grp = jnp.minimum(jnp.sum(entry[:, None] >= ends[None, :], axis=1), N_GROUPS - 1).astype(jnp.int32)
    idx = entry - starts[grp]
    listed = entry < ends[N_GROUPS - 1]
    closing = (N_GROUPS - 1) * REGION_TILES + n_full[N_GROUPS - 1]
    tile_block = jnp.where(listed, grp * REGION_TILES + idx, closing).astype(jnp.int32)
    tile_valid = (listed & (idx < n_full[grp])).astype(jnp.int32)
    a = _moe_up(tile_block, grp, tile_valid, xs, w_up, layer)
    ys = _moe_down(tile_block, grp, tile_valid, a, w_down_grouped, layer)
    return _moe_combine(tab.reshape(-1), ys, rowinfo, x, gate, split=layer == DEPTH - 1)


def kernel(x_prompt, x_sample, state_ret, cache_k, cache_v, c, c_ctx, w_mod, b_mod, norm_mix,
           norm_moe, ret_w_in, ret_decay, ret_w_out, att_w_in, att_q_norm, att_k_norm, att_sink,
           att_w_out, w_router, router_bias, moe_w_up, moe_w_down):
    x = jnp.concatenate([x_prompt.reshape(CTX_ROWS, D_MODEL), x_sample.reshape(LAT_ROWS, D_MODEL)], axis=0)
    cond8 = jnp.zeros((8, D_MODEL), F32).at[0].set(c_ctx).at[1:N_COND].set(c)
    mods = _modulation_all(cond8, w_mod, b_mod)
    mods = mods[:, :N_COND].reshape(DEPTH, N_COND, N_MOD, 1, D_MODEL)

    ret_cos, ret_sin = _rope_tables(RET_DK)
    att_cos, att_sin = _rope_tables(HEAD_DIM)

    perm = jnp.arange(N_EXPERTS).reshape(N_GROUPS, EXPERTS_PER_GROUP).T.reshape(-1)
    wr = jnp.zeros((D_MODEL, LANES), F32).at[:, :N_EXPERTS].set(w_router[:, perm])
    wr_hi = wr.astype(BF16)
    wr_lo = (wr - wr_hi.astype(F32)).astype(BF16)
    bias_col = router_bias[perm].astype(F32).reshape(N_EXPERTS, 1)
    router_w = (jnp.concatenate([wr_hi, wr_lo], axis=1), bias_col)

    w_up_all = moe_w_up.reshape(DEPTH * N_EXPERTS, D_MODEL, 2 * D_EXPERT)
    w_down_all = moe_w_down.reshape(DEPTH * N_GROUPS, EXPERTS_PER_GROUP * D_EXPERT, D_MODEL)
    ret_w_in_bf16 = ret_w_in.astype(BF16)
    new_state = jnp.zeros((BATCH, (DEPTH + 1) // 2, 2, RET_HEADS, RET_DK, RET_DV), F32)
    cache_shape = (BATCH, DEPTH // 2, ATT_KV_HEADS, SEQ, HEAD_DIM)
    new_kv = (jnp.zeros(cache_shape, F32), jnp.zeros(cache_shape, F32))
    for layer in range(DEPTH):
        sh_a, sc_a, g_a, sh_m, sc_m, g_m = [mods[layer, :, t] for t in range(N_MOD)]
        gain_mix = norm_mix[layer].reshape(1, D_MODEL)
        gain_moe = norm_moe[layer].reshape(1, D_MODEL)
        j = layer // 2
        if layer % 2 == 0:
            proj = _ret_proj(x, gain_mix, sh_a, sc_a, ret_w_in_bf16, j, ret_cos, ret_sin)
            o_ctx, new_state = _ret_core(proj, ret_decay[j], state_ret, j, latent=False, new_state=new_state)
            (o_lat,) = _ret_core(proj, ret_decay[j], state_ret, j, latent=True)
            x = _out_proj(o_ctx, o_lat, ret_w_out, j, x, g_a)
        else:
            q_gain = jnp.tile(att_q_norm[j], LANES // HEAD_DIM).reshape(1, LANES)
            k_gain = jnp.tile(att_k_norm[j], LANES // HEAD_DIM).reshape(1, LANES)
            qkv, new_k, new_v = _att_proj(x, gain_mix, sh_a, sc_a, att_w_in, j, q_gain, k_gain,
                                          att_cos, att_sin, new_kv)
            new_kv = (new_k, new_v)
            sink = att_sink[j].astype(F32)
            o_ctx = _ctx_attention(qkv, sink)
            o_lat = _lat_attention(qkv, sink, cache_k, cache_v, j)
            x = _out_proj(o_ctx, o_lat, att_w_out, j, x, g_a)
        x = _moe_layer(x, gain_moe, sh_m, sc_m, g_m, router_w, w_up_all, w_down_all, layer)

    x_ctx, x_lat = x
    y_prompt = x_ctx.reshape(BATCH, SEQ, D_MODEL)
    y_sample = x_lat.reshape(DEC_BATCH, DEC_SEQ, D_MODEL)
    return (y_prompt, y_sample, new_state, new_kv[0], new_kv[1])
```

```python
import functools

import jax
import jax.numpy as jnp
from jax import lax
from jax.experimental import pallas as pl
from jax.experimental.pallas import tpu as pltpu

F32 = jnp.float32
BF16 = jnp.bfloat16

D_MODEL = 1024
BATCH = 16
SEQ = 256
DEPTH = 4
DEC_BATCH = 2
DEC_SEQ = 4096
PAST_LEN = 512
GRID_W = 64
N_MOD = 6
EPS = 1e-6
ROPE_BASE = 10000.0
RET_HEADS = 8
RET_DK = 128
RET_DV = 256
RET_CHUNK = 128
RET_IN = 2 * RET_HEADS * RET_DK + 2 * RET_HEADS * RET_DV
ATT_HEADS = 16
ATT_KV_HEADS = 4
ATT_GROUP = 4
HEAD_DIM = 64
WINDOW = 128
ATT_BLOCK = 128
ATT_IN = (ATT_HEADS + 2 * ATT_KV_HEADS) * HEAD_DIM
ATT_SCALE = HEAD_DIM ** -0.5
N_EXPERTS = 16
N_GROUPS = 4
EXPERTS_PER_GROUP = 4
D_EXPERT = 512

CTX_ROWS = BATCH * SEQ
LAT_ROWS = DEC_BATCH * DEC_SEQ
T_ROWS = CTX_ROWS + LAT_ROWS
N_COND = 1 + DEC_BATCH

LANES = 128
TM = 512
N_CTX_TILES = CTX_ROWS // TM
N_LAT_TILES = DEC_SEQ // TM
TM_MOE = 512
N_MOE_TILES = 36
VMEM_LIMIT = 52 * 1024 * 1024

NT_DIMS = (((1,), (1,)), ((), ()))


def _params(n_axes, vmem=VMEM_LIMIT):
    return pltpu.CompilerParams(dimension_semantics=("arbitrary",) * n_axes, vmem_limit_bytes=vmem)


def _cond_row(i):
    return (i * TM) // DEC_SEQ


def _rope_row(i):
    return jnp.where(i < N_CTX_TILES, 0, 1 + (i - N_CTX_TILES) % N_LAT_TILES)


def _norm_mod(x, gain, shift, scale):
    ms = jnp.mean(x * x, axis=-1, keepdims=True)
    y = x * lax.rsqrt(ms + EPS) * gain
    return y * (1.0 + scale) + shift


def _silu(x):
    return x * jax.nn.sigmoid(x)


def _mod_kernel(c_ref, w_ref, b_ref, o_ref):
    c = c_ref[...]
    s = _silu(c).astype(BF16)
    o_ref[0] = jnp.dot(s, w_ref[0].astype(BF16), preferred_element_type=F32) + b_ref[0]


def _modulation_all(cond8, w_mod, b_mod):
    tn = 1536
    n = N_MOD * D_MODEL
    return pl.pallas_call(
        _mod_kernel,
        grid=(DEPTH, n // tn),
        in_specs=[
            pl.BlockSpec((8, D_MODEL), lambda l, j: (0, 0)),
            pl.BlockSpec((1, D_MODEL, tn), lambda l, j: (l, 0, j)),
            pl.BlockSpec((1, 1, tn), lambda l, j: (l, 0, j)),
        ],
        out_specs=pl.BlockSpec((1, 8, tn), lambda l, j: (l, 0, j)),
        out_shape=jax.ShapeDtypeStruct((DEPTH, 8, n), F32),
        compiler_params=_params(2),
        name="modulation",
    )(cond8, w_mod, b_mod.reshape(DEPTH, 1, n))


def _rope_tables(head_dim):
    half = head_dim // 2
    quarter = half // 2
    t = jnp.arange(DEC_SEQ)
    row = (t // GRID_W).astype(F32)
    col = (t % GRID_W).astype(F32)
    inv_freq = ROPE_BASE ** (-jnp.arange(quarter, dtype=F32) / quarter)
    lane = jnp.arange(LANES)
    d = lane % head_dim
    w = d % half
    f = w % quarter
    pos = jnp.where((d // half)[None, :] == 0, row[:, None], col[:, None])
    ang = pos * inv_freq[f][None, :]
    cos = jnp.cos(ang)
    sin = jnp.where((w < quarter)[None, :], -jnp.sin(ang), jnp.sin(ang))
    cos = jnp.concatenate([jnp.ones((TM, LANES), F32), cos], axis=0)
    sin = jnp.concatenate([jnp.zeros((TM, LANES), F32), sin], axis=0)
    return cos, sin


def _rope_chunk(xc, cos, sin, quarter):
    lane = lax.broadcasted_iota(jnp.int32, xc.shape, 1)
    first = (lane % (2 * quarter)) < quarter
    partner = jnp.where(first, pltpu.roll(xc, LANES - quarter, 1), pltpu.roll(xc, quarter, 1))
    return xc * cos + partner * sin


RET_TN = 1024
RET_Q_TILES = RET_HEADS * RET_DK // RET_TN
RET_QK_TILES = 2 * RET_Q_TILES


def _ret_proj_kernel(x_ref, gain_ref, sh_ref, sc_ref, w_ref, cos_ref, sin_ref, o_ref):
    h = _norm_mod(x_ref[...], gain_ref[...], sh_ref[0], sc_ref[0]).astype(BF16)
    cos = cos_ref[...]
    sin = sin_ref[...]
    for j in range(RET_IN // RET_TN):
        cols = slice(j * RET_TN, (j + 1) * RET_TN)
        acc = jnp.dot(h, w_ref[0, :, cols], preferred_element_type=F32)
        if j < RET_QK_TILES:
            scale = 1.0 if j < RET_Q_TILES else RET_DK ** -0.5
            for c in range(RET_TN // LANES):
                xc = acc[:, c * LANES:(c + 1) * LANES]
                lanes = slice(j * RET_TN + c * LANES, j * RET_TN + (c + 1) * LANES)
                o_ref[:, lanes] = (_rope_chunk(xc, cos, sin, RET_DK // 4) * scale).astype(BF16)
        else:
            o_ref[:, cols] = acc.astype(BF16)


def _ret_proj(x, gain, shift, scale, w_in_bf16, layer_j, cos, sin):
    return pl.pallas_call(
        _ret_proj_kernel,
        grid=(T_ROWS // TM,),
        in_specs=[
            pl.BlockSpec((TM, D_MODEL), lambda i: (i, 0)),
            pl.BlockSpec((1, D_MODEL), lambda i: (0, 0)),
            pl.BlockSpec((1, 1, D_MODEL), lambda i: (_cond_row(i), 0, 0)),
            pl.BlockSpec((1, 1, D_MODEL), lambda i: (_cond_row(i), 0, 0)),
            pl.BlockSpec((1, D_MODEL, RET_IN), lambda i: (layer_j, 0, 0), pipeline_mode=pl.Buffered(1)),
            pl.BlockSpec((TM, LANES), lambda i: (_rope_row(i), 0)),
            pl.BlockSpec((TM, LANES), lambda i: (_rope_row(i), 0)),
        ],
        out_specs=pl.BlockSpec((TM, RET_IN), lambda i: (i, 0)),
        out_shape=jax.ShapeDtypeStruct((T_ROWS, RET_IN), BF16),
        compiler_params=_params(1),
        name="ret_proj",
    )(x, gain, shift, scale, w_in_bf16, cos, sin)


def _log_sigmoid(x):
    return -(jnp.maximum(-x, 0.0) + jnp.log(1.0 + jnp.exp(-jnp.abs(x))))


def _ret_core_kernel(*refs, seq_len, n_seq, has_init, emit_state, n_alias):
    it = iter(refs)
    decay_ref = next(it)
    q_ref = next(it)
    k_ref = next(it)
    v_ref = next(it)
    g_ref = next(it)
    s0_ref = next(it) if has_init else None
    for _ in range(n_alias):
        next(it)
    o_ref = next(it)
    st_ref = next(it) if emit_state else None
    acc_ref = next(it)
    kv_ref = next(it)

    c = RET_CHUNK
    nc = seq_len // c
    n_chunks = n_seq * nc
    unroll = min(RET_UNROLL, n_chunks)
    head = pl.program_id(1)

    lg_all = _log_sigmoid(decay_ref[...])
    pick = lax.broadcasted_iota(jnp.int32, lg_all.shape, 1) == head
    lg = jnp.sum(jnp.where(pick, lg_all, 0.0), axis=1, keepdims=True)
    lg_f = lg[0:1, :]
    lg_b = lg[1:2, :]

    ri = lax.broadcasted_iota(jnp.int32, (c, c), 0).astype(F32)
    ci = lax.broadcasted_iota(jnp.int32, (c, c), 1).astype(F32)
    diff = ri - ci
    decay = (jnp.where(diff >= 0, jnp.exp(jnp.where(diff >= 0, diff, 0.0) * lg_f), 0.0)
             + jnp.where(diff <= 0, jnp.exp(jnp.where(diff <= 0, -diff, 0.0) * lg_b), 0.0))
    pos_col = lax.broadcasted_iota(jnp.int32, (c, 1), 0).astype(F32)
    pos_row = lax.broadcasted_iota(jnp.int32, (1, c), 1).astype(F32)
    xi_f = jnp.exp((pos_col + 1.0) * lg_f)
    xi_b = jnp.exp((c - pos_col) * lg_b)
    zeta_f = jnp.exp((c - 1.0 - pos_row) * lg_f)
    zeta_b = jnp.exp(pos_row * lg_b)
    cd_f = jnp.exp(c * lg_f)
    cd_b = jnp.exp(c * lg_b)

    def intra(n):
        r0 = pl.multiple_of(n * c, c)
        qn = q_ref[pl.ds(r0, c), :]
        kn = k_ref[pl.ds(r0, c), :]
        vn = v_ref[pl.ds(r0, c), :]
        s = lax.dot_general(qn, kn, NT_DIMS, preferred_element_type=F32) * decay
        acc_ref[pl.ds(r0, c), :] = jnp.dot(s.astype(BF16), vn, preferred_element_type=F32)
        kt = kn.astype(F32).T
        kz = jnp.concatenate([(kt * zeta_f).astype(BF16), (kt * zeta_b).astype(BF16)], axis=0)
        kv_ref[n] = jnp.dot(kz, vn, preferred_element_type=F32)

    def cross(n):
        r0 = pl.multiple_of(n * c, c)
        qn = q_ref[pl.ds(r0, c), :].astype(F32)
        qx = jnp.concatenate([(qn * xi_f).astype(BF16), (qn * xi_b).astype(BF16)], axis=1)
        o = acc_ref[pl.ds(r0, c), :] + jnp.dot(qx, kv_ref[n].astype(BF16), preferred_element_type=F32)
        mu = jnp.mean(o, axis=-1, keepdims=True)
        var = jnp.mean(jnp.square(o - mu), axis=-1, keepdims=True)
        on = (o - mu) * lax.rsqrt(var + EPS)
        gate = g_ref[pl.ds(r0, c), :].astype(F32)
        o_ref[pl.ds(r0, c), :] = (_silu(gate) * on).astype(BF16)

    def over_chunks(fn):
        def body(step, carry):
            for u in range(unroll):
                fn(step * unroll + u)
            return carry

        lax.fori_loop(0, n_chunks // unroll, body, 0)

    over_chunks(intra)

    for s in range(n_seq):
        if has_init:
            init_f = s0_ref[s, 0, 0, 0]
            init_b = s0_ref[s, 0, 1, 0]
        else:
            init_f = jnp.zeros((RET_DK, RET_DV), F32)
            init_b = init_f

        def fwd(n, state, s=s):
            kv = kv_ref[s * nc + n, 0:RET_DK, :]
            kv_ref[s * nc + n, 0:RET_DK, :] = state
            return cd_f * state + kv

        def bwd(n, state, s=s):
            m = s * nc + nc - 1 - n
            kv = kv_ref[m, RET_DK:, :]
            kv_ref[m, RET_DK:, :] = state
            return cd_b * state + kv

        final_f = lax.fori_loop(0, nc, fwd, init_f)
        final_b = lax.fori_loop(0, nc, bwd, init_b)
        if emit_state:
            st_ref[s, 0, 0, 0] = final_f
            st_ref[s, 0, 1, 0] = final_b

    over_chunks(cross)


RET_UNROLL = 16
RET_CTX_SEQS = 4


def _ret_core(proj, ret_decay_j, state_ret, layer_j, *, latent, new_state=None):
    if latent:
        nb, seq_len, n_seq, row0 = DEC_BATCH, DEC_SEQ, 1, CTX_ROWS // DEC_SEQ
    else:
        nb, seq_len, n_seq, row0 = BATCH // RET_CTX_SEQS, SEQ, RET_CTX_SEQS, 0
    rows = n_seq * seq_len
    kcol = RET_HEADS * RET_DK // RET_DK
    vcol = 2 * RET_HEADS * RET_DK // RET_DV
    gcol = vcol + RET_HEADS
    in_specs = [
        pl.BlockSpec((2, RET_HEADS), lambda b, h: (0, 0)),
        pl.BlockSpec((rows, RET_DK), lambda b, h: (row0 + b, h)),
        pl.BlockSpec((rows, RET_DK), lambda b, h: (row0 + b, kcol + h)),
        pl.BlockSpec((rows, RET_DV), lambda b, h: (row0 + b, vcol + h)),
        pl.BlockSpec((rows, RET_DV), lambda b, h: (row0 + b, gcol + h)),
    ]
    args = [ret_decay_j, proj, proj, proj, proj]
    if latent:
        in_specs.append(pl.BlockSpec((1, 1, 2, 1, RET_DK, RET_DV), lambda b, h: (b, layer_j, 0, h, 0, 0)))
        args.append(state_ret)
    aliases = {}
    if new_state is not None:
        aliases = {len(args): 1}
        in_specs.append(pl.BlockSpec(memory_space=pl.ANY))
        args.append(new_state)
    out_specs = [pl.BlockSpec((rows, RET_DV), lambda b, h: (b, h))]
    out_shape = [jax.ShapeDtypeStruct((nb * rows, RET_HEADS * RET_DV), BF16)]
    if not latent:
        out_specs.append(pl.BlockSpec((n_seq, 1, 2, 1, RET_DK, RET_DV), lambda b, h: (b, layer_j, 0, h, 0, 0)))
        out_shape.append(jax.ShapeDtypeStruct((BATCH, (DEPTH + 1) // 2, 2, RET_HEADS, RET_DK, RET_DV), F32))
    return pl.pallas_call(
        functools.partial(_ret_core_kernel, seq_len=seq_len, n_seq=n_seq, has_init=latent,
                          emit_state=not latent, n_alias=len(aliases)),
        grid=(nb, RET_HEADS),
        in_specs=in_specs,
        out_specs=out_specs,
        out_shape=out_shape,
        input_output_aliases=aliases,
        scratch_shapes=[
            pltpu.VMEM((rows, RET_DV), F32),
            pltpu.VMEM((rows // RET_CHUNK, 2 * RET_DK, RET_DV), F32),
        ],
        compiler_params=_params(2),
        name="ret_core_lat" if latent else "ret_core_ctx",
    )(*args)


ATT_Q_W = ATT_HEADS * HEAD_DIM
ATT_KV_W = ATT_KV_HEADS * HEAD_DIM


def _group_mean_sq(xc, gmat):
    sq = xc * xc
    hi = sq.astype(BF16)
    lo = (sq - hi.astype(F32)).astype(BF16)
    return jnp.dot(hi, gmat, preferred_element_type=F32) + jnp.dot(lo, gmat, preferred_element_type=F32)


SEQS_PER_TILE = TM // SEQ


def _att_proj_kernel(*refs, n_alias):
    (x_ref, gain_ref, sh_ref, sc_ref, w_ref, qg_ref, kg_ref, cos_ref, sin_ref) = refs[:9]
    o_ref, kc_ref, vc_ref, wbf_ref = refs[9 + n_alias:]
    i = pl.program_id(0)

    @pl.when(i == 0)
    def _():
        wbf_ref[...] = w_ref[0].astype(BF16)

    h = _norm_mod(x_ref[...], gain_ref[...], sh_ref[0], sc_ref[0]).astype(BF16)
    acc = jnp.dot(h, wbf_ref[...], preferred_element_type=F32)

    def to_cache(cache_ref, chunk, first_head):
        @pl.when(i < N_CTX_TILES)
        def _():
            for s in range(SEQS_PER_TILE):
                for hh in range(LANES // HEAD_DIM):
                    cache_ref[s, 0, first_head + hh] = chunk[s * SEQ:(s + 1) * SEQ, hh * HEAD_DIM:(hh + 1) * HEAD_DIM]

    r = lax.broadcasted_iota(jnp.int32, (LANES, LANES), 0) // HEAD_DIM
    c = lax.broadcasted_iota(jnp.int32, (LANES, LANES), 1) // HEAD_DIM
    gmat = jnp.where(r == c, 1.0 / HEAD_DIM, 0.0).astype(BF16)
    cos = cos_ref[...]
    sin = sin_ref[...]
    n_q = ATT_Q_W // LANES
    n_kv = ATT_KV_W // LANES
    heads_per_chunk = LANES // HEAD_DIM
    for cidx in range(n_q + n_kv):
        xc = acc[:, cidx * LANES:(cidx + 1) * LANES]
        gain = qg_ref[...] if cidx < n_q else kg_ref[...]
        normed = xc * lax.rsqrt(_group_mean_sq(xc, gmat) + EPS) * gain
        if cidx >= n_q:
            to_cache(kc_ref, normed, (cidx - n_q) * heads_per_chunk)
        o_ref[:, cidx * LANES:(cidx + 1) * LANES] = _rope_chunk(normed, cos, sin, HEAD_DIM // 4).astype(BF16)
    for vidx in range(n_kv):
        v = acc[:, ATT_Q_W + ATT_KV_W + vidx * LANES:ATT_Q_W + ATT_KV_W + (vidx + 1) * LANES]
        to_cache(vc_ref, v, vidx * heads_per_chunk)
    o_ref[:, ATT_Q_W + ATT_KV_W:] = acc[:, ATT_Q_W + ATT_KV_W:].astype(BF16)


def _att_proj(x, gain, shift, scale, w_in, layer_j, q_gain, k_gain, cos, sin, caches):
    n_alias = len(caches)
    cache_shape = caches[0].shape
    cache_spec = pl.BlockSpec((SEQS_PER_TILE, 1, ATT_KV_HEADS, SEQ, HEAD_DIM),
                              lambda i: (jnp.minimum(i, N_CTX_TILES - 1), layer_j, 0, 0, 0))
    in_specs = [
        pl.BlockSpec((TM, D_MODEL), lambda i: (i, 0)),
        pl.BlockSpec((1, D_MODEL), lambda i: (0, 0)),
        pl.BlockSpec((1, 1, D_MODEL), lambda i: (_cond_row(i), 0, 0)),
        pl.BlockSpec((1, 1, D_MODEL), lambda i: (_cond_row(i), 0, 0)),
        pl.BlockSpec((1, D_MODEL, ATT_IN), lambda i: (layer_j, 0, 0)),
        pl.BlockSpec((1, LANES), lambda i: (0, 0)),
        pl.BlockSpec((1, LANES), lambda i: (0, 0)),
        pl.BlockSpec((TM, LANES), lambda i: (_rope_row(i), 0)),
        pl.BlockSpec((TM, LANES), lambda i: (_rope_row(i), 0)),
    ] + [pl.BlockSpec(memory_space=pl.ANY)] * n_alias
    args = [x, gain, shift, scale, w_in, q_gain, k_gain, cos, sin] + list(caches)
    return pl.pallas_call(
        functools.partial(_att_proj_kernel, n_alias=n_alias),
        grid=(T_ROWS // TM,),
        in_specs=in_specs,
        out_specs=[pl.BlockSpec((TM, ATT_IN), lambda i: (i, 0)), cache_spec, cache_spec],
        out_shape=[
            jax.ShapeDtypeStruct((T_ROWS, ATT_IN), BF16),
            jax.ShapeDtypeStruct(cache_shape, F32),
            jax.ShapeDtypeStruct(cache_shape, F32),
        ],
        input_output_aliases={9: 1, 10: 2},
        scratch_shapes=[pltpu.VMEM((D_MODEL, ATT_IN), BF16)],
        compiler_params=_params(1),
        name="att_proj",
    )(*args)


SINK_ROWS = 16
TN_DIMS = (((0,), (0,)), ((), ()))


def _ones_column(n):
    lane = lax.broadcasted_iota(jnp.int32, (n, HEAD_DIM), 1)
    return jnp.where(lane == 0, 1.0, 0.0).astype(BF16)


def _sink_softmax_pv(qs, sink_row, blocks):
    r = qs.shape[0]
    q_scaled = qs * jnp.asarray(ATT_SCALE, BF16)
    scores = []
    values = []
    for k, v, bias in blocks:
        s = lax.dot_general(k, q_scaled, NT_DIMS, preferred_element_type=F32)
        scores.append(s if bias is None else s + bias)
        values.append(jnp.concatenate([v, _ones_column(v.shape[0])], axis=1))
    row = lax.broadcasted_iota(jnp.int32, (SINK_ROWS, r), 0)
    scores.append(jnp.where(row == 0, sink_row, -jnp.inf))
    values.append(jnp.concatenate([jnp.zeros((SINK_ROWS, HEAD_DIM), BF16), _ones_column(SINK_ROWS)], axis=1))
    st = jnp.concatenate(scores, axis=0)
    m = jnp.max(st, axis=0, keepdims=True)
    pt = jnp.exp(st - m).astype(BF16)
    ov = lax.dot_general(pt, jnp.concatenate(values, axis=0), TN_DIMS, preferred_element_type=F32)
    return ov[:, :HEAD_DIM] / ov[:, HEAD_DIM:HEAD_DIM + 1]


def _stack_heads(q, kvh, rows):
    parts = []
    for g in range(ATT_GROUP):
        hd = kvh * ATT_GROUP + g
        parts.append(q[:, hd * HEAD_DIM:(hd + 1) * HEAD_DIM])
    return jnp.concatenate(parts, axis=0)


def _sink_row(sink_ref, kvh, rows):
    head = lax.broadcasted_iota(jnp.int32, (1, ATT_GROUP * rows), 1) // rows
    out = jnp.full((1, ATT_GROUP * rows), sink_ref[kvh * ATT_GROUP], F32)
    for g in range(1, ATT_GROUP):
        out = jnp.where(head == g, sink_ref[kvh * ATT_GROUP + g], out)
    return out


def _store_heads(o_ref, o, kvh, rows):
    for g in range(ATT_GROUP):
        hd = kvh * ATT_GROUP + g
        o_ref[:, hd * HEAD_DIM:(hd + 1) * HEAD_DIM] = o[g * rows:(g + 1) * rows, :].astype(BF16)


def _ctx_att_kernel(sink_ref, q_ref, k_ref, v_ref, o_ref):
    q = q_ref[...]
    for kvh in range(ATT_KV_HEADS):
        qs = _stack_heads(q, kvh, SEQ)
        k = k_ref[:, kvh * HEAD_DIM:(kvh + 1) * HEAD_DIM]
        v = v_ref[:, kvh * HEAD_DIM:(kvh + 1) * HEAD_DIM]
        o = _sink_softmax_pv(qs, _sink_row(sink_ref, kvh, SEQ), [(k, v, None)])
        _store_heads(o_ref, o, kvh, SEQ)


def _lat_att_kernel(sink_ref, q_ref, k_ref, v_ref, ck_ref, cv_ref, o_ref):
    n = pl.program_id(1)
    nb = DEC_SEQ // ATT_BLOCK
    blk = ATT_BLOCK
    rows = ATT_GROUP * blk
    prev0 = pl.multiple_of(jnp.maximum(n - 1, 0) * blk, blk)
    cur0 = pl.multiple_of(n * blk, blk)
    next0 = pl.multiple_of(jnp.minimum(n + 1, nb - 1) * blk, blk)

    kj = lax.broadcasted_iota(jnp.int32, (3 * blk, rows), 0) - blk
    qi = lax.broadcasted_iota(jnp.int32, (3 * blk, rows), 1) % blk
    kabs = n * blk + kj
    valid = (jnp.abs(qi - kj) <= WINDOW) & (kabs >= 0) & (kabs < DEC_SEQ)
    bias = jnp.where(valid, 0.0, -jnp.inf)

    q = q_ref[...]
    for kvh in range(ATT_KV_HEADS):
        cols = slice(kvh * HEAD_DIM, (kvh + 1) * HEAD_DIM)
        qs = _stack_heads(q, kvh, blk)
        k_loc = jnp.concatenate([k_ref[pl.ds(prev0, blk), cols], k_ref[pl.ds(cur0, blk), cols],
                                 k_ref[pl.ds(next0, blk), cols]], axis=0)
        v_loc = jnp.concatenate([v_ref[pl.ds(prev0, blk), cols], v_ref[pl.ds(cur0, blk), cols],
                                 v_ref[pl.ds(next0, blk), cols]], axis=0)
        k_ctx = ck_ref[0, 0, kvh].astype(BF16)
        v_ctx = cv_ref[0, 0, kvh].astype(BF16)
        o = _sink_softmax_pv(qs, _sink_row(sink_ref, kvh, blk),
                             [(k_loc, v_loc, bias), (k_ctx, v_ctx, None)])
        _store_heads(o_ref, o, kvh, blk)


def _ctx_attention(qkv, sink):
    kcol = ATT_Q_W // ATT_KV_W
    return pl.pallas_call(
        _ctx_att_kernel,
        grid_spec=pltpu.PrefetchScalarGridSpec(
            num_scalar_prefetch=1,
            grid=(BATCH,),
            in_specs=[
                pl.BlockSpec((SEQ, ATT_Q_W), lambda b, s: (b, 0)),
                pl.BlockSpec((SEQ, ATT_KV_W), lambda b, s: (b, kcol)),
                pl.BlockSpec((SEQ, ATT_KV_W), lambda b, s: (b, kcol + 1)),
            ],
            out_specs=pl.BlockSpec((SEQ, ATT_Q_W), lambda b, s: (b, 0)),
        ),
        out_shape=jax.ShapeDtypeStruct((CTX_ROWS, ATT_Q_W), BF16),
        compiler_params=_params(1),
        name="ctx_attention",
    )(sink, qkv, qkv, qkv)


def _lat_attention(qkv, sink, cache_k, cache_v, layer_j):
    kcol = ATT_Q_W // ATT_KV_W
    nb = DEC_SEQ // ATT_BLOCK
    q0 = CTX_ROWS // ATT_BLOCK
    s0 = CTX_ROWS // DEC_SEQ
    cache_spec = pl.BlockSpec((1, 1, ATT_KV_HEADS, PAST_LEN, HEAD_DIM), lambda b, n, s: (b, layer_j, 0, 0, 0))
    return pl.pallas_call(
        _lat_att_kernel,
        grid_spec=pltpu.PrefetchScalarGridSpec(
            num_scalar_prefetch=1,
            grid=(DEC_BATCH, nb),
            in_specs=[
                pl.BlockSpec((ATT_BLOCK, ATT_Q_W), lambda b, n, s: (q0 + b * nb + n, 0)),
                pl.BlockSpec((DEC_SEQ, ATT_KV_W), lambda b, n, s: (s0 + b, kcol)),
                pl.BlockSpec((DEC_SEQ, ATT_KV_W), lambda b, n, s: (s0 + b, kcol + 1)),
                cache_spec,
                cache_spec,
            ],
            out_specs=pl.BlockSpec((ATT_BLOCK, ATT_Q_W), lambda b, n, s: (b * nb + n, 0)),
        ),
        out_shape=jax.ShapeDtypeStruct((LAT_ROWS, ATT_Q_W), BF16),
        compiler_params=_params(2),
        name="lat_attention",
    )(sink, qkv, qkv, qkv, cache_k, cache_v)


ROW_ALIGN = 16
S_SLOTS = 576
BUF_ROWS = S_SLOTS + TM_MOE
XW = D_MODEL + LANES
SLOT_LANE = 6
GATE_LANE0 = 8
REGION_TILES = 28
REGION_ROWS = REGION_TILES * TM_MOE
TAB_W = 16
HALF_WIN = TM_MOE // 2
N_TOK_TILES = T_ROWS // TM


def _router_kernel(ac_ref, al_ref, wo_ref, x_ref, ga_ref, gain_ref, sh_ref, sc_ref, wr_ref, bias_ref,
                   xo_ref, xs_hbm, row_ref, tab_ref, wobf_ref, buf_ref, off_ref, wide_ref, sem):
    i = pl.program_id(0)
    ng = N_GROUPS
    last = pl.num_programs(0) - 1

    @pl.when(i == 0)
    def _():
        wobf_ref[...] = wo_ref[0].astype(BF16)

    mixed = jnp.where(i < N_CTX_TILES, ac_ref[...], al_ref[...])
    x_new = x_ref[...] + ga_ref[0] * jnp.dot(mixed, wobf_ref[...], preferred_element_type=F32)
    xo_ref[...] = x_new

    @pl.when(i == 0)
    def _():
        buf_ref[...] = jnp.zeros_like(buf_ref)
        for g in range(ng):
            off_ref[g] = 0

        def clear(r, carry):
            for c in range(TAB_W):
                tab_ref[r, c] = 0
            return carry

        lax.fori_loop(0, N_TOK_TILES + 1, clear, 0)

    hf = _norm_mod(x_new, gain_ref[...], sh_ref[0], sc_ref[0])
    hb = hf.astype(BF16)
    hl = (hf - hb.astype(F32)).astype(BF16)
    both = jnp.dot(hb, wr_ref[...], preferred_element_type=F32)
    logits = (both[:, :LANES] + both[:, LANES:]
              + jnp.dot(hl, wr_ref[:, :LANES], preferred_element_type=F32))
    lt = logits.T[0:N_EXPERTS, :]
    scores = jax.nn.sigmoid(lt)
    sel = scores + bias_ref[...]
    xs = [sel[k * ng:(k + 1) * ng, :] for k in range(EXPERTS_PER_GROUP)]
    sc = [scores[k * ng:(k + 1) * ng, :] for k in range(EXPERTS_PER_GROUP)]
    a, b, c, d = xs
    gs = jnp.maximum(jnp.maximum(jnp.maximum(a + b, a + c), jnp.maximum(a + d, b + c)),
                     jnp.maximum(b + d, c + d))
    bv = gs[0:1, :]
    bg = jnp.zeros(bv.shape, jnp.int32)
    for g in range(1, ng):
        better = gs[g:g + 1, :] > bv
        bg = jnp.where(better, g, bg)
        bv = jnp.where(better, gs[g:g + 1, :], bv)
    giota = lax.broadcasted_iota(jnp.int32, (ng, TM), 0)
    onehot = giota == bg
    wk = []
    for k in range(EXPERTS_PER_GROUP):
        rank = jnp.zeros((ng, TM), F32)
        for j in range(EXPERTS_PER_GROUP):
            if j < k:
                rank = rank + (xs[j] >= xs[k]).astype(F32)
            elif j > k:
                rank = rank + (xs[j] > xs[k]).astype(F32)
        chosen = (rank < 2.0) & onehot
        wk.append(jnp.sum(jnp.where(chosen, sc[k], 0.0), axis=0, keepdims=True))
    den = wk[0] + wk[1] + wk[2] + wk[3]
    gates = [w / den for w in wk]

    oh = jnp.where(onehot, 1.0, 0.0)
    r = lax.broadcasted_iota(jnp.int32, (TM, TM), 0)
    cc = lax.broadcasted_iota(jnp.int32, (TM, TM), 1)
    tri = jnp.where(r < cc, 1.0, 0.0).astype(BF16)
    before = jnp.dot(oh.astype(BF16), tri, preferred_element_type=F32)
    rank_local = jnp.sum(jnp.where(onehot, before, 0.0), axis=0, keepdims=True).astype(jnp.int32)
    counts = [jnp.sum(jnp.where(bg == g, 1, 0)) for g in range(ng)]
    pads = [((cnt + ROW_ALIGN - 1) // ROW_ALIGN) * ROW_ALIGN for cnt in counts]
    segs = [0]
    for g in range(1, ng):
        segs.append(segs[-1] + pads[g - 1])
    total = segs[-1] + pads[-1]
    seg_of = jnp.zeros(bg.shape, jnp.int32)
    for g in range(1, ng):
        seg_of = jnp.where(bg == g, segs[g], seg_of)
    slot = seg_of + rank_local

    parts = []
    for gt in gates:
        hi = gt.astype(BF16).astype(F32)
        rest = gt - hi
        mid = rest.astype(BF16).astype(F32)
        parts += [hi, mid, (rest - mid).astype(BF16).astype(F32)]
    zero_row = jnp.zeros((1, TM), F32)
    info = jnp.concatenate(gates + [bg.astype(F32), zero_row, slot.astype(F32), zero_row] + parts
                           + [jnp.zeros((LANES - GATE_LANE0 - len(parts), TM), F32)], axis=0)
    rowinfo = info.T
    row_ref[...] = rowinfo

    h_ext = jnp.concatenate([hb, rowinfo.astype(BF16)], axis=1)
    srow = lax.broadcasted_iota(jnp.int32, (S_SLOTS, TM), 0)
    pick = jnp.where(srow == slot, 1.0, 0.0).astype(BF16)
    par = i % 2
    buf_ref[par, 0:S_SLOTS, :] = jnp.dot(pick, h_ext, preferred_element_type=F32).astype(BF16)

    def window_copy(g, src_row, dst_row, rows):
        return pltpu.make_async_copy(
            buf_ref.at[par, pl.ds(pl.multiple_of(src_row, ROW_ALIGN), rows), :],
            xs_hbm.at[pl.ds(pl.multiple_of(dst_row, ROW_ALIGN), rows), :],
            sem.at[g])

    def wait_windows(rows):
        for g in range(ng):
            window_copy(g, 0, 0, rows).wait()

    def wait_previous():
        @pl.when(wide_ref[0] == 1)
        def _():
            wait_windows(TM_MOE)

        @pl.when(wide_ref[0] == 0)
        def _():
            wait_windows(HALF_WIN)

    @pl.when(i > 0)
    def _():
        wait_previous()

    wide = jnp.maximum(jnp.maximum(pads[0], pads[1]), jnp.maximum(pads[2], pads[3])) > HALF_WIN
    offs = [off_ref[g] for g in range(ng)]

    def issue(rows):
        for g in range(ng):
            window_copy(g, segs[g], g * REGION_ROWS + offs[g], rows).start()

    @pl.when(wide)
    def _():
        issue(TM_MOE)

    @pl.when(jnp.logical_not(wide))
    def _():
        issue(HALF_WIN)

    wide_ref[0] = wide.astype(jnp.int32)
    for g in range(ng):
        tab_ref[i, g] = segs[g]
        tab_ref[i, ng + g] = offs[g]
        tab_ref[i, 2 * ng + 1 + g] = pads[g]
        off_ref[g] = offs[g] + pads[g]
    tab_ref[i, 2 * ng] = total

    @pl.when(i == last)
    def _():
        wait_previous()
        for g in range(ng):
            window_copy(g, S_SLOTS, g * REGION_ROWS + off_ref[g], TM_MOE).start()
            tab_ref[last + 1, g] = off_ref[g]
        wait_windows(TM_MOE)


def _mix_out_router(a_ctx, a_lat, w_out, layer_j, x, gate_a, gain, shift, scale, wr_parts, bias_col):
    k = a_ctx.shape[1]
    return pl.pallas_call(
        _router_kernel,
        grid=(N_TOK_TILES,),
        in_specs=[
            pl.BlockSpec((TM, k), lambda i: (jnp.minimum(i, N_CTX_TILES - 1), 0)),
            pl.BlockSpec((TM, k), lambda i: (jnp.maximum(i - N_CTX_TILES, 0), 0)),
            pl.BlockSpec((1, k, D_MODEL), lambda i: (layer_j, 0, 0), pipeline_mode=pl.Buffered(1)),
            pl.BlockSpec((TM, D_MODEL), lambda i: (i, 0)),
            pl.BlockSpec((1, 1, D_MODEL), lambda i: (_cond_row(i), 0, 0)),
            pl.BlockSpec((1, D_MODEL), lambda i: (0, 0)),
            pl.BlockSpec((1, 1, D_MODEL), lambda i: (_cond_row(i), 0, 0)),
            pl.BlockSpec((1, 1, D_MODEL), lambda i: (_cond_row(i), 0, 0)),
            pl.BlockSpec((D_MODEL, 2 * LANES), lambda i: (0, 0)),
            pl.BlockSpec((N_EXPERTS, 1), lambda i: (0, 0)),
        ],
        out_specs=[
            pl.BlockSpec((TM, D_MODEL), lambda i: (i, 0)),
            pl.BlockSpec(memory_space=pl.ANY),
            pl.BlockSpec((TM, LANES), lambda i: (i, 0)),
            pl.BlockSpec(memory_space=pltpu.SMEM),
        ],
        out_shape=[
            jax.ShapeDtypeStruct((T_ROWS, D_MODEL), F32),
            jax.ShapeDtypeStruct((N_GROUPS * REGION_ROWS, XW), BF16),
            jax.ShapeDtypeStruct((T_ROWS, LANES), F32),
            jax.ShapeDtypeStruct((N_TOK_TILES + 1, TAB_W), jnp.int32),
        ],
        scratch_shapes=[
            pltpu.VMEM((k, D_MODEL), BF16),
            pltpu.VMEM((2, BUF_ROWS, XW), BF16),
            pltpu.SMEM((N_GROUPS,), jnp.int32),
            pltpu.SMEM((1,), jnp.int32),
            pltpu.SemaphoreType.DMA((N_GROUPS,)),
        ],
        compiler_params=_params(1),
        name="mix_out_router",
    )(a_ctx, a_lat, w_out, x, gate_a, gain, shift, scale, wr_parts, bias_col)


UP_EXPERTS = 4


def _moe_up_kernel(tb_ref, tg_ref, tv_ref, xs_ref, w_ref, o_ref, wbf_ref):
    kk = pl.program_id(0)
    i = pl.program_id(1)
    new_w = (i == 0) | (tg_ref[i] != tg_ref[jnp.maximum(i - 1, 0)])

    @pl.when(new_w)
    def _():
        wbf_ref[...] = w_ref[...].astype(BF16)

    @pl.when(tv_ref[i] == 1)
    def _():
        xs = xs_ref[:, :D_MODEL]
        extra = xs_ref[:, D_MODEL:].astype(F32)
        lane = lax.broadcasted_iota(jnp.int32, extra.shape, 1)
        for e in range(UP_EXPERTS):
            hu = jnp.dot(xs, wbf_ref[e], preferred_element_type=F32)
            lane0 = GATE_LANE0 + 3 * (kk * UP_EXPERTS + e)
            gcol = jnp.sum(jnp.where((lane >= lane0) & (lane < lane0 + 3), extra, 0.0), axis=1, keepdims=True)
            a = _silu(hu[:, :D_EXPERT]) * hu[:, D_EXPERT:] * gcol
            o_ref[:, e * D_EXPERT:(e + 1) * D_EXPERT] = a.astype(BF16)

    @pl.when(tv_ref[i] != 1)
    def _():
        o_ref[...] = jnp.zeros_like(o_ref)


def _moe_up(tile_block, tile_group, tile_valid, xs, w_up, layer):
    steps = EXPERTS_PER_GROUP // UP_EXPERTS
    return pl.pallas_call(
        _moe_up_kernel,
        grid_spec=pltpu.PrefetchScalarGridSpec(
            num_scalar_prefetch=3,
            grid=(steps, N_MOE_TILES),
            in_specs=[
                pl.BlockSpec((TM_MOE, XW), lambda k, i, tb, tg, tv: (tb[i], 0)),
                pl.BlockSpec((UP_EXPERTS, D_MODEL, 2 * D_EXPERT),
                             lambda k, i, tb, tg, tv: ((layer * N_GROUPS + tg[i]) * steps + k, 0, 0)),
            ],
            out_specs=pl.BlockSpec((TM_MOE, UP_EXPERTS * D_EXPERT), lambda k, i, tb, tg, tv: (tb[i], k)),
            scratch_shapes=[pltpu.VMEM((UP_EXPERTS, D_MODEL, 2 * D_EXPERT), BF16)],
        ),
        out_shape=jax.ShapeDtypeStruct((N_GROUPS * REGION_ROWS, EXPERTS_PER_GROUP * D_EXPERT), BF16),
        compiler_params=_params(2),
        name="moe_up",
    )(tile_block, tile_group, tile_valid, xs, w_up)


def _moe_down_kernel(tb_ref, tg_ref, tv_ref, a_ref, w_ref, o_ref, wbf_ref):
    i = pl.program_id(0)
    new_w = (i == 0) | (tg_ref[i] != tg_ref[jnp.maximum(i - 1, 0)])

    @pl.when(new_w)
    def _():
        wbf_ref[...] = w_ref[0].astype(BF16)

    @pl.when(tv_ref[i] == 1)
    def _():
        o_ref[...] = jnp.dot(a_ref[...], wbf_ref[...], preferred_element_type=F32).astype(BF16)

    @pl.when(tv_ref[i] != 1)
    def _():
        o_ref[...] = jnp.zeros_like(o_ref)


def _moe_down(tile_block, tile_group, tile_valid, a, w_down_grouped, layer):
    kdim = EXPERTS_PER_GROUP * D_EXPERT
    return pl.pallas_call(
        _moe_down_kernel,
        grid_spec=pltpu.PrefetchScalarGridSpec(
            num_scalar_prefetch=3,
            grid=(N_MOE_TILES,),
            in_specs=[
                pl.BlockSpec((TM_MOE, kdim), lambda i, tb, tg, tv: (tb[i], 0)),
                pl.BlockSpec((1, kdim, D_MODEL), lambda i, tb, tg, tv: (layer * N_GROUPS + tg[i], 0, 0)),
            ],
            out_specs=pl.BlockSpec((TM_MOE, D_MODEL), lambda i, tb, tg, tv: (tb[i], 0)),
            scratch_shapes=[pltpu.VMEM((kdim, D_MODEL), BF16)],
        ),
        out_shape=jax.ShapeDtypeStruct((N_GROUPS * REGION_ROWS, D_MODEL), BF16),
        compiler_params=_params(1),
        name="moe_down",
    )(tile_block, tile_group, tile_valid, a, w_down_grouped)


def _moe_combine_kernel(tab_ref, *refs, split):
    first = refs[:N_GROUPS]
    second = refs[N_GROUPS:2 * N_GROUPS]
    row_ref, x_ref, g_ref = refs[2 * N_GROUPS:2 * N_GROUPS + 3]
    outs = refs[2 * N_GROUPS + 3:-1]
    buf_ref = refs[-1]
    i = pl.program_id(0)

    @pl.when(i == 0)
    def _():
        buf_ref[...] = jnp.zeros_like(buf_ref)

    for g in range(N_GROUPS):
        start = pl.multiple_of(tab_ref[i * TAB_W + g], ROW_ALIGN)
        buf_ref[pl.ds(start, HALF_WIN), :] = first[g][...]

        @pl.when(tab_ref[i * TAB_W + 2 * N_GROUPS + 1 + g] > HALF_WIN)
        def _(g=g, start=start):
            buf_ref[pl.ds(start + HALF_WIN, HALF_WIN), :] = second[g][...]

    slot = row_ref[:, SLOT_LANE:SLOT_LANE + 1].astype(jnp.int32)
    scol = lax.broadcasted_iota(jnp.int32, (TM, S_SLOTS), 1)
    pick = jnp.where(scol == slot, 1.0, 0.0).astype(BF16)
    y = jnp.dot(pick, buf_ref[0:S_SLOTS, :], preferred_element_type=F32)
    new_x = x_ref[...] + g_ref[0] * y
    if split:
        @pl.when(i < N_CTX_TILES)
        def _():
            outs[0][...] = new_x

        @pl.when(i >= N_CTX_TILES)
        def _():
            outs[1][...] = new_x
    else:
        outs[0][...] = new_x


def _moe_combine(tab_flat, ys, rowinfo, x, gate, split=False):
    if split:
        out_specs = [pl.BlockSpec((TM, D_MODEL), lambda i, tab: (jnp.minimum(i, N_CTX_TILES - 1), 0)),
                     pl.BlockSpec((TM, D_MODEL), lambda i, tab: (jnp.maximum(i - N_CTX_TILES, 0), 0))]
        out_shape = [jax.ShapeDtypeStruct((CTX_ROWS, D_MODEL), F32), jax.ShapeDtypeStruct((LAT_ROWS, D_MODEL), F32)]
    else:
        out_specs = [pl.BlockSpec((TM, D_MODEL), lambda i, tab: (i, 0))]
        out_shape = [jax.ShapeDtypeStruct((T_ROWS, D_MODEL), F32)]

    def window_spec(g, second):
        def index(i, tab):
            off = tab[i * TAB_W + N_GROUPS + g]
            if second:
                off = jnp.where(tab[i * TAB_W + 2 * N_GROUPS + 1 + g] > HALF_WIN, off + HALF_WIN, 0)
            return pl.multiple_of(off + g * REGION_ROWS, ROW_ALIGN), 0

        return pl.BlockSpec((pl.Element(HALF_WIN), pl.Element(D_MODEL)), index)

    windows = [window_spec(g, False) for g in range(N_GROUPS)] + [window_spec(g, True) for g in range(N_GROUPS)]
    outs = pl.pallas_call(
        functools.partial(_moe_combine_kernel, split=split),
        grid_spec=pltpu.PrefetchScalarGridSpec(
            num_scalar_prefetch=1,
            grid=(N_TOK_TILES,),
            in_specs=windows + [
                pl.BlockSpec((TM, LANES), lambda i, tab: (i, 0)),
                pl.BlockSpec((TM, D_MODEL), lambda i, tab: (i, 0)),
                pl.BlockSpec((1, 1, D_MODEL), lambda i, tab: (_cond_row(i), 0, 0)),
            ],
            out_specs=out_specs,
            scratch_shapes=[pltpu.VMEM((BUF_ROWS, D_MODEL), BF16)],
        ),
        out_shape=out_shape,
        compiler_params=_params(1),
        name="moe_combine",
    )(tab_flat, *([ys] * (2 * N_GROUPS)), rowinfo, x, gate)
    return tuple(outs) if split else outs[0]


def _mix_out_and_moe(mix, x, gate_a, gain, shift, scale, gate, router_w, w_up, w_down_grouped, layer):
    wr_parts, bias_col = router_w
    x, xs, rowinfo, tab = _mix_out_router(*mix, x, gate_a, gain, shift, scale, wr_parts, bias_col)
    n_full = (tab[N_TOK_TILES, :N_GROUPS] + TM_MOE - 1) // TM_MOE
    ends = jnp.cumsum(n_full + 1)
    starts = ends - (n_full + 1)
    entry = jnp.arange(N_MOE_TILES, dtype=jnp.int32)
    grp = jnp.minimum(jnp.sum(entry[:, None] >= ends[None, :], axis=1), N_GROUPS - 1).astype(jnp.int32)
    idx = entry - starts[grp]
    listed = entry < ends[N_GROUPS - 1]
    closing = (N_GROUPS - 1) * REGION_TILES + n_full[N_GROUPS - 1]
    tile_block = jnp.where(listed, grp * REGION_TILES + idx, closing).astype(jnp.int32)
    tile_valid = (listed & (idx < n_full[grp])).astype(jnp.int32)
    a = _moe_up(tile_block, grp, tile_valid, xs, w_up, layer)
    ys = _moe_down(tile_block, grp, tile_valid, a, w_down_grouped, layer)
    return _moe_combine(tab.reshape(-1), ys, rowinfo, x, gate, split=layer == DEPTH - 1)


def kernel(x_prompt, x_sample, state_ret, cache_k, cache_v, c, c_ctx, w_mod, b_mod, norm_mix,
           norm_moe, ret_w_in, ret_decay, ret_w_out, att_w_in, att_q_norm, att_k_norm, att_sink,
           att_w_out, w_router, router_bias, moe_w_up, moe_w_down):
    x = jnp.concatenate([x_prompt.reshape(CTX_ROWS, D_MODEL), x_sample.reshape(LAT_ROWS, D_MODEL)], axis=0)
    cond8 = jnp.zeros((8, D_MODEL), F32).at[0].set(c_ctx).at[1:N_COND].set(c)
    mods = _modulation_all(cond8, w_mod, b_mod)
    mods = mods[:, :N_COND].reshape(DEPTH, N_COND, N_MOD, 1, D_MODEL)

    ret_cos, ret_sin = _rope_tables(RET_DK)
    att_cos, att_sin = _rope_tables(HEAD_DIM)

    perm = jnp.arange(N_EXPERTS).reshape(N_GROUPS, EXPERTS_PER_GROUP).T.reshape(-1)
    wr = jnp.zeros((D_MODEL, LANES), F32).at[:, :N_EXPERTS].set(w_router[:, perm])
    wr_hi = wr.astype(BF16)
    wr_lo = (wr - wr_hi.astype(F32)).astype(BF16)
    bias_col = router_bias[perm].astype(F32).reshape(N_EXPERTS, 1)
    router_w = (jnp.concatenate([wr_hi, wr_lo], axis=1), bias_col)

    w_up_all = moe_w_up.reshape(DEPTH * N_EXPERTS, D_MODEL, 2 * D_EXPERT)
    w_down_all = moe_w_down.reshape(DEPTH * N_GROUPS, EXPERTS_PER_GROUP * D_EXPERT, D_MODEL)
    ret_w_in_bf16 = ret_w_in.astype(BF16)
    new_state = jnp.zeros((BATCH, (DEPTH + 1) // 2, 2, RET_HEADS, RET_DK, RET_DV), F32)
    cache_shape = (BATCH, DEPTH // 2, ATT_KV_HEADS, SEQ, HEAD_DIM)
    new_kv = (jnp.zeros(cache_shape, F32), jnp.zeros(cache_shape, F32))
    for layer in range(DEPTH):
        sh_a, sc_a, g_a, sh_m, sc_m, g_m = [mods[layer, :, t] for t in range(N_MOD)]
        gain_mix = norm_mix[layer].reshape(1, D_MODEL)
        gain_moe = norm_moe[layer].reshape(1, D_MODEL)
        j = layer // 2
        if layer % 2 == 0:
            proj = _ret_proj(x, gain_mix, sh_a, sc_a, ret_w_in_bf16, j, ret_cos, ret_sin)
            o_ctx, new_state = _ret_core(proj, ret_decay[j], state_ret, j, latent=False, new_state=new_state)
            (o_lat,) = _ret_core(proj, ret_decay[j], state_ret, j, latent=True)
            mix = (o_ctx, o_lat, ret_w_out, j)
        else:
            q_gain = jnp.tile(att_q_norm[j], LANES // HEAD_DIM).reshape(1, LANES)
            k_gain = jnp.tile(att_k_norm[j], LANES // HEAD_DIM).reshape(1, LANES)
            qkv, new_k, new_v = _att_proj(x, gain_mix, sh_a, sc_a, att_w_in, j, q_gain, k_gain,
                                          att_cos, att_sin, new_kv)
            new_kv = (new_k, new_v)
            sink = att_sink[j].astype(F32)
            o_ctx = _ctx_attention(qkv, sink)
            o_lat = _lat_attention(qkv, sink, cache_k, cache_v, j)
            mix = (o_ctx, o_lat, att_w_out, j)
        x = _mix_out_and_moe(mix, x, g_a, gain_moe, sh_m, sc_m, g_m, router_w, w_up_all, w_down_all, layer)

    x_ctx, x_lat = x
    y_prompt = x_ctx.reshape(BATCH, SEQ, D_MODEL)
    y_sample = x_lat.reshape(DEC_BATCH, DEC_SEQ, D_MODEL)
    return (y_prompt, y_sample, new_state, new_kv[0], new_kv[1])
```

```python
import functools
from typing import NamedTuple

import jax
import jax.numpy as jnp
import numpy as np
from jax import lax
from jax.experimental import pallas as pl
from jax.experimental.pallas import tpu as pltpu

F32 = jnp.float32
BF16 = jnp.bfloat16

D_MODEL = 1024
BATCH = 16
SEQ = 256
DEPTH = 4
DEC_BATCH = 2
DEC_SEQ = 4096
PAST_LEN = 512
GRID_W = 64
N_MOD = 6
EPS = 1e-6
ROPE_BASE = 10000.0
RET_HEADS = 8
RET_DK = 128
RET_DV = 256
RET_CHUNK = 128
RET_IN = 2 * RET_HEADS * RET_DK + 2 * RET_HEADS * RET_DV
ATT_HEADS = 16
ATT_KV_HEADS = 4
ATT_GROUP = 4
HEAD_DIM = 64
WINDOW = 128
ATT_BLOCK = 128
ATT_IN = (ATT_HEADS + 2 * ATT_KV_HEADS) * HEAD_DIM
ATT_SCALE = HEAD_DIM ** -0.5
N_EXPERTS = 16
N_GROUPS = 4
EXPERTS_PER_GROUP = 4
D_EXPERT = 512

CTX_ROWS = BATCH * SEQ
LAT_ROWS = DEC_BATCH * DEC_SEQ
T_ROWS = CTX_ROWS + LAT_ROWS
N_COND = 1 + DEC_BATCH

LANES = 128
TM = 512
N_CTX_TILES = CTX_ROWS // TM
N_LAT_TILES = DEC_SEQ // TM
TM_MOE = 512
N_MOE_TILES = 36
VMEM_LIMIT = 52 * 1024 * 1024

NT_DIMS = (((1,), (1,)), ((), ()))


def _params(n_axes, vmem=VMEM_LIMIT):
    return pltpu.CompilerParams(dimension_semantics=("arbitrary",) * n_axes, vmem_limit_bytes=vmem)


def _cond_row(i):
    return (i * TM) // DEC_SEQ


def _rope_row(i):
    return jnp.where(i < N_CTX_TILES, 0, 1 + (i - N_CTX_TILES) % N_LAT_TILES)


def _norm_mod(x, gain, shift, scale):
    ms = jnp.mean(x * x, axis=-1, keepdims=True)
    y = x * lax.rsqrt(ms + EPS) * gain
    return y * (1.0 + scale) + shift


def _silu(x):
    return x * jax.nn.sigmoid(x)


def _mod_kernel(c_ref, w_ref, b_ref, o_ref):
    c = c_ref[...]
    s = _silu(c).astype(BF16)
    o_ref[0] = jnp.dot(s, w_ref[0].astype(BF16), preferred_element_type=F32) + b_ref[0]


def _modulation_all(cond8, w_mod, b_mod):
    tn = 1536
    n = N_MOD * D_MODEL
    return pl.pallas_call(
        _mod_kernel,
        grid=(DEPTH, n // tn),
        in_specs=[
            pl.BlockSpec((8, D_MODEL), lambda l, j: (0, 0)),
            pl.BlockSpec((1, D_MODEL, tn), lambda l, j: (l, 0, j)),
            pl.BlockSpec((1, 1, tn), lambda l, j: (l, 0, j)),
        ],
        out_specs=pl.BlockSpec((1, 8, tn), lambda l, j: (l, 0, j)),
        out_shape=jax.ShapeDtypeStruct((DEPTH, 8, n), F32),
        compiler_params=_params(2),
        name="modulation",
    )(cond8, w_mod, b_mod.reshape(DEPTH, 1, n))


def _rope_tables(head_dim):
    half = head_dim // 2
    quarter = half // 2
    t = np.arange(DEC_SEQ)
    row = (t // GRID_W).astype(np.float64)
    col = (t % GRID_W).astype(np.float64)
    inv_freq = ROPE_BASE ** (-np.arange(quarter, dtype=np.float64) / quarter)
    lane = np.arange(LANES)
    d = lane % head_dim
    w = d % half
    f = w % quarter
    pos = np.where((d // half)[None, :] == 0, row[:, None], col[:, None])
    ang = pos * inv_freq[f][None, :]
    cos = np.cos(ang)
    sin = np.where((w < quarter)[None, :], -np.sin(ang), np.sin(ang))
    cos = np.concatenate([np.ones((TM, LANES)), cos], axis=0).astype(np.float32)
    sin = np.concatenate([np.zeros((TM, LANES)), sin], axis=0).astype(np.float32)
    return jnp.asarray(cos), jnp.asarray(sin)


def _rope_chunk(xc, cos, sin, quarter):
    lane = lax.broadcasted_iota(jnp.int32, xc.shape, 1)
    first = (lane % (2 * quarter)) < quarter
    partner = jnp.where(first, pltpu.roll(xc, LANES - quarter, 1), pltpu.roll(xc, quarter, 1))
    return xc * cos + partner * sin


RET_TN = 1024
RET_Q_TILES = RET_HEADS * RET_DK // RET_TN
RET_QK_TILES = 2 * RET_Q_TILES


def _ret_proj_kernel(*refs, from_moe):
    i = pl.program_id(0)
    if from_moe:
        n_src = 1 + N_WINDOWS + 3
        x = _combine_tile(i, refs[0], refs[1:1 + N_WINDOWS], *refs[1 + N_WINDOWS:n_src], refs[-1])
    else:
        n_src = 2
        x = jnp.where(i < N_CTX_TILES, refs[0][...], refs[1][...])
    gain_ref, sh_ref, sc_ref, w_ref, cos_ref, sin_ref, o_ref, xo_ref = refs[n_src:n_src + 8]
    xo_ref[...] = x
    h = _norm_mod(x, gain_ref[...], sh_ref[0], sc_ref[0]).astype(BF16)
    cos = cos_ref[...]
    sin = sin_ref[...]
    for j in range(RET_IN // RET_TN):
        cols = slice(j * RET_TN, (j + 1) * RET_TN)
        acc = jnp.dot(h, w_ref[0, :, cols], preferred_element_type=F32)
        if j < RET_QK_TILES:
            scale = 1.0 if j < RET_Q_TILES else RET_DK ** -0.5
            for c in range(RET_TN // LANES):
                xc = acc[:, c * LANES:(c + 1) * LANES]
                lanes = slice(j * RET_TN + c * LANES, j * RET_TN + (c + 1) * LANES)
                o_ref[:, lanes] = (_rope_chunk(xc, cos, sin, RET_DK // 4) * scale).astype(BF16)
        else:
            o_ref[:, cols] = acc.astype(BF16)


def _ret_proj(source, gain, shift, scale, w_in_bf16, layer_j, cos, sin):
    from_moe = isinstance(source, PendingMoe)
    if from_moe:
        src_specs, src_args, scratch = _pending_specs(source)
        n_prefetch = 1
    else:
        src_specs = [pl.BlockSpec((TM, D_MODEL), lambda i, *_: (jnp.minimum(i, N_CTX_TILES - 1), 0)),
                     pl.BlockSpec((TM, D_MODEL), lambda i, *_: (jnp.maximum(i - N_CTX_TILES, 0), 0))]
        src_args, scratch, n_prefetch = list(source), [], 0
    return pl.pallas_call(
        functools.partial(_ret_proj_kernel, from_moe=from_moe),
        grid_spec=pltpu.PrefetchScalarGridSpec(
            num_scalar_prefetch=n_prefetch,
            grid=(T_ROWS // TM,),
            in_specs=src_specs + [
                pl.BlockSpec((1, D_MODEL), lambda i, *_: (0, 0)),
                pl.BlockSpec((1, 1, D_MODEL), lambda i, *_: (_cond_row(i), 0, 0)),
                pl.BlockSpec((1, 1, D_MODEL), lambda i, *_: (_cond_row(i), 0, 0)),
                pl.BlockSpec((1, D_MODEL, RET_IN), lambda i, *_: (layer_j, 0, 0), pipeline_mode=pl.Buffered(1)),
                pl.BlockSpec((TM, LANES), lambda i, *_: (_rope_row(i), 0)),
                pl.BlockSpec((TM, LANES), lambda i, *_: (_rope_row(i), 0)),
            ],
            out_specs=[pl.BlockSpec((TM, RET_IN), lambda i, *_: (i, 0)),
                       pl.BlockSpec((TM, D_MODEL), lambda i, *_: (i, 0))],
            scratch_shapes=scratch,
        ),
        out_shape=[jax.ShapeDtypeStruct((T_ROWS, RET_IN), BF16), jax.ShapeDtypeStruct((T_ROWS, D_MODEL), F32)],
        compiler_params=_params(1),
        name="ret_proj",
    )(*src_args, gain, shift, scale, w_in_bf16, cos, sin)


def _log_sigmoid(x):
    return -(jnp.maximum(-x, 0.0) + jnp.log(1.0 + jnp.exp(-jnp.abs(x))))


def _ret_core_kernel(*refs, seq_len, n_seq, has_init, emit_state, n_alias):
    it = iter(refs)
    decay_ref = next(it)
    q_ref = next(it)
    k_ref = next(it)
    v_ref = next(it)
    g_ref = next(it)
    s0_ref = next(it) if has_init else None
    for _ in range(n_alias):
        next(it)
    o_ref = next(it)
    st_ref = next(it) if emit_state else None
    acc_ref = next(it)
    kv_ref = next(it)

    c = RET_CHUNK
    nc = seq_len // c
    n_chunks = n_seq * nc
    unroll = min(RET_UNROLL, n_chunks)
    head = pl.program_id(1)

    lg_all = _log_sigmoid(decay_ref[...])
    pick = lax.broadcasted_iota(jnp.int32, lg_all.shape, 1) == head
    lg = jnp.sum(jnp.where(pick, lg_all, 0.0), axis=1, keepdims=True)
    lg_f = lg[0:1, :]
    lg_b = lg[1:2, :]

    ri = lax.broadcasted_iota(jnp.int32, (c, c), 0).astype(F32)
    ci = lax.broadcasted_iota(jnp.int32, (c, c), 1).astype(F32)
    diff = ri - ci
    decay = (jnp.where(diff >= 0, jnp.exp(jnp.where(diff >= 0, diff, 0.0) * lg_f), 0.0)
             + jnp.where(diff <= 0, jnp.exp(jnp.where(diff <= 0, -diff, 0.0) * lg_b), 0.0))
    pos_col = lax.broadcasted_iota(jnp.int32, (c, 1), 0).astype(F32)
    pos_row = lax.broadcasted_iota(jnp.int32, (1, c), 1).astype(F32)
    xi_f = jnp.exp((pos_col + 1.0) * lg_f)
    xi_b = jnp.exp((c - pos_col) * lg_b)
    zeta_f = jnp.exp((c - 1.0 - pos_row) * lg_f)
    zeta_b = jnp.exp(pos_row * lg_b)
    cd_f = jnp.exp(c * lg_f)
    cd_b = jnp.exp(c * lg_b)

    def intra(n):
        r0 = pl.multiple_of(n * c, c)
        qn = q_ref[pl.ds(r0, c), :]
        kn = k_ref[pl.ds(r0, c), :]
        vn = v_ref[pl.ds(r0, c), :]
        s = lax.dot_general(qn, kn, NT_DIMS, preferred_element_type=F32) * decay
        acc_ref[pl.ds(r0, c), :] = jnp.dot(s.astype(BF16), vn, preferred_element_type=F32)
        kt = kn.astype(F32).T
        kz = jnp.concatenate([(kt * zeta_f).astype(BF16), (kt * zeta_b).astype(BF16)], axis=0)
        kv_ref[n] = jnp.dot(kz, vn, preferred_element_type=F32)

    def cross(n):
        r0 = pl.multiple_of(n * c, c)
        qn = q_ref[pl.ds(r0, c), :].astype(F32)
        qx = jnp.concatenate([(qn * xi_f).astype(BF16), (qn * xi_b).astype(BF16)], axis=1)
        o = acc_ref[pl.ds(r0, c), :] + jnp.dot(qx, kv_ref[n].astype(BF16), preferred_element_type=F32)
        mu = jnp.mean(o, axis=-1, keepdims=True)
        var = jnp.mean(jnp.square(o - mu), axis=-1, keepdims=True)
        on = (o - mu) * lax.rsqrt(var + EPS)
        gate = g_ref[pl.ds(r0, c), :].astype(F32)
        o_ref[pl.ds(r0, c), :] = (_silu(gate) * on).astype(BF16)

    def over_chunks(fn):
        def body(step, carry):
            for u in range(unroll):
                fn(step * unroll + u)
            return carry

        lax.fori_loop(0, n_chunks // unroll, body, 0)

    over_chunks(intra)

    for s in range(n_seq):
        if has_init:
            init_f = s0_ref[s, 0, 0, 0]
            init_b = s0_ref[s, 0, 1, 0]
        else:
            init_f = jnp.zeros((RET_DK, RET_DV), F32)
            init_b = init_f

        def fwd(n, state, s=s):
            kv = kv_ref[s * nc + n, 0:RET_DK, :]
            kv_ref[s * nc + n, 0:RET_DK, :] = state
            return cd_f * state + kv

        def bwd(n, state, s=s):
            m = s * nc + nc - 1 - n
            kv = kv_ref[m, RET_DK:, :]
            kv_ref[m, RET_DK:, :] = state
            return cd_b * state + kv

        final_f = lax.fori_loop(0, nc, fwd, init_f)
        final_b = lax.fori_loop(0, nc, bwd, init_b)
        if emit_state:
            st_ref[s, 0, 0, 0] = final_f
            st_ref[s, 0, 1, 0] = final_b

    over_chunks(cross)


RET_UNROLL = 16
RET_CTX_SEQS = 4


def _ret_core(proj, ret_decay_j, state_ret, layer_j, *, latent, new_state=None):
    if latent:
        nb, seq_len, n_seq, row0 = DEC_BATCH, DEC_SEQ, 1, CTX_ROWS // DEC_SEQ
    else:
        nb, seq_len, n_seq, row0 = BATCH // RET_CTX_SEQS, SEQ, RET_CTX_SEQS, 0
    rows = n_seq * seq_len
    kcol = RET_HEADS * RET_DK // RET_DK
    vcol = 2 * RET_HEADS * RET_DK // RET_DV
    gcol = vcol + RET_HEADS
    in_specs = [
        pl.BlockSpec((2, RET_HEADS), lambda b, h: (0, 0)),
        pl.BlockSpec((rows, RET_DK), lambda b, h: (row0 + b, h)),
        pl.BlockSpec((rows, RET_DK), lambda b, h: (row0 + b, kcol + h)),
        pl.BlockSpec((rows, RET_DV), lambda b, h: (row0 + b, vcol + h)),
        pl.BlockSpec((rows, RET_DV), lambda b, h: (row0 + b, gcol + h)),
    ]
    args = [ret_decay_j, proj, proj, proj, proj]
    if latent:
        in_specs.append(pl.BlockSpec((1, 1, 2, 1, RET_DK, RET_DV), lambda b, h: (b, layer_j, 0, h, 0, 0)))
        args.append(state_ret)
    aliases = {}
    if new_state is not None:
        aliases = {len(args): 1}
        in_specs.append(pl.BlockSpec(memory_space=pl.ANY))
        args.append(new_state)
    out_specs = [pl.BlockSpec((rows, RET_DV), lambda b, h: (b, h))]
    out_shape = [jax.ShapeDtypeStruct((nb * rows, RET_HEADS * RET_DV), BF16)]
    if not latent:
        out_specs.append(pl.BlockSpec((n_seq, 1, 2, 1, RET_DK, RET_DV), lambda b, h: (b, layer_j, 0, h, 0, 0)))
        out_shape.append(jax.ShapeDtypeStruct((BATCH, (DEPTH + 1) // 2, 2, RET_HEADS, RET_DK, RET_DV), F32))
    return pl.pallas_call(
        functools.partial(_ret_core_kernel, seq_len=seq_len, n_seq=n_seq, has_init=latent,
                          emit_state=not latent, n_alias=len(aliases)),
        grid=(nb, RET_HEADS),
        in_specs=in_specs,
        out_specs=out_specs,
        out_shape=out_shape,
        input_output_aliases=aliases,
        scratch_shapes=[
            pltpu.VMEM((rows, RET_DV), F32),
            pltpu.VMEM((rows // RET_CHUNK, 2 * RET_DK, RET_DV), F32),
        ],
        compiler_params=_params(2),
        name="ret_core_lat" if latent else "ret_core_ctx",
    )(*args)


ATT_Q_W = ATT_HEADS * HEAD_DIM
ATT_KV_W = ATT_KV_HEADS * HEAD_DIM


def _group_mean_sq(xc, gmat):
    sq = xc * xc
    hi = sq.astype(BF16)
    lo = (sq - hi.astype(F32)).astype(BF16)
    return jnp.dot(hi, gmat, preferred_element_type=F32) + jnp.dot(lo, gmat, preferred_element_type=F32)


SEQS_PER_TILE = TM // SEQ


def _att_proj_kernel(*refs, n_alias):
    n_src = 1 + N_WINDOWS + 3
    (gain_ref, sh_ref, sc_ref, w_ref, qg_ref, kg_ref, cos_ref, sin_ref) = refs[n_src:n_src + 8]
    o_ref, xo_ref, kc_ref, vc_ref, wbf_ref, buf_ref = refs[n_src + 8 + n_alias:]
    i = pl.program_id(0)

    @pl.when(i == 0)
    def _():
        wbf_ref[...] = w_ref[0].astype(BF16)

    x = _combine_tile(i, refs[0], refs[1:1 + N_WINDOWS], *refs[1 + N_WINDOWS:n_src], buf_ref)
    xo_ref[...] = x
    h = _norm_mod(x, gain_ref[...], sh_ref[0], sc_ref[0]).astype(BF16)
    acc = jnp.dot(h, wbf_ref[...], preferred_element_type=F32)

    def to_cache(cache_ref, chunk, first_head):
        @pl.when(i < N_CTX_TILES)
        def _():
            for s in range(SEQS_PER_TILE):
                for hh in range(LANES // HEAD_DIM):
                    cache_ref[s, 0, first_head + hh] = chunk[s * SEQ:(s + 1) * SEQ, hh * HEAD_DIM:(hh + 1) * HEAD_DIM]

    r = lax.broadcasted_iota(jnp.int32, (LANES, LANES), 0) // HEAD_DIM
    c = lax.broadcasted_iota(jnp.int32, (LANES, LANES), 1) // HEAD_DIM
    gmat = jnp.where(r == c, 1.0 / HEAD_DIM, 0.0).astype(BF16)
    cos = cos_ref[...]
    sin = sin_ref[...]
    n_q = ATT_Q_W // LANES
    n_kv = ATT_KV_W // LANES
    heads_per_chunk = LANES // HEAD_DIM
    for cidx in range(n_q + n_kv):
        xc = acc[:, cidx * LANES:(cidx + 1) * LANES]
        gain = qg_ref[...] if cidx < n_q else kg_ref[...]
        normed = xc * lax.rsqrt(_group_mean_sq(xc, gmat) + EPS) * gain
        if cidx >= n_q:
            to_cache(kc_ref, normed, (cidx - n_q) * heads_per_chunk)
        o_ref[:, cidx * LANES:(cidx + 1) * LANES] = _rope_chunk(normed, cos, sin, HEAD_DIM // 4).astype(BF16)
    for vidx in range(n_kv):
        v = acc[:, ATT_Q_W + ATT_KV_W + vidx * LANES:ATT_Q_W + ATT_KV_W + (vidx + 1) * LANES]
        to_cache(vc_ref, v, vidx * heads_per_chunk)
    o_ref[:, ATT_Q_W + ATT_KV_W:] = acc[:, ATT_Q_W + ATT_KV_W:].astype(BF16)


def _att_proj(pending, gain, shift, scale, w_in, layer_j, q_gain, k_gain, cos, sin, caches):
    n_alias = len(caches)
    cache_shape = caches[0].shape
    cache_spec = pl.BlockSpec((SEQS_PER_TILE, 1, ATT_KV_HEADS, SEQ, HEAD_DIM),
                              lambda i, *_: (jnp.minimum(i, N_CTX_TILES - 1), layer_j, 0, 0, 0))
    src_specs, src_args, scratch = _pending_specs(pending)
    in_specs = src_specs + [
        pl.BlockSpec((1, D_MODEL), lambda i, *_: (0, 0)),
        pl.BlockSpec((1, 1, D_MODEL), lambda i, *_: (_cond_row(i), 0, 0)),
        pl.BlockSpec((1, 1, D_MODEL), lambda i, *_: (_cond_row(i), 0, 0)),
        pl.BlockSpec((1, D_MODEL, ATT_IN), lambda i, *_: (layer_j, 0, 0)),
        pl.BlockSpec((1, LANES), lambda i, *_: (0, 0)),
        pl.BlockSpec((1, LANES), lambda i, *_: (0, 0)),
        pl.BlockSpec((TM, LANES), lambda i, *_: (_rope_row(i), 0)),
        pl.BlockSpec((TM, LANES), lambda i, *_: (_rope_row(i), 0)),
    ] + [pl.BlockSpec(memory_space=pl.ANY)] * n_alias
    args = src_args + [gain, shift, scale, w_in, q_gain, k_gain, cos, sin] + list(caches)
    first_cache = len(args) - n_alias
    return pl.pallas_call(
        functools.partial(_att_proj_kernel, n_alias=n_alias),
        grid_spec=pltpu.PrefetchScalarGridSpec(
            num_scalar_prefetch=1,
            grid=(T_ROWS // TM,),
            in_specs=in_specs,
            out_specs=[pl.BlockSpec((TM, ATT_IN), lambda i, *_: (i, 0)),
                       pl.BlockSpec((TM, D_MODEL), lambda i, *_: (i, 0)), cache_spec, cache_spec],
            scratch_shapes=[pltpu.VMEM((D_MODEL, ATT_IN), BF16)] + scratch,
        ),
        out_shape=[
            jax.ShapeDtypeStruct((T_ROWS, ATT_IN), BF16),
            jax.ShapeDtypeStruct((T_ROWS, D_MODEL), F32),
            jax.ShapeDtypeStruct(cache_shape, F32),
            jax.ShapeDtypeStruct(cache_shape, F32),
        ],
        input_output_aliases={first_cache: 2, first_cache + 1: 3},
        compiler_params=_params(1),
        name="att_proj",
    )(*args)


SINK_ROWS = 16
TN_DIMS = (((0,), (0,)), ((), ()))


def _ones_column(n):
    lane = lax.broadcasted_iota(jnp.int32, (n, HEAD_DIM), 1)
    return jnp.where(lane == 0, 1.0, 0.0).astype(BF16)


def _sink_softmax_pv(qs, sink_row, blocks):
    r = qs.shape[0]
    q_scaled = qs * jnp.asarray(ATT_SCALE, BF16)
    scores = []
    values = []
    for k, v, bias in blocks:
        s = lax.dot_general(k, q_scaled, NT_DIMS, preferred_element_type=F32)
        scores.append(s if bias is None else s + bias)
        values.append(jnp.concatenate([v, _ones_column(v.shape[0])], axis=1))
    row = lax.broadcasted_iota(jnp.int32, (SINK_ROWS, r), 0)
    scores.append(jnp.where(row == 0, sink_row, -jnp.inf))
    values.append(jnp.concatenate([jnp.zeros((SINK_ROWS, HEAD_DIM), BF16), _ones_column(SINK_ROWS)], axis=1))
    st = jnp.concatenate(scores, axis=0)
    m = jnp.max(st, axis=0, keepdims=True)
    pt = jnp.exp(st - m).astype(BF16)
    ov = lax.dot_general(pt, jnp.concatenate(values, axis=0), TN_DIMS, preferred_element_type=F32)
    return ov[:, :HEAD_DIM] / ov[:, HEAD_DIM:HEAD_DIM + 1]


def _stack_heads(q, kvh, rows):
    parts = []
    for g in range(ATT_GROUP):
        hd = kvh * ATT_GROUP + g
        parts.append(q[:, hd * HEAD_DIM:(hd + 1) * HEAD_DIM])
    return jnp.concatenate(parts, axis=0)


def _sink_row(sink_ref, kvh, rows):
    head = lax.broadcasted_iota(jnp.int32, (1, ATT_GROUP * rows), 1) // rows
    out = jnp.full((1, ATT_GROUP * rows), sink_ref[kvh * ATT_GROUP], F32)
    for g in range(1, ATT_GROUP):
        out = jnp.where(head == g, sink_ref[kvh * ATT_GROUP + g], out)
    return out


def _store_heads(o_ref, o, kvh, rows):
    for g in range(ATT_GROUP):
        hd = kvh * ATT_GROUP + g
        o_ref[:, hd * HEAD_DIM:(hd + 1) * HEAD_DIM] = o[g * rows:(g + 1) * rows, :].astype(BF16)


def _ctx_att_kernel(sink_ref, q_ref, k_ref, v_ref, o_ref):
    q = q_ref[...]
    for kvh in range(ATT_KV_HEADS):
        qs = _stack_heads(q, kvh, SEQ)
        k = k_ref[:, kvh * HEAD_DIM:(kvh + 1) * HEAD_DIM]
        v = v_ref[:, kvh * HEAD_DIM:(kvh + 1) * HEAD_DIM]
        o = _sink_softmax_pv(qs, _sink_row(sink_ref, kvh, SEQ), [(k, v, None)])
        _store_heads(o_ref, o, kvh, SEQ)


def _lat_att_kernel(sink_ref, q_ref, k_ref, v_ref, ck_ref, cv_ref, o_ref):
    n = pl.program_id(1)
    nb = DEC_SEQ // ATT_BLOCK
    blk = ATT_BLOCK
    rows = ATT_GROUP * blk
    prev0 = pl.multiple_of(jnp.maximum(n - 1, 0) * blk, blk)
    cur0 = pl.multiple_of(n * blk, blk)
    next0 = pl.multiple_of(jnp.minimum(n + 1, nb - 1) * blk, blk)

    kj = lax.broadcasted_iota(jnp.int32, (3 * blk, rows), 0) - blk
    qi = lax.broadcasted_iota(jnp.int32, (3 * blk, rows), 1) % blk
    kabs = n * blk + kj
    valid = (jnp.abs(qi - kj) <= WINDOW) & (kabs >= 0) & (kabs < DEC_SEQ)
    bias = jnp.where(valid, 0.0, -jnp.inf)

    q = q_ref[...]
    for kvh in range(ATT_KV_HEADS):
        cols = slice(kvh * HEAD_DIM, (kvh + 1) * HEAD_DIM)
        qs = _stack_heads(q, kvh, blk)
        k_loc = jnp.concatenate([k_ref[pl.ds(prev0, blk), cols], k_ref[pl.ds(cur0, blk), cols],
                                 k_ref[pl.ds(next0, blk), cols]], axis=0)
        v_loc = jnp.concatenate([v_ref[pl.ds(prev0, blk), cols], v_ref[pl.ds(cur0, blk), cols],
                                 v_ref[pl.ds(next0, blk), cols]], axis=0)
        k_ctx = ck_ref[0, 0, kvh].astype(BF16)
        v_ctx = cv_ref[0, 0, kvh].astype(BF16)
        o = _sink_softmax_pv(qs, _sink_row(sink_ref, kvh, blk),
                             [(k_loc, v_loc, bias), (k_ctx, v_ctx, None)])
        _store_heads(o_ref, o, kvh, blk)


def _ctx_attention(qkv, sink):
    kcol = ATT_Q_W // ATT_KV_W
    return pl.pallas_call(
        _ctx_att_kernel,
        grid_spec=pltpu.PrefetchScalarGridSpec(
            num_scalar_prefetch=1,
            grid=(BATCH,),
            in_specs=[
                pl.BlockSpec((SEQ, ATT_Q_W), lambda b, s: (b, 0)),
                pl.BlockSpec((SEQ, ATT_KV_W), lambda b, s: (b, kcol)),
                pl.BlockSpec((SEQ, ATT_KV_W), lambda b, s: (b, kcol + 1)),
            ],
            out_specs=pl.BlockSpec((SEQ, ATT_Q_W), lambda b, s: (b, 0)),
        ),
        out_shape=jax.ShapeDtypeStruct((CTX_ROWS, ATT_Q_W), BF16),
        compiler_params=_params(1),
        name="ctx_attention",
    )(sink, qkv, qkv, qkv)


def _lat_attention(qkv, sink, cache_k, cache_v, layer_j):
    kcol = ATT_Q_W // ATT_KV_W
    nb = DEC_SEQ // ATT_BLOCK
    q0 = CTX_ROWS // ATT_BLOCK
    s0 = CTX_ROWS // DEC_SEQ
    cache_spec = pl.BlockSpec((1, 1, ATT_KV_HEADS, PAST_LEN, HEAD_DIM), lambda b, n, s: (b, layer_j, 0, 0, 0))
    return pl.pallas_call(
        _lat_att_kernel,
        grid_spec=pltpu.PrefetchScalarGridSpec(
            num_scalar_prefetch=1,
            grid=(DEC_BATCH, nb),
            in_specs=[
                pl.BlockSpec((ATT_BLOCK, ATT_Q_W), lambda b, n, s: (q0 + b * nb + n, 0)),
                pl.BlockSpec((DEC_SEQ, ATT_KV_W), lambda b, n, s: (s0 + b, kcol)),
                pl.BlockSpec((DEC_SEQ, ATT_KV_W), lambda b, n, s: (s0 + b, kcol + 1)),
                cache_spec,
                cache_spec,
            ],
            out_specs=pl.BlockSpec((ATT_BLOCK, ATT_Q_W), lambda b, n, s: (b * nb + n, 0)),
        ),
        out_shape=jax.ShapeDtypeStruct((LAT_ROWS, ATT_Q_W), BF16),
        compiler_params=_params(2),
        name="lat_attention",
    )(sink, qkv, qkv, qkv, cache_k, cache_v)


ROW_ALIGN = 16
S_SLOTS = 576
BUF_ROWS = S_SLOTS + TM_MOE
XW = D_MODEL + LANES
SLOT_LANE = 6
GATE_LANE0 = 8
REGION_TILES = 28
REGION_ROWS = REGION_TILES * TM_MOE
TAB_W = 16
HALF_WIN = TM_MOE // 2
N_TOK_TILES = T_ROWS // TM


def _router_kernel(ac_ref, al_ref, wo_ref, x_ref, ga_ref, gain_ref, sh_ref, sc_ref, wr_ref, bias_ref,
                   xo_ref, xs_hbm, row_ref, tab_ref, wobf_ref, buf_ref, off_ref, wide_ref, sem):
    i = pl.program_id(0)
    ng = N_GROUPS
    last = pl.num_programs(0) - 1

    @pl.when(i == 0)
    def _():
        wobf_ref[...] = wo_ref[0].astype(BF16)

    mixed = jnp.where(i < N_CTX_TILES, ac_ref[...], al_ref[...])
    x_new = x_ref[...] + ga_ref[0] * jnp.dot(mixed, wobf_ref[...], preferred_element_type=F32)
    xo_ref[...] = x_new

    @pl.when(i == 0)
    def _():
        buf_ref[...] = jnp.zeros_like(buf_ref)
        for g in range(ng):
            off_ref[g] = 0

        def clear(r, carry):
            for c in range(TAB_W):
                tab_ref[r, c] = 0
            return carry

        lax.fori_loop(0, N_TOK_TILES + 1, clear, 0)

    hf = _norm_mod(x_new, gain_ref[...], sh_ref[0], sc_ref[0])
    hb = hf.astype(BF16)
    hl = (hf - hb.astype(F32)).astype(BF16)
    both = jnp.dot(hb, wr_ref[...], preferred_element_type=F32)
    logits = (both[:, :LANES] + both[:, LANES:]
              + jnp.dot(hl, wr_ref[:, :LANES], preferred_element_type=F32))
    lt = logits.T[0:N_EXPERTS, :]
    scores = jax.nn.sigmoid(lt)
    sel = scores + bias_ref[...]
    xs = [sel[k * ng:(k + 1) * ng, :] for k in range(EXPERTS_PER_GROUP)]
    sc = [scores[k * ng:(k + 1) * ng, :] for k in range(EXPERTS_PER_GROUP)]
    a, b, c, d = xs
    gs = jnp.maximum(jnp.maximum(jnp.maximum(a + b, a + c), jnp.maximum(a + d, b + c)),
                     jnp.maximum(b + d, c + d))
    bv = gs[0:1, :]
    bg = jnp.zeros(bv.shape, jnp.int32)
    for g in range(1, ng):
        better = gs[g:g + 1, :] > bv
        bg = jnp.where(better, g, bg)
        bv = jnp.where(better, gs[g:g + 1, :], bv)
    giota = lax.broadcasted_iota(jnp.int32, (ng, TM), 0)
    onehot = giota == bg
    wk = []
    for k in range(EXPERTS_PER_GROUP):
        rank = jnp.zeros((ng, TM), F32)
        for j in range(EXPERTS_PER_GROUP):
            if j < k:
                rank = rank + (xs[j] >= xs[k]).astype(F32)
            elif j > k:
                rank = rank + (xs[j] > xs[k]).astype(F32)
        chosen = (rank < 2.0) & onehot
        wk.append(jnp.sum(jnp.where(chosen, sc[k], 0.0), axis=0, keepdims=True))
    den = wk[0] + wk[1] + wk[2] + wk[3]
    gates = [w / den for w in wk]

    oh = jnp.where(onehot, 1.0, 0.0)
    r = lax.broadcasted_iota(jnp.int32, (TM, TM), 0)
    cc = lax.broadcasted_iota(jnp.int32, (TM, TM), 1)
    tri = jnp.where(r < cc, 1.0, 0.0).astype(BF16)
    before = jnp.dot(oh.astype(BF16), tri, preferred_element_type=F32)
    rank_local = jnp.sum(jnp.where(onehot, before, 0.0), axis=0, keepdims=True).astype(jnp.int32)
    counts = [jnp.sum(jnp.where(bg == g, 1, 0)) for g in range(ng)]
    pads = [((cnt + ROW_ALIGN - 1) // ROW_ALIGN) * ROW_ALIGN for cnt in counts]
    segs = [0]
    for g in range(1, ng):
        segs.append(segs[-1] + pads[g - 1])
    total = segs[-1] + pads[-1]
    seg_of = jnp.zeros(bg.shape, jnp.int32)
    for g in range(1, ng):
        seg_of = jnp.where(bg == g, segs[g], seg_of)
    slot = seg_of + rank_local

    parts = []
    for gt in gates:
        hi = gt.astype(BF16).astype(F32)
        rest = gt - hi
        mid = rest.astype(BF16).astype(F32)
        parts += [hi, mid, (rest - mid).astype(BF16).astype(F32)]
    zero_row = jnp.zeros((1, TM), F32)
    info = jnp.concatenate(gates + [bg.astype(F32), zero_row, slot.astype(F32), zero_row] + parts
                           + [jnp.zeros((LANES - GATE_LANE0 - len(parts), TM), F32)], axis=0)
    rowinfo = info.T
    row_ref[...] = rowinfo

    h_ext = jnp.concatenate([hb, rowinfo.astype(BF16)], axis=1)
    srow = lax.broadcasted_iota(jnp.int32, (S_SLOTS, TM), 0)
    pick = jnp.where(srow == slot, 1.0, 0.0).astype(BF16)
    par = i % 2
    buf_ref[par, 0:S_SLOTS, :] = jnp.dot(pick, h_ext, preferred_element_type=F32).astype(BF16)

    def window_copy(g, src_row, dst_row, rows):
        return pltpu.make_async_copy(
            buf_ref.at[par, pl.ds(pl.multiple_of(src_row, ROW_ALIGN), rows), :],
            xs_hbm.at[pl.ds(pl.multiple_of(dst_row, ROW_ALIGN), rows), :],
            sem.at[g])

    def wait_windows(rows):
        for g in range(ng):
            window_copy(g, 0, 0, rows).wait()

    def wait_previous():
        @pl.when(wide_ref[0] == 1)
        def _():
            wait_windows(TM_MOE)

        @pl.when(wide_ref[0] == 0)
        def _():
            wait_windows(HALF_WIN)

    @pl.when(i > 0)
    def _():
        wait_previous()

    wide = jnp.maximum(jnp.maximum(pads[0], pads[1]), jnp.maximum(pads[2], pads[3])) > HALF_WIN
    offs = [off_ref[g] for g in range(ng)]

    def issue(rows):
        for g in range(ng):
            window_copy(g, segs[g], g * REGION_ROWS + offs[g], rows).start()

    @pl.when(wide)
    def _():
        issue(TM_MOE)

    @pl.when(jnp.logical_not(wide))
    def _():
        issue(HALF_WIN)

    wide_ref[0] = wide.astype(jnp.int32)
    for g in range(ng):
        tab_ref[i, g] = segs[g]
        tab_ref[i, ng + g] = offs[g]
        tab_ref[i, 2 * ng + 1 + g] = pads[g]
        off_ref[g] = offs[g] + pads[g]
    tab_ref[i, 2 * ng] = total

    @pl.when(i == last)
    def _():
        wait_previous()
        for g in range(ng):
            window_copy(g, S_SLOTS, g * REGION_ROWS + off_ref[g], TM_MOE).start()
            tab_ref[last + 1, g] = off_ref[g]
        wait_windows(TM_MOE)


def _mix_out_router(a_ctx, a_lat, w_out, layer_j, x, gate_a, gain, shift, scale, wr_parts, bias_col):
    k = a_ctx.shape[1]
    return pl.pallas_call(
        _router_kernel,
        grid=(N_TOK_TILES,),
        in_specs=[
            pl.BlockSpec((TM, k), lambda i: (jnp.minimum(i, N_CTX_TILES - 1), 0)),
            pl.BlockSpec((TM, k), lambda i: (jnp.maximum(i - N_CTX_TILES, 0), 0)),
            pl.BlockSpec((1, k, D_MODEL), lambda i: (layer_j, 0, 0), pipeline_mode=pl.Buffered(1)),
            pl.BlockSpec((TM, D_MODEL), lambda i: (i, 0)),
            pl.BlockSpec((1, 1, D_MODEL), lambda i: (_cond_row(i), 0, 0)),
            pl.BlockSpec((1, D_MODEL), lambda i: (0, 0)),
            pl.BlockSpec((1, 1, D_MODEL), lambda i: (_cond_row(i), 0, 0)),
            pl.BlockSpec((1, 1, D_MODEL), lambda i: (_cond_row(i), 0, 0)),
            pl.BlockSpec((D_MODEL, 2 * LANES), lambda i: (0, 0)),
            pl.BlockSpec((N_EXPERTS, 1), lambda i: (0, 0)),
        ],
        out_specs=[
            pl.BlockSpec((TM, D_MODEL), lambda i: (i, 0)),
            pl.BlockSpec(memory_space=pl.ANY),
            pl.BlockSpec((TM, LANES), lambda i: (i, 0)),
            pl.BlockSpec(memory_space=pltpu.SMEM),
        ],
        out_shape=[
            jax.ShapeDtypeStruct((T_ROWS, D_MODEL), F32),
            jax.ShapeDtypeStruct((N_GROUPS * REGION_ROWS, XW), BF16),
            jax.ShapeDtypeStruct((T_ROWS, LANES), F32),
            jax.ShapeDtypeStruct((N_TOK_TILES + 1, TAB_W), jnp.int32),
        ],
        scratch_shapes=[
            pltpu.VMEM((k, D_MODEL), BF16),
            pltpu.VMEM((2, BUF_ROWS, XW), BF16),
            pltpu.SMEM((N_GROUPS,), jnp.int32),
            pltpu.SMEM((1,), jnp.int32),
            pltpu.SemaphoreType.DMA((N_GROUPS,)),
        ],
        compiler_params=_params(1),
        name="mix_out_router",
    )(a_ctx, a_lat, w_out, x, gate_a, gain, shift, scale, wr_parts, bias_col)


UP_EXPERTS = 4


def _moe_up_kernel(tb_ref, tg_ref, tv_ref, xs_ref, w_ref, o_ref, wbf_ref):
    kk = pl.program_id(0)
    i = pl.program_id(1)
    new_w = (i == 0) | (tg_ref[i] != tg_ref[jnp.maximum(i - 1, 0)])

    @pl.when(new_w)
    def _():
        wbf_ref[...] = w_ref[...].astype(BF16)

    @pl.when(tv_ref[i] == 1)
    def _():
        xs = xs_ref[:, :D_MODEL]
        extra = xs_ref[:, D_MODEL:].astype(F32)
        lane = lax.broadcasted_iota(jnp.int32, extra.shape, 1)
        for e in range(UP_EXPERTS):
            hu = jnp.dot(xs, wbf_ref[e], preferred_element_type=F32)
            lane0 = GATE_LANE0 + 3 * (kk * UP_EXPERTS + e)
            gcol = jnp.sum(jnp.where((lane >= lane0) & (lane < lane0 + 3), extra, 0.0), axis=1, keepdims=True)
            a = _silu(hu[:, :D_EXPERT]) * hu[:, D_EXPERT:] * gcol
            o_ref[:, e * D_EXPERT:(e + 1) * D_EXPERT] = a.astype(BF16)

    @pl.when(tv_ref[i] != 1)
    def _():
        o_ref[...] = jnp.zeros_like(o_ref)


def _moe_up(tile_block, tile_group, tile_valid, xs, w_up, layer):
    steps = EXPERTS_PER_GROUP // UP_EXPERTS
    return pl.pallas_call(
        _moe_up_kernel,
        grid_spec=pltpu.PrefetchScalarGridSpec(
            num_scalar_prefetch=3,
            grid=(steps, N_MOE_TILES),
            in_specs=[
                pl.BlockSpec((TM_MOE, XW), lambda k, i, tb, tg, tv: (tb[i], 0)),
                pl.BlockSpec((UP_EXPERTS, D_MODEL, 2 * D_EXPERT),
                             lambda k, i, tb, tg, tv: ((layer * N_GROUPS + tg[i]) * steps + k, 0, 0)),
            ],
            out_specs=pl.BlockSpec((TM_MOE, UP_EXPERTS * D_EXPERT), lambda k, i, tb, tg, tv: (tb[i], k)),
            scratch_shapes=[pltpu.VMEM((UP_EXPERTS, D_MODEL, 2 * D_EXPERT), BF16)],
        ),
        out_shape=jax.ShapeDtypeStruct((N_GROUPS * REGION_ROWS, EXPERTS_PER_GROUP * D_EXPERT), BF16),
        compiler_params=_params(2),
        name="moe_up",
    )(tile_block, tile_group, tile_valid, xs, w_up)


def _moe_down_kernel(tb_ref, tg_ref, tv_ref, a_ref, w_ref, o_ref, wbf_ref):
    i = pl.program_id(0)
    new_w = (i == 0) | (tg_ref[i] != tg_ref[jnp.maximum(i - 1, 0)])

    @pl.when(new_w)
    def _():
        wbf_ref[...] = w_ref[0].astype(BF16)

    @pl.when(tv_ref[i] == 1)
    def _():
        o_ref[...] = jnp.dot(a_ref[...], wbf_ref[...], preferred_element_type=F32).astype(BF16)

    @pl.when(tv_ref[i] != 1)
    def _():
        o_ref[...] = jnp.zeros_like(o_ref)


def _moe_down(tile_block, tile_group, tile_valid, a, w_down_grouped, layer):
    kdim = EXPERTS_PER_GROUP * D_EXPERT
    return pl.pallas_call(
        _moe_down_kernel,
        grid_spec=pltpu.PrefetchScalarGridSpec(
            num_scalar_prefetch=3,
            grid=(N_MOE_TILES,),
            in_specs=[
                pl.BlockSpec((TM_MOE, kdim), lambda i, tb, tg, tv: (tb[i], 0)),
                pl.BlockSpec((1, kdim, D_MODEL), lambda i, tb, tg, tv: (layer * N_GROUPS + tg[i], 0, 0)),
            ],
            out_specs=pl.BlockSpec((TM_MOE, D_MODEL), lambda i, tb, tg, tv: (tb[i], 0)),
            scratch_shapes=[pltpu.VMEM((kdim, D_MODEL), BF16)],
        ),
        out_shape=jax.ShapeDtypeStruct((N_GROUPS * REGION_ROWS, D_MODEL), BF16),
        compiler_params=_params(1),
        name="moe_down",
    )(tile_block, tile_group, tile_valid, a, w_down_grouped)


N_WINDOWS = 2 * N_GROUPS


class PendingMoe(NamedTuple):
    tab_flat: jax.Array
    ys: jax.Array
    rowinfo: jax.Array
    x: jax.Array
    gate: jax.Array


def _combine_tile(i, tab_ref, windows, row_ref, x_ref, g_ref, buf_ref):
    first = windows[:N_GROUPS]
    second = windows[N_GROUPS:]

    @pl.when(i == 0)
    def _():
        buf_ref[...] = jnp.zeros_like(buf_ref)

    for g in range(N_GROUPS):
        start = pl.multiple_of(tab_ref[i * TAB_W + g], ROW_ALIGN)
        buf_ref[pl.ds(start, HALF_WIN), :] = first[g][...]

        @pl.when(tab_ref[i * TAB_W + 2 * N_GROUPS + 1 + g] > HALF_WIN)
        def _(g=g, start=start):
            buf_ref[pl.ds(start + HALF_WIN, HALF_WIN), :] = second[g][...]

    slot = row_ref[:, SLOT_LANE:SLOT_LANE + 1].astype(jnp.int32)
    scol = lax.broadcasted_iota(jnp.int32, (TM, S_SLOTS), 1)
    pick = jnp.where(scol == slot, 1.0, 0.0).astype(BF16)
    y = jnp.dot(pick, buf_ref[0:S_SLOTS, :], preferred_element_type=F32)
    return x_ref[...] + g_ref[0] * y


def _pending_specs(p):
    def window_spec(g, second):
        def index(i, tab):
            off = tab[i * TAB_W + N_GROUPS + g]
            if second:
                off = jnp.where(tab[i * TAB_W + 2 * N_GROUPS + 1 + g] > HALF_WIN, off + HALF_WIN, 0)
            return pl.multiple_of(off + g * REGION_ROWS, ROW_ALIGN), 0

        return pl.BlockSpec((pl.Element(HALF_WIN), pl.Element(D_MODEL)), index)

    specs = [window_spec(g, False) for g in range(N_GROUPS)] + [window_spec(g, True) for g in range(N_GROUPS)]
    specs += [
        pl.BlockSpec((TM, LANES), lambda i, tab: (i, 0)),
        pl.BlockSpec((TM, D_MODEL), lambda i, tab: (i, 0)),
        pl.BlockSpec((1, 1, D_MODEL), lambda i, tab: (_cond_row(i), 0, 0)),
    ]
    args = [p.tab_flat] + [p.ys] * N_WINDOWS + [p.rowinfo, p.x, p.gate]
    return specs, args, [pltpu.VMEM((BUF_ROWS, D_MODEL), BF16)]


def _moe_combine_kernel(tab_ref, *refs):
    row_ref, x_ref, g_ref, ctx_ref, lat_ref, buf_ref = refs[N_WINDOWS:]
    i = pl.program_id(0)
    new_x = _combine_tile(i, tab_ref, refs[:N_WINDOWS], row_ref, x_ref, g_ref, buf_ref)

    @pl.when(i < N_CTX_TILES)
    def _():
        ctx_ref[...] = new_x

    @pl.when(i >= N_CTX_TILES)
    def _():
        lat_ref[...] = new_x


def _moe_combine(pending):
    specs, args, scratch = _pending_specs(pending)
    return pl.pallas_call(
        _moe_combine_kernel,
        grid_spec=pltpu.PrefetchScalarGridSpec(
            num_scalar_prefetch=1,
            grid=(N_TOK_TILES,),
            in_specs=specs,
            out_specs=[pl.BlockSpec((TM, D_MODEL), lambda i, tab: (jnp.minimum(i, N_CTX_TILES - 1), 0)),
                       pl.BlockSpec((TM, D_MODEL), lambda i, tab: (jnp.maximum(i - N_CTX_TILES, 0), 0))],
            scratch_shapes=scratch,
        ),
        out_shape=[jax.ShapeDtypeStruct((CTX_ROWS, D_MODEL), F32), jax.ShapeDtypeStruct((LAT_ROWS, D_MODEL), F32)],
        compiler_params=_params(1),
        name="moe_combine",
    )(*args)


def _mix_out_and_moe(mix, x, gate_a, gain, shift, scale, gate, router_w, w_up, w_down_grouped, layer):
    wr_parts, bias_col = router_w
    x, xs, rowinfo, tab = _mix_out_router(*mix, x, gate_a, gain, shift, scale, wr_parts, bias_col)
    n_full = (tab[N_TOK_TILES, :N_GROUPS] + TM_MOE - 1) // TM_MOE
    ends = jnp.cumsum(n_full + 1)
    starts = ends - (n_full + 1)
    entry = jnp.arange(N_MOE_TILES, dtype=jnp.int32)
    grp = jnp.minimum(jnp.sum(entry[:, None] >= ends[None, :], axis=1), N_GROUPS - 1).astype(jnp.int32)
    idx = entry - starts[grp]
    listed = entry < ends[N_GROUPS - 1]
    closing = (N_GROUPS - 1) * REGION_TILES + n_full[N_GROUPS - 1]
    tile_block = jnp.where(listed, grp * REGION_TILES + idx, closing).astype(jnp.int32)
    tile_valid = (listed & (idx < n_full[grp])).astype(jnp.int32)
    a = _moe_up(tile_block, grp, tile_valid, xs, w_up, layer)
    ys = _moe_down(tile_block, grp, tile_valid, a, w_down_grouped, layer)
    return PendingMoe(tab.reshape(-1), ys, rowinfo, x, gate)


def kernel(x_prompt, x_sample, state_ret, cache_k, cache_v, c, c_ctx, w_mod, b_mod, norm_mix,
           norm_moe, ret_w_in, ret_decay, ret_w_out, att_w_in, att_q_norm, att_k_norm, att_sink,
           att_w_out, w_router, router_bias, moe_w_up, moe_w_down):
    cond8 = jnp.zeros((8, D_MODEL), F32).at[0].set(c_ctx).at[1:N_COND].set(c)
    mods = _modulation_all(cond8, w_mod, b_mod)
    mods = mods[:, :N_COND].reshape(DEPTH, N_COND, N_MOD, 1, D_MODEL)

    ret_cos, ret_sin = _rope_tables(RET_DK)
    att_cos, att_sin = _rope_tables(HEAD_DIM)

    perm = jnp.arange(N_EXPERTS).reshape(N_GROUPS, EXPERTS_PER_GROUP).T.reshape(-1)
    wr = jnp.zeros((D_MODEL, LANES), F32).at[:, :N_EXPERTS].set(w_router[:, perm])
    wr_hi = wr.astype(BF16)
    wr_lo = (wr - wr_hi.astype(F32)).astype(BF16)
    bias_col = router_bias[perm].astype(F32).reshape(N_EXPERTS, 1)
    router_w = (jnp.concatenate([wr_hi, wr_lo], axis=1), bias_col)

    w_up_all = moe_w_up.reshape(DEPTH * N_EXPERTS, D_MODEL, 2 * D_EXPERT)
    w_down_all = moe_w_down.reshape(DEPTH * N_GROUPS, EXPERTS_PER_GROUP * D_EXPERT, D_MODEL)
    ret_w_in_bf16 = ret_w_in.astype(BF16)
    new_state = jnp.zeros((BATCH, (DEPTH + 1) // 2, 2, RET_HEADS, RET_DK, RET_DV), F32)
    cache_shape = (BATCH, DEPTH // 2, ATT_KV_HEADS, SEQ, HEAD_DIM)
    new_kv = (jnp.zeros(cache_shape, F32), jnp.zeros(cache_shape, F32))
    source = (x_prompt.reshape(CTX_ROWS, D_MODEL), x_sample.reshape(LAT_ROWS, D_MODEL))
    for layer in range(DEPTH):
        sh_a, sc_a, g_a, sh_m, sc_m, g_m = [mods[layer, :, t] for t in range(N_MOD)]
        gain_mix = norm_mix[layer].reshape(1, D_MODEL)
        gain_moe = norm_moe[layer].reshape(1, D_MODEL)
        j = layer // 2
        if layer % 2 == 0:
            proj, x = _ret_proj(source, gain_mix, sh_a, sc_a, ret_w_in_bf16, j, ret_cos, ret_sin)
            o_ctx, new_state = _ret_core(proj, ret_decay[j], state_ret, j, latent=False, new_state=new_state)
            (o_lat,) = _ret_core(proj, ret_decay[j], state_ret, j, latent=True)
            mix = (o_ctx, o_lat, ret_w_out, j)
        else:
            q_gain = jnp.tile(att_q_norm[j], LANES // HEAD_DIM).reshape(1, LANES)
            k_gain = jnp.tile(att_k_norm[j], LANES // HEAD_DIM).reshape(1, LANES)
            qkv, x, new_k, new_v = _att_proj(source, gain_mix, sh_a, sc_a, att_w_in, j, q_gain, k_gain,
                                             att_cos, att_sin, new_kv)
            new_kv = (new_k, new_v)
            sink = att_sink[j].astype(F32)
            o_ctx = _ctx_attention(qkv, sink)
            o_lat = _lat_attention(qkv, sink, cache_k, cache_v, j)
            mix = (o_ctx, o_lat, att_w_out, j)
        source = _mix_out_and_moe(mix, x, g_a, gain_moe, sh_m, sc_m, g_m, router_w, w_up_all, w_down_all, layer)

    x_ctx, x_lat = _moe_combine(source)
    y_prompt = x_ctx.reshape(BATCH, SEQ, D_MODEL)
    y_sample = x_lat.reshape(DEC_BATCH, DEC_SEQ, D_MODEL)
    return (y_prompt, y_sample, new_state, new_kv[0], new_kv[1])
```

```python
import functools
from typing import NamedTuple

import jax
import jax.numpy as jnp
import numpy as np
from jax import lax
from jax.experimental import pallas as pl
from jax.experimental.pallas import tpu as pltpu

F32 = jnp.float32
BF16 = jnp.bfloat16

D_MODEL = 1024
BATCH = 16
SEQ = 256
DEPTH = 4
DEC_BATCH = 2
DEC_SEQ = 4096
PAST_LEN = 512
GRID_W = 64
N_MOD = 6
EPS = 1e-6
ROPE_BASE = 10000.0
RET_HEADS = 8
RET_DK = 128
RET_DV = 256
RET_CHUNK = 256
RET_IN = 2 * RET_HEADS * RET_DK + 2 * RET_HEADS * RET_DV
ATT_HEADS = 16
ATT_KV_HEADS = 4
ATT_GROUP = 4
HEAD_DIM = 64
WINDOW = 128
ATT_BLOCK = 128
ATT_IN = (ATT_HEADS + 2 * ATT_KV_HEADS) * HEAD_DIM
ATT_SCALE = HEAD_DIM ** -0.5
N_EXPERTS = 16
N_GROUPS = 4
EXPERTS_PER_GROUP = 4
D_EXPERT = 512

CTX_ROWS = BATCH * SEQ
LAT_ROWS = DEC_BATCH * DEC_SEQ
T_ROWS = CTX_ROWS + LAT_ROWS
N_COND = 1 + DEC_BATCH

LANES = 128
TM = 512
N_CTX_TILES = CTX_ROWS // TM
N_LAT_TILES = DEC_SEQ // TM
TM_MOE = 512
N_MOE_TILES = 36
VMEM_LIMIT = 52 * 1024 * 1024

NT_DIMS = (((1,), (1,)), ((), ()))


def _params(n_axes, vmem=VMEM_LIMIT):
    return pltpu.CompilerParams(dimension_semantics=("arbitrary",) * n_axes, vmem_limit_bytes=vmem)


def _cond_row(i):
    return (i * TM) // DEC_SEQ


def _rope_row(i):
    return jnp.where(i < N_CTX_TILES, 0, 1 + (i - N_CTX_TILES) % N_LAT_TILES)


def _norm_mod(x, gain, shift, scale):
    ms = jnp.mean(x * x, axis=-1, keepdims=True)
    y = x * lax.rsqrt(ms + EPS) * gain
    return y * (1.0 + scale) + shift


def _silu(x):
    return x * jax.nn.sigmoid(x)


def _mod_kernel(c_ref, w_ref, b_ref, o_ref):
    c = c_ref[...]
    s = _silu(c).astype(BF16)
    o_ref[0] = jnp.dot(s, w_ref[0].astype(BF16), preferred_element_type=F32) + b_ref[0]


def _modulation_all(cond8, w_mod, b_mod):
    tn = 1536
    n = N_MOD * D_MODEL
    return pl.pallas_call(
        _mod_kernel,
        grid=(DEPTH, n // tn),
        in_specs=[
            pl.BlockSpec((8, D_MODEL), lambda l, j: (0, 0)),
            pl.BlockSpec((1, D_MODEL, tn), lambda l, j: (l, 0, j)),
            pl.BlockSpec((1, 1, tn), lambda l, j: (l, 0, j)),
        ],
        out_specs=pl.BlockSpec((1, 8, tn), lambda l, j: (l, 0, j)),
        out_shape=jax.ShapeDtypeStruct((DEPTH, 8, n), F32),
        compiler_params=_params(2),
        name="modulation",
    )(cond8, w_mod, b_mod.reshape(DEPTH, 1, n))


def _rope_tables(head_dim):
    half = head_dim // 2
    quarter = half // 2
    t = np.arange(DEC_SEQ)
    row = (t // GRID_W).astype(np.float64)
    col = (t % GRID_W).astype(np.float64)
    inv_freq = ROPE_BASE ** (-np.arange(quarter, dtype=np.float64) / quarter)
    lane = np.arange(LANES)
    d = lane % head_dim
    w = d % half
    f = w % quarter
    pos = np.where((d // half)[None, :] == 0, row[:, None], col[:, None])
    ang = pos * inv_freq[f][None, :]
    cos = np.cos(ang)
    sin = np.where((w < quarter)[None, :], -np.sin(ang), np.sin(ang))
    cos = np.concatenate([np.ones((TM, LANES)), cos], axis=0).astype(np.float32)
    sin = np.concatenate([np.zeros((TM, LANES)), sin], axis=0).astype(np.float32)
    return jnp.asarray(cos), jnp.asarray(sin)


def _rope_chunk(xc, cos, sin, quarter):
    lane = lax.broadcasted_iota(jnp.int32, xc.shape, 1)
    first = (lane % (2 * quarter)) < quarter
    partner = jnp.where(first, pltpu.roll(xc, LANES - quarter, 1), pltpu.roll(xc, quarter, 1))
    return xc * cos + partner * sin


RET_TN = 1024
RET_Q_TILES = RET_HEADS * RET_DK // RET_TN
RET_QK_TILES = 2 * RET_Q_TILES


def _ret_proj_kernel(*refs, from_moe):
    i = pl.program_id(0)
    if from_moe:
        n_src = 1 + N_WINDOWS + 3
        x = _combine_tile(i, refs[0], refs[1:1 + N_WINDOWS], *refs[1 + N_WINDOWS:n_src], refs[-1])
    else:
        n_src = 2
        x = jnp.where(i < N_CTX_TILES, refs[0][...], refs[1][...])
    gain_ref, sh_ref, sc_ref, w_ref, cos_ref, sin_ref, o_ref, xo_ref = refs[n_src:n_src + 8]
    xo_ref[...] = x
    h = _norm_mod(x, gain_ref[...], sh_ref[0], sc_ref[0]).astype(BF16)
    cos = cos_ref[...]
    sin = sin_ref[...]
    for j in range(RET_IN // RET_TN):
        cols = slice(j * RET_TN, (j + 1) * RET_TN)
        acc = jnp.dot(h, w_ref[0, :, cols], preferred_element_type=F32)
        if j < RET_QK_TILES:
            scale = 1.0 if j < RET_Q_TILES else RET_DK ** -0.5
            for c in range(RET_TN // LANES):
                xc = acc[:, c * LANES:(c + 1) * LANES]
                lanes = slice(j * RET_TN + c * LANES, j * RET_TN + (c + 1) * LANES)
                o_ref[:, lanes] = (_rope_chunk(xc, cos, sin, RET_DK // 4) * scale).astype(BF16)
        else:
            o_ref[:, cols] = acc.astype(BF16)


def _ret_proj(source, gain, shift, scale, w_in_bf16, layer_j, cos, sin):
    from_moe = isinstance(source, PendingMoe)
    if from_moe:
        src_specs, src_args, scratch = _pending_specs(source)
        n_prefetch = 1
    else:
        src_specs = [pl.BlockSpec((TM, D_MODEL), lambda i, *_: (jnp.minimum(i, N_CTX_TILES - 1), 0)),
                     pl.BlockSpec((TM, D_MODEL), lambda i, *_: (jnp.maximum(i - N_CTX_TILES, 0), 0))]
        src_args, scratch, n_prefetch = list(source), [], 0
    return pl.pallas_call(
        functools.partial(_ret_proj_kernel, from_moe=from_moe),
        grid_spec=pltpu.PrefetchScalarGridSpec(
            num_scalar_prefetch=n_prefetch,
            grid=(T_ROWS // TM,),
            in_specs=src_specs + [
                pl.BlockSpec((1, D_MODEL), lambda i, *_: (0, 0)),
                pl.BlockSpec((1, 1, D_MODEL), lambda i, *_: (_cond_row(i), 0, 0)),
                pl.BlockSpec((1, 1, D_MODEL), lambda i, *_: (_cond_row(i), 0, 0)),
                pl.BlockSpec((1, D_MODEL, RET_IN), lambda i, *_: (layer_j, 0, 0), pipeline_mode=pl.Buffered(1)),
                pl.BlockSpec((TM, LANES), lambda i, *_: (_rope_row(i), 0)),
                pl.BlockSpec((TM, LANES), lambda i, *_: (_rope_row(i), 0)),
            ],
            out_specs=[pl.BlockSpec((TM, RET_IN), lambda i, *_: (i, 0)),
                       pl.BlockSpec((TM, D_MODEL), lambda i, *_: (i, 0))],
            scratch_shapes=scratch,
        ),
        out_shape=[jax.ShapeDtypeStruct((T_ROWS, RET_IN), BF16), jax.ShapeDtypeStruct((T_ROWS, D_MODEL), F32)],
        compiler_params=_params(1),
        name="ret_proj",
    )(*src_args, gain, shift, scale, w_in_bf16, cos, sin)


def _log_sigmoid(x):
    return -(jnp.maximum(-x, 0.0) + jnp.log(1.0 + jnp.exp(-jnp.abs(x))))


def _ret_core_kernel(*refs, seq_len, n_seq, has_init, emit_state, n_alias):
    it = iter(refs)
    decay_ref = next(it)
    q_ref = next(it)
    k_ref = next(it)
    v_ref = next(it)
    g_ref = next(it)
    s0_ref = next(it) if has_init else None
    for _ in range(n_alias):
        next(it)
    o_ref = next(it)
    st_ref = next(it) if emit_state else None
    acc_ref = next(it)
    kv_ref = next(it)

    c = RET_CHUNK
    nc = seq_len // c
    n_chunks = n_seq * nc
    unroll = min(RET_UNROLL, n_chunks)
    head = pl.program_id(1)

    lg_all = _log_sigmoid(decay_ref[...])
    pick = lax.broadcasted_iota(jnp.int32, lg_all.shape, 1) == head
    lg = jnp.sum(jnp.where(pick, lg_all, 0.0), axis=1, keepdims=True)
    lg_f = lg[0:1, :]
    lg_b = lg[1:2, :]

    ri = lax.broadcasted_iota(jnp.int32, (c, c), 0).astype(F32)
    ci = lax.broadcasted_iota(jnp.int32, (c, c), 1).astype(F32)
    diff = ri - ci
    decay = (jnp.where(diff >= 0, jnp.exp(jnp.where(diff >= 0, diff, 0.0) * lg_f), 0.0)
             + jnp.where(diff <= 0, jnp.exp(jnp.where(diff <= 0, -diff, 0.0) * lg_b), 0.0))
    pos_col = lax.broadcasted_iota(jnp.int32, (c, 1), 0).astype(F32)
    pos_row = lax.broadcasted_iota(jnp.int32, (1, c), 1).astype(F32)
    xi_f = jnp.exp((pos_col + 1.0) * lg_f)
    xi_b = jnp.exp((c - pos_col) * lg_b)
    zeta_f = jnp.exp((c - 1.0 - pos_row) * lg_f)
    zeta_b = jnp.exp(pos_row * lg_b)
    cd_f = jnp.exp(c * lg_f)
    cd_b = jnp.exp(c * lg_b)

    def intra(n):
        r0 = pl.multiple_of(n * c, c)
        qn = q_ref[pl.ds(r0, c), :]
        kn = k_ref[pl.ds(r0, c), :]
        vn = v_ref[pl.ds(r0, c), :]
        s = lax.dot_general(qn, kn, NT_DIMS, preferred_element_type=F32) * decay
        acc_ref[pl.ds(r0, c), :] = jnp.dot(s.astype(BF16), vn, preferred_element_type=F32)
        kt = kn.astype(F32).T
        kz = jnp.concatenate([(kt * zeta_f).astype(BF16), (kt * zeta_b).astype(BF16)], axis=0)
        kv_ref[n] = jnp.dot(kz, vn, preferred_element_type=F32)

    def cross(n):
        r0 = pl.multiple_of(n * c, c)
        qn = q_ref[pl.ds(r0, c), :].astype(F32)
        qx = jnp.concatenate([(qn * xi_f).astype(BF16), (qn * xi_b).astype(BF16)], axis=1)
        o = acc_ref[pl.ds(r0, c), :] + jnp.dot(qx, kv_ref[n].astype(BF16), preferred_element_type=F32)
        mu = jnp.mean(o, axis=-1, keepdims=True)
        var = jnp.mean(jnp.square(o - mu), axis=-1, keepdims=True)
        on = (o - mu) * lax.rsqrt(var + EPS)
        gate = g_ref[pl.ds(r0, c), :].astype(F32)
        o_ref[pl.ds(r0, c), :] = (_silu(gate) * on).astype(BF16)

    def over_chunks(fn):
        def body(step, carry):
            for u in range(unroll):
                fn(step * unroll + u)
            return carry

        lax.fori_loop(0, n_chunks // unroll, body, 0)

    over_chunks(intra)

    for s in range(n_seq):
        if has_init:
            init_f = s0_ref[s, 0, 0, 0]
            init_b = s0_ref[s, 0, 1, 0]
        else:
            init_f = jnp.zeros((RET_DK, RET_DV), F32)
            init_b = init_f

        def fwd(n, state, s=s):
            kv = kv_ref[s * nc + n, 0:RET_DK, :]
            kv_ref[s * nc + n, 0:RET_DK, :] = state
            return cd_f * state + kv

        def bwd(n, state, s=s):
            m = s * nc + nc - 1 - n
            kv = kv_ref[m, RET_DK:, :]
            kv_ref[m, RET_DK:, :] = state
            return cd_b * state + kv

        final_f = lax.fori_loop(0, nc, fwd, init_f)
        final_b = lax.fori_loop(0, nc, bwd, init_b)
        if emit_state:
            st_ref[s, 0, 0, 0] = final_f
            st_ref[s, 0, 1, 0] = final_b

    over_chunks(cross)


RET_UNROLL = 16
RET_CTX_SEQS = 8


def _ret_core(proj, ret_decay_j, state_ret, layer_j, *, latent, new_state=None):
    if latent:
        nb, seq_len, n_seq, row0 = DEC_BATCH, DEC_SEQ, 1, CTX_ROWS // DEC_SEQ
    else:
        nb, seq_len, n_seq, row0 = BATCH // RET_CTX_SEQS, SEQ, RET_CTX_SEQS, 0
    rows = n_seq * seq_len
    kcol = RET_HEADS * RET_DK // RET_DK
    vcol = 2 * RET_HEADS * RET_DK // RET_DV
    gcol = vcol + RET_HEADS
    in_specs = [
        pl.BlockSpec((2, RET_HEADS), lambda b, h: (0, 0)),
        pl.BlockSpec((rows, RET_DK), lambda b, h: (row0 + b, h)),
        pl.BlockSpec((rows, RET_DK), lambda b, h: (row0 + b, kcol + h)),
        pl.BlockSpec((rows, RET_DV), lambda b, h: (row0 + b, vcol + h)),
        pl.BlockSpec((rows, RET_DV), lambda b, h: (row0 + b, gcol + h)),
    ]
    args = [ret_decay_j, proj, proj, proj, proj]
    if latent:
        in_specs.append(pl.BlockSpec((1, 1, 2, 1, RET_DK, RET_DV), lambda b, h: (b, layer_j, 0, h, 0, 0)))
        args.append(state_ret)
    aliases = {}
    if new_state is not None:
        aliases = {len(args): 1}
        in_specs.append(pl.BlockSpec(memory_space=pl.ANY))
        args.append(new_state)
    out_specs = [pl.BlockSpec((rows, RET_DV), lambda b, h: (b, h))]
    out_shape = [jax.ShapeDtypeStruct((nb * rows, RET_HEADS * RET_DV), BF16)]
    if not latent:
        out_specs.append(pl.BlockSpec((n_seq, 1, 2, 1, RET_DK, RET_DV), lambda b, h: (b, layer_j, 0, h, 0, 0)))
        out_shape.append(jax.ShapeDtypeStruct((BATCH, (DEPTH + 1) // 2, 2, RET_HEADS, RET_DK, RET_DV), F32))
    return pl.pallas_call(
        functools.partial(_ret_core_kernel, seq_len=seq_len, n_seq=n_seq, has_init=latent,
                          emit_state=not latent, n_alias=len(aliases)),
        grid=(nb, RET_HEADS),
        in_specs=in_specs,
        out_specs=out_specs,
        out_shape=out_shape,
        input_output_aliases=aliases,
        scratch_shapes=[
            pltpu.VMEM((rows, RET_DV), F32),
            pltpu.VMEM((rows // RET_CHUNK, 2 * RET_DK, RET_DV), F32),
        ],
        compiler_params=_params(2),
        name="ret_core_lat" if latent else "ret_core_ctx",
    )(*args)


ATT_Q_W = ATT_HEADS * HEAD_DIM
ATT_KV_W = ATT_KV_HEADS * HEAD_DIM


def _group_mean_sq(xc, gmat):
    sq = xc * xc
    hi = sq.astype(BF16)
    lo = (sq - hi.astype(F32)).astype(BF16)
    return jnp.dot(hi, gmat, preferred_element_type=F32) + jnp.dot(lo, gmat, preferred_element_type=F32)


SEQS_PER_TILE = TM // SEQ


def _att_proj_kernel(*refs, n_alias):
    n_src = 1 + N_WINDOWS + 3
    (gain_ref, sh_ref, sc_ref, w_ref, qg_ref, kg_ref, cos_ref, sin_ref) = refs[n_src:n_src + 8]
    o_ref, xo_ref, kc_ref, vc_ref, wbf_ref, buf_ref = refs[n_src + 8 + n_alias:]
    i = pl.program_id(0)

    @pl.when(i == 0)
    def _():
        wbf_ref[...] = w_ref[0].astype(BF16)

    x = _combine_tile(i, refs[0], refs[1:1 + N_WINDOWS], *refs[1 + N_WINDOWS:n_src], buf_ref)
    xo_ref[...] = x
    h = _norm_mod(x, gain_ref[...], sh_ref[0], sc_ref[0]).astype(BF16)
    acc = jnp.dot(h, wbf_ref[...], preferred_element_type=F32)

    def to_cache(cache_ref, chunk, first_head):
        @pl.when(i < N_CTX_TILES)
        def _():
            for s in range(SEQS_PER_TILE):
                for hh in range(LANES // HEAD_DIM):
                    cache_ref[s, 0, first_head + hh] = chunk[s * SEQ:(s + 1) * SEQ, hh * HEAD_DIM:(hh + 1) * HEAD_DIM]

    r = lax.broadcasted_iota(jnp.int32, (LANES, LANES), 0) // HEAD_DIM
    c = lax.broadcasted_iota(jnp.int32, (LANES, LANES), 1) // HEAD_DIM
    gmat = jnp.where(r == c, 1.0 / HEAD_DIM, 0.0).astype(BF16)
    cos = cos_ref[...]
    sin = sin_ref[...]
    n_q = ATT_Q_W // LANES
    n_kv = ATT_KV_W // LANES
    heads_per_chunk = LANES // HEAD_DIM
    for cidx in range(n_q + n_kv):
        xc = acc[:, cidx * LANES:(cidx + 1) * LANES]
        gain = qg_ref[...] if cidx < n_q else kg_ref[...]
        normed = xc * lax.rsqrt(_group_mean_sq(xc, gmat) + EPS) * gain
        if cidx >= n_q:
            to_cache(kc_ref, normed, (cidx - n_q) * heads_per_chunk)
        o_ref[:, cidx * LANES:(cidx + 1) * LANES] = _rope_chunk(normed, cos, sin, HEAD_DIM // 4).astype(BF16)
    for vidx in range(n_kv):
        v = acc[:, ATT_Q_W + ATT_KV_W + vidx * LANES:ATT_Q_W + ATT_KV_W + (vidx + 1) * LANES]
        to_cache(vc_ref, v, vidx * heads_per_chunk)
    o_ref[:, ATT_Q_W + ATT_KV_W:] = acc[:, ATT_Q_W + ATT_KV_W:].astype(BF16)


def _att_proj(pending, gain, shift, scale, w_in, layer_j, q_gain, k_gain, cos, sin, caches):
    n_alias = len(caches)
    cache_shape = caches[0].shape
    cache_spec = pl.BlockSpec((SEQS_PER_TILE, 1, ATT_KV_HEADS, SEQ, HEAD_DIM),
                              lambda i, *_: (jnp.minimum(i, N_CTX_TILES - 1), layer_j, 0, 0, 0))
    src_specs, src_args, scratch = _pending_specs(pending)
    in_specs = src_specs + [
        pl.BlockSpec((1, D_MODEL), lambda i, *_: (0, 0)),
        pl.BlockSpec((1, 1, D_MODEL), lambda i, *_: (_cond_row(i), 0, 0)),
        pl.BlockSpec((1, 1, D_MODEL), lambda i, *_: (_cond_row(i), 0, 0)),
        pl.BlockSpec((1, D_MODEL, ATT_IN), lambda i, *_: (layer_j, 0, 0)),
        pl.BlockSpec((1, LANES), lambda i, *_: (0, 0)),
        pl.BlockSpec((1, LANES), lambda i, *_: (0, 0)),
        pl.BlockSpec((TM, LANES), lambda i, *_: (_rope_row(i), 0)),
        pl.BlockSpec((TM, LANES), lambda i, *_: (_rope_row(i), 0)),
    ] + [pl.BlockSpec(memory_space=pl.ANY)] * n_alias
    args = src_args + [gain, shift, scale, w_in, q_gain, k_gain, cos, sin] + list(caches)
    first_cache = len(args) - n_alias
    return pl.pallas_call(
        functools.partial(_att_proj_kernel, n_alias=n_alias),
        grid_spec=pltpu.PrefetchScalarGridSpec(
            num_scalar_prefetch=1,
            grid=(T_ROWS // TM,),
            in_specs=in_specs,
            out_specs=[pl.BlockSpec((TM, ATT_IN), lambda i, *_: (i, 0)),
                       pl.BlockSpec((TM, D_MODEL), lambda i, *_: (i, 0)), cache_spec, cache_spec],
            scratch_shapes=[pltpu.VMEM((D_MODEL, ATT_IN), BF16)] + scratch,
        ),
        out_shape=[
            jax.ShapeDtypeStruct((T_ROWS, ATT_IN), BF16),
            jax.ShapeDtypeStruct((T_ROWS, D_MODEL), F32),
            jax.ShapeDtypeStruct(cache_shape, F32),
            jax.ShapeDtypeStruct(cache_shape, F32),
        ],
        input_output_aliases={first_cache: 2, first_cache + 1: 3},
        compiler_params=_params(1),
        name="att_proj",
    )(*args)


SINK_ROWS = 16
TN_DIMS = (((0,), (0,)), ((), ()))


def _ones_column(n):
    lane = lax.broadcasted_iota(jnp.int32, (n, HEAD_DIM), 1)
    return jnp.where(lane == 0, 1.0, 0.0).astype(BF16)


def _sink_softmax_pv(qs, sink_row, blocks):
    r = qs.shape[0]
    q_scaled = qs * jnp.asarray(ATT_SCALE, BF16)
    scores = []
    values = []
    for k, v, bias in blocks:
        s = lax.dot_general(k, q_scaled, NT_DIMS, preferred_element_type=F32)
        scores.append(s if bias is None else s + bias)
        values.append(jnp.concatenate([v, _ones_column(v.shape[0])], axis=1))
    row = lax.broadcasted_iota(jnp.int32, (SINK_ROWS, r), 0)
    scores.append(jnp.where(row == 0, sink_row, -jnp.inf))
    values.append(jnp.concatenate([jnp.zeros((SINK_ROWS, HEAD_DIM), BF16), _ones_column(SINK_ROWS)], axis=1))
    st = jnp.concatenate(scores, axis=0)
    m = jnp.max(st, axis=0, keepdims=True)
    pt = jnp.exp(st - m).astype(BF16)
    ov = lax.dot_general(pt, jnp.concatenate(values, axis=0), TN_DIMS, preferred_element_type=F32)
    return ov[:, :HEAD_DIM] / ov[:, HEAD_DIM:HEAD_DIM + 1]


def _stack_heads(q, kvh, rows):
    parts = []
    for g in range(ATT_GROUP):
        hd = kvh * ATT_GROUP + g
        parts.append(q[:, hd * HEAD_DIM:(hd + 1) * HEAD_DIM])
    return jnp.concatenate(parts, axis=0)


def _sink_row(sink_ref, kvh, rows):
    head = lax.broadcasted_iota(jnp.int32, (1, ATT_GROUP * rows), 1) // rows
    out = jnp.full((1, ATT_GROUP * rows), sink_ref[kvh * ATT_GROUP], F32)
    for g in range(1, ATT_GROUP):
        out = jnp.where(head == g, sink_ref[kvh * ATT_GROUP + g], out)
    return out


def _store_heads(o_ref, o, kvh, rows):
    for g in range(ATT_GROUP):
        hd = kvh * ATT_GROUP + g
        o_ref[:, hd * HEAD_DIM:(hd + 1) * HEAD_DIM] = o[g * rows:(g + 1) * rows, :].astype(BF16)


def _ctx_att_kernel(sink_ref, q_ref, k_ref, v_ref, o_ref):
    q = q_ref[...]
    for kvh in range(ATT_KV_HEADS):
        qs = _stack_heads(q, kvh, SEQ)
        k = k_ref[:, kvh * HEAD_DIM:(kvh + 1) * HEAD_DIM]
        v = v_ref[:, kvh * HEAD_DIM:(kvh + 1) * HEAD_DIM]
        o = _sink_softmax_pv(qs, _sink_row(sink_ref, kvh, SEQ), [(k, v, None)])
        _store_heads(o_ref, o, kvh, SEQ)


def _lat_att_kernel(sink_ref, q_ref, k_ref, v_ref, ck_ref, cv_ref, o_ref):
    n = pl.program_id(1)
    nb = DEC_SEQ // ATT_BLOCK
    blk = ATT_BLOCK
    rows = ATT_GROUP * blk
    prev0 = pl.multiple_of(jnp.maximum(n - 1, 0) * blk, blk)
    cur0 = pl.multiple_of(n * blk, blk)
    next0 = pl.multiple_of(jnp.minimum(n + 1, nb - 1) * blk, blk)

    kj = lax.broadcasted_iota(jnp.int32, (3 * blk, rows), 0) - blk
    qi = lax.broadcasted_iota(jnp.int32, (3 * blk, rows), 1) % blk
    kabs = n * blk + kj
    valid = (jnp.abs(qi - kj) <= WINDOW) & (kabs >= 0) & (kabs < DEC_SEQ)
    bias = jnp.where(valid, 0.0, -jnp.inf)

    q = q_ref[...]
    for kvh in range(ATT_KV_HEADS):
        cols = slice(kvh * HEAD_DIM, (kvh + 1) * HEAD_DIM)
        qs = _stack_heads(q, kvh, blk)
        k_loc = jnp.concatenate([k_ref[pl.ds(prev0, blk), cols], k_ref[pl.ds(cur0, blk), cols],
                                 k_ref[pl.ds(next0, blk), cols]], axis=0)
        v_loc = jnp.concatenate([v_ref[pl.ds(prev0, blk), cols], v_ref[pl.ds(cur0, blk), cols],
                                 v_ref[pl.ds(next0, blk), cols]], axis=0)
        k_ctx = ck_ref[0, 0, kvh].astype(BF16)
        v_ctx = cv_ref[0, 0, kvh].astype(BF16)
        o = _sink_softmax_pv(qs, _sink_row(sink_ref, kvh, blk),
                             [(k_loc, v_loc, bias), (k_ctx, v_ctx, None)])
        _store_heads(o_ref, o, kvh, blk)


def _ctx_attention(qkv, sink):
    kcol = ATT_Q_W // ATT_KV_W
    return pl.pallas_call(
        _ctx_att_kernel,
        grid_spec=pltpu.PrefetchScalarGridSpec(
            num_scalar_prefetch=1,
            grid=(BATCH,),
            in_specs=[
                pl.BlockSpec((SEQ, ATT_Q_W), lambda b, s: (b, 0)),
                pl.BlockSpec((SEQ, ATT_KV_W), lambda b, s: (b, kcol)),
                pl.BlockSpec((SEQ, ATT_KV_W), lambda b, s: (b, kcol + 1)),
            ],
            out_specs=pl.BlockSpec((SEQ, ATT_Q_W), lambda b, s: (b, 0)),
        ),
        out_shape=jax.ShapeDtypeStruct((CTX_ROWS, ATT_Q_W), BF16),
        compiler_params=_params(1),
        name="ctx_attention",
    )(sink, qkv, qkv, qkv)


def _lat_attention(qkv, sink, cache_k, cache_v, layer_j):
    kcol = ATT_Q_W // ATT_KV_W
    nb = DEC_SEQ // ATT_BLOCK
    q0 = CTX_ROWS // ATT_BLOCK
    s0 = CTX_ROWS // DEC_SEQ
    cache_spec = pl.BlockSpec((1, 1, ATT_KV_HEADS, PAST_LEN, HEAD_DIM), lambda b, n, s: (b, layer_j, 0, 0, 0))
    return pl.pallas_call(
        _lat_att_kernel,
        grid_spec=pltpu.PrefetchScalarGridSpec(
            num_scalar_prefetch=1,
            grid=(DEC_BATCH, nb),
            in_specs=[
                pl.BlockSpec((ATT_BLOCK, ATT_Q_W), lambda b, n, s: (q0 + b * nb + n, 0)),
                pl.BlockSpec((DEC_SEQ, ATT_KV_W), lambda b, n, s: (s0 + b, kcol)),
                pl.BlockSpec((DEC_SEQ, ATT_KV_W), lambda b, n, s: (s0 + b, kcol + 1)),
                cache_spec,
                cache_spec,
            ],
            out_specs=pl.BlockSpec((ATT_BLOCK, ATT_Q_W), lambda b, n, s: (b * nb + n, 0)),
        ),
        out_shape=jax.ShapeDtypeStruct((LAT_ROWS, ATT_Q_W), BF16),
        compiler_params=_params(2),
        name="lat_attention",
    )(sink, qkv, qkv, qkv, cache_k, cache_v)


ROW_ALIGN = 16
S_SLOTS = 576
BUF_ROWS = S_SLOTS + TM_MOE
XW = D_MODEL + LANES
SLOT_LANE = 6
GATE_LANE0 = 8
REGION_TILES = 28
REGION_ROWS = REGION_TILES * TM_MOE
TAB_W = 16
HALF_WIN = TM_MOE // 2
N_TOK_TILES = T_ROWS // TM


def _router_kernel(ac_ref, al_ref, wo_ref, x_ref, ga_ref, gain_ref, sh_ref, sc_ref, wr_ref, bias_ref,
                   xo_ref, xs_hbm, row_ref, tab_ref, wobf_ref, buf_ref, off_ref, wide_ref, sem):
    i = pl.program_id(0)
    ng = N_GROUPS
    last = pl.num_programs(0) - 1

    @pl.when(i == 0)
    def _():
        wobf_ref[...] = wo_ref[0].astype(BF16)

    mixed = jnp.where(i < N_CTX_TILES, ac_ref[...], al_ref[...])
    x_new = x_ref[...] + ga_ref[0] * jnp.dot(mixed, wobf_ref[...], preferred_element_type=F32)
    xo_ref[...] = x_new

    @pl.when(i == 0)
    def _():
        buf_ref[...] = jnp.zeros_like(buf_ref)
        for g in range(ng):
            off_ref[g] = 0

        def clear(r, carry):
            for c in range(TAB_W):
                tab_ref[r, c] = 0
            return carry

        lax.fori_loop(0, N_TOK_TILES + 1, clear, 0)

    hf = _norm_mod(x_new, gain_ref[...], sh_ref[0], sc_ref[0])
    hb = hf.astype(BF16)
    hl = (hf - hb.astype(F32)).astype(BF16)
    both = jnp.dot(hb, wr_ref[...], preferred_element_type=F32)
    logits = (both[:, :LANES] + both[:, LANES:]
              + jnp.dot(hl, wr_ref[:, :LANES], preferred_element_type=F32))
    lt = logits.T[0:N_EXPERTS, :]
    scores = jax.nn.sigmoid(lt)
    sel = scores + bias_ref[...]
    xs = [sel[k * ng:(k + 1) * ng, :] for k in range(EXPERTS_PER_GROUP)]
    sc = [scores[k * ng:(k + 1) * ng, :] for k in range(EXPERTS_PER_GROUP)]
    a, b, c, d = xs
    gs = jnp.maximum(jnp.maximum(jnp.maximum(a + b, a + c), jnp.maximum(a + d, b + c)),
                     jnp.maximum(b + d, c + d))
    bv = gs[0:1, :]
    bg = jnp.zeros(bv.shape, jnp.int32)
    for g in range(1, ng):
        better = gs[g:g + 1, :] > bv
        bg = jnp.where(better, g, bg)
        bv = jnp.where(better, gs[g:g + 1, :], bv)
    giota = lax.broadcasted_iota(jnp.int32, (ng, TM), 0)
    onehot = giota == bg
    wk = []
    for k in range(EXPERTS_PER_GROUP):
        rank = jnp.zeros((ng, TM), F32)
        for j in range(EXPERTS_PER_GROUP):
            if j < k:
                rank = rank + (xs[j] >= xs[k]).astype(F32)
            elif j > k:
                rank = rank + (xs[j] > xs[k]).astype(F32)
        chosen = (rank < 2.0) & onehot
        wk.append(jnp.sum(jnp.where(chosen, sc[k], 0.0), axis=0, keepdims=True))
    den = wk[0] + wk[1] + wk[2] + wk[3]
    gates = [w / den for w in wk]

    oh = jnp.where(onehot, 1.0, 0.0)
    r = lax.broadcasted_iota(jnp.int32, (TM, TM), 0)
    cc = lax.broadcasted_iota(jnp.int32, (TM, TM), 1)
    tri = jnp.where(r < cc, 1.0, 0.0).astype(BF16)
    before = jnp.dot(oh.astype(BF16), tri, preferred_element_type=F32)
    rank_local = jnp.sum(jnp.where(onehot, before, 0.0), axis=0, keepdims=True).astype(jnp.int32)
    counts = [jnp.sum(jnp.where(bg == g, 1, 0)) for g in range(ng)]
    pads = [((cnt + ROW_ALIGN - 1) // ROW_ALIGN) * ROW_ALIGN for cnt in counts]
    segs = [0]
    for g in range(1, ng):
        segs.append(segs[-1] + pads[g - 1])
    total = segs[-1] + pads[-1]
    seg_of = jnp.zeros(bg.shape, jnp.int32)
    for g in range(1, ng):
        seg_of = jnp.where(bg == g, segs[g], seg_of)
    slot = seg_of + rank_local

    parts = []
    for gt in gates:
        hi = gt.astype(BF16).astype(F32)
        rest = gt - hi
        mid = rest.astype(BF16).astype(F32)
        parts += [hi, mid, (rest - mid).astype(BF16).astype(F32)]
    zero_row = jnp.zeros((1, TM), F32)
    info = jnp.concatenate(gates + [bg.astype(F32), zero_row, slot.astype(F32), zero_row] + parts
                           + [jnp.zeros((LANES - GATE_LANE0 - len(parts), TM), F32)], axis=0)
    rowinfo = info.T
    row_ref[...] = rowinfo

    h_ext = jnp.concatenate([hb, rowinfo.astype(BF16)], axis=1)
    srow = lax.broadcasted_iota(jnp.int32, (S_SLOTS, TM), 0)
    pick = jnp.where(srow == slot, 1.0, 0.0).astype(BF16)
    par = i % 2
    buf_ref[par, 0:S_SLOTS, :] = jnp.dot(pick, h_ext, preferred_element_type=F32).astype(BF16)

    def window_copy(g, src_row, dst_row, rows):
        return pltpu.make_async_copy(
            buf_ref.at[par, pl.ds(pl.multiple_of(src_row, ROW_ALIGN), rows), :],
            xs_hbm.at[pl.ds(pl.multiple_of(dst_row, ROW_ALIGN), rows), :],
            sem.at[g])

    def wait_windows(rows):
        for g in range(ng):
            window_copy(g, 0, 0, rows).wait()

    def wait_previous():
        @pl.when(wide_ref[0] == 1)
        def _():
            wait_windows(TM_MOE)

        @pl.when(wide_ref[0] == 0)
        def _():
            wait_windows(HALF_WIN)

    @pl.when(i > 0)
    def _():
        wait_previous()

    wide = jnp.maximum(jnp.maximum(pads[0], pads[1]), jnp.maximum(pads[2], pads[3])) > HALF_WIN
    offs = [off_ref[g] for g in range(ng)]

    def issue(rows):
        for g in range(ng):
            window_copy(g, segs[g], g * REGION_ROWS + offs[g], rows).start()

    @pl.when(wide)
    def _():
        issue(TM_MOE)

    @pl.when(jnp.logical_not(wide))
    def _():
        issue(HALF_WIN)

    wide_ref[0] = wide.astype(jnp.int32)
    for g in range(ng):
        tab_ref[i, g] = segs[g]
        tab_ref[i, ng + g] = offs[g]
        tab_ref[i, 2 * ng + 1 + g] = pads[g]
        off_ref[g] = offs[g] + pads[g]
    tab_ref[i, 2 * ng] = total

    @pl.when(i == last)
    def _():
        wait_previous()
        for g in range(ng):
            window_copy(g, S_SLOTS, g * REGION_ROWS + off_ref[g], TM_MOE).start()
            tab_ref[last + 1, g] = off_ref[g]
        wait_windows(TM_MOE)


def _mix_out_router(a_ctx, a_lat, w_out, layer_j, x, gate_a, gain, shift, scale, wr_parts, bias_col):
    k = a_ctx.shape[1]
    return pl.pallas_call(
        _router_kernel,
        grid=(N_TOK_TILES,),
        in_specs=[
            pl.BlockSpec((TM, k), lambda i: (jnp.minimum(i, N_CTX_TILES - 1), 0)),
            pl.BlockSpec((TM, k), lambda i: (jnp.maximum(i - N_CTX_TILES, 0), 0)),
            pl.BlockSpec((1, k, D_MODEL), lambda i: (layer_j, 0, 0), pipeline_mode=pl.Buffered(1)),
            pl.BlockSpec((TM, D_MODEL), lambda i: (i, 0)),
            pl.BlockSpec((1, 1, D_MODEL), lambda i: (_cond_row(i), 0, 0)),
            pl.BlockSpec((1, D_MODEL), lambda i: (0, 0)),
            pl.BlockSpec((1, 1, D_MODEL), lambda i: (_cond_row(i), 0, 0)),
            pl.BlockSpec((1, 1, D_MODEL), lambda i: (_cond_row(i), 0, 0)),
            pl.BlockSpec((D_MODEL, 2 * LANES), lambda i: (0, 0)),
            pl.BlockSpec((N_EXPERTS, 1), lambda i: (0, 0)),
        ],
        out_specs=[
            pl.BlockSpec((TM, D_MODEL), lambda i: (i, 0)),
            pl.BlockSpec(memory_space=pl.ANY),
            pl.BlockSpec((TM, LANES), lambda i: (i, 0)),
            pl.BlockSpec(memory_space=pltpu.SMEM),
        ],
        out_shape=[
            jax.ShapeDtypeStruct((T_ROWS, D_MODEL), F32),
            jax.ShapeDtypeStruct((N_GROUPS * REGION_ROWS, XW), BF16),
            jax.ShapeDtypeStruct((T_ROWS, LANES), F32),
            jax.ShapeDtypeStruct((N_TOK_TILES + 1, TAB_W), jnp.int32),
        ],
        scratch_shapes=[
            pltpu.VMEM((k, D_MODEL), BF16),
            pltpu.VMEM((2, BUF_ROWS, XW), BF16),
            pltpu.SMEM((N_GROUPS,), jnp.int32),
            pltpu.SMEM((1,), jnp.int32),
            pltpu.SemaphoreType.DMA((N_GROUPS,)),
        ],
        compiler_params=_params(1),
        name="mix_out_router",
    )(a_ctx, a_lat, w_out, x, gate_a, gain, shift, scale, wr_parts, bias_col)


UP_EXPERTS = 4


def _moe_up_kernel(tb_ref, tg_ref, tv_ref, xs_ref, w_ref, o_ref, wbf_ref):
    kk = pl.program_id(0)
    i = pl.program_id(1)
    new_w = (i == 0) | (tg_ref[i] != tg_ref[jnp.maximum(i - 1, 0)])

    @pl.when(new_w)
    def _():
        wbf_ref[...] = w_ref[...].astype(BF16)

    @pl.when(tv_ref[i] == 1)
    def _():
        xs = xs_ref[:, :D_MODEL]
        extra = xs_ref[:, D_MODEL:].astype(F32)
        lane = lax.broadcasted_iota(jnp.int32, extra.shape, 1)
        for e in range(UP_EXPERTS):
            hu = jnp.dot(xs, wbf_ref[e], preferred_element_type=F32)
            lane0 = GATE_LANE0 + 3 * (kk * UP_EXPERTS + e)
            gcol = jnp.sum(jnp.where((lane >= lane0) & (lane < lane0 + 3), extra, 0.0), axis=1, keepdims=True)
            a = _silu(hu[:, :D_EXPERT]) * hu[:, D_EXPERT:] * gcol
            o_ref[:, e * D_EXPERT:(e + 1) * D_EXPERT] = a.astype(BF16)

    @pl.when(tv_ref[i] != 1)
    def _():
        o_ref[...] = jnp.zeros_like(o_ref)


def _moe_up(tile_block, tile_group, tile_valid, xs, w_up, layer):
    steps = EXPERTS_PER_GROUP // UP_EXPERTS
    return pl.pallas_call(
        _moe_up_kernel,
        grid_spec=pltpu.PrefetchScalarGridSpec(
            num_scalar_prefetch=3,
            grid=(steps, N_MOE_TILES),
            in_specs=[
                pl.BlockSpec((TM_MOE, XW), lambda k, i, tb, tg, tv: (tb[i], 0)),
                pl.BlockSpec((UP_EXPERTS, D_MODEL, 2 * D_EXPERT),
                             lambda k, i, tb, tg, tv: ((layer * N_GROUPS + tg[i]) * steps + k, 0, 0)),
            ],
            out_specs=pl.BlockSpec((TM_MOE, UP_EXPERTS * D_EXPERT), lambda k, i, tb, tg, tv: (tb[i], k)),
            scratch_shapes=[pltpu.VMEM((UP_EXPERTS, D_MODEL, 2 * D_EXPERT), BF16)],
        ),
        out_shape=jax.ShapeDtypeStruct((N_GROUPS * REGION_ROWS, EXPERTS_PER_GROUP * D_EXPERT), BF16),
        compiler_params=_params(2),
        name="moe_up",
    )(tile_block, tile_group, tile_valid, xs, w_up)


def _moe_down_kernel(tb_ref, tg_ref, tv_ref, a_ref, w_ref, o_ref, wbf_ref):
    i = pl.program_id(0)
    new_w = (i == 0) | (tg_ref[i] != tg_ref[jnp.maximum(i - 1, 0)])

    @pl.when(new_w)
    def _():
        wbf_ref[...] = w_ref[0].astype(BF16)

    @pl.when(tv_ref[i] == 1)
    def _():
        o_ref[...] = jnp.dot(a_ref[...], wbf_ref[...], preferred_element_type=F32).astype(BF16)

    @pl.when(tv_ref[i] != 1)
    def _():
        o_ref[...] = jnp.zeros_like(o_ref)


def _moe_down(tile_block, tile_group, tile_valid, a, w_down_grouped, layer):
    kdim = EXPERTS_PER_GROUP * D_EXPERT
    return pl.pallas_call(
        _moe_down_kernel,
        grid_spec=pltpu.PrefetchScalarGridSpec(
            num_scalar_prefetch=3,
            grid=(N_MOE_TILES,),
            in_specs=[
                pl.BlockSpec((TM_MOE, kdim), lambda i, tb, tg, tv: (tb[i], 0)),
                pl.BlockSpec((1, kdim, D_MODEL), lambda i, tb, tg, tv: (layer * N_GROUPS + tg[i], 0, 0)),
            ],
            out_specs=pl.BlockSpec((TM_MOE, D_MODEL), lambda i, tb, tg, tv: (tb[i], 0)),
            scratch_shapes=[pltpu.VMEM((kdim, D_MODEL), BF16)],
        ),
        out_shape=jax.ShapeDtypeStruct((N_GROUPS * REGION_ROWS, D_MODEL), BF16),
        compiler_params=_params(1),
        name="moe_down",
    )(tile_block, tile_group, tile_valid, a, w_down_grouped)


N_WINDOWS = 2 * N_GROUPS


class PendingMoe(NamedTuple):
    tab_flat: jax.Array
    ys: jax.Array
    rowinfo: jax.Array
    x: jax.Array
    gate: jax.Array


def _combine_tile(i, tab_ref, windows, row_ref, x_ref, g_ref, buf_ref):
    first = windows[:N_GROUPS]
    second = windows[N_GROUPS:]

    @pl.when(i == 0)
    def _():
        buf_ref[...] = jnp.zeros_like(buf_ref)

    for g in range(N_GROUPS):
        start = pl.multiple_of(tab_ref[i * TAB_W + g], ROW_ALIGN)
        buf_ref[pl.ds(start, HALF_WIN), :] = first[g][...]

        @pl.when(tab_ref[i * TAB_W + 2 * N_GROUPS + 1 + g] > HALF_WIN)
        def _(g=g, start=start):
            buf_ref[pl.ds(start + HALF_WIN, HALF_WIN), :] = second[g][...]

    slot = row_ref[:, SLOT_LANE:SLOT_LANE + 1].astype(jnp.int32)
    scol = lax.broadcasted_iota(jnp.int32, (TM, S_SLOTS), 1)
    pick = jnp.where(scol == slot, 1.0, 0.0).astype(BF16)
    y = jnp.dot(pick, buf_ref[0:S_SLOTS, :], preferred_element_type=F32)
    return x_ref[...] + g_ref[0] * y


def _pending_specs(p):
    def window_spec(g, second):
        def index(i, tab):
            off = tab[i * TAB_W + N_GROUPS + g]
            if second:
                off = jnp.where(tab[i * TAB_W + 2 * N_GROUPS + 1 + g] > HALF_WIN, off + HALF_WIN, 0)
            return pl.multiple_of(off + g * REGION_ROWS, ROW_ALIGN), 0

        return pl.BlockSpec((pl.Element(HALF_WIN), pl.Element(D_MODEL)), index)

    specs = [window_spec(g, False) for g in range(N_GROUPS)] + [window_spec(g, True) for g in range(N_GROUPS)]
    specs += [
        pl.BlockSpec((TM, LANES), lambda i, tab: (i, 0)),
        pl.BlockSpec((TM, D_MODEL), lambda i, tab: (i, 0)),
        pl.BlockSpec((1, 1, D_MODEL), lambda i, tab: (_cond_row(i), 0, 0)),
    ]
    args = [p.tab_flat] + [p.ys] * N_WINDOWS + [p.rowinfo, p.x, p.gate]
    return specs, args, [pltpu.VMEM((BUF_ROWS, D_MODEL), BF16)]


def _moe_combine_kernel(tab_ref, *refs):
    row_ref, x_ref, g_ref, ctx_ref, lat_ref, buf_ref = refs[N_WINDOWS:]
    i = pl.program_id(0)
    new_x = _combine_tile(i, tab_ref, refs[:N_WINDOWS], row_ref, x_ref, g_ref, buf_ref)

    @pl.when(i < N_CTX_TILES)
    def _():
        ctx_ref[...] = new_x

    @pl.when(i >= N_CTX_TILES)
    def _():
        lat_ref[...] = new_x


def _moe_combine(pending):
    specs, args, scratch = _pending_specs(pending)
    return pl.pallas_call(
        _moe_combine_kernel,
        grid_spec=pltpu.PrefetchScalarGridSpec(
            num_scalar_prefetch=1,
            grid=(N_TOK_TILES,),
            in_specs=specs,
            out_specs=[pl.BlockSpec((TM, D_MODEL), lambda i, tab: (jnp.minimum(i, N_CTX_TILES - 1), 0)),
                       pl.BlockSpec((TM, D_MODEL), lambda i, tab: (jnp.maximum(i - N_CTX_TILES, 0), 0))],
            scratch_shapes=scratch,
        ),
        out_shape=[jax.ShapeDtypeStruct((CTX_ROWS, D_MODEL), F32), jax.ShapeDtypeStruct((LAT_ROWS, D_MODEL), F32)],
        compiler_params=_params(1),
        name="moe_combine",
    )(*args)


def _mix_out_and_moe(mix, x, gate_a, gain, shift, scale, gate, router_w, w_up, w_down_grouped, layer):
    wr_parts, bias_col = router_w
    x, xs, rowinfo, tab = _mix_out_router(*mix, x, gate_a, gain, shift, scale, wr_parts, bias_col)
    n_full = (tab[N_TOK_TILES, :N_GROUPS] + TM_MOE - 1) // TM_MOE
    ends = jnp.cumsum(n_full + 1)
    starts = ends - (n_full + 1)
    entry = jnp.arange(N_MOE_TILES, dtype=jnp.int32)
    grp = jnp.minimum(jnp.sum(entry[:, None] >= ends[None, :], axis=1), N_GROUPS - 1).astype(jnp.int32)
    idx = entry - starts[grp]
    listed = entry < ends[N_GROUPS - 1]
    closing = (N_GROUPS - 1) * REGION_TILES + n_full[N_GROUPS - 1]
    tile_block = jnp.where(listed, grp * REGION_TILES + idx, closing).astype(jnp.int32)
    tile_valid = (listed & (idx < n_full[grp])).astype(jnp.int32)
    a = _moe_up(tile_block, grp, tile_valid, xs, w_up, layer)
    ys = _moe_down(tile_block, grp, tile_valid, a, w_down_grouped, layer)
    return PendingMoe(tab.reshape(-1), ys, rowinfo, x, gate)


def kernel(x_prompt, x_sample, state_ret, cache_k, cache_v, c, c_ctx, w_mod, b_mod, norm_mix,
           norm_moe, ret_w_in, ret_decay, ret_w_out, att_w_in, att_q_norm, att_k_norm, att_sink,
           att_w_out, w_router, router_bias, moe_w_up, moe_w_down):
    cond8 = jnp.zeros((8, D_MODEL), F32).at[0].set(c_ctx).at[1:N_COND].set(c)
    mods = _modulation_all(cond8, w_mod, b_mod)
    mods = mods[:, :N_COND].reshape(DEPTH, N_COND, N_MOD, 1, D_MODEL)

    ret_cos, ret_sin = _rope_tables(RET_DK)
    att_cos, att_sin = _rope_tables(HEAD_DIM)

    perm = jnp.arange(N_EXPERTS).reshape(N_GROUPS, EXPERTS_PER_GROUP).T.reshape(-1)
    wr = jnp.zeros((D_MODEL, LANES), F32).at[:, :N_EXPERTS].set(w_router[:, perm])
    wr_hi = wr.astype(BF16)
    wr_lo = (wr - wr_hi.astype(F32)).astype(BF16)
    bias_col = router_bias[perm].astype(F32).reshape(N_EXPERTS, 1)
    router_w = (jnp.concatenate([wr_hi, wr_lo], axis=1), bias_col)

    w_up_all = moe_w_up.reshape(DEPTH * N_EXPERTS, D_MODEL, 2 * D_EXPERT)
    w_down_all = moe_w_down.reshape(DEPTH * N_GROUPS, EXPERTS_PER_GROUP * D_EXPERT, D_MODEL)
    ret_w_in_bf16 = ret_w_in.astype(BF16)
    new_state = jnp.zeros((BATCH, (DEPTH + 1) // 2, 2, RET_HEADS, RET_DK, RET_DV), F32)
    cache_shape = (BATCH, DEPTH // 2, ATT_KV_HEADS, SEQ, HEAD_DIM)
    new_kv = (jnp.zeros(cache_shape, F32), jnp.zeros(cache_shape, F32))
    source = (x_prompt.reshape(CTX_ROWS, D_MODEL), x_sample.reshape(LAT_ROWS, D_MODEL))
    for layer in range(DEPTH):
        sh_a, sc_a, g_a, sh_m, sc_m, g_m = [mods[layer, :, t] for t in range(N_MOD)]
        gain_mix = norm_mix[layer].reshape(1, D_MODEL)
        gain_moe = norm_moe[layer].reshape(1, D_MODEL)
        j = layer // 2
        if layer % 2 == 0:
            proj, x = _ret_proj(source, gain_mix, sh_a, sc_a, ret_w_in_bf16, j, ret_cos, ret_sin)
            o_ctx, new_state = _ret_core(proj, ret_decay[j], state_ret, j, latent=False, new_state=new_state)
            (o_lat,) = _ret_core(proj, ret_decay[j], state_ret, j, latent=True)
            mix = (o_ctx, o_lat, ret_w_out, j)
        else:
            q_gain = jnp.tile(att_q_norm[j], LANES // HEAD_DIM).reshape(1, LANES)
            k_gain = jnp.tile(att_k_norm[j], LANES // HEAD_DIM).reshape(1, LANES)
            qkv, x, new_k, new_v = _att_proj(source, gain_mix, sh_a, sc_a, att_w_in, j, q_gain, k_gain,
                                             att_cos, att_sin, new_kv)
            new_kv = (new_k, new_v)
            sink = att_sink[j].astype(F32)
            o_ctx = _ctx_attention(qkv, sink)
            o_lat = _lat_attention(qkv, sink, cache_k, cache_v, j)
            mix = (o_ctx, o_lat, att_w_out, j)
        source = _mix_out_and_moe(mix, x, g_a, gain_moe, sh_m, sc_m, g_m, router_w, w_up_all, w_down_all, layer)

    x_ctx, x_lat = _moe_combine(source)
    y_prompt = x_ctx.reshape(BATCH, SEQ, D_MODEL)
    y_sample = x_lat.reshape(DEC_BATCH, DEC_SEQ, D_MODEL)
    return (y_prompt, y_sample, new_state, new_kv[0], new_kv[1])
```

```python
import functools
from typing import NamedTuple

import jax
import jax.numpy as jnp
import numpy as np
from jax import lax
from jax.experimental import pallas as pl
from jax.experimental.pallas import tpu as pltpu

F32 = jnp.float32
BF16 = jnp.bfloat16

D_MODEL = 1024
BATCH = 16
SEQ = 256
DEPTH = 4
DEC_BATCH = 2
DEC_SEQ = 4096
PAST_LEN = 512
GRID_W = 64
N_MOD = 6
EPS = 1e-6
ROPE_BASE = 10000.0
RET_HEADS = 8
RET_DK = 128
RET_DV = 256
RET_CHUNK = 256
RET_IN = 2 * RET_HEADS * RET_DK + 2 * RET_HEADS * RET_DV
ATT_HEADS = 16
ATT_KV_HEADS = 4
ATT_GROUP = 4
HEAD_DIM = 64
WINDOW = 128
ATT_BLOCK = 128
ATT_IN = (ATT_HEADS + 2 * ATT_KV_HEADS) * HEAD_DIM
ATT_SCALE = HEAD_DIM ** -0.5
N_EXPERTS = 16
N_GROUPS = 4
EXPERTS_PER_GROUP = 4
D_EXPERT = 512

CTX_ROWS = BATCH * SEQ
LAT_ROWS = DEC_BATCH * DEC_SEQ
T_ROWS = CTX_ROWS + LAT_ROWS
N_COND = 1 + DEC_BATCH

LANES = 128
TM = 512
N_CTX_TILES = CTX_ROWS // TM
N_LAT_TILES = DEC_SEQ // TM
TM_MOE = 512
N_MOE_TILES = 36
VMEM_LIMIT = 52 * 1024 * 1024

NT_DIMS = (((1,), (1,)), ((), ()))


def _params(n_axes, vmem=VMEM_LIMIT):
    return pltpu.CompilerParams(dimension_semantics=("arbitrary",) * n_axes, vmem_limit_bytes=vmem)


def _cond_row(i):
    return (i * TM) // DEC_SEQ


def _rope_row(i):
    return jnp.where(i < N_CTX_TILES, 0, 1 + (i - N_CTX_TILES) % N_LAT_TILES)


def _norm_mod(x, gain, shift, scale):
    ms = jnp.mean(x * x, axis=-1, keepdims=True)
    y = x * lax.rsqrt(ms + EPS) * gain
    return y * (1.0 + scale) + shift


def _silu(x):
    return x * jax.nn.sigmoid(x)


def _mod_kernel(c_ref, w_ref, b_ref, o_ref):
    c = c_ref[...]
    s = _silu(c).astype(BF16)
    o_ref[0] = jnp.dot(s, w_ref[0].astype(BF16), preferred_element_type=F32) + b_ref[0]


def _modulation_all(cond8, w_mod, b_mod):
    tn = 1536
    n = N_MOD * D_MODEL
    return pl.pallas_call(
        _mod_kernel,
        grid=(DEPTH, n // tn),
        in_specs=[
            pl.BlockSpec((8, D_MODEL), lambda l, j: (0, 0)),
            pl.BlockSpec((1, D_MODEL, tn), lambda l, j: (l, 0, j)),
            pl.BlockSpec((1, 1, tn), lambda l, j: (l, 0, j)),
        ],
        out_specs=pl.BlockSpec((1, 8, tn), lambda l, j: (l, 0, j)),
        out_shape=jax.ShapeDtypeStruct((DEPTH, 8, n), F32),
        compiler_params=_params(2),
        name="modulation",
    )(cond8, w_mod, b_mod.reshape(DEPTH, 1, n))


def _rope_tables(head_dim):
    half = head_dim // 2
    quarter = half // 2
    t = np.arange(DEC_SEQ)
    row = (t // GRID_W).astype(np.float64)
    col = (t % GRID_W).astype(np.float64)
    inv_freq = ROPE_BASE ** (-np.arange(quarter, dtype=np.float64) / quarter)
    lane = np.arange(LANES)
    d = lane % head_dim
    w = d % half
    f = w % quarter
    pos = np.where((d // half)[None, :] == 0, row[:, None], col[:, None])
    ang = pos * inv_freq[f][None, :]
    cos = np.cos(ang)
    sin = np.where((w < quarter)[None, :], -np.sin(ang), np.sin(ang))
    cos = np.concatenate([np.ones((TM, LANES)), cos], axis=0).astype(np.float32)
    sin = np.concatenate([np.zeros((TM, LANES)), sin], axis=0).astype(np.float32)
    return jnp.asarray(cos), jnp.asarray(sin)


def _rope_chunk(xc, cos, sin, quarter):
    lane = lax.broadcasted_iota(jnp.int32, xc.shape, 1)
    first = (lane % (2 * quarter)) < quarter
    partner = jnp.where(first, pltpu.roll(xc, LANES - quarter, 1), pltpu.roll(xc, quarter, 1))
    return xc * cos + partner * sin


RET_TN = 1024
RET_Q_TILES = RET_HEADS * RET_DK // RET_TN
RET_QK_TILES = 2 * RET_Q_TILES


def _ret_proj_kernel(*refs, from_moe):
    i = pl.program_id(0)
    if from_moe:
        n_src = 1 + N_WINDOWS + 3
        x = _combine_tile(i, refs[0], refs[1:1 + N_WINDOWS], *refs[1 + N_WINDOWS:n_src], refs[-1])
    else:
        n_src = 2
        x = jnp.where(i < N_CTX_TILES, refs[0][...], refs[1][...])
    gain_ref, sh_ref, sc_ref, w_ref, cos_ref, sin_ref, o_ref, xo_ref = refs[n_src:n_src + 8]
    xo_ref[...] = x
    h = _norm_mod(x, gain_ref[...], sh_ref[0], sc_ref[0]).astype(BF16)
    cos = cos_ref[...]
    sin = sin_ref[...]
    for j in range(RET_IN // RET_TN):
        cols = slice(j * RET_TN, (j + 1) * RET_TN)
        acc = jnp.dot(h, w_ref[0, :, cols], preferred_element_type=F32)
        if j < RET_QK_TILES:
            scale = 1.0 if j < RET_Q_TILES else RET_DK ** -0.5
            for c in range(RET_TN // LANES):
                xc = acc[:, c * LANES:(c + 1) * LANES]
                lanes = slice(j * RET_TN + c * LANES, j * RET_TN + (c + 1) * LANES)
                o_ref[:, lanes] = (_rope_chunk(xc, cos, sin, RET_DK // 4) * scale).astype(BF16)
        else:
            o_ref[:, cols] = acc.astype(BF16)


def _ret_proj(source, gain, shift, scale, w_in_bf16, layer_j, cos, sin):
    from_moe = isinstance(source, PendingMoe)
    if from_moe:
        src_specs, src_args, scratch = _pending_specs(source)
        n_prefetch = 1
    else:
        src_specs = [pl.BlockSpec((TM, D_MODEL), lambda i, *_: (jnp.minimum(i, N_CTX_TILES - 1), 0)),
                     pl.BlockSpec((TM, D_MODEL), lambda i, *_: (jnp.maximum(i - N_CTX_TILES, 0), 0))]
        src_args, scratch, n_prefetch = list(source), [], 0
    return pl.pallas_call(
        functools.partial(_ret_proj_kernel, from_moe=from_moe),
        grid_spec=pltpu.PrefetchScalarGridSpec(
            num_scalar_prefetch=n_prefetch,
            grid=(T_ROWS // TM,),
            in_specs=src_specs + [
                pl.BlockSpec((1, D_MODEL), lambda i, *_: (0, 0)),
                pl.BlockSpec((1, 1, D_MODEL), lambda i, *_: (_cond_row(i), 0, 0)),
                pl.BlockSpec((1, 1, D_MODEL), lambda i, *_: (_cond_row(i), 0, 0)),
                pl.BlockSpec((1, D_MODEL, RET_IN), lambda i, *_: (layer_j, 0, 0), pipeline_mode=pl.Buffered(1)),
                pl.BlockSpec((TM, LANES), lambda i, *_: (_rope_row(i), 0)),
                pl.BlockSpec((TM, LANES), lambda i, *_: (_rope_row(i), 0)),
            ],
            out_specs=[pl.BlockSpec((TM, RET_IN), lambda i, *_: (i, 0)),
                       pl.BlockSpec((TM, D_MODEL), lambda i, *_: (i, 0))],
            scratch_shapes=scratch,
        ),
        out_shape=[jax.ShapeDtypeStruct((T_ROWS, RET_IN), BF16), jax.ShapeDtypeStruct((T_ROWS, D_MODEL), F32)],
        compiler_params=_params(1),
        name="ret_proj",
    )(*src_args, gain, shift, scale, w_in_bf16, cos, sin)


def _log_sigmoid(x):
    return -(jnp.maximum(-x, 0.0) + jnp.log(1.0 + jnp.exp(-jnp.abs(x))))


def _ret_core_kernel(*refs, seq_len, n_seq, chunk, has_init, emit_state, n_alias):
    it = iter(refs)
    decay_ref = next(it)
    q_ref = next(it)
    k_ref = next(it)
    v_ref = next(it)
    g_ref = next(it)
    s0_ref = next(it) if has_init else None
    for _ in range(n_alias):
        next(it)
    o_ref = next(it)
    st_ref = next(it) if emit_state else None
    acc_ref = next(it)
    kv_ref = next(it)

    c = chunk
    nc = seq_len // c
    n_chunks = n_seq * nc
    unroll = min(RET_UNROLL, n_chunks)
    head = pl.program_id(1)

    lg_all = _log_sigmoid(decay_ref[...])
    pick = lax.broadcasted_iota(jnp.int32, lg_all.shape, 1) == head
    lg = jnp.sum(jnp.where(pick, lg_all, 0.0), axis=1, keepdims=True)
    lg_f = lg[0:1, :]
    lg_b = lg[1:2, :]

    ri = lax.broadcasted_iota(jnp.int32, (c, c), 0).astype(F32)
    ci = lax.broadcasted_iota(jnp.int32, (c, c), 1).astype(F32)
    diff = ri - ci
    decay = (jnp.where(diff >= 0, jnp.exp(jnp.where(diff >= 0, diff, 0.0) * lg_f), 0.0)
             + jnp.where(diff <= 0, jnp.exp(jnp.where(diff <= 0, -diff, 0.0) * lg_b), 0.0))
    pos_col = lax.broadcasted_iota(jnp.int32, (c, 1), 0).astype(F32)
    pos_row = lax.broadcasted_iota(jnp.int32, (1, c), 1).astype(F32)
    xi_f = jnp.exp((pos_col + 1.0) * lg_f)
    xi_b = jnp.exp((c - pos_col) * lg_b)
    zeta_f = jnp.exp((c - 1.0 - pos_row) * lg_f)
    zeta_b = jnp.exp(pos_row * lg_b)
    cd_f = jnp.exp(c * lg_f)
    cd_b = jnp.exp(c * lg_b)

    def intra(n):
        r0 = pl.multiple_of(n * c, c)
        qn = q_ref[pl.ds(r0, c), :]
        kn = k_ref[pl.ds(r0, c), :]
        vn = v_ref[pl.ds(r0, c), :]
        s = lax.dot_general(qn, kn, NT_DIMS, preferred_element_type=F32) * decay
        acc_ref[pl.ds(r0, c), :] = jnp.dot(s.astype(BF16), vn, preferred_element_type=F32)
        kt = kn.astype(F32).T
        kz = jnp.concatenate([(kt * zeta_f).astype(BF16), (kt * zeta_b).astype(BF16)], axis=0)
        kv_ref[n] = jnp.dot(kz, vn, preferred_element_type=F32)

    def cross(n):
        r0 = pl.multiple_of(n * c, c)
        qn = q_ref[pl.ds(r0, c), :].astype(F32)
        qx = jnp.concatenate([(qn * xi_f).astype(BF16), (qn * xi_b).astype(BF16)], axis=1)
        o = acc_ref[pl.ds(r0, c), :] + jnp.dot(qx, kv_ref[n].astype(BF16), preferred_element_type=F32)
        mu = jnp.mean(o, axis=-1, keepdims=True)
        var = jnp.mean(jnp.square(o - mu), axis=-1, keepdims=True)
        on = (o - mu) * lax.rsqrt(var + EPS)
        gate = g_ref[pl.ds(r0, c), :].astype(F32)
        o_ref[pl.ds(r0, c), :] = (_silu(gate) * on).astype(BF16)

    def over_chunks(fn):
        def body(step, carry):
            for u in range(unroll):
                fn(step * unroll + u)
            return carry

        lax.fori_loop(0, n_chunks // unroll, body, 0)

    over_chunks(intra)

    for s in range(n_seq):
        if has_init:
            init_f = s0_ref[s, 0, 0, 0]
            init_b = s0_ref[s, 0, 1, 0]
        else:
            init_f = jnp.zeros((RET_DK, RET_DV), F32)
            init_b = init_f

        def fwd(n, state, s=s):
            kv = kv_ref[s * nc + n, 0:RET_DK, :]
            kv_ref[s * nc + n, 0:RET_DK, :] = state
            return cd_f * state + kv

        def bwd(n, state, s=s):
            m = s * nc + nc - 1 - n
            kv = kv_ref[m, RET_DK:, :]
            kv_ref[m, RET_DK:, :] = state
            return cd_b * state + kv

        final_f = lax.fori_loop(0, nc, fwd, init_f)
        final_b = lax.fori_loop(0, nc, bwd, init_b)
        if emit_state:
            st_ref[s, 0, 0, 0] = final_f
            st_ref[s, 0, 1, 0] = final_b

    over_chunks(cross)


RET_UNROLL = 16
RET_CTX_SEQS = 8


def _ret_core(proj, ret_decay_j, state_ret, layer_j, *, latent, new_state=None):
    if latent:
        nb, seq_len, n_seq, row0 = DEC_BATCH, DEC_SEQ, 1, CTX_ROWS // DEC_SEQ
    else:
        nb, seq_len, n_seq, row0 = BATCH // RET_CTX_SEQS, SEQ, RET_CTX_SEQS, 0
    rows = n_seq * seq_len
    chunk = min(RET_CHUNK, seq_len)
    kcol = RET_HEADS * RET_DK // RET_DK
    vcol = 2 * RET_HEADS * RET_DK // RET_DV
    gcol = vcol + RET_HEADS
    in_specs = [
        pl.BlockSpec((2, RET_HEADS), lambda b, h: (0, 0)),
        pl.BlockSpec((rows, RET_DK), lambda b, h: (row0 + b, h)),
        pl.BlockSpec((rows, RET_DK), lambda b, h: (row0 + b, kcol + h)),
        pl.BlockSpec((rows, RET_DV), lambda b, h: (row0 + b, vcol + h)),
        pl.BlockSpec((rows, RET_DV), lambda b, h: (row0 + b, gcol + h)),
    ]
    args = [ret_decay_j, proj, proj, proj, proj]
    if latent:
        in_specs.append(pl.BlockSpec((1, 1, 2, 1, RET_DK, RET_DV), lambda b, h: (b, layer_j, 0, h, 0, 0)))
        args.append(state_ret)
    aliases = {}
    if new_state is not None:
        aliases = {len(args): 1}
        in_specs.append(pl.BlockSpec(memory_space=pl.ANY))
        args.append(new_state)
    out_specs = [pl.BlockSpec((rows, RET_DV), lambda b, h: (b, h))]
    out_shape = [jax.ShapeDtypeStruct((nb * rows, RET_HEADS * RET_DV), BF16)]
    if not latent:
        out_specs.append(pl.BlockSpec((n_seq, 1, 2, 1, RET_DK, RET_DV), lambda b, h: (b, layer_j, 0, h, 0, 0)))
        out_shape.append(jax.ShapeDtypeStruct((BATCH, (DEPTH + 1) // 2, 2, RET_HEADS, RET_DK, RET_DV), F32))
    return pl.pallas_call(
        functools.partial(_ret_core_kernel, seq_len=seq_len, n_seq=n_seq, chunk=chunk, has_init=latent,
                          emit_state=not latent, n_alias=len(aliases)),
        grid=(nb, RET_HEADS),
        in_specs=in_specs,
        out_specs=out_specs,
        out_shape=out_shape,
        input_output_aliases=aliases,
        scratch_shapes=[
            pltpu.VMEM((rows, RET_DV), F32),
            pltpu.VMEM((rows // chunk, 2 * RET_DK, RET_DV), F32),
        ],
        compiler_params=_params(2),
        name="ret_core_lat" if latent else "ret_core_ctx",
    )(*args)


ATT_Q_W = ATT_HEADS * HEAD_DIM
ATT_KV_W = ATT_KV_HEADS * HEAD_DIM


def _group_mean_sq(xc, gmat):
    sq = xc * xc
    hi = sq.astype(BF16)
    lo = (sq - hi.astype(F32)).astype(BF16)
    return jnp.dot(hi, gmat, preferred_element_type=F32) + jnp.dot(lo, gmat, preferred_element_type=F32)


SEQS_PER_TILE = TM // SEQ


def _att_proj_kernel(*refs, n_alias):
    n_src = 1 + N_WINDOWS + 3
    (gain_ref, sh_ref, sc_ref, w_ref, qg_ref, kg_ref, cos_ref, sin_ref) = refs[n_src:n_src + 8]
    o_ref, xo_ref, kc_ref, vc_ref, wbf_ref, buf_ref = refs[n_src + 8 + n_alias:]
    i = pl.program_id(0)

    @pl.when(i == 0)
    def _():
        wbf_ref[...] = w_ref[0].astype(BF16)

    x = _combine_tile(i, refs[0], refs[1:1 + N_WINDOWS], *refs[1 + N_WINDOWS:n_src], buf_ref)
    xo_ref[...] = x
    h = _norm_mod(x, gain_ref[...], sh_ref[0], sc_ref[0]).astype(BF16)
    acc = jnp.dot(h, wbf_ref[...], preferred_element_type=F32)

    def to_cache(cache_ref, chunk, first_head):
        @pl.when(i < N_CTX_TILES)
        def _():
            for s in range(SEQS_PER_TILE):
                for hh in range(LANES // HEAD_DIM):
                    cache_ref[s, 0, first_head + hh] = chunk[s * SEQ:(s + 1) * SEQ, hh * HEAD_DIM:(hh + 1) * HEAD_DIM]

    r = lax.broadcasted_iota(jnp.int32, (LANES, LANES), 0) // HEAD_DIM
    c = lax.broadcasted_iota(jnp.int32, (LANES, LANES), 1) // HEAD_DIM
    gmat = jnp.where(r == c, 1.0 / HEAD_DIM, 0.0).astype(BF16)
    cos = cos_ref[...]
    sin = sin_ref[...]
    n_q = ATT_Q_W // LANES
    n_kv = ATT_KV_W // LANES
    heads_per_chunk = LANES // HEAD_DIM
    for cidx in range(n_q + n_kv):
        xc = acc[:, cidx * LANES:(cidx + 1) * LANES]
        gain = qg_ref[...] if cidx < n_q else kg_ref[...]
        normed = xc * lax.rsqrt(_group_mean_sq(xc, gmat) + EPS) * gain
        if cidx >= n_q:
            to_cache(kc_ref, normed, (cidx - n_q) * heads_per_chunk)
        o_ref[:, cidx * LANES:(cidx + 1) * LANES] = _rope_chunk(normed, cos, sin, HEAD_DIM // 4).astype(BF16)
    for vidx in range(n_kv):
        v = acc[:, ATT_Q_W + ATT_KV_W + vidx * LANES:ATT_Q_W + ATT_KV_W + (vidx + 1) * LANES]
        to_cache(vc_ref, v, vidx * heads_per_chunk)
    o_ref[:, ATT_Q_W + ATT_KV_W:] = acc[:, ATT_Q_W + ATT_KV_W:].astype(BF16)


def _att_proj(pending, gain, shift, scale, w_in, layer_j, q_gain, k_gain, cos, sin, caches):
    n_alias = len(caches)
    cache_shape = caches[0].shape
    cache_spec = pl.BlockSpec((SEQS_PER_TILE, 1, ATT_KV_HEADS, SEQ, HEAD_DIM),
                              lambda i, *_: (jnp.minimum(i, N_CTX_TILES - 1), layer_j, 0, 0, 0))
    src_specs, src_args, scratch = _pending_specs(pending)
    in_specs = src_specs + [
        pl.BlockSpec((1, D_MODEL), lambda i, *_: (0, 0)),
        pl.BlockSpec((1, 1, D_MODEL), lambda i, *_: (_cond_row(i), 0, 0)),
        pl.BlockSpec((1, 1, D_MODEL), lambda i, *_: (_cond_row(i), 0, 0)),
        pl.BlockSpec((1, D_MODEL, ATT_IN), lambda i, *_: (layer_j, 0, 0)),
        pl.BlockSpec((1, LANES), lambda i, *_: (0, 0)),
        pl.BlockSpec((1, LANES), lambda i, *_: (0, 0)),
        pl.BlockSpec((TM, LANES), lambda i, *_: (_rope_row(i), 0)),
        pl.BlockSpec((TM, LANES), lambda i, *_: (_rope_row(i), 0)),
    ] + [pl.BlockSpec(memory_space=pl.ANY)] * n_alias
    args = src_args + [gain, shift, scale, w_in, q_gain, k_gain, cos, sin] + list(caches)
    first_cache = len(args) - n_alias
    return pl.pallas_call(
        functools.partial(_att_proj_kernel, n_alias=n_alias),
        grid_spec=pltpu.PrefetchScalarGridSpec(
            num_scalar_prefetch=1,
            grid=(T_ROWS // TM,),
            in_specs=in_specs,
            out_specs=[pl.BlockSpec((TM, ATT_IN), lambda i, *_: (i, 0)),
                       pl.BlockSpec((TM, D_MODEL), lambda i, *_: (i, 0)), cache_spec, cache_spec],
            scratch_shapes=[pltpu.VMEM((D_MODEL, ATT_IN), BF16)] + scratch,
        ),
        out_shape=[
            jax.ShapeDtypeStruct((T_ROWS, ATT_IN), BF16),
            jax.ShapeDtypeStruct((T_ROWS, D_MODEL), F32),
            jax.ShapeDtypeStruct(cache_shape, F32),
            jax.ShapeDtypeStruct(cache_shape, F32),
        ],
        input_output_aliases={first_cache: 2, first_cache + 1: 3},
        compiler_params=_params(1),
        name="att_proj",
    )(*args)


SINK_ROWS = 16
TN_DIMS = (((0,), (0,)), ((), ()))


def _ones_column(n):
    lane = lax.broadcasted_iota(jnp.int32, (n, HEAD_DIM), 1)
    return jnp.where(lane == 0, 1.0, 0.0).astype(BF16)


def _sink_softmax_pv(qs, sink_row, blocks):
    r = qs.shape[0]
    q_scaled = qs * jnp.asarray(ATT_SCALE, BF16)
    scores = []
    values = []
    for k, v, bias in blocks:
        s = lax.dot_general(k, q_scaled, NT_DIMS, preferred_element_type=F32)
        scores.append(s if bias is None else s + bias)
        values.append(jnp.concatenate([v, _ones_column(v.shape[0])], axis=1))
    row = lax.broadcasted_iota(jnp.int32, (SINK_ROWS, r), 0)
    scores.append(jnp.where(row == 0, sink_row, -jnp.inf))
    values.append(jnp.concatenate([jnp.zeros((SINK_ROWS, HEAD_DIM), BF16), _ones_column(SINK_ROWS)], axis=1))
    st = jnp.concatenate(scores, axis=0)
    m = jnp.max(st, axis=0, keepdims=True)
    pt = jnp.exp(st - m).astype(BF16)
    ov = lax.dot_general(pt, jnp.concatenate(values, axis=0), TN_DIMS, preferred_element_type=F32)
    return ov[:, :HEAD_DIM] / ov[:, HEAD_DIM:HEAD_DIM + 1]


def _stack_heads(q, kvh, rows):
    parts = []
    for g in range(ATT_GROUP):
        hd = kvh * ATT_GROUP + g
        parts.append(q[:, hd * HEAD_DIM:(hd + 1) * HEAD_DIM])
    return jnp.concatenate(parts, axis=0)


def _sink_row(sink_ref, kvh, rows):
    head = lax.broadcasted_iota(jnp.int32, (1, ATT_GROUP * rows), 1) // rows
    out = jnp.full((1, ATT_GROUP * rows), sink_ref[kvh * ATT_GROUP], F32)
    for g in range(1, ATT_GROUP):
        out = jnp.where(head == g, sink_ref[kvh * ATT_GROUP + g], out)
    return out


def _store_heads(o_ref, o, kvh, rows):
    for g in range(ATT_GROUP):
        hd = kvh * ATT_GROUP + g
        o_ref[:, hd * HEAD_DIM:(hd + 1) * HEAD_DIM] = o[g * rows:(g + 1) * rows, :].astype(BF16)


def _ctx_att_kernel(sink_ref, q_ref, k_ref, v_ref, o_ref):
    q = q_ref[...]
    for kvh in range(ATT_KV_HEADS):
        qs = _stack_heads(q, kvh, SEQ)
        k = k_ref[:, kvh * HEAD_DIM:(kvh + 1) * HEAD_DIM]
        v = v_ref[:, kvh * HEAD_DIM:(kvh + 1) * HEAD_DIM]
        o = _sink_softmax_pv(qs, _sink_row(sink_ref, kvh, SEQ), [(k, v, None)])
        _store_heads(o_ref, o, kvh, SEQ)


def _lat_att_kernel(sink_ref, q_ref, k_ref, v_ref, ck_ref, cv_ref, o_ref):
    n = pl.program_id(1)
    nb = DEC_SEQ // ATT_BLOCK
    blk = ATT_BLOCK
    rows = ATT_GROUP * blk
    prev0 = pl.multiple_of(jnp.maximum(n - 1, 0) * blk, blk)
    cur0 = pl.multiple_of(n * blk, blk)
    next0 = pl.multiple_of(jnp.minimum(n + 1, nb - 1) * blk, blk)

    kj = lax.broadcasted_iota(jnp.int32, (3 * blk, rows), 0) - blk
    qi = lax.broadcasted_iota(jnp.int32, (3 * blk, rows), 1) % blk
    kabs = n * blk + kj
    valid = (jnp.abs(qi - kj) <= WINDOW) & (kabs >= 0) & (kabs < DEC_SEQ)
    bias = jnp.where(valid, 0.0, -jnp.inf)

    q = q_ref[...]
    for kvh in range(ATT_KV_HEADS):
        cols = slice(kvh * HEAD_DIM, (kvh + 1) * HEAD_DIM)
        qs = _stack_heads(q, kvh, blk)
        k_loc = jnp.concatenate([k_ref[pl.ds(prev0, blk), cols], k_ref[pl.ds(cur0, blk), cols],
                                 k_ref[pl.ds(next0, blk), cols]], axis=0)
        v_loc = jnp.concatenate([v_ref[pl.ds(prev0, blk), cols], v_ref[pl.ds(cur0, blk), cols],
                                 v_ref[pl.ds(next0, blk), cols]], axis=0)
        k_ctx = ck_ref[0, 0, kvh].astype(BF16)
        v_ctx = cv_ref[0, 0, kvh].astype(BF16)
        o = _sink_softmax_pv(qs, _sink_row(sink_ref, kvh, blk),
                             [(k_loc, v_loc, bias), (k_ctx, v_ctx, None)])
        _store_heads(o_ref, o, kvh, blk)


def _ctx_attention(qkv, sink):
    kcol = ATT_Q_W // ATT_KV_W
    return pl.pallas_call(
        _ctx_att_kernel,
        grid_spec=pltpu.PrefetchScalarGridSpec(
            num_scalar_prefetch=1,
            grid=(BATCH,),
            in_specs=[
                pl.BlockSpec((SEQ, ATT_Q_W), lambda b, s: (b, 0)),
                pl.BlockSpec((SEQ, ATT_KV_W), lambda b, s: (b, kcol)),
                pl.BlockSpec((SEQ, ATT_KV_W), lambda b, s: (b, kcol + 1)),
            ],
            out_specs=pl.BlockSpec((SEQ, ATT_Q_W), lambda b, s: (b, 0)),
        ),
        out_shape=jax.ShapeDtypeStruct((CTX_ROWS, ATT_Q_W), BF16),
        compiler_params=_params(1),
        name="ctx_attention",
    )(sink, qkv, qkv, qkv)


def _lat_attention(qkv, sink, cache_k, cache_v, layer_j):
    kcol = ATT_Q_W // ATT_KV_W
    nb = DEC_SEQ // ATT_BLOCK
    q0 = CTX_ROWS // ATT_BLOCK
    s0 = CTX_ROWS // DEC_SEQ
    cache_spec = pl.BlockSpec((1, 1, ATT_KV_HEADS, PAST_LEN, HEAD_DIM), lambda b, n, s: (b, layer_j, 0, 0, 0))
    return pl.pallas_call(
        _lat_att_kernel,
        grid_spec=pltpu.PrefetchScalarGridSpec(
            num_scalar_prefetch=1,
            grid=(DEC_BATCH, nb),
            in_specs=[
                pl.BlockSpec((ATT_BLOCK, ATT_Q_W), lambda b, n, s: (q0 + b * nb + n, 0)),
                pl.BlockSpec((DEC_SEQ, ATT_KV_W), lambda b, n, s: (s0 + b, kcol)),
                pl.BlockSpec((DEC_SEQ, ATT_KV_W), lambda b, n, s: (s0 + b, kcol + 1)),
                cache_spec,
                cache_spec,
            ],
            out_specs=pl.BlockSpec((ATT_BLOCK, ATT_Q_W), lambda b, n, s: (b * nb + n, 0)),
        ),
        out_shape=jax.ShapeDtypeStruct((LAT_ROWS, ATT_Q_W), BF16),
        compiler_params=_params(2),
        name="lat_attention",
    )(sink, qkv, qkv, qkv, cache_k, cache_v)


ROW_ALIGN = 16
S_SLOTS = 576
BUF_ROWS = S_SLOTS + TM_MOE
XW = D_MODEL + LANES
SLOT_LANE = 6
GATE_LANE0 = 8
REGION_TILES = 28
REGION_ROWS = REGION_TILES * TM_MOE
TAB_W = 16
HALF_WIN = TM_MOE // 2
N_TOK_TILES = T_ROWS // TM


def _router_kernel(ac_ref, al_ref, wo_ref, x_ref, ga_ref, gain_ref, sh_ref, sc_ref, wr_ref, bias_ref,
                   xo_ref, xs_hbm, row_ref, tab_ref, wl_ref, wobf_ref, buf_ref, off_ref, wide_ref, sem):
    i = pl.program_id(0)
    ng = N_GROUPS
    last = pl.num_programs(0) - 1

    @pl.when(i == 0)
    def _():
        wobf_ref[...] = wo_ref[0].astype(BF16)

    mixed = jnp.where(i < N_CTX_TILES, ac_ref[...], al_ref[...])
    x_new = x_ref[...] + ga_ref[0] * jnp.dot(mixed, wobf_ref[...], preferred_element_type=F32)
    xo_ref[...] = x_new

    @pl.when(i == 0)
    def _():
        buf_ref[...] = jnp.zeros_like(buf_ref)
        for g in range(ng):
            off_ref[g] = 0

        def clear(r, carry):
            for c in range(TAB_W):
                tab_ref[r * TAB_W + c] = 0
            return carry

        lax.fori_loop(0, N_TOK_TILES, clear, 0)

    hf = _norm_mod(x_new, gain_ref[...], sh_ref[0], sc_ref[0])
    hb = hf.astype(BF16)
    hl = (hf - hb.astype(F32)).astype(BF16)
    both = jnp.dot(hb, wr_ref[...], preferred_element_type=F32)
    logits = (both[:, :LANES] + both[:, LANES:]
              + jnp.dot(hl, wr_ref[:, :LANES], preferred_element_type=F32))
    lt = logits.T[0:N_EXPERTS, :]
    scores = jax.nn.sigmoid(lt)
    sel = scores + bias_ref[...]
    xs = [sel[k * ng:(k + 1) * ng, :] for k in range(EXPERTS_PER_GROUP)]
    sc = [scores[k * ng:(k + 1) * ng, :] for k in range(EXPERTS_PER_GROUP)]
    a, b, c, d = xs
    gs = jnp.maximum(jnp.maximum(jnp.maximum(a + b, a + c), jnp.maximum(a + d, b + c)),
                     jnp.maximum(b + d, c + d))
    bv = gs[0:1, :]
    bg = jnp.zeros(bv.shape, jnp.int32)
    for g in range(1, ng):
        better = gs[g:g + 1, :] > bv
        bg = jnp.where(better, g, bg)
        bv = jnp.where(better, gs[g:g + 1, :], bv)
    giota = lax.broadcasted_iota(jnp.int32, (ng, TM), 0)
    onehot = giota == bg
    wk = []
    for k in range(EXPERTS_PER_GROUP):
        rank = jnp.zeros((ng, TM), F32)
        for j in range(EXPERTS_PER_GROUP):
            if j < k:
                rank = rank + (xs[j] >= xs[k]).astype(F32)
            elif j > k:
                rank = rank + (xs[j] > xs[k]).astype(F32)
        chosen = (rank < 2.0) & onehot
        wk.append(jnp.sum(jnp.where(chosen, sc[k], 0.0), axis=0, keepdims=True))
    den = wk[0] + wk[1] + wk[2] + wk[3]
    gates = [w / den for w in wk]

    oh = jnp.where(onehot, 1.0, 0.0)
    r = lax.broadcasted_iota(jnp.int32, (TM, TM), 0)
    cc = lax.broadcasted_iota(jnp.int32, (TM, TM), 1)
    tri = jnp.where(r < cc, 1.0, 0.0).astype(BF16)
    before = jnp.dot(oh.astype(BF16), tri, preferred_element_type=F32)
    rank_local = jnp.sum(jnp.where(onehot, before, 0.0), axis=0, keepdims=True).astype(jnp.int32)
    counts = [jnp.sum(jnp.where(bg == g, 1, 0)) for g in range(ng)]
    pads = [((cnt + ROW_ALIGN - 1) // ROW_ALIGN) * ROW_ALIGN for cnt in counts]
    segs = [0]
    for g in range(1, ng):
        segs.append(segs[-1] + pads[g - 1])
    total = segs[-1] + pads[-1]
    seg_of = jnp.zeros(bg.shape, jnp.int32)
    for g in range(1, ng):
        seg_of = jnp.where(bg == g, segs[g], seg_of)
    slot = seg_of + rank_local

    parts = []
    for gt in gates:
        hi = gt.astype(BF16).astype(F32)
        rest = gt - hi
        mid = rest.astype(BF16).astype(F32)
        parts += [hi, mid, (rest - mid).astype(BF16).astype(F32)]
    zero_row = jnp.zeros((1, TM), F32)
    info = jnp.concatenate(gates + [bg.astype(F32), zero_row, slot.astype(F32), zero_row] + parts
                           + [jnp.zeros((LANES - GATE_LANE0 - len(parts), TM), F32)], axis=0)
    rowinfo = info.T
    row_ref[...] = rowinfo

    h_ext = jnp.concatenate([hb, rowinfo.astype(BF16)], axis=1)
    srow = lax.broadcasted_iota(jnp.int32, (S_SLOTS, TM), 0)
    pick = jnp.where(srow == slot, 1.0, 0.0).astype(BF16)
    par = i % 2
    buf_ref[par, 0:S_SLOTS, :] = jnp.dot(pick, h_ext, preferred_element_type=F32).astype(BF16)

    def window_copy(g, src_row, dst_row, rows):
        return pltpu.make_async_copy(
            buf_ref.at[par, pl.ds(pl.multiple_of(src_row, ROW_ALIGN), rows), :],
            xs_hbm.at[pl.ds(pl.multiple_of(dst_row, ROW_ALIGN), rows), :],
            sem.at[g])

    def wait_windows(rows):
        for g in range(ng):
            window_copy(g, 0, 0, rows).wait()

    def wait_previous():
        @pl.when(wide_ref[0] == 1)
        def _():
            wait_windows(TM_MOE)

        @pl.when(wide_ref[0] == 0)
        def _():
            wait_windows(HALF_WIN)

    @pl.when(i > 0)
    def _():
        wait_previous()

    wide = jnp.maximum(jnp.maximum(pads[0], pads[1]), jnp.maximum(pads[2], pads[3])) > HALF_WIN
    offs = [off_ref[g] for g in range(ng)]

    def issue(rows):
        for g in range(ng):
            window_copy(g, segs[g], g * REGION_ROWS + offs[g], rows).start()

    @pl.when(wide)
    def _():
        issue(TM_MOE)

    @pl.when(jnp.logical_not(wide))
    def _():
        issue(HALF_WIN)

    wide_ref[0] = wide.astype(jnp.int32)
    for g in range(ng):
        tab_ref[i * TAB_W + g] = segs[g]
        tab_ref[i * TAB_W + ng + g] = offs[g]
        tab_ref[i * TAB_W + 2 * ng + 1 + g] = pads[g]
        off_ref[g] = offs[g] + pads[g]
    tab_ref[i * TAB_W + 2 * ng] = total

    @pl.when(i == last)
    def _():
        wait_previous()
        for g in range(ng):
            window_copy(g, S_SLOTS, g * REGION_ROWS + off_ref[g], TM_MOE).start()
        wait_windows(TM_MOE)

        entry = 0
        full = 0
        for g in range(ng):
            full = (off_ref[g] + TM_MOE - 1) // TM_MOE

            def add(j, e, g=g, full=full):
                wl_ref[e] = g * REGION_TILES + j
                wl_ref[N_MOE_TILES + e] = g
                wl_ref[2 * N_MOE_TILES + e] = (j < full).astype(jnp.int32)
                return e + 1

            entry = lax.fori_loop(0, full + 1, add, entry)

        def pad(e, carry, full=full):
            wl_ref[e] = (ng - 1) * REGION_TILES + full
            wl_ref[N_MOE_TILES + e] = ng - 1
            wl_ref[2 * N_MOE_TILES + e] = 0
            return carry

        lax.fori_loop(entry, N_MOE_TILES, pad, 0)


def _mix_out_router(a_ctx, a_lat, w_out, layer_j, x, gate_a, gain, shift, scale, wr_parts, bias_col):
    k = a_ctx.shape[1]
    return pl.pallas_call(
        _router_kernel,
        grid=(N_TOK_TILES,),
        in_specs=[
            pl.BlockSpec((TM, k), lambda i: (jnp.minimum(i, N_CTX_TILES - 1), 0)),
            pl.BlockSpec((TM, k), lambda i: (jnp.maximum(i - N_CTX_TILES, 0), 0)),
            pl.BlockSpec((1, k, D_MODEL), lambda i: (layer_j, 0, 0), pipeline_mode=pl.Buffered(1)),
            pl.BlockSpec((TM, D_MODEL), lambda i: (i, 0)),
            pl.BlockSpec((1, 1, D_MODEL), lambda i: (_cond_row(i), 0, 0)),
            pl.BlockSpec((1, D_MODEL), lambda i: (0, 0)),
            pl.BlockSpec((1, 1, D_MODEL), lambda i: (_cond_row(i), 0, 0)),
            pl.BlockSpec((1, 1, D_MODEL), lambda i: (_cond_row(i), 0, 0)),
            pl.BlockSpec((D_MODEL, 2 * LANES), lambda i: (0, 0)),
            pl.BlockSpec((N_EXPERTS, 1), lambda i: (0, 0)),
        ],
        out_specs=[
            pl.BlockSpec((TM, D_MODEL), lambda i: (i, 0)),
            pl.BlockSpec(memory_space=pl.ANY),
            pl.BlockSpec((TM, LANES), lambda i: (i, 0)),
            pl.BlockSpec(memory_space=pltpu.SMEM),
            pl.BlockSpec(memory_space=pltpu.SMEM),
        ],
        out_shape=[
            jax.ShapeDtypeStruct((T_ROWS, D_MODEL), F32),
            jax.ShapeDtypeStruct((N_GROUPS * REGION_ROWS, XW), BF16),
            jax.ShapeDtypeStruct((T_ROWS, LANES), F32),
            jax.ShapeDtypeStruct((N_TOK_TILES * TAB_W,), jnp.int32),
            jax.ShapeDtypeStruct((3 * N_MOE_TILES,), jnp.int32),
        ],
        scratch_shapes=[
            pltpu.VMEM((k, D_MODEL), BF16),
            pltpu.VMEM((2, BUF_ROWS, XW), BF16),
            pltpu.SMEM((N_GROUPS,), jnp.int32),
            pltpu.SMEM((1,), jnp.int32),
            pltpu.SemaphoreType.DMA((N_GROUPS,)),
        ],
        compiler_params=_params(1),
        name="mix_out_router",
    )(a_ctx, a_lat, w_out, x, gate_a, gain, shift, scale, wr_parts, bias_col)


UP_EXPERTS = 4


def _wl_block(wl, i):
    return wl[i]


def _wl_group(wl, i):
    return wl[N_MOE_TILES + i]


def _wl_valid(wl, i):
    return wl[2 * N_MOE_TILES + i]


def _moe_up_kernel(wl_ref, xs_ref, w_ref, o_ref, wbf_ref):
    kk = pl.program_id(0)
    i = pl.program_id(1)
    new_w = (i == 0) | (_wl_group(wl_ref, i) != _wl_group(wl_ref, jnp.maximum(i - 1, 0)))
    valid = _wl_valid(wl_ref, i)

    @pl.when(new_w)
    def _():
        wbf_ref[...] = w_ref[...].astype(BF16)

    @pl.when(valid == 1)
    def _():
        xs = xs_ref[:, :D_MODEL]
        extra = xs_ref[:, D_MODEL:].astype(F32)
        lane = lax.broadcasted_iota(jnp.int32, extra.shape, 1)
        for e in range(UP_EXPERTS):
            hu = jnp.dot(xs, wbf_ref[e], preferred_element_type=F32)
            lane0 = GATE_LANE0 + 3 * (kk * UP_EXPERTS + e)
            gcol = jnp.sum(jnp.where((lane >= lane0) & (lane < lane0 + 3), extra, 0.0), axis=1, keepdims=True)
            a = _silu(hu[:, :D_EXPERT]) * hu[:, D_EXPERT:] * gcol
            o_ref[:, e * D_EXPERT:(e + 1) * D_EXPERT] = a.astype(BF16)

    @pl.when(valid != 1)
    def _():
        o_ref[...] = jnp.zeros_like(o_ref)


def _moe_up(work_list, xs, w_up, layer):
    steps = EXPERTS_PER_GROUP // UP_EXPERTS
    return pl.pallas_call(
        _moe_up_kernel,
        grid_spec=pltpu.PrefetchScalarGridSpec(
            num_scalar_prefetch=1,
            grid=(steps, N_MOE_TILES),
            in_specs=[
                pl.BlockSpec((TM_MOE, XW), lambda k, i, wl: (_wl_block(wl, i), 0)),
                pl.BlockSpec((UP_EXPERTS, D_MODEL, 2 * D_EXPERT),
                             lambda k, i, wl: ((layer * N_GROUPS + _wl_group(wl, i)) * steps + k, 0, 0)),
            ],
            out_specs=pl.BlockSpec((TM_MOE, UP_EXPERTS * D_EXPERT), lambda k, i, wl: (_wl_block(wl, i), k)),
            scratch_shapes=[pltpu.VMEM((UP_EXPERTS, D_MODEL, 2 * D_EXPERT), BF16)],
        ),
        out_shape=jax.ShapeDtypeStruct((N_GROUPS * REGION_ROWS, EXPERTS_PER_GROUP * D_EXPERT), BF16),
        compiler_params=_params(2),
        name="moe_up",
    )(work_list, xs, w_up)


def _moe_down_kernel(wl_ref, a_ref, w_ref, o_ref, wbf_ref):
    i = pl.program_id(0)
    new_w = (i == 0) | (_wl_group(wl_ref, i) != _wl_group(wl_ref, jnp.maximum(i - 1, 0)))
    valid = _wl_valid(wl_ref, i)

    @pl.when(new_w)
    def _():
        wbf_ref[...] = w_ref[0].astype(BF16)

    @pl.when(valid == 1)
    def _():
        o_ref[...] = jnp.dot(a_ref[...], wbf_ref[...], preferred_element_type=F32).astype(BF16)

    @pl.when(valid != 1)
    def _():
        o_ref[...] = jnp.zeros_like(o_ref)


def _moe_down(work_list, a, w_down_grouped, layer):
    kdim = EXPERTS_PER_GROUP * D_EXPERT
    return pl.pallas_call(
        _moe_down_kernel,
        grid_spec=pltpu.PrefetchScalarGridSpec(
            num_scalar_prefetch=1,
            grid=(N_MOE_TILES,),
            in_specs=[
                pl.BlockSpec((TM_MOE, kdim), lambda i, wl: (_wl_block(wl, i), 0)),
                pl.BlockSpec((1, kdim, D_MODEL), lambda i, wl: (layer * N_GROUPS + _wl_group(wl, i), 0, 0)),
            ],
            out_specs=pl.BlockSpec((TM_MOE, D_MODEL), lambda i, wl: (_wl_block(wl, i), 0)),
            scratch_shapes=[pltpu.VMEM((kdim, D_MODEL), BF16)],
        ),
        out_shape=jax.ShapeDtypeStruct((N_GROUPS * REGION_ROWS, D_MODEL), BF16),
        compiler_params=_params(1),
        name="moe_down",
    )(work_list, a, w_down_grouped)


N_WINDOWS = 2 * N_GROUPS


class PendingMoe(NamedTuple):
    tab_flat: jax.Array
    ys: jax.Array
    rowinfo: jax.Array
    x: jax.Array
    gate: jax.Array


def _combine_tile(i, tab_ref, windows, row_ref, x_ref, g_ref, buf_ref):
    first = windows[:N_GROUPS]
    second = windows[N_GROUPS:]

    @pl.when(i == 0)
    def _():
        buf_ref[...] = jnp.zeros_like(buf_ref)

    for g in range(N_GROUPS):
        start = pl.multiple_of(tab_ref[i * TAB_W + g], ROW_ALIGN)
        buf_ref[pl.ds(start, HALF_WIN), :] = first[g][...]

        @pl.when(tab_ref[i * TAB_W + 2 * N_GROUPS + 1 + g] > HALF_WIN)
        def _(g=g, start=start):
            buf_ref[pl.ds(start + HALF_WIN, HALF_WIN), :] = second[g][...]

    slot = row_ref[:, SLOT_LANE:SLOT_LANE + 1].astype(jnp.int32)
    scol = lax.broadcasted_iota(jnp.int32, (TM, S_SLOTS), 1)
    pick = jnp.where(scol == slot, 1.0, 0.0).astype(BF16)
    y = jnp.dot(pick, buf_ref[0:S_SLOTS, :], preferred_element_type=F32)
    return x_ref[...] + g_ref[0] * y


def _pending_specs(p):
    def window_spec(g, second):
        def index(i, tab):
            off = tab[i * TAB_W + N_GROUPS + g]
            if second:
                off = jnp.where(tab[i * TAB_W + 2 * N_GROUPS + 1 + g] > HALF_WIN, off + HALF_WIN, 0)
            return pl.multiple_of(off + g * REGION_ROWS, ROW_ALIGN), 0

        return pl.BlockSpec((pl.Element(HALF_WIN), pl.Element(D_MODEL)), index)

    specs = [window_spec(g, False) for g in range(N_GROUPS)] + [window_spec(g, True) for g in range(N_GROUPS)]
    specs += [
        pl.BlockSpec((TM, LANES), lambda i, tab: (i, 0)),
        pl.BlockSpec((TM, D_MODEL), lambda i, tab: (i, 0)),
        pl.BlockSpec((1, 1, D_MODEL), lambda i, tab: (_cond_row(i), 0, 0)),
    ]
    args = [p.tab_flat] + [p.ys] * N_WINDOWS + [p.rowinfo, p.x, p.gate]
    return specs, args, [pltpu.VMEM((BUF_ROWS, D_MODEL), BF16)]


def _moe_combine_kernel(tab_ref, *refs):
    row_ref, x_ref, g_ref, ctx_ref, lat_ref, buf_ref = refs[N_WINDOWS:]
    i = pl.program_id(0)
    new_x = _combine_tile(i, tab_ref, refs[:N_WINDOWS], row_ref, x_ref, g_ref, buf_ref)

    @pl.when(i < N_CTX_TILES)
    def _():
        ctx_ref[...] = new_x

    @pl.when(i >= N_CTX_TILES)
    def _():
        lat_ref[...] = new_x


def _moe_combine(pending):
    specs, args, scratch = _pending_specs(pending)
    return pl.pallas_call(
        _moe_combine_kernel,
        grid_spec=pltpu.PrefetchScalarGridSpec(
            num_scalar_prefetch=1,
            grid=(N_TOK_TILES,),
            in_specs=specs,
            out_specs=[pl.BlockSpec((TM, D_MODEL), lambda i, tab: (jnp.minimum(i, N_CTX_TILES - 1), 0)),
                       pl.BlockSpec((TM, D_MODEL), lambda i, tab: (jnp.maximum(i - N_CTX_TILES, 0), 0))],
            scratch_shapes=scratch,
        ),
        out_shape=[jax.ShapeDtypeStruct((CTX_ROWS, D_MODEL), F32), jax.ShapeDtypeStruct((LAT_ROWS, D_MODEL), F32)],
        compiler_params=_params(1),
        name="moe_combine",
    )(*args)


def _mix_out_and_moe(mix, x, gate_a, gain, shift, scale, gate, router_w, w_up, w_down_grouped, layer):
    wr_parts, bias_col = router_w
    x, xs, rowinfo, tab, work_list = _mix_out_router(*mix, x, gate_a, gain, shift, scale, wr_parts, bias_col)
    a = _moe_up(work_list, xs, w_up, layer)
    ys = _moe_down(work_list, a, w_down_grouped, layer)
    return PendingMoe(tab, ys, rowinfo, x, gate)


def kernel(x_prompt, x_sample, state_ret, cache_k, cache_v, c, c_ctx, w_mod, b_mod, norm_mix,
           norm_moe, ret_w_in, ret_decay, ret_w_out, att_w_in, att_q_norm, att_k_norm, att_sink,
           att_w_out, w_router, router_bias, moe_w_up, moe_w_down):
    cond8 = jnp.zeros((8, D_MODEL), F32).at[0].set(c_ctx).at[1:N_COND].set(c)
    mods = _modulation_all(cond8, w_mod, b_mod)
    mods = mods[:, :N_COND].reshape(DEPTH, N_COND, N_MOD, 1, D_MODEL)

    ret_cos, ret_sin = _rope_tables(RET_DK)
    att_cos, att_sin = _rope_tables(HEAD_DIM)

    perm = jnp.arange(N_EXPERTS).reshape(N_GROUPS, EXPERTS_PER_GROUP).T.reshape(-1)
    wr = jnp.zeros((D_MODEL, LANES), F32).at[:, :N_EXPERTS].set(w_router[:, perm])
    wr_hi = wr.astype(BF16)
    wr_lo = (wr - wr_hi.astype(F32)).astype(BF16)
    bias_col = router_bias[perm].astype(F32).reshape(N_EXPERTS, 1)
    router_w = (jnp.concatenate([wr_hi, wr_lo], axis=1), bias_col)

    w_up_all = moe_w_up.reshape(DEPTH * N_EXPERTS, D_MODEL, 2 * D_EXPERT)
    w_down_all = moe_w_down.reshape(DEPTH * N_GROUPS, EXPERTS_PER_GROUP * D_EXPERT, D_MODEL)
    ret_w_in_bf16 = ret_w_in.astype(BF16)
    new_state = jnp.zeros((BATCH, (DEPTH + 1) // 2, 2, RET_HEADS, RET_DK, RET_DV), F32)
    cache_shape = (BATCH, DEPTH // 2, ATT_KV_HEADS, SEQ, HEAD_DIM)
    new_kv = (jnp.zeros(cache_shape, F32), jnp.zeros(cache_shape, F32))
    source = (x_prompt.reshape(CTX_ROWS, D_MODEL), x_sample.reshape(LAT_ROWS, D_MODEL))
    for layer in range(DEPTH):
        sh_a, sc_a, g_a, sh_m, sc_m, g_m = [mods[layer, :, t] for t in range(N_MOD)]
        gain_mix = norm_mix[layer].reshape(1, D_MODEL)
        gain_moe = norm_moe[layer].reshape(1, D_MODEL)
        j = layer // 2
        if layer % 2 == 0:
            proj, x = _ret_proj(source, gain_mix, sh_a, sc_a, ret_w_in_bf16, j, ret_cos, ret_sin)
            o_ctx, new_state = _ret_core(proj, ret_decay[j], state_ret, j, latent=False, new_state=new_state)
            (o_lat,) = _ret_core(proj, ret_decay[j], state_ret, j, latent=True)
            mix = (o_ctx, o_lat, ret_w_out, j)
        else:
            q_gain = jnp.tile(att_q_norm[j], LANES // HEAD_DIM).reshape(1, LANES)
            k_gain = jnp.tile(att_k_norm[j], LANES // HEAD_DIM).reshape(1, LANES)
            qkv, x, new_k, new_v = _att_proj(source, gain_mix, sh_a, sc_a, att_w_in, j, q_gain, k_gain,
                                             att_cos, att_sin, new_kv)
            new_kv = (new_k, new_v)
            sink = att_sink[j].astype(F32)
            o_ctx = _ctx_attention(qkv, sink)
            o_lat = _lat_attention(qkv, sink, cache_k, cache_v, j)
            mix = (o_ctx, o_lat, att_w_out, j)
        source = _mix_out_and_moe(mix, x, g_a, gain_moe, sh_m, sc_m, g_m, router_w, w_up_all, w_down_all, layer)

    x_ctx, x_lat = _moe_combine(source)
    y_prompt = x_ctx.reshape(BATCH, SEQ, D_MODEL)
    y_sample = x_lat.reshape(DEC_BATCH, DEC_SEQ, D_MODEL)
    return (y_prompt, y_sample, new_state, new_kv[0], new_kv[1])
```

```python
import functools
from typing import NamedTuple

import jax
import jax.numpy as jnp
import numpy as np
from jax import lax
from jax.experimental import pallas as pl
from jax.experimental.pallas import tpu as pltpu

F32 = jnp.float32
BF16 = jnp.bfloat16

D_MODEL = 1024
BATCH = 16
SEQ = 256
DEPTH = 4
DEC_BATCH = 2
DEC_SEQ = 4096
PAST_LEN = 512
GRID_W = 64
N_MOD = 6
EPS = 1e-6
ROPE_BASE = 10000.0
RET_HEADS = 8
RET_DK = 128
RET_DV = 256
RET_CHUNK = 256
RET_IN = 2 * RET_HEADS * RET_DK + 2 * RET_HEADS * RET_DV
ATT_HEADS = 16
ATT_KV_HEADS = 4
ATT_GROUP = 4
HEAD_DIM = 64
WINDOW = 128
ATT_BLOCK = 128
ATT_IN = (ATT_HEADS + 2 * ATT_KV_HEADS) * HEAD_DIM
ATT_SCALE = HEAD_DIM ** -0.5
N_EXPERTS = 16
N_GROUPS = 4
EXPERTS_PER_GROUP = 4
D_EXPERT = 512

CTX_ROWS = BATCH * SEQ
LAT_ROWS = DEC_BATCH * DEC_SEQ
T_ROWS = CTX_ROWS + LAT_ROWS
N_COND = 1 + DEC_BATCH

LANES = 128
TM = 512
N_CTX_TILES = CTX_ROWS // TM
N_LAT_TILES = DEC_SEQ // TM
TM_MOE = 512
N_MOE_TILES = 36
VMEM_LIMIT = 52 * 1024 * 1024

NT_DIMS = (((1,), (1,)), ((), ()))


def _params(n_axes, vmem=VMEM_LIMIT):
    return pltpu.CompilerParams(dimension_semantics=("arbitrary",) * n_axes, vmem_limit_bytes=vmem)


def _cond_row(i):
    return (i * TM) // DEC_SEQ


def _rope_row(i):
    return jnp.where(i < N_CTX_TILES, 0, 1 + (i - N_CTX_TILES) % N_LAT_TILES)


def _norm_mod(x, gain, shift, scale):
    ms = jnp.mean(x * x, axis=-1, keepdims=True)
    y = x * lax.rsqrt(ms + EPS) * gain
    return y * (1.0 + scale) + shift


def _silu(x):
    return x * jax.nn.sigmoid(x)


def _mod_kernel(c_ref, w_ref, b_ref, o_ref):
    c = c_ref[...]
    s = _silu(c).astype(BF16)
    o_ref[0] = jnp.dot(s, w_ref[0].astype(BF16), preferred_element_type=F32) + b_ref[0]


def _modulation_all(cond8, w_mod, b_mod):
    tn = 1536
    n = N_MOD * D_MODEL
    return pl.pallas_call(
        _mod_kernel,
        grid=(DEPTH, n // tn),
        in_specs=[
            pl.BlockSpec((8, D_MODEL), lambda l, j: (0, 0)),
            pl.BlockSpec((1, D_MODEL, tn), lambda l, j: (l, 0, j)),
            pl.BlockSpec((1, 1, tn), lambda l, j: (l, 0, j)),
        ],
        out_specs=pl.BlockSpec((1, 8, tn), lambda l, j: (l, 0, j)),
        out_shape=jax.ShapeDtypeStruct((DEPTH, 8, n), F32),
        compiler_params=_params(2),
        name="modulation",
    )(cond8, w_mod, b_mod.reshape(DEPTH, 1, n))


def _rope_tables(head_dim):
    half = head_dim // 2
    quarter = half // 2
    t = np.arange(DEC_SEQ)
    row = (t // GRID_W).astype(np.float64)
    col = (t % GRID_W).astype(np.float64)
    inv_freq = ROPE_BASE ** (-np.arange(quarter, dtype=np.float64) / quarter)
    lane = np.arange(LANES)
    d = lane % head_dim
    w = d % half
    f = w % quarter
    pos = np.where((d // half)[None, :] == 0, row[:, None], col[:, None])
    ang = pos * inv_freq[f][None, :]
    cos = np.cos(ang)
    sin = np.where((w < quarter)[None, :], -np.sin(ang), np.sin(ang))
    cos = np.concatenate([np.ones((TM, LANES)), cos], axis=0).astype(np.float32)
    sin = np.concatenate([np.zeros((TM, LANES)), sin], axis=0).astype(np.float32)
    return jnp.asarray(cos), jnp.asarray(sin)


def _rope_chunk(xc, cos, sin, quarter):
    lane = lax.broadcasted_iota(jnp.int32, xc.shape, 1)
    first = (lane % (2 * quarter)) < quarter
    partner = jnp.where(first, pltpu.roll(xc, LANES - quarter, 1), pltpu.roll(xc, quarter, 1))
    return xc * cos + partner * sin


RET_TN = 1024
RET_Q_TILES = RET_HEADS * RET_DK // RET_TN
RET_QK_TILES = 2 * RET_Q_TILES


def _ret_proj_kernel(*refs, from_moe):
    i = pl.program_id(0)
    if from_moe:
        n_src = 1 + N_WINDOWS + 3
        x = _combine_tile(i, refs[0], refs[1:1 + N_WINDOWS], *refs[1 + N_WINDOWS:n_src], refs[-1])
    else:
        n_src = 2
        x = jnp.where(i < N_CTX_TILES, refs[0][...], refs[1][...])
    gain_ref, sh_ref, sc_ref, w_ref, cos_ref, sin_ref, o_ref, xo_ref = refs[n_src:n_src + 8]
    xo_ref[...] = x
    h = _norm_mod(x, gain_ref[...], sh_ref[0], sc_ref[0]).astype(BF16)
    cos = cos_ref[...]
    sin = sin_ref[...]
    for j in range(RET_IN // RET_TN):
        cols = slice(j * RET_TN, (j + 1) * RET_TN)
        acc = jnp.dot(h, w_ref[0, :, cols], preferred_element_type=F32)
        if j < RET_QK_TILES:
            scale = 1.0 if j < RET_Q_TILES else RET_DK ** -0.5
            for c in range(RET_TN // LANES):
                xc = acc[:, c * LANES:(c + 1) * LANES]
                lanes = slice(j * RET_TN + c * LANES, j * RET_TN + (c + 1) * LANES)
                o_ref[:, lanes] = (_rope_chunk(xc, cos, sin, RET_DK // 4) * scale).astype(BF16)
        else:
            o_ref[:, cols] = acc.astype(BF16)


def _ret_proj(source, gain, shift, scale, w_in_bf16, layer_j, cos, sin):
    from_moe = isinstance(source, PendingMoe)
    if from_moe:
        src_specs, src_args, scratch = _pending_specs(source)
        n_prefetch = 1
    else:
        src_specs = [pl.BlockSpec((TM, D_MODEL), lambda i, *_: (jnp.minimum(i, N_CTX_TILES - 1), 0)),
                     pl.BlockSpec((TM, D_MODEL), lambda i, *_: (jnp.maximum(i - N_CTX_TILES, 0), 0))]
        src_args, scratch, n_prefetch = list(source), [], 0
    return pl.pallas_call(
        functools.partial(_ret_proj_kernel, from_moe=from_moe),
        grid_spec=pltpu.PrefetchScalarGridSpec(
            num_scalar_prefetch=n_prefetch,
            grid=(T_ROWS // TM,),
            in_specs=src_specs + [
                pl.BlockSpec((1, D_MODEL), lambda i, *_: (0, 0)),
                pl.BlockSpec((1, 1, D_MODEL), lambda i, *_: (_cond_row(i), 0, 0)),
                pl.BlockSpec((1, 1, D_MODEL), lambda i, *_: (_cond_row(i), 0, 0)),
                pl.BlockSpec((1, D_MODEL, RET_IN), lambda i, *_: (layer_j, 0, 0), pipeline_mode=pl.Buffered(1)),
                pl.BlockSpec((TM, LANES), lambda i, *_: (_rope_row(i), 0)),
                pl.BlockSpec((TM, LANES), lambda i, *_: (_rope_row(i), 0)),
            ],
            out_specs=[pl.BlockSpec((TM, RET_IN), lambda i, *_: (i, 0)),
                       pl.BlockSpec((TM, D_MODEL), lambda i, *_: (i, 0))],
            scratch_shapes=scratch,
        ),
        out_shape=[jax.ShapeDtypeStruct((T_ROWS, RET_IN), BF16), jax.ShapeDtypeStruct((T_ROWS, D_MODEL), F32)],
        compiler_params=_params(1),
        name="ret_proj",
    )(*src_args, gain, shift, scale, w_in_bf16, cos, sin)


def _log_sigmoid(x):
    return -(jnp.maximum(-x, 0.0) + jnp.log(1.0 + jnp.exp(-jnp.abs(x))))


def _ret_core_kernel(*refs, seq_len, n_seq, chunk, has_init, emit_state, n_alias):
    it = iter(refs)
    decay_ref = next(it)
    q_ref = next(it)
    k_ref = next(it)
    v_ref = next(it)
    g_ref = next(it)
    s0_ref = next(it) if has_init else None
    for _ in range(n_alias):
        next(it)
    o_ref = next(it)
    st_ref = next(it) if emit_state else None
    acc_ref = next(it)
    kv_ref = next(it)

    c = chunk
    nc = seq_len // c
    n_chunks = n_seq * nc
    unroll = min(RET_UNROLL, n_chunks)
    head = pl.program_id(1)

    lg_all = _log_sigmoid(decay_ref[...])
    pick = lax.broadcasted_iota(jnp.int32, lg_all.shape, 1) == head
    lg = jnp.sum(jnp.where(pick, lg_all, 0.0), axis=1, keepdims=True)
    lg_f = lg[0:1, :]
    lg_b = lg[1:2, :]

    ri = lax.broadcasted_iota(jnp.int32, (c, c), 0).astype(F32)
    ci = lax.broadcasted_iota(jnp.int32, (c, c), 1).astype(F32)
    diff = ri - ci
    decay = (jnp.where(diff >= 0, jnp.exp(jnp.where(diff >= 0, diff, 0.0) * lg_f), 0.0)
             + jnp.where(diff <= 0, jnp.exp(jnp.where(diff <= 0, -diff, 0.0) * lg_b), 0.0))
    pos_col = lax.broadcasted_iota(jnp.int32, (c, 1), 0).astype(F32)
    pos_row = lax.broadcasted_iota(jnp.int32, (1, c), 1).astype(F32)
    xi_f = jnp.exp((pos_col + 1.0) * lg_f)
    xi_b = jnp.exp((c - pos_col) * lg_b)
    zeta_f = jnp.exp((c - 1.0 - pos_row) * lg_f)
    zeta_b = jnp.exp(pos_row * lg_b)
    cd_f = jnp.exp(c * lg_f)
    cd_b = jnp.exp(c * lg_b)

    def intra(n):
        r0 = pl.multiple_of(n * c, c)
        qn = q_ref[pl.ds(r0, c), :]
        kn = k_ref[pl.ds(r0, c), :]
        vn = v_ref[pl.ds(r0, c), :]
        s = lax.dot_general(qn, kn, NT_DIMS, preferred_element_type=F32) * decay
        acc_ref[pl.ds(r0, c), :] = jnp.dot(s.astype(BF16), vn, preferred_element_type=F32)
        kt = kn.astype(F32).T
        kz = jnp.concatenate([(kt * zeta_f).astype(BF16), (kt * zeta_b).astype(BF16)], axis=0)
        kv_ref[n] = jnp.dot(kz, vn, preferred_element_type=F32)

    def cross(n):
        r0 = pl.multiple_of(n * c, c)
        qn = q_ref[pl.ds(r0, c), :].astype(F32)
        qx = jnp.concatenate([(qn * xi_f).astype(BF16), (qn * xi_b).astype(BF16)], axis=1)
        o = acc_ref[pl.ds(r0, c), :] + jnp.dot(qx, kv_ref[n].astype(BF16), preferred_element_type=F32)
        mu = jnp.mean(o, axis=-1, keepdims=True)
        var = jnp.mean(jnp.square(o - mu), axis=-1, keepdims=True)
        on = (o - mu) * lax.rsqrt(var + EPS)
        gate = g_ref[pl.ds(r0, c), :].astype(F32)
        o_ref[pl.ds(r0, c), :] = (_silu(gate) * on).astype(BF16)

    def over_chunks(fn):
        def body(step, carry):
            for u in range(unroll):
                fn(step * unroll + u)
            return carry

        lax.fori_loop(0, n_chunks // unroll, body, 0)

    over_chunks(intra)

    for s in range(n_seq):
        if has_init:
            init_f = s0_ref[s, 0, 0, 0]
            init_b = s0_ref[s, 0, 1, 0]
        else:
            init_f = jnp.zeros((RET_DK, RET_DV), F32)
            init_b = init_f

        def fwd(n, state, s=s):
            kv = kv_ref[s * nc + n, 0:RET_DK, :]
            kv_ref[s * nc + n, 0:RET_DK, :] = state
            return cd_f * state + kv

        def bwd(n, state, s=s):
            m = s * nc + nc - 1 - n
            kv = kv_ref[m, RET_DK:, :]
            kv_ref[m, RET_DK:, :] = state
            return cd_b * state + kv

        final_f = lax.fori_loop(0, nc, fwd, init_f)
        final_b = lax.fori_loop(0, nc, bwd, init_b)
        if emit_state:
            st_ref[s, 0, 0, 0] = final_f
            st_ref[s, 0, 1, 0] = final_b

    over_chunks(cross)


RET_UNROLL = 16
RET_CTX_SEQS = 8


def _ret_core(proj, ret_decay_j, state_ret, layer_j, *, latent, new_state=None):
    if latent:
        nb, seq_len, n_seq, row0 = DEC_BATCH, DEC_SEQ, 1, CTX_ROWS // DEC_SEQ
    else:
        nb, seq_len, n_seq, row0 = BATCH // RET_CTX_SEQS, SEQ, RET_CTX_SEQS, 0
    rows = n_seq * seq_len
    chunk = min(RET_CHUNK, seq_len)
    kcol = RET_HEADS * RET_DK // RET_DK
    vcol = 2 * RET_HEADS * RET_DK // RET_DV
    gcol = vcol + RET_HEADS
    in_specs = [
        pl.BlockSpec((2, RET_HEADS), lambda b, h: (0, 0)),
        pl.BlockSpec((rows, RET_DK), lambda b, h: (row0 + b, h)),
        pl.BlockSpec((rows, RET_DK), lambda b, h: (row0 + b, kcol + h)),
        pl.BlockSpec((rows, RET_DV), lambda b, h: (row0 + b, vcol + h)),
        pl.BlockSpec((rows, RET_DV), lambda b, h: (row0 + b, gcol + h)),
    ]
    args = [ret_decay_j, proj, proj, proj, proj]
    if latent:
        in_specs.append(pl.BlockSpec((1, 1, 2, 1, RET_DK, RET_DV), lambda b, h: (b, layer_j, 0, h, 0, 0)))
        args.append(state_ret)
    aliases = {}
    if new_state is not None:
        aliases = {len(args): 1}
        in_specs.append(pl.BlockSpec(memory_space=pl.ANY))
        args.append(new_state)
    out_specs = [pl.BlockSpec((rows, RET_DV), lambda b, h: (b, h))]
    out_shape = [jax.ShapeDtypeStruct((nb * rows, RET_HEADS * RET_DV), BF16)]
    if not latent:
        out_specs.append(pl.BlockSpec((n_seq, 1, 2, 1, RET_DK, RET_DV), lambda b, h: (b, layer_j, 0, h, 0, 0)))
        out_shape.append(jax.ShapeDtypeStruct((BATCH, (DEPTH + 1) // 2, 2, RET_HEADS, RET_DK, RET_DV), F32))
    return pl.pallas_call(
        functools.partial(_ret_core_kernel, seq_len=seq_len, n_seq=n_seq, chunk=chunk, has_init=latent,
                          emit_state=not latent, n_alias=len(aliases)),
        grid=(nb, RET_HEADS),
        in_specs=in_specs,
        out_specs=out_specs,
        out_shape=out_shape,
        input_output_aliases=aliases,
        scratch_shapes=[
            pltpu.VMEM((rows, RET_DV), F32),
            pltpu.VMEM((rows // chunk, 2 * RET_DK, RET_DV), F32),
        ],
        compiler_params=_params(2),
        name="ret_core_lat" if latent else "ret_core_ctx",
    )(*args)


ATT_Q_W = ATT_HEADS * HEAD_DIM
ATT_KV_W = ATT_KV_HEADS * HEAD_DIM


def _group_mean_sq(xc, gmat):
    sq = xc * xc
    hi = sq.astype(BF16)
    lo = (sq - hi.astype(F32)).astype(BF16)
    return jnp.dot(hi, gmat, preferred_element_type=F32) + jnp.dot(lo, gmat, preferred_element_type=F32)


SEQS_PER_TILE = TM // SEQ


def _att_proj_kernel(*refs, n_alias):
    n_src = 1 + N_WINDOWS + 3
    (gain_ref, sh_ref, sc_ref, w_ref, qg_ref, kg_ref, cos_ref, sin_ref) = refs[n_src:n_src + 8]
    o_ref, xo_ref, kc_ref, vc_ref, wbf_ref, buf_ref = refs[n_src + 8 + n_alias:]
    i = pl.program_id(0)

    @pl.when(i == 0)
    def _():
        wbf_ref[...] = w_ref[0].astype(BF16)

    x = _combine_tile(i, refs[0], refs[1:1 + N_WINDOWS], *refs[1 + N_WINDOWS:n_src], buf_ref)
    xo_ref[...] = x
    h = _norm_mod(x, gain_ref[...], sh_ref[0], sc_ref[0]).astype(BF16)
    acc = jnp.dot(h, wbf_ref[...], preferred_element_type=F32)

    def to_cache(cache_ref, chunk, first_head):
        @pl.when(i < N_CTX_TILES)
        def _():
            for s in range(SEQS_PER_TILE):
                for hh in range(LANES // HEAD_DIM):
                    cache_ref[s, 0, first_head + hh] = chunk[s * SEQ:(s + 1) * SEQ, hh * HEAD_DIM:(hh + 1) * HEAD_DIM]

    r = lax.broadcasted_iota(jnp.int32, (LANES, LANES), 0) // HEAD_DIM
    c = lax.broadcasted_iota(jnp.int32, (LANES, LANES), 1) // HEAD_DIM
    gmat = jnp.where(r == c, 1.0 / HEAD_DIM, 0.0).astype(BF16)
    cos = cos_ref[...]
    sin = sin_ref[...]
    n_q = ATT_Q_W // LANES
    n_kv = ATT_KV_W // LANES
    heads_per_chunk = LANES // HEAD_DIM
    for cidx in range(n_q + n_kv):
        xc = acc[:, cidx * LANES:(cidx + 1) * LANES]
        gain = qg_ref[...] if cidx < n_q else kg_ref[...]
        normed = xc * lax.rsqrt(_group_mean_sq(xc, gmat) + EPS) * gain
        if cidx >= n_q:
            to_cache(kc_ref, normed, (cidx - n_q) * heads_per_chunk)
        o_ref[:, cidx * LANES:(cidx + 1) * LANES] = _rope_chunk(normed, cos, sin, HEAD_DIM // 4).astype(BF16)
    for vidx in range(n_kv):
        v = acc[:, ATT_Q_W + ATT_KV_W + vidx * LANES:ATT_Q_W + ATT_KV_W + (vidx + 1) * LANES]
        to_cache(vc_ref, v, vidx * heads_per_chunk)
    o_ref[:, ATT_Q_W + ATT_KV_W:] = acc[:, ATT_Q_W + ATT_KV_W:].astype(BF16)


def _att_proj(pending, gain, shift, scale, w_in, layer_j, q_gain, k_gain, cos, sin, caches):
    n_alias = len(caches)
    cache_shape = caches[0].shape
    cache_spec = pl.BlockSpec((SEQS_PER_TILE, 1, ATT_KV_HEADS, SEQ, HEAD_DIM),
                              lambda i, *_: (jnp.minimum(i, N_CTX_TILES - 1), layer_j, 0, 0, 0))
    src_specs, src_args, scratch = _pending_specs(pending)
    in_specs = src_specs + [
        pl.BlockSpec((1, D_MODEL), lambda i, *_: (0, 0)),
        pl.BlockSpec((1, 1, D_MODEL), lambda i, *_: (_cond_row(i), 0, 0)),
        pl.BlockSpec((1, 1, D_MODEL), lambda i, *_: (_cond_row(i), 0, 0)),
        pl.BlockSpec((1, D_MODEL, ATT_IN), lambda i, *_: (layer_j, 0, 0)),
        pl.BlockSpec((1, LANES), lambda i, *_: (0, 0)),
        pl.BlockSpec((1, LANES), lambda i, *_: (0, 0)),
        pl.BlockSpec((TM, LANES), lambda i, *_: (_rope_row(i), 0)),
        pl.BlockSpec((TM, LANES), lambda i, *_: (_rope_row(i), 0)),
    ] + [pl.BlockSpec(memory_space=pl.ANY)] * n_alias
    args = src_args + [gain, shift, scale, w_in, q_gain, k_gain, cos, sin] + list(caches)
    first_cache = len(args) - n_alias
    return pl.pallas_call(
        functools.partial(_att_proj_kernel, n_alias=n_alias),
        grid_spec=pltpu.PrefetchScalarGridSpec(
            num_scalar_prefetch=1,
            grid=(T_ROWS // TM,),
            in_specs=in_specs,
            out_specs=[pl.BlockSpec((TM, ATT_IN), lambda i, *_: (i, 0)),
                       pl.BlockSpec((TM, D_MODEL), lambda i, *_: (i, 0)), cache_spec, cache_spec],
            scratch_shapes=[pltpu.VMEM((D_MODEL, ATT_IN), BF16)] + scratch,
        ),
        out_shape=[
            jax.ShapeDtypeStruct((T_ROWS, ATT_IN), BF16),
            jax.ShapeDtypeStruct((T_ROWS, D_MODEL), F32),
            jax.ShapeDtypeStruct(cache_shape, F32),
            jax.ShapeDtypeStruct(cache_shape, F32),
        ],
        input_output_aliases={first_cache: 2, first_cache + 1: 3},
        compiler_params=_params(1),
        name="att_proj",
    )(*args)


SINK_ROWS = 16
TN_DIMS = (((0,), (0,)), ((), ()))


def _ones_column(n):
    lane = lax.broadcasted_iota(jnp.int32, (n, HEAD_DIM), 1)
    return jnp.where(lane == 0, 1.0, 0.0).astype(BF16)


def _sink_softmax_pv(qs, sink_row, blocks):
    r = qs.shape[0]
    q_scaled = qs * jnp.asarray(ATT_SCALE, BF16)
    scores = []
    values = []
    for k, v, bias in blocks:
        s = lax.dot_general(k, q_scaled, NT_DIMS, preferred_element_type=F32)
        scores.append(s if bias is None else s + bias)
        values.append(jnp.concatenate([v, _ones_column(v.shape[0])], axis=1))
    row = lax.broadcasted_iota(jnp.int32, (SINK_ROWS, r), 0)
    scores.append(jnp.where(row == 0, sink_row, -jnp.inf))
    values.append(jnp.concatenate([jnp.zeros((SINK_ROWS, HEAD_DIM), BF16), _ones_column(SINK_ROWS)], axis=1))
    st = jnp.concatenate(scores, axis=0)
    m = jnp.max(st, axis=0, keepdims=True)
    pt = jnp.exp(st - m).astype(BF16)
    ov = lax.dot_general(pt, jnp.concatenate(values, axis=0), TN_DIMS, preferred_element_type=F32)
    return ov[:, :HEAD_DIM] / ov[:, HEAD_DIM:HEAD_DIM + 1]


def _stack_heads(q, kvh, rows):
    parts = []
    for g in range(ATT_GROUP):
        hd = kvh * ATT_GROUP + g
        parts.append(q[:, hd * HEAD_DIM:(hd + 1) * HEAD_DIM])
    return jnp.concatenate(parts, axis=0)


def _sink_row(sink_ref, kvh, rows):
    head = lax.broadcasted_iota(jnp.int32, (1, ATT_GROUP * rows), 1) // rows
    out = jnp.full((1, ATT_GROUP * rows), sink_ref[kvh * ATT_GROUP], F32)
    for g in range(1, ATT_GROUP):
        out = jnp.where(head == g, sink_ref[kvh * ATT_GROUP + g], out)
    return out


def _store_heads(o_ref, o, kvh, rows):
    for g in range(ATT_GROUP):
        hd = kvh * ATT_GROUP + g
        o_ref[:, hd * HEAD_DIM:(hd + 1) * HEAD_DIM] = o[g * rows:(g + 1) * rows, :].astype(BF16)


CTX_ATT_SEQS = 1


def _ctx_att_kernel(sink_ref, q_ref, k_ref, v_ref, o_ref):
    for s in range(CTX_ATT_SEQS):
        seq = slice(s * SEQ, (s + 1) * SEQ)
        q = q_ref[seq, :]
        for kvh in range(ATT_KV_HEADS):
            cols = slice(kvh * HEAD_DIM, (kvh + 1) * HEAD_DIM)
            qs = _stack_heads(q, kvh, SEQ)
            o = _sink_softmax_pv(qs, _sink_row(sink_ref, kvh, SEQ), [(k_ref[seq, cols], v_ref[seq, cols], None)])
            _store_heads(o_ref.at[seq, :], o, kvh, SEQ)


LAT_Q_BLOCKS = 2


def _lat_att_kernel(sink_ref, q_ref, k_ref, v_ref, ck_ref, cv_ref, o_ref):
    nb = DEC_SEQ // ATT_BLOCK
    blk = ATT_BLOCK
    rows = ATT_GROUP * blk
    kj = lax.broadcasted_iota(jnp.int32, (3 * blk, rows), 0) - blk
    qi = lax.broadcasted_iota(jnp.int32, (3 * blk, rows), 1) % blk
    in_window = jnp.abs(qi - kj) <= WINDOW
    for sub in range(LAT_Q_BLOCKS):
        n = pl.program_id(1) * LAT_Q_BLOCKS + sub
        prev0 = pl.multiple_of(jnp.maximum(n - 1, 0) * blk, blk)
        cur0 = pl.multiple_of(n * blk, blk)
        next0 = pl.multiple_of(jnp.minimum(n + 1, nb - 1) * blk, blk)
        kabs = n * blk + kj
        bias = jnp.where(in_window & (kabs >= 0) & (kabs < DEC_SEQ), 0.0, -jnp.inf)
        q = q_ref[sub * blk:(sub + 1) * blk, :]
        for kvh in range(ATT_KV_HEADS):
            cols = slice(kvh * HEAD_DIM, (kvh + 1) * HEAD_DIM)
            qs = _stack_heads(q, kvh, blk)
            k_loc = jnp.concatenate([k_ref[pl.ds(prev0, blk), cols], k_ref[pl.ds(cur0, blk), cols],
                                     k_ref[pl.ds(next0, blk), cols]], axis=0)
            v_loc = jnp.concatenate([v_ref[pl.ds(prev0, blk), cols], v_ref[pl.ds(cur0, blk), cols],
                                     v_ref[pl.ds(next0, blk), cols]], axis=0)
            k_ctx = ck_ref[0, 0, kvh].astype(BF16)
            v_ctx = cv_ref[0, 0, kvh].astype(BF16)
            o = _sink_softmax_pv(qs, _sink_row(sink_ref, kvh, blk),
                                 [(k_loc, v_loc, bias), (k_ctx, v_ctx, None)])
            _store_heads(o_ref.at[sub * blk:(sub + 1) * blk, :], o, kvh, blk)


def _ctx_attention(qkv, sink):
    kcol = ATT_Q_W // ATT_KV_W
    rows = CTX_ATT_SEQS * SEQ
    return pl.pallas_call(
        _ctx_att_kernel,
        grid_spec=pltpu.PrefetchScalarGridSpec(
            num_scalar_prefetch=1,
            grid=(BATCH // CTX_ATT_SEQS,),
            in_specs=[
                pl.BlockSpec((rows, ATT_Q_W), lambda b, s: (b, 0)),
                pl.BlockSpec((rows, ATT_KV_W), lambda b, s: (b, kcol)),
                pl.BlockSpec((rows, ATT_KV_W), lambda b, s: (b, kcol + 1)),
            ],
            out_specs=pl.BlockSpec((rows, ATT_Q_W), lambda b, s: (b, 0)),
        ),
        out_shape=jax.ShapeDtypeStruct((CTX_ROWS, ATT_Q_W), BF16),
        compiler_params=_params(1),
        name="ctx_attention",
    )(sink, qkv, qkv, qkv)


def _lat_attention(qkv, sink, cache_k, cache_v, layer_j):
    kcol = ATT_Q_W // ATT_KV_W
    qrows = LAT_Q_BLOCKS * ATT_BLOCK
    nb = DEC_SEQ // qrows
    q0 = CTX_ROWS // qrows
    s0 = CTX_ROWS // DEC_SEQ
    cache_spec = pl.BlockSpec((1, 1, ATT_KV_HEADS, PAST_LEN, HEAD_DIM), lambda b, n, s: (b, layer_j, 0, 0, 0))
    return pl.pallas_call(
        _lat_att_kernel,
        grid_spec=pltpu.PrefetchScalarGridSpec(
            num_scalar_prefetch=1,
            grid=(DEC_BATCH, nb),
            in_specs=[
                pl.BlockSpec((qrows, ATT_Q_W), lambda b, n, s: (q0 + b * nb + n, 0)),
                pl.BlockSpec((DEC_SEQ, ATT_KV_W), lambda b, n, s: (s0 + b, kcol)),
                pl.BlockSpec((DEC_SEQ, ATT_KV_W), lambda b, n, s: (s0 + b, kcol + 1)),
                cache_spec,
                cache_spec,
            ],
            out_specs=pl.BlockSpec((qrows, ATT_Q_W), lambda b, n, s: (b * nb + n, 0)),
        ),
        out_shape=jax.ShapeDtypeStruct((LAT_ROWS, ATT_Q_W), BF16),
        compiler_params=_params(2),
        name="lat_attention",
    )(sink, qkv, qkv, qkv, cache_k, cache_v)


ROW_ALIGN = 16
S_SLOTS = 576
BUF_ROWS = S_SLOTS + TM_MOE
XW = D_MODEL + LANES
SLOT_LANE = 6
GATE_LANE0 = 8
REGION_TILES = 28
REGION_ROWS = REGION_TILES * TM_MOE
TAB_W = 16
HALF_WIN = TM_MOE // 2
N_TOK_TILES = T_ROWS // TM


def _router_kernel(ac_ref, al_ref, wo_ref, x_ref, ga_ref, gain_ref, sh_ref, sc_ref, wr_ref, bias_ref,
                   xo_ref, xs_hbm, row_ref, tab_ref, wl_ref, wobf_ref, buf_ref, off_ref, wide_ref, sem):
    i = pl.program_id(0)
    ng = N_GROUPS
    last = pl.num_programs(0) - 1

    @pl.when(i == 0)
    def _():
        wobf_ref[...] = wo_ref[0].astype(BF16)

    mixed = jnp.where(i < N_CTX_TILES, ac_ref[...], al_ref[...])
    x_new = x_ref[...] + ga_ref[0] * jnp.dot(mixed, wobf_ref[...], preferred_element_type=F32)
    xo_ref[...] = x_new

    @pl.when(i == 0)
    def _():
        buf_ref[...] = jnp.zeros_like(buf_ref)
        for g in range(ng):
            off_ref[g] = 0

        def clear(r, carry):
            for c in range(TAB_W):
                tab_ref[r * TAB_W + c] = 0
            return carry

        lax.fori_loop(0, N_TOK_TILES, clear, 0)

    hf = _norm_mod(x_new, gain_ref[...], sh_ref[0], sc_ref[0])
    hb = hf.astype(BF16)
    hl = (hf - hb.astype(F32)).astype(BF16)
    both = jnp.dot(hb, wr_ref[...], preferred_element_type=F32)
    logits = (both[:, :LANES] + both[:, LANES:]
              + jnp.dot(hl, wr_ref[:, :LANES], preferred_element_type=F32))
    lt = logits.T[0:N_EXPERTS, :]
    scores = jax.nn.sigmoid(lt)
    sel = scores + bias_ref[...]
    xs = [sel[k * ng:(k + 1) * ng, :] for k in range(EXPERTS_PER_GROUP)]
    sc = [scores[k * ng:(k + 1) * ng, :] for k in range(EXPERTS_PER_GROUP)]
    a, b, c, d = xs
    gs = jnp.maximum(jnp.maximum(jnp.maximum(a + b, a + c), jnp.maximum(a + d, b + c)),
                     jnp.maximum(b + d, c + d))
    bv = gs[0:1, :]
    bg = jnp.zeros(bv.shape, jnp.int32)
    for g in range(1, ng):
        better = gs[g:g + 1, :] > bv
        bg = jnp.where(better, g, bg)
        bv = jnp.where(better, gs[g:g + 1, :], bv)
    giota = lax.broadcasted_iota(jnp.int32, (ng, TM), 0)
    onehot = giota == bg
    wk = []
    for k in range(EXPERTS_PER_GROUP):
        rank = jnp.zeros((ng, TM), F32)
        for j in range(EXPERTS_PER_GROUP):
            if j < k:
                rank = rank + (xs[j] >= xs[k]).astype(F32)
            elif j > k:
                rank = rank + (xs[j] > xs[k]).astype(F32)
        chosen = (rank < 2.0) & onehot
        wk.append(jnp.sum(jnp.where(chosen, sc[k], 0.0), axis=0, keepdims=True))
    den = wk[0] + wk[1] + wk[2] + wk[3]
    gates = [w / den for w in wk]

    oh = jnp.where(onehot, 1.0, 0.0)
    r = lax.broadcasted_iota(jnp.int32, (TM, TM), 0)
    cc = lax.broadcasted_iota(jnp.int32, (TM, TM), 1)
    tri = jnp.where(r < cc, 1.0, 0.0).astype(BF16)
    before = jnp.dot(oh.astype(BF16), tri, preferred_element_type=F32)
    rank_local = jnp.sum(jnp.where(onehot, before, 0.0), axis=0, keepdims=True).astype(jnp.int32)
    counts = [jnp.sum(jnp.where(bg == g, 1, 0)) for g in range(ng)]
    pads = [((cnt + ROW_ALIGN - 1) // ROW_ALIGN) * ROW_ALIGN for cnt in counts]
    segs = [0]
    for g in range(1, ng):
        segs.append(segs[-1] + pads[g - 1])
    total = segs[-1] + pads[-1]
    seg_of = jnp.zeros(bg.shape, jnp.int32)
    for g in range(1, ng):
        seg_of = jnp.where(bg == g, segs[g], seg_of)
    slot = seg_of + rank_local

    parts = []
    for gt in gates:
        hi = gt.astype(BF16).astype(F32)
        rest = gt - hi
        mid = rest.astype(BF16).astype(F32)
        parts += [hi, mid, (rest - mid).astype(BF16).astype(F32)]
    zero_row = jnp.zeros((1, TM), F32)
    info = jnp.concatenate(gates + [bg.astype(F32), zero_row, slot.astype(F32), zero_row] + parts
                           + [jnp.zeros((LANES - GATE_LANE0 - len(parts), TM), F32)], axis=0)
    rowinfo = info.T
    row_ref[...] = rowinfo

    h_ext = jnp.concatenate([hb, rowinfo.astype(BF16)], axis=1)
    srow = lax.broadcasted_iota(jnp.int32, (S_SLOTS, TM), 0)
    pick = jnp.where(srow == slot, 1.0, 0.0).astype(BF16)
    par = i % 2
    buf_ref[par, 0:S_SLOTS, :] = jnp.dot(pick, h_ext, preferred_element_type=F32).astype(BF16)

    def window_copy(g, src_row, dst_row, rows):
        return pltpu.make_async_copy(
            buf_ref.at[par, pl.ds(pl.multiple_of(src_row, ROW_ALIGN), rows), :],
            xs_hbm.at[pl.ds(pl.multiple_of(dst_row, ROW_ALIGN), rows), :],
            sem.at[g])

    def wait_windows(rows):
        for g in range(ng):
            window_copy(g, 0, 0, rows).wait()

    def wait_previous():
        @pl.when(wide_ref[0] == 1)
        def _():
            wait_windows(TM_MOE)

        @pl.when(wide_ref[0] == 0)
        def _():
            wait_windows(HALF_WIN)

    @pl.when(i > 0)
    def _():
        wait_previous()

    wide = jnp.maximum(jnp.maximum(pads[0], pads[1]), jnp.maximum(pads[2], pads[3])) > HALF_WIN
    offs = [off_ref[g] for g in range(ng)]

    def issue(rows):
        for g in range(ng):
            window_copy(g, segs[g], g * REGION_ROWS + offs[g], rows).start()

    @pl.when(wide)
    def _():
        issue(TM_MOE)

    @pl.when(jnp.logical_not(wide))
    def _():
        issue(HALF_WIN)

    wide_ref[0] = wide.astype(jnp.int32)
    for g in range(ng):
        tab_ref[i * TAB_W + g] = segs[g]
        tab_ref[i * TAB_W + ng + g] = offs[g]
        tab_ref[i * TAB_W + 2 * ng + 1 + g] = pads[g]
        off_ref[g] = offs[g] + pads[g]
    tab_ref[i * TAB_W + 2 * ng] = total

    @pl.when(i == last)
    def _():
        wait_previous()
        for g in range(ng):
            window_copy(g, S_SLOTS, g * REGION_ROWS + off_ref[g], TM_MOE).start()
        wait_windows(TM_MOE)

        entry = 0
        full = 0
        for g in range(ng):
            full = (off_ref[g] + TM_MOE - 1) // TM_MOE

            def add(j, e, g=g, full=full):
                wl_ref[e] = g * REGION_TILES + j
                wl_ref[N_MOE_TILES + e] = g
                wl_ref[2 * N_MOE_TILES + e] = (j < full).astype(jnp.int32)
                return e + 1

            entry = lax.fori_loop(0, full + 1, add, entry)

        def pad(e, carry, full=full):
            wl_ref[e] = (ng - 1) * REGION_TILES + full
            wl_ref[N_MOE_TILES + e] = ng - 1
            wl_ref[2 * N_MOE_TILES + e] = 0
            return carry

        lax.fori_loop(entry, N_MOE_TILES, pad, 0)


def _mix_out_router(a_ctx, a_lat, w_out, layer_j, x, gate_a, gain, shift, scale, wr_parts, bias_col):
    k = a_ctx.shape[1]
    return pl.pallas_call(
        _router_kernel,
        grid=(N_TOK_TILES,),
        in_specs=[
            pl.BlockSpec((TM, k), lambda i: (jnp.minimum(i, N_CTX_TILES - 1), 0)),
            pl.BlockSpec((TM, k), lambda i: (jnp.maximum(i - N_CTX_TILES, 0), 0)),
            pl.BlockSpec((1, k, D_MODEL), lambda i: (layer_j, 0, 0), pipeline_mode=pl.Buffered(1)),
            pl.BlockSpec((TM, D_MODEL), lambda i: (i, 0)),
            pl.BlockSpec((1, 1, D_MODEL), lambda i: (_cond_row(i), 0, 0)),
            pl.BlockSpec((1, D_MODEL), lambda i: (0, 0)),
            pl.BlockSpec((1, 1, D_MODEL), lambda i: (_cond_row(i), 0, 0)),
            pl.BlockSpec((1, 1, D_MODEL), lambda i: (_cond_row(i), 0, 0)),
            pl.BlockSpec((D_MODEL, 2 * LANES), lambda i: (0, 0)),
            pl.BlockSpec((N_EXPERTS, 1), lambda i: (0, 0)),
        ],
        out_specs=[
            pl.BlockSpec((TM, D_MODEL), lambda i: (i, 0)),
            pl.BlockSpec(memory_space=pl.ANY),
            pl.BlockSpec((TM, LANES), lambda i: (i, 0)),
            pl.BlockSpec(memory_space=pltpu.SMEM),
            pl.BlockSpec(memory_space=pltpu.SMEM),
        ],
        out_shape=[
            jax.ShapeDtypeStruct((T_ROWS, D_MODEL), F32),
            jax.ShapeDtypeStruct((N_GROUPS * REGION_ROWS, XW), BF16),
            jax.ShapeDtypeStruct((T_ROWS, LANES), F32),
            jax.ShapeDtypeStruct((N_TOK_TILES * TAB_W,), jnp.int32),
            jax.ShapeDtypeStruct((3 * N_MOE_TILES,), jnp.int32),
        ],
        scratch_shapes=[
            pltpu.VMEM((k, D_MODEL), BF16),
            pltpu.VMEM((2, BUF_ROWS, XW), BF16),
            pltpu.SMEM((N_GROUPS,), jnp.int32),
            pltpu.SMEM((1,), jnp.int32),
            pltpu.SemaphoreType.DMA((N_GROUPS,)),
        ],
        compiler_params=_params(1),
        name="mix_out_router",
    )(a_ctx, a_lat, w_out, x, gate_a, gain, shift, scale, wr_parts, bias_col)


UP_EXPERTS = 4


def _wl_block(wl, i):
    return wl[i]


def _wl_group(wl, i):
    return wl[N_MOE_TILES + i]


def _wl_valid(wl, i):
    return wl[2 * N_MOE_TILES + i]


def _moe_up_kernel(wl_ref, xs_ref, w_ref, o_ref, wbf_ref):
    kk = pl.program_id(0)
    i = pl.program_id(1)
    new_w = (i == 0) | (_wl_group(wl_ref, i) != _wl_group(wl_ref, jnp.maximum(i - 1, 0)))
    valid = _wl_valid(wl_ref, i)

    @pl.when(new_w)
    def _():
        wbf_ref[...] = w_ref[...].astype(BF16)

    @pl.when(valid == 1)
    def _():
        xs = xs_ref[:, :D_MODEL]
        extra = xs_ref[:, D_MODEL:].astype(F32)
        lane = lax.broadcasted_iota(jnp.int32, extra.shape, 1)
        for e in range(UP_EXPERTS):
            hu = jnp.dot(xs, wbf_ref[e], preferred_element_type=F32)
            lane0 = GATE_LANE0 + 3 * (kk * UP_EXPERTS + e)
            gcol = jnp.sum(jnp.where((lane >= lane0) & (lane < lane0 + 3), extra, 0.0), axis=1, keepdims=True)
            a = _silu(hu[:, :D_EXPERT]) * hu[:, D_EXPERT:] * gcol
            o_ref[:, e * D_EXPERT:(e + 1) * D_EXPERT] = a.astype(BF16)

    @pl.when(valid != 1)
    def _():
        o_ref[...] = jnp.zeros_like(o_ref)


def _moe_up(work_list, xs, w_up, layer):
    steps = EXPERTS_PER_GROUP // UP_EXPERTS
    return pl.pallas_call(
        _moe_up_kernel,
        grid_spec=pltpu.PrefetchScalarGridSpec(
            num_scalar_prefetch=1,
            grid=(steps, N_MOE_TILES),
            in_specs=[
                pl.BlockSpec((TM_MOE, XW), lambda k, i, wl: (_wl_block(wl, i), 0)),
                pl.BlockSpec((UP_EXPERTS, D_MODEL, 2 * D_EXPERT),
                             lambda k, i, wl: ((layer * N_GROUPS + _wl_group(wl, i)) * steps + k, 0, 0)),
            ],
            out_specs=pl.BlockSpec((TM_MOE, UP_EXPERTS * D_EXPERT), lambda k, i, wl: (_wl_block(wl, i), k)),
            scratch_shapes=[pltpu.VMEM((UP_EXPERTS, D_MODEL, 2 * D_EXPERT), BF16)],
        ),
        out_shape=jax.ShapeDtypeStruct((N_GROUPS * REGION_ROWS, EXPERTS_PER_GROUP * D_EXPERT), BF16),
        compiler_params=_params(2),
        name="moe_up",
    )(work_list, xs, w_up)


def _moe_down_kernel(wl_ref, a_ref, w_ref, o_ref, wbf_ref):
    i = pl.program_id(0)
    new_w = (i == 0) | (_wl_group(wl_ref, i) != _wl_group(wl_ref, jnp.maximum(i - 1, 0)))
    valid = _wl_valid(wl_ref, i)

    @pl.when(new_w)
    def _():
        wbf_ref[...] = w_ref[0].astype(BF16)

    @pl.when(valid == 1)
    def _():
        o_ref[...] = jnp.dot(a_ref[...], wbf_ref[...], preferred_element_type=F32).astype(BF16)

    @pl.when(valid != 1)
    def _():
        o_ref[...] = jnp.zeros_like(o_ref)


def _moe_down(work_list, a, w_down_grouped, layer):
    kdim = EXPERTS_PER_GROUP * D_EXPERT
    return pl.pallas_call(
        _moe_down_kernel,
        grid_spec=pltpu.PrefetchScalarGridSpec(
            num_scalar_prefetch=1,
            grid=(N_MOE_TILES,),
            in_specs=[
                pl.BlockSpec((TM_MOE, kdim), lambda i, wl: (_wl_block(wl, i), 0)),
                pl.BlockSpec((1, kdim, D_MODEL), lambda i, wl: (layer * N_GROUPS + _wl_group(wl, i), 0, 0)),
            ],
            out_specs=pl.BlockSpec((TM_MOE, D_MODEL), lambda i, wl: (_wl_block(wl, i), 0)),
            scratch_shapes=[pltpu.VMEM((kdim, D_MODEL), BF16)],
        ),
        out_shape=jax.ShapeDtypeStruct((N_GROUPS * REGION_ROWS, D_MODEL), BF16),
        compiler_params=_params(1),
        name="moe_down",
    )(work_list, a, w_down_grouped)


N_WINDOWS = 2 * N_GROUPS


class PendingMoe(NamedTuple):
    tab_flat: jax.Array
    ys: jax.Array
    rowinfo: jax.Array
    x: jax.Array
    gate: jax.Array


def _combine_tile(i, tab_ref, windows, row_ref, x_ref, g_ref, buf_ref):
    first = windows[:N_GROUPS]
    second = windows[N_GROUPS:]

    @pl.when(i == 0)
    def _():
        buf_ref[...] = jnp.zeros_like(buf_ref)

    for g in range(N_GROUPS):
        start = pl.multiple_of(tab_ref[i * TAB_W + g], ROW_ALIGN)
        buf_ref[pl.ds(start, HALF_WIN), :] = first[g][...]

        @pl.when(tab_ref[i * TAB_W + 2 * N_GROUPS + 1 + g] > HALF_WIN)
        def _(g=g, start=start):
            buf_ref[pl.ds(start + HALF_WIN, HALF_WIN), :] = second[g][...]

    slot = row_ref[:, SLOT_LANE:SLOT_LANE + 1].astype(jnp.int32)
    scol = lax.broadcasted_iota(jnp.int32, (TM, S_SLOTS), 1)
    pick = jnp.where(scol == slot, 1.0, 0.0).astype(BF16)
    y = jnp.dot(pick, buf_ref[0:S_SLOTS, :], preferred_element_type=F32)
    return x_ref[...] + g_ref[0] * y


def _pending_specs(p):
    def window_spec(g, second):
        def index(i, tab):
            off = tab[i * TAB_W + N_GROUPS + g]
            if second:
                off = jnp.where(tab[i * TAB_W + 2 * N_GROUPS + 1 + g] > HALF_WIN, off + HALF_WIN, 0)
            return pl.multiple_of(off + g * REGION_ROWS, ROW_ALIGN), 0

        return pl.BlockSpec((pl.Element(HALF_WIN), pl.Element(D_MODEL)), index)

    specs = [window_spec(g, False) for g in range(N_GROUPS)] + [window_spec(g, True) for g in range(N_GROUPS)]
    specs += [
        pl.BlockSpec((TM, LANES), lambda i, tab: (i, 0)),
        pl.BlockSpec((TM, D_MODEL), lambda i, tab: (i, 0)),
        pl.BlockSpec((1, 1, D_MODEL), lambda i, tab: (_cond_row(i), 0, 0)),
    ]
    args = [p.tab_flat] + [p.ys] * N_WINDOWS + [p.rowinfo, p.x, p.gate]
    return specs, args, [pltpu.VMEM((BUF_ROWS, D_MODEL), BF16)]


def _moe_combine_kernel(tab_ref, *refs):
    row_ref, x_ref, g_ref, ctx_ref, lat_ref, buf_ref = refs[N_WINDOWS:]
    i = pl.program_id(0)
    new_x = _combine_tile(i, tab_ref, refs[:N_WINDOWS], row_ref, x_ref, g_ref, buf_ref)

    @pl.when(i < N_CTX_TILES)
    def _():
        ctx_ref[...] = new_x

    @pl.when(i >= N_CTX_TILES)
    def _():
        lat_ref[...] = new_x


def _moe_combine(pending):
    specs, args, scratch = _pending_specs(pending)
    return pl.pallas_call(
        _moe_combine_kernel,
        grid_spec=pltpu.PrefetchScalarGridSpec(
            num_scalar_prefetch=1,
            grid=(N_TOK_TILES,),
            in_specs=specs,
            out_specs=[pl.BlockSpec((TM, D_MODEL), lambda i, tab: (jnp.minimum(i, N_CTX_TILES - 1), 0)),
                       pl.BlockSpec((TM, D_MODEL), lambda i, tab: (jnp.maximum(i - N_CTX_TILES, 0), 0))],
            scratch_shapes=scratch,
        ),
        out_shape=[jax.ShapeDtypeStruct((CTX_ROWS, D_MODEL), F32), jax.ShapeDtypeStruct((LAT_ROWS, D_MODEL), F32)],
        compiler_params=_params(1),
        name="moe_combine",
    )(*args)


def _mix_out_and_moe(mix, x, gate_a, gain, shift, scale, gate, router_w, w_up, w_down_grouped, layer):
    wr_parts, bias_col = router_w
    x, xs, rowinfo, tab, work_list = _mix_out_router(*mix, x, gate_a, gain, shift, scale, wr_parts, bias_col)
    a = _moe_up(work_list, xs, w_up, layer)
    ys = _moe_down(work_list, a, w_down_grouped, layer)
    return PendingMoe(tab, ys, rowinfo, x, gate)


def kernel(x_prompt, x_sample, state_ret, cache_k, cache_v, c, c_ctx, w_mod, b_mod, norm_mix,
           norm_moe, ret_w_in, ret_decay, ret_w_out, att_w_in, att_q_norm, att_k_norm, att_sink,
           att_w_out, w_router, router_bias, moe_w_up, moe_w_down):
    cond8 = jnp.zeros((8, D_MODEL), F32).at[0].set(c_ctx).at[1:N_COND].set(c)
    mods = _modulation_all(cond8, w_mod, b_mod)
    mods = mods[:, :N_COND].reshape(DEPTH, N_COND, N_MOD, 1, D_MODEL)

    ret_cos, ret_sin = _rope_tables(RET_DK)
    att_cos, att_sin = _rope_tables(HEAD_DIM)

    perm = jnp.arange(N_EXPERTS).reshape(N_GROUPS, EXPERTS_PER_GROUP).T.reshape(-1)
    wr = jnp.zeros((D_MODEL, LANES), F32).at[:, :N_EXPERTS].set(w_router[:, perm])
    wr_hi = wr.astype(BF16)
    wr_lo = (wr - wr_hi.astype(F32)).astype(BF16)
    bias_col = router_bias[perm].astype(F32).reshape(N_EXPERTS, 1)
    router_w = (jnp.concatenate([wr_hi, wr_lo], axis=1), bias_col)

    w_up_all = moe_w_up.reshape(DEPTH * N_EXPERTS, D_MODEL, 2 * D_EXPERT)
    w_down_all = moe_w_down.reshape(DEPTH * N_GROUPS, EXPERTS_PER_GROUP * D_EXPERT, D_MODEL)
    ret_w_in_bf16 = ret_w_in.astype(BF16)
    new_state = jnp.zeros((BATCH, (DEPTH + 1) // 2, 2, RET_HEADS, RET_DK, RET_DV), F32)
    cache_shape = (BATCH, DEPTH // 2, ATT_KV_HEADS, SEQ, HEAD_DIM)
    new_kv = (jnp.zeros(cache_shape, F32), jnp.zeros(cache_shape, F32))
    source = (x_prompt.reshape(CTX_ROWS, D_MODEL), x_sample.reshape(LAT_ROWS, D_MODEL))
    for layer in range(DEPTH):
        sh_a, sc_a, g_a, sh_m, sc_m, g_m = [mods[layer, :, t] for t in range(N_MOD)]
        gain_mix = norm_mix[layer].reshape(1, D_MODEL)
        gain_moe = norm_moe[layer].reshape(1, D_MODEL)
        j = layer // 2
        if layer % 2 == 0:
            proj, x = _ret_proj(source, gain_mix, sh_a, sc_a, ret_w_in_bf16, j, ret_cos, ret_sin)
            o_ctx, new_state = _ret_core(proj, ret_decay[j], state_ret, j, latent=False, new_state=new_state)
            (o_lat,) = _ret_core(proj, ret_decay[j], state_ret, j, latent=True)
            mix = (o_ctx, o_lat, ret_w_out, j)
        else:
            q_gain = jnp.tile(att_q_norm[j], LANES // HEAD_DIM).reshape(1, LANES)
            k_gain = jnp.tile(att_k_norm[j], LANES // HEAD_DIM).reshape(1, LANES)
            qkv, x, new_k, new_v = _att_proj(source, gain_mix, sh_a, sc_a, att_w_in, j, q_gain, k_gain,
                                             att_cos, att_sin, new_kv)
            new_kv = (new_k, new_v)
            sink = att_sink[j].astype(F32)
            o_ctx = _ctx_attention(qkv, sink)
            o_lat = _lat_attention(qkv, sink, cache_k, cache_v, j)
            mix = (o_ctx, o_lat, att_w_out, j)
        source = _mix_out_and_moe(mix, x, g_a, gain_moe, sh_m, sc_m, g_m, router_w, w_up_all, w_down_all, layer)

    x_ctx, x_lat = _moe_combine(source)
    y_prompt = x_ctx.reshape(BATCH, SEQ, D_MODEL)
    y_sample = x_lat.reshape(DEC_BATCH, DEC_SEQ, D_MODEL)
    return (y_prompt, y_sample, new_state, new_kv[0], new_kv[1])
```

```python
import functools
from typing import NamedTuple

import jax
import jax.numpy as jnp
import numpy as np
from jax import lax
from jax.experimental import pallas as pl
from jax.experimental.pallas import tpu as pltpu

F32 = jnp.float32
BF16 = jnp.bfloat16

D_MODEL = 1024
BATCH = 16
SEQ = 256
DEPTH = 4
DEC_BATCH = 2
DEC_SEQ = 4096
PAST_LEN = 512
GRID_W = 64
N_MOD = 6
EPS = 1e-6
ROPE_BASE = 10000.0
RET_HEADS = 8
RET_DK = 128
RET_DV = 256
RET_CHUNK = 256
RET_IN = 2 * RET_HEADS * RET_DK + 2 * RET_HEADS * RET_DV
ATT_HEADS = 16
ATT_KV_HEADS = 4
ATT_GROUP = 4
HEAD_DIM = 64
WINDOW = 128
ATT_BLOCK = 128
ATT_IN = (ATT_HEADS + 2 * ATT_KV_HEADS) * HEAD_DIM
ATT_SCALE = HEAD_DIM ** -0.5
N_EXPERTS = 16
N_GROUPS = 4
EXPERTS_PER_GROUP = 4
D_EXPERT = 512

CTX_ROWS = BATCH * SEQ
LAT_ROWS = DEC_BATCH * DEC_SEQ
T_ROWS = CTX_ROWS + LAT_ROWS
N_COND = 1 + DEC_BATCH

LANES = 128
TM = 512
N_CTX_TILES = CTX_ROWS // TM
N_LAT_TILES = DEC_SEQ // TM
TM_MOE = 512
N_MOE_TILES = 36
VMEM_LIMIT = 52 * 1024 * 1024

NT_DIMS = (((1,), (1,)), ((), ()))


def _params(n_axes, vmem=VMEM_LIMIT):
    return pltpu.CompilerParams(dimension_semantics=("arbitrary",) * n_axes, vmem_limit_bytes=vmem)


def _cond_row(i):
    return (i * TM) // DEC_SEQ


def _rope_row(i):
    return jnp.where(i < N_CTX_TILES, 0, 1 + (i - N_CTX_TILES) % N_LAT_TILES)


def _norm_mod(x, gain, shift, scale):
    ms = jnp.mean(x * x, axis=-1, keepdims=True)
    y = x * lax.rsqrt(ms + EPS) * gain
    return y * (1.0 + scale) + shift


def _silu(x):
    return x * jax.nn.sigmoid(x)


def _mod_kernel(c_ref, w_ref, b_ref, o_ref):
    c = c_ref[...]
    s = _silu(c).astype(BF16)
    o_ref[0] = jnp.dot(s, w_ref[0].astype(BF16), preferred_element_type=F32) + b_ref[0]


def _modulation_all(cond8, w_mod, b_mod):
    tn = 1536
    n = N_MOD * D_MODEL
    return pl.pallas_call(
        _mod_kernel,
        grid=(DEPTH, n // tn),
        in_specs=[
            pl.BlockSpec((8, D_MODEL), lambda l, j: (0, 0)),
            pl.BlockSpec((1, D_MODEL, tn), lambda l, j: (l, 0, j)),
            pl.BlockSpec((1, 1, tn), lambda l, j: (l, 0, j)),
        ],
        out_specs=pl.BlockSpec((1, 8, tn), lambda l, j: (l, 0, j)),
        out_shape=jax.ShapeDtypeStruct((DEPTH, 8, n), F32),
        compiler_params=_params(2),
        name="modulation",
    )(cond8, w_mod, b_mod.reshape(DEPTH, 1, n))


def _rope_tables(head_dim):
    half = head_dim // 2
    quarter = half // 2
    t = np.arange(DEC_SEQ)
    row = (t // GRID_W).astype(np.float64)
    col = (t % GRID_W).astype(np.float64)
    inv_freq = ROPE_BASE ** (-np.arange(quarter, dtype=np.float64) / quarter)
    lane = np.arange(LANES)
    d = lane % head_dim
    w = d % half
    f = w % quarter
    pos = np.where((d // half)[None, :] == 0, row[:, None], col[:, None])
    ang = pos * inv_freq[f][None, :]
    cos = np.cos(ang)
    sin = np.where((w < quarter)[None, :], -np.sin(ang), np.sin(ang))
    cos = np.concatenate([np.ones((TM, LANES)), cos], axis=0).astype(np.float32)
    sin = np.concatenate([np.zeros((TM, LANES)), sin], axis=0).astype(np.float32)
    return jnp.asarray(cos), jnp.asarray(sin)


def _rope_chunk(xc, cos, sin, quarter):
    lane = lax.broadcasted_iota(jnp.int32, xc.shape, 1)
    first = (lane % (2 * quarter)) < quarter
    partner = jnp.where(first, pltpu.roll(xc, LANES - quarter, 1), pltpu.roll(xc, quarter, 1))
    return xc * cos + partner * sin


RET_TN = 1024
RET_Q_TILES = RET_HEADS * RET_DK // RET_TN
RET_QK_TILES = 2 * RET_Q_TILES


def _ret_proj_kernel(*refs, from_moe):
    i = pl.program_id(0)
    if from_moe:
        n_src = 1 + N_WINDOWS + 3
        x = _combine_tile(i, refs[0], refs[1:1 + N_WINDOWS], *refs[1 + N_WINDOWS:n_src], refs[-1])
    else:
        n_src = 2
        x = jnp.where(i < N_CTX_TILES, refs[0][...], refs[1][...])
    gain_ref, sh_ref, sc_ref, w_ref, cos_ref, sin_ref, o_ref, xo_ref = refs[n_src:n_src + 8]
    xo_ref[...] = x
    h = _norm_mod(x, gain_ref[...], sh_ref[0], sc_ref[0]).astype(BF16)
    cos = cos_ref[...]
    sin = sin_ref[...]
    for j in range(RET_IN // RET_TN):
        cols = slice(j * RET_TN, (j + 1) * RET_TN)
        acc = jnp.dot(h, w_ref[0, :, cols], preferred_element_type=F32)
        if j < RET_QK_TILES:
            scale = 1.0 if j < RET_Q_TILES else RET_DK ** -0.5
            for c in range(RET_TN // LANES):
                xc = acc[:, c * LANES:(c + 1) * LANES]
                lanes = slice(j * RET_TN + c * LANES, j * RET_TN + (c + 1) * LANES)
                o_ref[:, lanes] = (_rope_chunk(xc, cos, sin, RET_DK // 4) * scale).astype(BF16)
        else:
            o_ref[:, cols] = acc.astype(BF16)


def _ret_proj(source, gain, shift, scale, w_in_bf16, layer_j, cos, sin):
    from_moe = isinstance(source, PendingMoe)
    if from_moe:
        src_specs, src_args, scratch = _pending_specs(source)
        n_prefetch = 1
    else:
        src_specs = [pl.BlockSpec((TM, D_MODEL), lambda i, *_: (jnp.minimum(i, N_CTX_TILES - 1), 0)),
                     pl.BlockSpec((TM, D_MODEL), lambda i, *_: (jnp.maximum(i - N_CTX_TILES, 0), 0))]
        src_args, scratch, n_prefetch = list(source), [], 0
    return pl.pallas_call(
        functools.partial(_ret_proj_kernel, from_moe=from_moe),
        grid_spec=pltpu.PrefetchScalarGridSpec(
            num_scalar_prefetch=n_prefetch,
            grid=(T_ROWS // TM,),
            in_specs=src_specs + [
                pl.BlockSpec((1, D_MODEL), lambda i, *_: (0, 0)),
                pl.BlockSpec((1, 1, D_MODEL), lambda i, *_: (_cond_row(i), 0, 0)),
                pl.BlockSpec((1, 1, D_MODEL), lambda i, *_: (_cond_row(i), 0, 0)),
                pl.BlockSpec((1, D_MODEL, RET_IN), lambda i, *_: (layer_j, 0, 0), pipeline_mode=pl.Buffered(1)),
                pl.BlockSpec((TM, LANES), lambda i, *_: (_rope_row(i), 0)),
                pl.BlockSpec((TM, LANES), lambda i, *_: (_rope_row(i), 0)),
            ],
            out_specs=[pl.BlockSpec((TM, RET_IN), lambda i, *_: (i, 0)),
                       pl.BlockSpec((TM, D_MODEL), lambda i, *_: (i, 0))],
            scratch_shapes=scratch,
        ),
        out_shape=[jax.ShapeDtypeStruct((T_ROWS, RET_IN), BF16), jax.ShapeDtypeStruct((T_ROWS, D_MODEL), F32)],
        compiler_params=_params(1),
        name="ret_proj",
    )(*src_args, gain, shift, scale, w_in_bf16, cos, sin)


def _log_sigmoid(x):
    return -(jnp.maximum(-x, 0.0) + jnp.log(1.0 + jnp.exp(-jnp.abs(x))))


def _ret_core_kernel(*refs, seq_len, n_seq, chunk, has_init, emit_state, n_alias):
    it = iter(refs)
    decay_ref = next(it)
    q_ref = next(it)
    k_ref = next(it)
    v_ref = next(it)
    g_ref = next(it)
    s0_ref = next(it) if has_init else None
    for _ in range(n_alias):
        next(it)
    o_ref = next(it)
    st_ref = next(it) if emit_state else None
    acc_ref = next(it)
    kv_ref = next(it)

    c = chunk
    nc = seq_len // c
    n_chunks = n_seq * nc
    unroll = min(RET_UNROLL, n_chunks)
    head = pl.program_id(1)

    lg_all = _log_sigmoid(decay_ref[...])
    pick = lax.broadcasted_iota(jnp.int32, lg_all.shape, 1) == head
    lg = jnp.sum(jnp.where(pick, lg_all, 0.0), axis=1, keepdims=True)
    lg_f = lg[0:1, :]
    lg_b = lg[1:2, :]

    ri = lax.broadcasted_iota(jnp.int32, (c, c), 0).astype(F32)
    ci = lax.broadcasted_iota(jnp.int32, (c, c), 1).astype(F32)
    diff = ri - ci
    decay = (jnp.where(diff >= 0, jnp.exp(jnp.where(diff >= 0, diff, 0.0) * lg_f), 0.0)
             + jnp.where(diff <= 0, jnp.exp(jnp.where(diff <= 0, -diff, 0.0) * lg_b), 0.0))
    pos_col = lax.broadcasted_iota(jnp.int32, (c, 1), 0).astype(F32)
    pos_row = lax.broadcasted_iota(jnp.int32, (1, c), 1).astype(F32)
    xi_f = jnp.exp((pos_col + 1.0) * lg_f)
    xi_b = jnp.exp((c - pos_col) * lg_b)
    zeta_f = jnp.exp((c - 1.0 - pos_row) * lg_f)
    zeta_b = jnp.exp(pos_row * lg_b)
    cd_f = jnp.exp(c * lg_f)
    cd_b = jnp.exp(c * lg_b)

    def intra(n):
        r0 = pl.multiple_of(n * c, c)
        qn = q_ref[pl.ds(r0, c), :]
        kn = k_ref[pl.ds(r0, c), :]
        vn = v_ref[pl.ds(r0, c), :]
        s = lax.dot_general(qn, kn, NT_DIMS, preferred_element_type=F32) * decay
        acc_ref[pl.ds(r0, c), :] = jnp.dot(s.astype(BF16), vn, preferred_element_type=F32)
        kt = kn.astype(F32).T
        kz = jnp.concatenate([(kt * zeta_f).astype(BF16), (kt * zeta_b).astype(BF16)], axis=0)
        kv_ref[n] = jnp.dot(kz, vn, preferred_element_type=F32)

    def cross(n):
        r0 = pl.multiple_of(n * c, c)
        qn = q_ref[pl.ds(r0, c), :].astype(F32)
        qx = jnp.concatenate([(qn * xi_f).astype(BF16), (qn * xi_b).astype(BF16)], axis=1)
        o = acc_ref[pl.ds(r0, c), :] + jnp.dot(qx, kv_ref[n].astype(BF16), preferred_element_type=F32)
        mu = jnp.mean(o, axis=-1, keepdims=True)
        var = jnp.mean(jnp.square(o - mu), axis=-1, keepdims=True)
        on = (o - mu) * lax.rsqrt(var + EPS)
        gate = g_ref[pl.ds(r0, c), :].astype(F32)
        o_ref[pl.ds(r0, c), :] = (_silu(gate) * on).astype(BF16)

    def over_chunks(fn):
        def body(step, carry):
            for u in range(unroll):
                fn(step * unroll + u)
            return carry

        lax.fori_loop(0, n_chunks // unroll, body, 0)

    over_chunks(intra)

    for s in range(n_seq):
        if has_init:
            init_f = s0_ref[s, 0, 0, 0]
            init_b = s0_ref[s, 0, 1, 0]
        else:
            init_f = jnp.zeros((RET_DK, RET_DV), F32)
            init_b = init_f

        def fwd(n, state, s=s):
            kv = kv_ref[s * nc + n, 0:RET_DK, :]
            kv_ref[s * nc + n, 0:RET_DK, :] = state
            return cd_f * state + kv

        def bwd(n, state, s=s):
            m = s * nc + nc - 1 - n
            kv = kv_ref[m, RET_DK:, :]
            kv_ref[m, RET_DK:, :] = state
            return cd_b * state + kv

        final_f = lax.fori_loop(0, nc, fwd, init_f)
        final_b = lax.fori_loop(0, nc, bwd, init_b)
        if emit_state:
            st_ref[s, 0, 0, 0] = final_f
            st_ref[s, 0, 1, 0] = final_b

    over_chunks(cross)


RET_UNROLL = 16
RET_CTX_SEQS = 8


def _ret_core(proj, ret_decay_j, state_ret, layer_j, *, latent, new_state=None):
    if latent:
        nb, seq_len, n_seq, row0 = DEC_BATCH, DEC_SEQ, 1, CTX_ROWS // DEC_SEQ
    else:
        nb, seq_len, n_seq, row0 = BATCH // RET_CTX_SEQS, SEQ, RET_CTX_SEQS, 0
    rows = n_seq * seq_len
    chunk = min(RET_CHUNK, seq_len)
    kcol = RET_HEADS * RET_DK // RET_DK
    vcol = 2 * RET_HEADS * RET_DK // RET_DV
    gcol = vcol + RET_HEADS
    in_specs = [
        pl.BlockSpec((2, RET_HEADS), lambda b, h: (0, 0)),
        pl.BlockSpec((rows, RET_DK), lambda b, h: (row0 + b, h)),
        pl.BlockSpec((rows, RET_DK), lambda b, h: (row0 + b, kcol + h)),
        pl.BlockSpec((rows, RET_DV), lambda b, h: (row0 + b, vcol + h)),
        pl.BlockSpec((rows, RET_DV), lambda b, h: (row0 + b, gcol + h)),
    ]
    args = [ret_decay_j, proj, proj, proj, proj]
    if latent:
        in_specs.append(pl.BlockSpec((1, 1, 2, 1, RET_DK, RET_DV), lambda b, h: (b, layer_j, 0, h, 0, 0)))
        args.append(state_ret)
    aliases = {}
    if new_state is not None:
        aliases = {len(args): 1}
        in_specs.append(pl.BlockSpec(memory_space=pl.ANY))
        args.append(new_state)
    out_specs = [pl.BlockSpec((rows, RET_DV), lambda b, h: (b, h))]
    out_shape = [jax.ShapeDtypeStruct((nb * rows, RET_HEADS * RET_DV), BF16)]
    if not latent:
        out_specs.append(pl.BlockSpec((n_seq, 1, 2, 1, RET_DK, RET_DV), lambda b, h: (b, layer_j, 0, h, 0, 0)))
        out_shape.append(jax.ShapeDtypeStruct((BATCH, (DEPTH + 1) // 2, 2, RET_HEADS, RET_DK, RET_DV), F32))
    return pl.pallas_call(
        functools.partial(_ret_core_kernel, seq_len=seq_len, n_seq=n_seq, chunk=chunk, has_init=latent,
                          emit_state=not latent, n_alias=len(aliases)),
        grid=(nb, RET_HEADS),
        in_specs=in_specs,
        out_specs=out_specs,
        out_shape=out_shape,
        input_output_aliases=aliases,
        scratch_shapes=[
            pltpu.VMEM((rows, RET_DV), F32),
            pltpu.VMEM((rows // chunk, 2 * RET_DK, RET_DV), F32),
        ],
        compiler_params=_params(2),
        name="ret_core_lat" if latent else "ret_core_ctx",
    )(*args)


ATT_Q_W = ATT_HEADS * HEAD_DIM
ATT_KV_W = ATT_KV_HEADS * HEAD_DIM


def _group_mean_sq(xc, gmat):
    sq = xc * xc
    hi = sq.astype(BF16)
    lo = (sq - hi.astype(F32)).astype(BF16)
    return jnp.dot(hi, gmat, preferred_element_type=F32) + jnp.dot(lo, gmat, preferred_element_type=F32)


SEQS_PER_TILE = TM // SEQ


def _att_proj_kernel(*refs, n_alias):
    n_src = 1 + N_WINDOWS + 3
    (gain_ref, sh_ref, sc_ref, w_ref, qg_ref, kg_ref, cos_ref, sin_ref) = refs[n_src:n_src + 8]
    o_ref, xo_ref, kc_ref, vc_ref, wbf_ref, kv_ref, buf_ref = refs[n_src + 8 + n_alias:]
    i = pl.program_id(0)

    @pl.when(i == 0)
    def _():
        wbf_ref[...] = w_ref[0].astype(BF16)

    x = _combine_tile(i, refs[0], refs[1:1 + N_WINDOWS], *refs[1 + N_WINDOWS:n_src], buf_ref)
    xo_ref[...] = x
    h = _norm_mod(x, gain_ref[...], sh_ref[0], sc_ref[0]).astype(BF16)
    acc = jnp.dot(h, wbf_ref[...], preferred_element_type=F32)

    r = lax.broadcasted_iota(jnp.int32, (LANES, LANES), 0) // HEAD_DIM
    c = lax.broadcasted_iota(jnp.int32, (LANES, LANES), 1) // HEAD_DIM
    gmat = jnp.where(r == c, 1.0 / HEAD_DIM, 0.0).astype(BF16)
    cos = cos_ref[...]
    sin = sin_ref[...]
    n_q = ATT_Q_W // LANES
    n_kv = ATT_KV_W // LANES
    for cidx in range(n_q + n_kv):
        xc = acc[:, cidx * LANES:(cidx + 1) * LANES]
        gain = qg_ref[...] if cidx < n_q else kg_ref[...]
        normed = xc * lax.rsqrt(_group_mean_sq(xc, gmat) + EPS) * gain
        if cidx >= n_q:
            kv_ref[:, (cidx - n_q) * LANES:(cidx - n_q + 1) * LANES] = normed
        o_ref[:, cidx * LANES:(cidx + 1) * LANES] = _rope_chunk(normed, cos, sin, HEAD_DIM // 4).astype(BF16)
    v = acc[:, ATT_Q_W + ATT_KV_W:]
    kv_ref[:, ATT_KV_W:] = v
    o_ref[:, ATT_Q_W + ATT_KV_W:] = v.astype(BF16)

    @pl.when(i < N_CTX_TILES)
    def _():
        for s in range(SEQS_PER_TILE):
            for hd in range(ATT_KV_HEADS):
                rows = slice(s * SEQ, (s + 1) * SEQ)
                kc_ref[s, 0, hd] = kv_ref[rows, hd * HEAD_DIM:(hd + 1) * HEAD_DIM]
                vc_ref[s, 0, hd] = kv_ref[rows, ATT_KV_W + hd * HEAD_DIM:ATT_KV_W + (hd + 1) * HEAD_DIM]


def _att_proj(pending, gain, shift, scale, w_in, layer_j, q_gain, k_gain, cos, sin, caches):
    n_alias = len(caches)
    cache_shape = caches[0].shape
    cache_spec = pl.BlockSpec((SEQS_PER_TILE, 1, ATT_KV_HEADS, SEQ, HEAD_DIM),
                              lambda i, *_: (jnp.minimum(i, N_CTX_TILES - 1), layer_j, 0, 0, 0))
    src_specs, src_args, scratch = _pending_specs(pending)
    in_specs = src_specs + [
        pl.BlockSpec((1, D_MODEL), lambda i, *_: (0, 0)),
        pl.BlockSpec((1, 1, D_MODEL), lambda i, *_: (_cond_row(i), 0, 0)),
        pl.BlockSpec((1, 1, D_MODEL), lambda i, *_: (_cond_row(i), 0, 0)),
        pl.BlockSpec((1, D_MODEL, ATT_IN), lambda i, *_: (layer_j, 0, 0)),
        pl.BlockSpec((1, LANES), lambda i, *_: (0, 0)),
        pl.BlockSpec((1, LANES), lambda i, *_: (0, 0)),
        pl.BlockSpec((TM, LANES), lambda i, *_: (_rope_row(i), 0)),
        pl.BlockSpec((TM, LANES), lambda i, *_: (_rope_row(i), 0)),
    ] + [pl.BlockSpec(memory_space=pl.ANY)] * n_alias
    args = src_args + [gain, shift, scale, w_in, q_gain, k_gain, cos, sin] + list(caches)
    first_cache = len(args) - n_alias
    return pl.pallas_call(
        functools.partial(_att_proj_kernel, n_alias=n_alias),
        grid_spec=pltpu.PrefetchScalarGridSpec(
            num_scalar_prefetch=1,
            grid=(T_ROWS // TM,),
            in_specs=in_specs,
            out_specs=[pl.BlockSpec((TM, ATT_IN), lambda i, *_: (i, 0)),
                       pl.BlockSpec((TM, D_MODEL), lambda i, *_: (i, 0)), cache_spec, cache_spec],
            scratch_shapes=[pltpu.VMEM((D_MODEL, ATT_IN), BF16), pltpu.VMEM((TM, 2 * ATT_KV_W), F32)] + scratch,
        ),
        out_shape=[
            jax.ShapeDtypeStruct((T_ROWS, ATT_IN), BF16),
            jax.ShapeDtypeStruct((T_ROWS, D_MODEL), F32),
            jax.ShapeDtypeStruct(cache_shape, F32),
            jax.ShapeDtypeStruct(cache_shape, F32),
        ],
        input_output_aliases={first_cache: 2, first_cache + 1: 3},
        compiler_params=_params(1),
        name="att_proj",
    )(*args)


SINK_ROWS = 16
TN_DIMS = (((0,), (0,)), ((), ()))


def _ones_column(n):
    lane = lax.broadcasted_iota(jnp.int32, (n, HEAD_DIM), 1)
    return jnp.where(lane == 0, 1.0, 0.0).astype(BF16)


def _sink_softmax_pv(qs, sink_row, blocks):
    r = qs.shape[0]
    q_scaled = qs * jnp.asarray(ATT_SCALE, BF16)
    scores = []
    values = []
    for k, v, bias in blocks:
        s = lax.dot_general(k, q_scaled, NT_DIMS, preferred_element_type=F32)
        scores.append(s if bias is None else s + bias)
        values.append(jnp.concatenate([v, _ones_column(v.shape[0])], axis=1))
    row = lax.broadcasted_iota(jnp.int32, (SINK_ROWS, r), 0)
    scores.append(jnp.where(row == 0, sink_row, -jnp.inf))
    values.append(jnp.concatenate([jnp.zeros((SINK_ROWS, HEAD_DIM), BF16), _ones_column(SINK_ROWS)], axis=1))
    st = jnp.concatenate(scores, axis=0)
    m = jnp.max(st, axis=0, keepdims=True)
    pt = jnp.exp(st - m).astype(BF16)
    ov = lax.dot_general(pt, jnp.concatenate(values, axis=0), TN_DIMS, preferred_element_type=F32)
    return ov[:, :HEAD_DIM] / ov[:, HEAD_DIM:HEAD_DIM + 1]


def _stack_heads(q, kvh, rows):
    parts = []
    for g in range(ATT_GROUP):
        hd = kvh * ATT_GROUP + g
        parts.append(q[:, hd * HEAD_DIM:(hd + 1) * HEAD_DIM])
    return jnp.concatenate(parts, axis=0)


def _sink_row(sink_ref, kvh, rows):
    head = lax.broadcasted_iota(jnp.int32, (1, ATT_GROUP * rows), 1) // rows
    out = jnp.full((1, ATT_GROUP * rows), sink_ref[kvh * ATT_GROUP], F32)
    for g in range(1, ATT_GROUP):
        out = jnp.where(head == g, sink_ref[kvh * ATT_GROUP + g], out)
    return out


def _store_heads(o_ref, o, kvh, rows):
    for g in range(ATT_GROUP):
        hd = kvh * ATT_GROUP + g
        o_ref[:, hd * HEAD_DIM:(hd + 1) * HEAD_DIM] = o[g * rows:(g + 1) * rows, :].astype(BF16)


CTX_ATT_SEQS = 1


def _ctx_att_kernel(sink_ref, q_ref, k_ref, v_ref, o_ref):
    for s in range(CTX_ATT_SEQS):
        seq = slice(s * SEQ, (s + 1) * SEQ)
        q = q_ref[seq, :]
        for kvh in range(ATT_KV_HEADS):
            cols = slice(kvh * HEAD_DIM, (kvh + 1) * HEAD_DIM)
            qs = _stack_heads(q, kvh, SEQ)
            o = _sink_softmax_pv(qs, _sink_row(sink_ref, kvh, SEQ), [(k_ref[seq, cols], v_ref[seq, cols], None)])
            _store_heads(o_ref.at[seq, :], o, kvh, SEQ)


LAT_Q_BLOCKS = 4


def _lat_att_kernel(sink_ref, q_ref, k_ref, v_ref, ck_ref, cv_ref, o_ref):
    nb = DEC_SEQ // ATT_BLOCK
    blk = ATT_BLOCK
    rows = ATT_GROUP * blk
    kj = lax.broadcasted_iota(jnp.int32, (3 * blk, rows), 0) - blk
    qi = lax.broadcasted_iota(jnp.int32, (3 * blk, rows), 1) % blk
    in_window = jnp.abs(qi - kj) <= WINDOW
    for sub in range(LAT_Q_BLOCKS):
        n = pl.program_id(1) * LAT_Q_BLOCKS + sub
        prev0 = pl.multiple_of(jnp.maximum(n - 1, 0) * blk, blk)
        cur0 = pl.multiple_of(n * blk, blk)
        next0 = pl.multiple_of(jnp.minimum(n + 1, nb - 1) * blk, blk)
        kabs = n * blk + kj
        bias = jnp.where(in_window & (kabs >= 0) & (kabs < DEC_SEQ), 0.0, -jnp.inf)
        q = q_ref[sub * blk:(sub + 1) * blk, :]
        for kvh in range(ATT_KV_HEADS):
            cols = slice(kvh * HEAD_DIM, (kvh + 1) * HEAD_DIM)
            qs = _stack_heads(q, kvh, blk)
            k_loc = jnp.concatenate([k_ref[pl.ds(prev0, blk), cols], k_ref[pl.ds(cur0, blk), cols],
                                     k_ref[pl.ds(next0, blk), cols]], axis=0)
            v_loc = jnp.concatenate([v_ref[pl.ds(prev0, blk), cols], v_ref[pl.ds(cur0, blk), cols],
                                     v_ref[pl.ds(next0, blk), cols]], axis=0)
            k_ctx = ck_ref[0, 0, kvh].astype(BF16)
            v_ctx = cv_ref[0, 0, kvh].astype(BF16)
            o = _sink_softmax_pv(qs, _sink_row(sink_ref, kvh, blk),
                                 [(k_loc, v_loc, bias), (k_ctx, v_ctx, None)])
            _store_heads(o_ref.at[sub * blk:(sub + 1) * blk, :], o, kvh, blk)


def _ctx_attention(qkv, sink):
    kcol = ATT_Q_W // ATT_KV_W
    rows = CTX_ATT_SEQS * SEQ
    return pl.pallas_call(
        _ctx_att_kernel,
        grid_spec=pltpu.PrefetchScalarGridSpec(
            num_scalar_prefetch=1,
            grid=(BATCH // CTX_ATT_SEQS,),
            in_specs=[
                pl.BlockSpec((rows, ATT_Q_W), lambda b, s: (b, 0)),
                pl.BlockSpec((rows, ATT_KV_W), lambda b, s: (b, kcol)),
                pl.BlockSpec((rows, ATT_KV_W), lambda b, s: (b, kcol + 1)),
            ],
            out_specs=pl.BlockSpec((rows, ATT_Q_W), lambda b, s: (b, 0)),
        ),
        out_shape=jax.ShapeDtypeStruct((CTX_ROWS, ATT_Q_W), BF16),
        compiler_params=_params(1),
        name="ctx_attention",
    )(sink, qkv, qkv, qkv)


def _lat_attention(qkv, sink, cache_k, cache_v, layer_j):
    kcol = ATT_Q_W // ATT_KV_W
    qrows = LAT_Q_BLOCKS * ATT_BLOCK
    nb = DEC_SEQ // qrows
    q0 = CTX_ROWS // qrows
    s0 = CTX_ROWS // DEC_SEQ
    cache_spec = pl.BlockSpec((1, 1, ATT_KV_HEADS, PAST_LEN, HEAD_DIM), lambda b, n, s: (b, layer_j, 0, 0, 0))
    return pl.pallas_call(
        _lat_att_kernel,
        grid_spec=pltpu.PrefetchScalarGridSpec(
            num_scalar_prefetch=1,
            grid=(DEC_BATCH, nb),
            in_specs=[
                pl.BlockSpec((qrows, ATT_Q_W), lambda b, n, s: (q0 + b * nb + n, 0)),
                pl.BlockSpec((DEC_SEQ, ATT_KV_W), lambda b, n, s: (s0 + b, kcol)),
                pl.BlockSpec((DEC_SEQ, ATT_KV_W), lambda b, n, s: (s0 + b, kcol + 1)),
                cache_spec,
                cache_spec,
            ],
            out_specs=pl.BlockSpec((qrows, ATT_Q_W), lambda b, n, s: (b * nb + n, 0)),
        ),
        out_shape=jax.ShapeDtypeStruct((LAT_ROWS, ATT_Q_W), BF16),
        compiler_params=_params(2),
        name="lat_attention",
    )(sink, qkv, qkv, qkv, cache_k, cache_v)


ROW_ALIGN = 16
S_SLOTS = 576
BUF_ROWS = S_SLOTS + TM_MOE
XW = D_MODEL + LANES
SLOT_LANE = 6
GATE_LANE0 = 8
REGION_TILES = 28
REGION_ROWS = REGION_TILES * TM_MOE
TAB_W = 16
HALF_WIN = TM_MOE // 2
N_TOK_TILES = T_ROWS // TM


def _router_kernel(ac_ref, al_ref, wo_ref, x_ref, ga_ref, gain_ref, sh_ref, sc_ref, wr_ref, bias_ref,
                   xo_ref, xs_hbm, row_ref, tab_ref, wl_ref, wobf_ref, buf_ref, off_ref, wide_ref, sem):
    i = pl.program_id(0)
    ng = N_GROUPS
    last = pl.num_programs(0) - 1

    @pl.when(i == 0)
    def _():
        wobf_ref[...] = wo_ref[0].astype(BF16)

    mixed = jnp.where(i < N_CTX_TILES, ac_ref[...], al_ref[...])
    x_new = x_ref[...] + ga_ref[0] * jnp.dot(mixed, wobf_ref[...], preferred_element_type=F32)
    xo_ref[...] = x_new

    @pl.when(i == 0)
    def _():
        buf_ref[...] = jnp.zeros_like(buf_ref)
        for g in range(ng):
            off_ref[g] = 0

        def clear(r, carry):
            for c in range(TAB_W):
                tab_ref[r * TAB_W + c] = 0
            return carry

        lax.fori_loop(0, N_TOK_TILES, clear, 0)

    hf = _norm_mod(x_new, gain_ref[...], sh_ref[0], sc_ref[0])
    hb = hf.astype(BF16)
    hl = (hf - hb.astype(F32)).astype(BF16)
    both = jnp.dot(hb, wr_ref[...], preferred_element_type=F32)
    logits = (both[:, :LANES] + both[:, LANES:]
              + jnp.dot(hl, wr_ref[:, :LANES], preferred_element_type=F32))
    lt = logits.T[0:N_EXPERTS, :]
    scores = jax.nn.sigmoid(lt)
    sel = scores + bias_ref[...]
    xs = [sel[k * ng:(k + 1) * ng, :] for k in range(EXPERTS_PER_GROUP)]
    sc = [scores[k * ng:(k + 1) * ng, :] for k in range(EXPERTS_PER_GROUP)]
    a, b, c, d = xs
    gs = jnp.maximum(jnp.maximum(jnp.maximum(a + b, a + c), jnp.maximum(a + d, b + c)),
                     jnp.maximum(b + d, c + d))
    bv = gs[0:1, :]
    bg = jnp.zeros(bv.shape, jnp.int32)
    for g in range(1, ng):
        better = gs[g:g + 1, :] > bv
        bg = jnp.where(better, g, bg)
        bv = jnp.where(better, gs[g:g + 1, :], bv)
    giota = lax.broadcasted_iota(jnp.int32, (ng, TM), 0)
    onehot = giota == bg
    wk = []
    for k in range(EXPERTS_PER_GROUP):
        rank = jnp.zeros((ng, TM), F32)
        for j in range(EXPERTS_PER_GROUP):
            if j < k:
                rank = rank + (xs[j] >= xs[k]).astype(F32)
            elif j > k:
                rank = rank + (xs[j] > xs[k]).astype(F32)
        chosen = (rank < 2.0) & onehot
        wk.append(jnp.sum(jnp.where(chosen, sc[k], 0.0), axis=0, keepdims=True))
    den = wk[0] + wk[1] + wk[2] + wk[3]
    gates = [w / den for w in wk]

    oh = jnp.where(onehot, 1.0, 0.0)
    r = lax.broadcasted_iota(jnp.int32, (TM, TM), 0)
    cc = lax.broadcasted_iota(jnp.int32, (TM, TM), 1)
    tri = jnp.where(r < cc, 1.0, 0.0).astype(BF16)
    before = jnp.dot(oh.astype(BF16), tri, preferred_element_type=F32)
    rank_local = jnp.sum(jnp.where(onehot, before, 0.0), axis=0, keepdims=True).astype(jnp.int32)
    counts = [jnp.sum(jnp.where(bg == g, 1, 0)) for g in range(ng)]
    pads = [((cnt + ROW_ALIGN - 1) // ROW_ALIGN) * ROW_ALIGN for cnt in counts]
    segs = [0]
    for g in range(1, ng):
        segs.append(segs[-1] + pads[g - 1])
    total = segs[-1] + pads[-1]
    seg_of = jnp.zeros(bg.shape, jnp.int32)
    for g in range(1, ng):
        seg_of = jnp.where(bg == g, segs[g], seg_of)
    slot = seg_of + rank_local

    parts = []
    for gt in gates:
        hi = gt.astype(BF16).astype(F32)
        rest = gt - hi
        mid = rest.astype(BF16).astype(F32)
        parts += [hi, mid, (rest - mid).astype(BF16).astype(F32)]
    zero_row = jnp.zeros((1, TM), F32)
    info = jnp.concatenate(gates + [bg.astype(F32), zero_row, slot.astype(F32), zero_row] + parts
                           + [jnp.zeros((LANES - GATE_LANE0 - len(parts), TM), F32)], axis=0)
    rowinfo = info.T
    row_ref[...] = rowinfo

    h_ext = jnp.concatenate([hb, rowinfo.astype(BF16)], axis=1)
    srow = lax.broadcasted_iota(jnp.int32, (S_SLOTS, TM), 0)
    pick = jnp.where(srow == slot, 1.0, 0.0).astype(BF16)
    par = i % 2
    buf_ref[par, 0:S_SLOTS, :] = jnp.dot(pick, h_ext, preferred_element_type=F32).astype(BF16)

    def window_copy(g, src_row, dst_row, rows):
        return pltpu.make_async_copy(
            buf_ref.at[par, pl.ds(pl.multiple_of(src_row, ROW_ALIGN), rows), :],
            xs_hbm.at[pl.ds(pl.multiple_of(dst_row, ROW_ALIGN), rows), :],
            sem.at[g])

    def wait_windows(rows):
        for g in range(ng):
            window_copy(g, 0, 0, rows).wait()

    def wait_previous():
        @pl.when(wide_ref[0] == 1)
        def _():
            wait_windows(TM_MOE)

        @pl.when(wide_ref[0] == 0)
        def _():
            wait_windows(HALF_WIN)

    @pl.when(i > 0)
    def _():
        wait_previous()

    wide = jnp.maximum(jnp.maximum(pads[0], pads[1]), jnp.maximum(pads[2], pads[3])) > HALF_WIN
    offs = [off_ref[g] for g in range(ng)]

    def issue(rows):
        for g in range(ng):
            window_copy(g, segs[g], g * REGION_ROWS + offs[g], rows).start()

    @pl.when(wide)
    def _():
        issue(TM_MOE)

    @pl.when(jnp.logical_not(wide))
    def _():
        issue(HALF_WIN)

    wide_ref[0] = wide.astype(jnp.int32)
    for g in range(ng):
        tab_ref[i * TAB_W + g] = segs[g]
        tab_ref[i * TAB_W + ng + g] = offs[g]
        tab_ref[i * TAB_W + 2 * ng + 1 + g] = pads[g]
        off_ref[g] = offs[g] + pads[g]
    tab_ref[i * TAB_W + 2 * ng] = total

    @pl.when(i == last)
    def _():
        wait_previous()
        for g in range(ng):
            window_copy(g, S_SLOTS, g * REGION_ROWS + off_ref[g], TM_MOE).start()
        wait_windows(TM_MOE)

        entry = 0
        full = 0
        for g in range(ng):
            full = (off_ref[g] + TM_MOE - 1) // TM_MOE

            def add(j, e, g=g, full=full):
                wl_ref[e] = g * REGION_TILES + j
                wl_ref[N_MOE_TILES + e] = g
                wl_ref[2 * N_MOE_TILES + e] = (j < full).astype(jnp.int32)
                return e + 1

            entry = lax.fori_loop(0, full + 1, add, entry)

        def pad(e, carry, full=full):
            wl_ref[e] = (ng - 1) * REGION_TILES + full
            wl_ref[N_MOE_TILES + e] = ng - 1
            wl_ref[2 * N_MOE_TILES + e] = 0
            return carry

        lax.fori_loop(entry, N_MOE_TILES, pad, 0)


def _mix_out_router(a_ctx, a_lat, w_out, layer_j, x, gate_a, gain, shift, scale, wr_parts, bias_col):
    k = a_ctx.shape[1]
    return pl.pallas_call(
        _router_kernel,
        grid=(N_TOK_TILES,),
        in_specs=[
            pl.BlockSpec((TM, k), lambda i: (jnp.minimum(i, N_CTX_TILES - 1), 0)),
            pl.BlockSpec((TM, k), lambda i: (jnp.maximum(i - N_CTX_TILES, 0), 0)),
            pl.BlockSpec((1, k, D_MODEL), lambda i: (layer_j, 0, 0), pipeline_mode=pl.Buffered(1)),
            pl.BlockSpec((TM, D_MODEL), lambda i: (i, 0)),
            pl.BlockSpec((1, 1, D_MODEL), lambda i: (_cond_row(i), 0, 0)),
            pl.BlockSpec((1, D_MODEL), lambda i: (0, 0)),
            pl.BlockSpec((1, 1, D_MODEL), lambda i: (_cond_row(i), 0, 0)),
            pl.BlockSpec((1, 1, D_MODEL), lambda i: (_cond_row(i), 0, 0)),
            pl.BlockSpec((D_MODEL, 2 * LANES), lambda i: (0, 0)),
            pl.BlockSpec((N_EXPERTS, 1), lambda i: (0, 0)),
        ],
        out_specs=[
            pl.BlockSpec((TM, D_MODEL), lambda i: (i, 0)),
            pl.BlockSpec(memory_space=pl.ANY),
            pl.BlockSpec((TM, LANES), lambda i: (i, 0)),
            pl.BlockSpec(memory_space=pltpu.SMEM),
            pl.BlockSpec(memory_space=pltpu.SMEM),
        ],
        out_shape=[
            jax.ShapeDtypeStruct((T_ROWS, D_MODEL), F32),
            jax.ShapeDtypeStruct((N_GROUPS * REGION_ROWS, XW), BF16),
            jax.ShapeDtypeStruct((T_ROWS, LANES), F32),
            jax.ShapeDtypeStruct((N_TOK_TILES * TAB_W,), jnp.int32),
            jax.ShapeDtypeStruct((3 * N_MOE_TILES,), jnp.int32),
        ],
        scratch_shapes=[
            pltpu.VMEM((k, D_MODEL), BF16),
            pltpu.VMEM((2, BUF_ROWS, XW), BF16),
            pltpu.SMEM((N_GROUPS,), jnp.int32),
            pltpu.SMEM((1,), jnp.int32),
            pltpu.SemaphoreType.DMA((N_GROUPS,)),
        ],
        compiler_params=_params(1),
        name="mix_out_router",
    )(a_ctx, a_lat, w_out, x, gate_a, gain, shift, scale, wr_parts, bias_col)


UP_EXPERTS = 4


def _wl_block(wl, i):
    return wl[i]


def _wl_group(wl, i):
    return wl[N_MOE_TILES + i]


def _wl_valid(wl, i):
    return wl[2 * N_MOE_TILES + i]


def _moe_up_kernel(wl_ref, xs_ref, w_ref, o_ref, wbf_ref):
    kk = pl.program_id(0)
    i = pl.program_id(1)
    new_w = (i == 0) | (_wl_group(wl_ref, i) != _wl_group(wl_ref, jnp.maximum(i - 1, 0)))
    valid = _wl_valid(wl_ref, i)

    @pl.when(new_w)
    def _():
        wbf_ref[...] = w_ref[...].astype(BF16)

    @pl.when(valid == 1)
    def _():
        xs = xs_ref[:, :D_MODEL]
        extra = xs_ref[:, D_MODEL:].astype(F32)
        lane = lax.broadcasted_iota(jnp.int32, extra.shape, 1)
        for e in range(UP_EXPERTS):
            hu = jnp.dot(xs, wbf_ref[e], preferred_element_type=F32)
            lane0 = GATE_LANE0 + 3 * (kk * UP_EXPERTS + e)
            gcol = jnp.sum(jnp.where((lane >= lane0) & (lane < lane0 + 3), extra, 0.0), axis=1, keepdims=True)
            a = _silu(hu[:, :D_EXPERT]) * hu[:, D_EXPERT:] * gcol
            o_ref[:, e * D_EXPERT:(e + 1) * D_EXPERT] = a.astype(BF16)

    @pl.when(valid != 1)
    def _():
        o_ref[...] = jnp.zeros_like(o_ref)


def _moe_up(work_list, xs, w_up, layer):
    steps = EXPERTS_PER_GROUP // UP_EXPERTS
    return pl.pallas_call(
        _moe_up_kernel,
        grid_spec=pltpu.PrefetchScalarGridSpec(
            num_scalar_prefetch=1,
            grid=(steps, N_MOE_TILES),
            in_specs=[
                pl.BlockSpec((TM_MOE, XW), lambda k, i, wl: (_wl_block(wl, i), 0)),
                pl.BlockSpec((UP_EXPERTS, D_MODEL, 2 * D_EXPERT),
                             lambda k, i, wl: ((layer * N_GROUPS + _wl_group(wl, i)) * steps + k, 0, 0)),
            ],
            out_specs=pl.BlockSpec((TM_MOE, UP_EXPERTS * D_EXPERT), lambda k, i, wl: (_wl_block(wl, i), k)),
            scratch_shapes=[pltpu.VMEM((UP_EXPERTS, D_MODEL, 2 * D_EXPERT), BF16)],
        ),
        out_shape=jax.ShapeDtypeStruct((N_GROUPS * REGION_ROWS, EXPERTS_PER_GROUP * D_EXPERT), BF16),
        compiler_params=_params(2),
        name="moe_up",
    )(work_list, xs, w_up)


def _moe_down_kernel(wl_ref, a_ref, w_ref, o_ref, wbf_ref):
    i = pl.program_id(0)
    new_w = (i == 0) | (_wl_group(wl_ref, i) != _wl_group(wl_ref, jnp.maximum(i - 1, 0)))
    valid = _wl_valid(wl_ref, i)

    @pl.when(new_w)
    def _():
        wbf_ref[...] = w_ref[0].astype(BF16)

    @pl.when(valid == 1)
    def _():
        o_ref[...] = jnp.dot(a_ref[...], wbf_ref[...], preferred_element_type=F32).astype(BF16)

    @pl.when(valid != 1)
    def _():
        o_ref[...] = jnp.zeros_like(o_ref)


def _moe_down(work_list, a, w_down_grouped, layer):
    kdim = EXPERTS_PER_GROUP * D_EXPERT
    return pl.pallas_call(
        _moe_down_kernel,
        grid_spec=pltpu.PrefetchScalarGridSpec(
            num_scalar_prefetch=1,
            grid=(N_MOE_TILES,),
            in_specs=[
                pl.BlockSpec((TM_MOE, kdim), lambda i, wl: (_wl_block(wl, i), 0)),
                pl.BlockSpec((1, kdim, D_MODEL), lambda i, wl: (layer * N_GROUPS + _wl_group(wl, i), 0, 0)),
            ],
            out_specs=pl.BlockSpec((TM_MOE, D_MODEL), lambda i, wl: (_wl_block(wl, i), 0)),
            scratch_shapes=[pltpu.VMEM((kdim, D_MODEL), BF16)],
        ),
        out_shape=jax.ShapeDtypeStruct((N_GROUPS * REGION_ROWS, D_MODEL), BF16),
        compiler_params=_params(1),
        name="moe_down",
    )(work_list, a, w_down_grouped)


N_WINDOWS = 2 * N_GROUPS


class PendingMoe(NamedTuple):
    tab_flat: jax.Array
    ys: jax.Array
    rowinfo: jax.Array
    x: jax.Array
    gate: jax.Array


def _combine_tile(i, tab_ref, windows, row_ref, x_ref, g_ref, buf_ref):
    first = windows[:N_GROUPS]
    second = windows[N_GROUPS:]

    @pl.when(i == 0)
    def _():
        buf_ref[...] = jnp.zeros_like(buf_ref)

    for g in range(N_GROUPS):
        start = pl.multiple_of(tab_ref[i * TAB_W + g], ROW_ALIGN)
        buf_ref[pl.ds(start, HALF_WIN), :] = first[g][...]

        @pl.when(tab_ref[i * TAB_W + 2 * N_GROUPS + 1 + g] > HALF_WIN)
        def _(g=g, start=start):
            buf_ref[pl.ds(start + HALF_WIN, HALF_WIN), :] = second[g][...]

    slot = row_ref[:, SLOT_LANE:SLOT_LANE + 1].astype(jnp.int32)
    scol = lax.broadcasted_iota(jnp.int32, (TM, S_SLOTS), 1)
    pick = jnp.where(scol == slot, 1.0, 0.0).astype(BF16)
    y = jnp.dot(pick, buf_ref[0:S_SLOTS, :], preferred_element_type=F32)
    return x_ref[...] + g_ref[0] * y


def _pending_specs(p):
    def window_spec(g, second):
        def index(i, tab):
            off = tab[i * TAB_W + N_GROUPS + g]
            if second:
                off = jnp.where(tab[i * TAB_W + 2 * N_GROUPS + 1 + g] > HALF_WIN, off + HALF_WIN, 0)
            return pl.multiple_of(off + g * REGION_ROWS, ROW_ALIGN), 0

        return pl.BlockSpec((pl.Element(HALF_WIN), pl.Element(D_MODEL)), index)

    specs = [window_spec(g, False) for g in range(N_GROUPS)] + [window_spec(g, True) for g in range(N_GROUPS)]
    specs += [
        pl.BlockSpec((TM, LANES), lambda i, tab: (i, 0)),
        pl.BlockSpec((TM, D_MODEL), lambda i, tab: (i, 0)),
        pl.BlockSpec((1, 1, D_MODEL), lambda i, tab: (_cond_row(i), 0, 0)),
    ]
    args = [p.tab_flat] + [p.ys] * N_WINDOWS + [p.rowinfo, p.x, p.gate]
    return specs, args, [pltpu.VMEM((BUF_ROWS, D_MODEL), BF16)]


def _moe_combine_kernel(tab_ref, *refs):
    row_ref, x_ref, g_ref, ctx_ref, lat_ref, buf_ref = refs[N_WINDOWS:]
    i = pl.program_id(0)
    new_x = _combine_tile(i, tab_ref, refs[:N_WINDOWS], row_ref, x_ref, g_ref, buf_ref)

    @pl.when(i < N_CTX_TILES)
    def _():
        ctx_ref[...] = new_x

    @pl.when(i >= N_CTX_TILES)
    def _():
        lat_ref[...] = new_x


def _moe_combine(pending):
    specs, args, scratch = _pending_specs(pending)
    return pl.pallas_call(
        _moe_combine_kernel,
        grid_spec=pltpu.PrefetchScalarGridSpec(
            num_scalar_prefetch=1,
            grid=(N_TOK_TILES,),
            in_specs=specs,
            out_specs=[pl.BlockSpec((TM, D_MODEL), lambda i, tab: (jnp.minimum(i, N_CTX_TILES - 1), 0)),
                       pl.BlockSpec((TM, D_MODEL), lambda i, tab: (jnp.maximum(i - N_CTX_TILES, 0), 0))],
            scratch_shapes=scratch,
        ),
        out_shape=[jax.ShapeDtypeStruct((CTX_ROWS, D_MODEL), F32), jax.ShapeDtypeStruct((LAT_ROWS, D_MODEL), F32)],
        compiler_params=_params(1),
        name="moe_combine",
    )(*args)


def _mix_out_and_moe(mix, x, gate_a, gain, shift, scale, gate, router_w, w_up, w_down_grouped, layer):
    wr_parts, bias_col = router_w
    x, xs, rowinfo, tab, work_list = _mix_out_router(*mix, x, gate_a, gain, shift, scale, wr_parts, bias_col)
    a = _moe_up(work_list, xs, w_up, layer)
    ys = _moe_down(work_list, a, w_down_grouped, layer)
    return PendingMoe(tab, ys, rowinfo, x, gate)


def kernel(x_prompt, x_sample, state_ret, cache_k, cache_v, c, c_ctx, w_mod, b_mod, norm_mix,
           norm_moe, ret_w_in, ret_decay, ret_w_out, att_w_in, att_q_norm, att_k_norm, att_sink,
           att_w_out, w_router, router_bias, moe_w_up, moe_w_down):
    cond8 = jnp.zeros((8, D_MODEL), F32).at[0].set(c_ctx).at[1:N_COND].set(c)
    mods = _modulation_all(cond8, w_mod, b_mod)
    mods = mods[:, :N_COND].reshape(DEPTH, N_COND, N_MOD, 1, D_MODEL)

    ret_cos, ret_sin = _rope_tables(RET_DK)
    att_cos, att_sin = _rope_tables(HEAD_DIM)

    perm = jnp.arange(N_EXPERTS).reshape(N_GROUPS, EXPERTS_PER_GROUP).T.reshape(-1)
    wr = jnp.zeros((D_MODEL, LANES), F32).at[:, :N_EXPERTS].set(w_router[:, perm])
    wr_hi = wr.astype(BF16)
    wr_lo = (wr - wr_hi.astype(F32)).astype(BF16)
    bias_col = router_bias[perm].astype(F32).reshape(N_EXPERTS, 1)
    router_w = (jnp.concatenate([wr_hi, wr_lo], axis=1), bias_col)

    w_up_all = moe_w_up.reshape(DEPTH * N_EXPERTS, D_MODEL, 2 * D_EXPERT)
    w_down_all = moe_w_down.reshape(DEPTH * N_GROUPS, EXPERTS_PER_GROUP * D_EXPERT, D_MODEL)
    ret_w_in_bf16 = ret_w_in.astype(BF16)
    new_state = jnp.zeros((BATCH, (DEPTH + 1) // 2, 2, RET_HEADS, RET_DK, RET_DV), F32)
    cache_shape = (BATCH, DEPTH // 2, ATT_KV_HEADS, SEQ, HEAD_DIM)
    new_kv = (jnp.zeros(cache_shape, F32), jnp.zeros(cache_shape, F32))
    source = (x_prompt.reshape(CTX_ROWS, D_MODEL), x_sample.reshape(LAT_ROWS, D_MODEL))
    for layer in range(DEPTH):
        sh_a, sc_a, g_a, sh_m, sc_m, g_m = [mods[layer, :, t] for t in range(N_MOD)]
        gain_mix = norm_mix[layer].reshape(1, D_MODEL)
        gain_moe = norm_moe[layer].reshape(1, D_MODEL)
        j = layer // 2
        if layer % 2 == 0:
            proj, x = _ret_proj(source, gain_mix, sh_a, sc_a, ret_w_in_bf16, j, ret_cos, ret_sin)
            o_ctx, new_state = _ret_core(proj, ret_decay[j], state_ret, j, latent=False, new_state=new_state)
            (o_lat,) = _ret_core(proj, ret_decay[j], state_ret, j, latent=True)
            mix = (o_ctx, o_lat, ret_w_out, j)
        else:
            q_gain = jnp.tile(att_q_norm[j], LANES // HEAD_DIM).reshape(1, LANES)
            k_gain = jnp.tile(att_k_norm[j], LANES // HEAD_DIM).reshape(1, LANES)
            qkv, x, new_k, new_v = _att_proj(source, gain_mix, sh_a, sc_a, att_w_in, j, q_gain, k_gain,
                                             att_cos, att_sin, new_kv)
            new_kv = (new_k, new_v)
            sink = att_sink[j].astype(F32)
            o_ctx = _ctx_attention(qkv, sink)
            o_lat = _lat_attention(qkv, sink, cache_k, cache_v, j)
            mix = (o_ctx, o_lat, att_w_out, j)
        source = _mix_out_and_moe(mix, x, g_a, gain_moe, sh_m, sc_m, g_m, router_w, w_up_all, w_down_all, layer)

    x_ctx, x_lat = _moe_combine(source)
    y_prompt = x_ctx.reshape(BATCH, SEQ, D_MODEL)
    y_sample = x_lat.reshape(DEC_BATCH, DEC_SEQ, D_MODEL)
    return (y_prompt, y_sample, new_state, new_kv[0], new_kv[1])
```

```python
import functools
from typing import NamedTuple

import jax
import jax.numpy as jnp
import numpy as np
from jax import lax
from jax.experimental import pallas as pl
from jax.experimental.pallas import tpu as pltpu

F32 = jnp.float32
BF16 = jnp.bfloat16

D_MODEL = 1024
BATCH = 16
SEQ = 256
DEPTH = 4
DEC_BATCH = 2
DEC_SEQ = 4096
PAST_LEN = 512
GRID_W = 64
N_MOD = 6
EPS = 1e-6
ROPE_BASE = 10000.0
RET_HEADS = 8
RET_DK = 128
RET_DV = 256
RET_CHUNK = 256
RET_IN = 2 * RET_HEADS * RET_DK + 2 * RET_HEADS * RET_DV
ATT_HEADS = 16
ATT_KV_HEADS = 4
ATT_GROUP = 4
HEAD_DIM = 64
WINDOW = 128
ATT_BLOCK = 128
ATT_IN = (ATT_HEADS + 2 * ATT_KV_HEADS) * HEAD_DIM
ATT_SCALE = HEAD_DIM ** -0.5
N_EXPERTS = 16
N_GROUPS = 4
EXPERTS_PER_GROUP = 4
D_EXPERT = 512

CTX_ROWS = BATCH * SEQ
LAT_ROWS = DEC_BATCH * DEC_SEQ
T_ROWS = CTX_ROWS + LAT_ROWS
N_COND = 1 + DEC_BATCH

LANES = 128
TM = 512
N_CTX_TILES = CTX_ROWS // TM
N_LAT_TILES = DEC_SEQ // TM
TM_MOE = 512
N_MOE_TILES = 36
VMEM_LIMIT = 52 * 1024 * 1024

NT_DIMS = (((1,), (1,)), ((), ()))


def _params(n_axes, vmem=VMEM_LIMIT):
    return pltpu.CompilerParams(dimension_semantics=("arbitrary",) * n_axes, vmem_limit_bytes=vmem)


def _cond_row(i):
    return (i * TM) // DEC_SEQ


def _rope_row(i):
    return jnp.where(i < N_CTX_TILES, 0, 1 + (i - N_CTX_TILES) % N_LAT_TILES)


def _norm_mod(x, gain, shift, scale):
    ms = jnp.mean(x * x, axis=-1, keepdims=True)
    y = x * lax.rsqrt(ms + EPS) * gain
    return y * (1.0 + scale) + shift


def _silu(x):
    return x * jax.nn.sigmoid(x)


def _mod_kernel(c_ref, w_ref, b_ref, o_ref):
    c = c_ref[...]
    s = _silu(c).astype(BF16)
    o_ref[0] = jnp.dot(s, w_ref[0].astype(BF16), preferred_element_type=F32) + b_ref[0]


def _modulation_all(cond8, w_mod, b_mod):
    tn = 1536
    n = N_MOD * D_MODEL
    return pl.pallas_call(
        _mod_kernel,
        grid=(DEPTH, n // tn),
        in_specs=[
            pl.BlockSpec((8, D_MODEL), lambda l, j: (0, 0)),
            pl.BlockSpec((1, D_MODEL, tn), lambda l, j: (l, 0, j)),
            pl.BlockSpec((1, 1, tn), lambda l, j: (l, 0, j)),
        ],
        out_specs=pl.BlockSpec((1, 8, tn), lambda l, j: (l, 0, j)),
        out_shape=jax.ShapeDtypeStruct((DEPTH, 8, n), F32),
        compiler_params=_params(2),
        name="modulation",
    )(cond8, w_mod, b_mod.reshape(DEPTH, 1, n))


def _rope_tables(head_dim):
    half = head_dim // 2
    quarter = half // 2
    t = np.arange(DEC_SEQ)
    row = (t // GRID_W).astype(np.float64)
    col = (t % GRID_W).astype(np.float64)
    inv_freq = ROPE_BASE ** (-np.arange(quarter, dtype=np.float64) / quarter)
    lane = np.arange(LANES)
    d = lane % head_dim
    w = d % half
    f = w % quarter
    pos = np.where((d // half)[None, :] == 0, row[:, None], col[:, None])
    ang = pos * inv_freq[f][None, :]
    cos = np.cos(ang)
    sin = np.where((w < quarter)[None, :], -np.sin(ang), np.sin(ang))
    cos = np.concatenate([np.ones((TM, LANES)), cos], axis=0).astype(np.float32)
    sin = np.concatenate([np.zeros((TM, LANES)), sin], axis=0).astype(np.float32)
    return jnp.asarray(cos), jnp.asarray(sin)


def _rope_chunk(xc, cos, sin, quarter):
    lane = lax.broadcasted_iota(jnp.int32, xc.shape, 1)
    first = (lane % (2 * quarter)) < quarter
    partner = jnp.where(first, pltpu.roll(xc, LANES - quarter, 1), pltpu.roll(xc, quarter, 1))
    return xc * cos + partner * sin


RET_TN = 1024
RET_Q_TILES = RET_HEADS * RET_DK // RET_TN
RET_QK_TILES = 2 * RET_Q_TILES


def _ret_proj_kernel(*refs, from_moe):
    i = pl.program_id(0)
    if from_moe:
        n_src = 1 + N_WINDOWS + 3
        x = _combine_tile(i, refs[0], refs[1:1 + N_WINDOWS], *refs[1 + N_WINDOWS:n_src], refs[-1])
    else:
        n_src = 2
        x = jnp.where(i < N_CTX_TILES, refs[0][...], refs[1][...])
    gain_ref, sh_ref, sc_ref, w_ref, cos_ref, sin_ref, o_ref, xo_ref = refs[n_src:n_src + 8]
    xo_ref[...] = x
    h = _norm_mod(x, gain_ref[...], sh_ref[0], sc_ref[0]).astype(BF16)
    cos = cos_ref[...]
    sin = sin_ref[...]
    for j in range(RET_IN // RET_TN):
        cols = slice(j * RET_TN, (j + 1) * RET_TN)
        acc = jnp.dot(h, w_ref[0, :, cols], preferred_element_type=F32)
        if j < RET_QK_TILES:
            scale = 1.0 if j < RET_Q_TILES else RET_DK ** -0.5
            for c in range(RET_TN // LANES):
                xc = acc[:, c * LANES:(c + 1) * LANES]
                lanes = slice(j * RET_TN + c * LANES, j * RET_TN + (c + 1) * LANES)
                o_ref[:, lanes] = (_rope_chunk(xc, cos, sin, RET_DK // 4) * scale).astype(BF16)
        else:
            o_ref[:, cols] = acc.astype(BF16)


def _ret_proj(source, gain, shift, scale, w_in_bf16, layer_j, cos, sin):
    from_moe = isinstance(source, PendingMoe)
    if from_moe:
        src_specs, src_args, scratch = _pending_specs(source)
        n_prefetch = 1
    else:
        src_specs = [pl.BlockSpec((TM, D_MODEL), lambda i, *_: (jnp.minimum(i, N_CTX_TILES - 1), 0)),
                     pl.BlockSpec((TM, D_MODEL), lambda i, *_: (jnp.maximum(i - N_CTX_TILES, 0), 0))]
        src_args, scratch, n_prefetch = list(source), [], 0
    return pl.pallas_call(
        functools.partial(_ret_proj_kernel, from_moe=from_moe),
        grid_spec=pltpu.PrefetchScalarGridSpec(
            num_scalar_prefetch=n_prefetch,
            grid=(T_ROWS // TM,),
            in_specs=src_specs + [
                pl.BlockSpec((1, D_MODEL), lambda i, *_: (0, 0)),
                pl.BlockSpec((1, 1, D_MODEL), lambda i, *_: (_cond_row(i), 0, 0)),
                pl.BlockSpec((1, 1, D_MODEL), lambda i, *_: (_cond_row(i), 0, 0)),
                pl.BlockSpec((1, D_MODEL, RET_IN), lambda i, *_: (layer_j, 0, 0), pipeline_mode=pl.Buffered(1)),
                pl.BlockSpec((TM, LANES), lambda i, *_: (_rope_row(i), 0)),
                pl.BlockSpec((TM, LANES), lambda i, *_: (_rope_row(i), 0)),
            ],
            out_specs=[pl.BlockSpec((TM, RET_IN), lambda i, *_: (i, 0)),
                       pl.BlockSpec((TM, D_MODEL), lambda i, *_: (i, 0))],
            scratch_shapes=scratch,
        ),
        out_shape=[jax.ShapeDtypeStruct((T_ROWS, RET_IN), BF16), jax.ShapeDtypeStruct((T_ROWS, D_MODEL), F32)],
        compiler_params=_params(1),
        name="ret_proj",
    )(*src_args, gain, shift, scale, w_in_bf16, cos, sin)


def _log_sigmoid(x):
    return -(jnp.maximum(-x, 0.0) + jnp.log(1.0 + jnp.exp(-jnp.abs(x))))


def _ret_core_kernel(*refs, seq_len, n_seq, chunk, has_init, emit_state, n_alias):
    it = iter(refs)
    decay_ref = next(it)
    q_ref = next(it)
    k_ref = next(it)
    v_ref = next(it)
    g_ref = next(it)
    s0_ref = next(it) if has_init else None
    for _ in range(n_alias):
        next(it)
    o_ref = next(it)
    st_ref = next(it) if emit_state else None
    acc_ref = next(it)
    kv_ref = next(it)

    c = chunk
    nc = seq_len // c
    n_chunks = n_seq * nc
    unroll = min(RET_UNROLL, n_chunks)
    head = pl.program_id(1)

    lg_all = _log_sigmoid(decay_ref[...])
    pick = lax.broadcasted_iota(jnp.int32, lg_all.shape, 1) == head
    lg = jnp.sum(jnp.where(pick, lg_all, 0.0), axis=1, keepdims=True)
    lg_f = lg[0:1, :]
    lg_b = lg[1:2, :]

    ri = lax.broadcasted_iota(jnp.int32, (c, c), 0).astype(F32)
    ci = lax.broadcasted_iota(jnp.int32, (c, c), 1).astype(F32)
    diff = ri - ci
    decay = (jnp.where(diff >= 0, jnp.exp(jnp.where(diff >= 0, diff, 0.0) * lg_f), 0.0)
             + jnp.where(diff <= 0, jnp.exp(jnp.where(diff <= 0, -diff, 0.0) * lg_b), 0.0))
    pos_col = lax.broadcasted_iota(jnp.int32, (c, 1), 0).astype(F32)
    pos_row = lax.broadcasted_iota(jnp.int32, (1, c), 1).astype(F32)
    xi_f = jnp.exp((pos_col + 1.0) * lg_f)
    xi_b = jnp.exp((c - pos_col) * lg_b)
    zeta_f = jnp.exp((c - 1.0 - pos_row) * lg_f)
    zeta_b = jnp.exp(pos_row * lg_b)
    cd_f = jnp.exp(c * lg_f)
    cd_b = jnp.exp(c * lg_b)

    def intra(n):
        r0 = pl.multiple_of(n * c, c)
        qn = q_ref[pl.ds(r0, c), :]
        kn = k_ref[pl.ds(r0, c), :]
        vn = v_ref[pl.ds(r0, c), :]
        s = lax.dot_general(qn, kn, NT_DIMS, preferred_element_type=F32) * decay
        acc_ref[pl.ds(r0, c), :] = jnp.dot(s.astype(BF16), vn, preferred_element_type=F32)
        kt = kn.astype(F32).T
        kz = jnp.concatenate([(kt * zeta_f).astype(BF16), (kt * zeta_b).astype(BF16)], axis=0)
        kv_ref[n] = jnp.dot(kz, vn, preferred_element_type=F32)

    def cross(n):
        r0 = pl.multiple_of(n * c, c)
        qn = q_ref[pl.ds(r0, c), :].astype(F32)
        qx = jnp.concatenate([(qn * xi_f).astype(BF16), (qn * xi_b).astype(BF16)], axis=1)
        o = acc_ref[pl.ds(r0, c), :] + jnp.dot(qx, kv_ref[n].astype(BF16), preferred_element_type=F32)
        mu = jnp.mean(o, axis=-1, keepdims=True)
        var = jnp.mean(jnp.square(o - mu), axis=-1, keepdims=True)
        on = (o - mu) * lax.rsqrt(var + EPS)
        gate = g_ref[pl.ds(r0, c), :].astype(F32)
        o_ref[pl.ds(r0, c), :] = (_silu(gate) * on).astype(BF16)

    def over_chunks(fn):
        def body(step, carry):
            for u in range(unroll):
                fn(step * unroll + u)
            return carry

        lax.fori_loop(0, n_chunks // unroll, body, 0)

    over_chunks(intra)

    for s in range(n_seq):
        if has_init:
            init_f = s0_ref[s, 0, 0, 0]
            init_b = s0_ref[s, 0, 1, 0]
        else:
            init_f = jnp.zeros((RET_DK, RET_DV), F32)
            init_b = init_f

        def fwd(n, state, s=s):
            kv = kv_ref[s * nc + n, 0:RET_DK, :]
            kv_ref[s * nc + n, 0:RET_DK, :] = state
            return cd_f * state + kv

        def bwd(n, state, s=s):
            m = s * nc + nc - 1 - n
            kv = kv_ref[m, RET_DK:, :]
            kv_ref[m, RET_DK:, :] = state
            return cd_b * state + kv

        final_f = lax.fori_loop(0, nc, fwd, init_f)
        final_b = lax.fori_loop(0, nc, bwd, init_b)
        if emit_state:
            st_ref[s, 0, 0, 0] = final_f
            st_ref[s, 0, 1, 0] = final_b

    over_chunks(cross)


RET_UNROLL = 16
RET_CTX_SEQS = 8


def _ret_core(proj, ret_decay_j, state_ret, layer_j, *, latent, new_state=None):
    if latent:
        nb, seq_len, n_seq, row0 = DEC_BATCH, DEC_SEQ, 1, CTX_ROWS // DEC_SEQ
    else:
        nb, seq_len, n_seq, row0 = BATCH // RET_CTX_SEQS, SEQ, RET_CTX_SEQS, 0
    rows = n_seq * seq_len
    chunk = min(RET_CHUNK, seq_len)
    kcol = RET_HEADS * RET_DK // RET_DK
    vcol = 2 * RET_HEADS * RET_DK // RET_DV
    gcol = vcol + RET_HEADS
    in_specs = [
        pl.BlockSpec((2, RET_HEADS), lambda b, h: (0, 0)),
        pl.BlockSpec((rows, RET_DK), lambda b, h: (row0 + b, h)),
        pl.BlockSpec((rows, RET_DK), lambda b, h: (row0 + b, kcol + h)),
        pl.BlockSpec((rows, RET_DV), lambda b, h: (row0 + b, vcol + h)),
        pl.BlockSpec((rows, RET_DV), lambda b, h: (row0 + b, gcol + h)),
    ]
    args = [ret_decay_j, proj, proj, proj, proj]
    if latent:
        in_specs.append(pl.BlockSpec((1, 1, 2, 1, RET_DK, RET_DV), lambda b, h: (b, layer_j, 0, h, 0, 0)))
        args.append(state_ret)
    aliases = {}
    if new_state is not None:
        aliases = {len(args): 1}
        in_specs.append(pl.BlockSpec(memory_space=pl.ANY))
        args.append(new_state)
    out_specs = [pl.BlockSpec((rows, RET_DV), lambda b, h: (b, h))]
    out_shape = [jax.ShapeDtypeStruct((nb * rows, RET_HEADS * RET_DV), BF16)]
    if not latent:
        out_specs.append(pl.BlockSpec((n_seq, 1, 2, 1, RET_DK, RET_DV), lambda b, h: (b, layer_j, 0, h, 0, 0)))
        out_shape.append(jax.ShapeDtypeStruct((BATCH, (DEPTH + 1) // 2, 2, RET_HEADS, RET_DK, RET_DV), F32))
    return pl.pallas_call(
        functools.partial(_ret_core_kernel, seq_len=seq_len, n_seq=n_seq, chunk=chunk, has_init=latent,
                          emit_state=not latent, n_alias=len(aliases)),
        grid=(nb, RET_HEADS),
        in_specs=in_specs,
        out_specs=out_specs,
        out_shape=out_shape,
        input_output_aliases=aliases,
        scratch_shapes=[
            pltpu.VMEM((rows, RET_DV), F32),
            pltpu.VMEM((rows // chunk, 2 * RET_DK, RET_DV), F32),
        ],
        compiler_params=_params(2),
        name="ret_core_lat" if latent else "ret_core_ctx",
    )(*args)


ATT_Q_W = ATT_HEADS * HEAD_DIM
ATT_KV_W = ATT_KV_HEADS * HEAD_DIM


def _group_mean_sq(xc, gmat):
    sq = xc * xc
    hi = sq.astype(BF16)
    lo = (sq - hi.astype(F32)).astype(BF16)
    return jnp.dot(hi, gmat, preferred_element_type=F32) + jnp.dot(lo, gmat, preferred_element_type=F32)


SEQS_PER_TILE = TM // SEQ


def _att_proj_kernel(*refs, n_alias):
    n_src = 1 + N_WINDOWS + 3
    (gain_ref, sh_ref, sc_ref, w_ref, qg_ref, kg_ref, cos_ref, sin_ref) = refs[n_src:n_src + 8]
    o_ref, xo_ref, kc_ref, vc_ref, wbf_ref, kv_ref, buf_ref = refs[n_src + 8 + n_alias:]
    i = pl.program_id(0)

    @pl.when(i == 0)
    def _():
        wbf_ref[...] = w_ref[0].astype(BF16)

    x = _combine_tile(i, refs[0], refs[1:1 + N_WINDOWS], *refs[1 + N_WINDOWS:n_src], buf_ref)
    xo_ref[...] = x
    h = _norm_mod(x, gain_ref[...], sh_ref[0], sc_ref[0]).astype(BF16)
    acc = jnp.dot(h, wbf_ref[...], preferred_element_type=F32)

    r = lax.broadcasted_iota(jnp.int32, (LANES, LANES), 0) // HEAD_DIM
    c = lax.broadcasted_iota(jnp.int32, (LANES, LANES), 1) // HEAD_DIM
    gmat = jnp.where(r == c, 1.0 / HEAD_DIM, 0.0).astype(BF16)
    cos = cos_ref[...]
    sin = sin_ref[...]
    n_q = ATT_Q_W // LANES
    n_kv = ATT_KV_W // LANES
    for cidx in range(n_q + n_kv):
        xc = acc[:, cidx * LANES:(cidx + 1) * LANES]
        gain = qg_ref[...] if cidx < n_q else kg_ref[...]
        normed = xc * lax.rsqrt(_group_mean_sq(xc, gmat) + EPS) * gain
        if cidx >= n_q:
            kv_ref[:, (cidx - n_q) * LANES:(cidx - n_q + 1) * LANES] = normed
        o_ref[:, cidx * LANES:(cidx + 1) * LANES] = _rope_chunk(normed, cos, sin, HEAD_DIM // 4).astype(BF16)
    v = acc[:, ATT_Q_W + ATT_KV_W:]
    kv_ref[:, ATT_KV_W:] = v
    o_ref[:, ATT_Q_W + ATT_KV_W:] = v.astype(BF16)

    @pl.when(i < N_CTX_TILES)
    def _():
        for s in range(SEQS_PER_TILE):
            for hd in range(ATT_KV_HEADS):
                rows = slice(s * SEQ, (s + 1) * SEQ)
                kc_ref[s, 0, hd] = kv_ref[rows, hd * HEAD_DIM:(hd + 1) * HEAD_DIM]
                vc_ref[s, 0, hd] = kv_ref[rows, ATT_KV_W + hd * HEAD_DIM:ATT_KV_W + (hd + 1) * HEAD_DIM]


def _att_proj(pending, gain, shift, scale, w_in, layer_j, q_gain, k_gain, cos, sin, caches):
    n_alias = len(caches)
    cache_shape = caches[0].shape
    cache_spec = pl.BlockSpec((SEQS_PER_TILE, 1, ATT_KV_HEADS, SEQ, HEAD_DIM),
                              lambda i, *_: (jnp.minimum(i, N_CTX_TILES - 1), layer_j, 0, 0, 0))
    src_specs, src_args, scratch = _pending_specs(pending)
    in_specs = src_specs + [
        pl.BlockSpec((1, D_MODEL), lambda i, *_: (0, 0)),
        pl.BlockSpec((1, 1, D_MODEL), lambda i, *_: (_cond_row(i), 0, 0)),
        pl.BlockSpec((1, 1, D_MODEL), lambda i, *_: (_cond_row(i), 0, 0)),
        pl.BlockSpec((1, D_MODEL, ATT_IN), lambda i, *_: (layer_j, 0, 0)),
        pl.BlockSpec((1, LANES), lambda i, *_: (0, 0)),
        pl.BlockSpec((1, LANES), lambda i, *_: (0, 0)),
        pl.BlockSpec((TM, LANES), lambda i, *_: (_rope_row(i), 0)),
        pl.BlockSpec((TM, LANES), lambda i, *_: (_rope_row(i), 0)),
    ] + [pl.BlockSpec(memory_space=pl.ANY)] * n_alias
    args = src_args + [gain, shift, scale, w_in, q_gain, k_gain, cos, sin] + list(caches)
    first_cache = len(args) - n_alias
    return pl.pallas_call(
        functools.partial(_att_proj_kernel, n_alias=n_alias),
        grid_spec=pltpu.PrefetchScalarGridSpec(
            num_scalar_prefetch=1,
            grid=(T_ROWS // TM,),
            in_specs=in_specs,
            out_specs=[pl.BlockSpec((TM, ATT_IN), lambda i, *_: (i, 0)),
                       pl.BlockSpec((TM, D_MODEL), lambda i, *_: (i, 0)), cache_spec, cache_spec],
            scratch_shapes=[pltpu.VMEM((D_MODEL, ATT_IN), BF16), pltpu.VMEM((TM, 2 * ATT_KV_W), F32)] + scratch,
        ),
        out_shape=[
            jax.ShapeDtypeStruct((T_ROWS, ATT_IN), BF16),
            jax.ShapeDtypeStruct((T_ROWS, D_MODEL), F32),
            jax.ShapeDtypeStruct(cache_shape, F32),
            jax.ShapeDtypeStruct(cache_shape, F32),
        ],
        input_output_aliases={first_cache: 2, first_cache + 1: 3},
        compiler_params=_params(1),
        name="att_proj",
    )(*args)


SINK_ROWS = 16
TN_DIMS = (((0,), (0,)), ((), ()))


def _ones_column(n):
    lane = lax.broadcasted_iota(jnp.int32, (n, HEAD_DIM), 1)
    return jnp.where(lane == 0, 1.0, 0.0).astype(BF16)


def _sink_softmax_pv(qs, sink_row, blocks):
    r = qs.shape[0]
    q_scaled = qs * jnp.asarray(ATT_SCALE, BF16)
    scores = []
    values = []
    for k, v, bias in blocks:
        s = lax.dot_general(k, q_scaled, NT_DIMS, preferred_element_type=F32)
        scores.append(s if bias is None else s + bias)
        values.append(jnp.concatenate([v, _ones_column(v.shape[0])], axis=1))
    row = lax.broadcasted_iota(jnp.int32, (SINK_ROWS, r), 0)
    scores.append(jnp.where(row == 0, sink_row, -jnp.inf))
    values.append(jnp.concatenate([jnp.zeros((SINK_ROWS, HEAD_DIM), BF16), _ones_column(SINK_ROWS)], axis=1))
    st = jnp.concatenate(scores, axis=0)
    m = jnp.max(st, axis=0, keepdims=True)
    pt = jnp.exp(st - m).astype(BF16)
    ov = lax.dot_general(pt, jnp.concatenate(values, axis=0), TN_DIMS, preferred_element_type=F32)
    return ov[:, :HEAD_DIM] / ov[:, HEAD_DIM:HEAD_DIM + 1]


def _stack_heads(q, kvh, rows):
    parts = []
    for g in range(ATT_GROUP):
        hd = kvh * ATT_GROUP + g
        parts.append(q[:, hd * HEAD_DIM:(hd + 1) * HEAD_DIM])
    return jnp.concatenate(parts, axis=0)


def _sink_row(sink_ref, kvh, rows):
    head = lax.broadcasted_iota(jnp.int32, (1, ATT_GROUP * rows), 1) // rows
    out = jnp.full((1, ATT_GROUP * rows), sink_ref[kvh * ATT_GROUP], F32)
    for g in range(1, ATT_GROUP):
        out = jnp.where(head == g, sink_ref[kvh * ATT_GROUP + g], out)
    return out


def _store_heads(o_ref, o, kvh, rows):
    for g in range(ATT_GROUP):
        hd = kvh * ATT_GROUP + g
        o_ref[:, hd * HEAD_DIM:(hd + 1) * HEAD_DIM] = o[g * rows:(g + 1) * rows, :].astype(BF16)


CTX_ATT_SEQS = 1


def _ctx_att_kernel(sink_ref, q_ref, k_ref, v_ref, o_ref):
    for s in range(CTX_ATT_SEQS):
        seq = slice(s * SEQ, (s + 1) * SEQ)
        q = q_ref[seq, :]
        for kvh in range(ATT_KV_HEADS):
            cols = slice(kvh * HEAD_DIM, (kvh + 1) * HEAD_DIM)
            qs = _stack_heads(q, kvh, SEQ)
            o = _sink_softmax_pv(qs, _sink_row(sink_ref, kvh, SEQ), [(k_ref[seq, cols], v_ref[seq, cols], None)])
            _store_heads(o_ref.at[seq, :], o, kvh, SEQ)


LAT_Q_BLOCKS = 4


def _lat_att_kernel(sink_ref, q_ref, k_ref, v_ref, ck_ref, cv_ref, o_ref):
    nb = DEC_SEQ // ATT_BLOCK
    blk = ATT_BLOCK
    rows = ATT_GROUP * blk
    kj = lax.broadcasted_iota(jnp.int32, (3 * blk, rows), 0) - blk
    qi = lax.broadcasted_iota(jnp.int32, (3 * blk, rows), 1) % blk
    in_window = jnp.abs(qi - kj) <= WINDOW
    for sub in range(LAT_Q_BLOCKS):
        n = pl.program_id(1) * LAT_Q_BLOCKS + sub
        prev0 = pl.multiple_of(jnp.maximum(n - 1, 0) * blk, blk)
        cur0 = pl.multiple_of(n * blk, blk)
        next0 = pl.multiple_of(jnp.minimum(n + 1, nb - 1) * blk, blk)
        kabs = n * blk + kj
        bias = jnp.where(in_window & (kabs >= 0) & (kabs < DEC_SEQ), 0.0, -jnp.inf)
        q = q_ref[sub * blk:(sub + 1) * blk, :]
        for kvh in range(ATT_KV_HEADS):
            cols = slice(kvh * HEAD_DIM, (kvh + 1) * HEAD_DIM)
            qs = _stack_heads(q, kvh, blk)
            k_loc = jnp.concatenate([k_ref[pl.ds(prev0, blk), cols], k_ref[pl.ds(cur0, blk), cols],
                                     k_ref[pl.ds(next0, blk), cols]], axis=0)
            v_loc = jnp.concatenate([v_ref[pl.ds(prev0, blk), cols], v_ref[pl.ds(cur0, blk), cols],
                                     v_ref[pl.ds(next0, blk), cols]], axis=0)
            k_ctx = ck_ref[0, 0, kvh].astype(BF16)
            v_ctx = cv_ref[0, 0, kvh].astype(BF16)
            o = _sink_softmax_pv(qs, _sink_row(sink_ref, kvh, blk),
                                 [(k_loc, v_loc, bias), (k_ctx, v_ctx, None)])
            _store_heads(o_ref.at[sub * blk:(sub + 1) * blk, :], o, kvh, blk)


def _ctx_attention(qkv, sink):
    kcol = ATT_Q_W // ATT_KV_W
    rows = CTX_ATT_SEQS * SEQ
    return pl.pallas_call(
        _ctx_att_kernel,
        grid_spec=pltpu.PrefetchScalarGridSpec(
            num_scalar_prefetch=1,
            grid=(BATCH // CTX_ATT_SEQS,),
            in_specs=[
                pl.BlockSpec((rows, ATT_Q_W), lambda b, s: (b, 0)),
                pl.BlockSpec((rows, ATT_KV_W), lambda b, s: (b, kcol)),
                pl.BlockSpec((rows, ATT_KV_W), lambda b, s: (b, kcol + 1)),
            ],
            out_specs=pl.BlockSpec((rows, ATT_Q_W), lambda b, s: (b, 0)),
        ),
        out_shape=jax.ShapeDtypeStruct((CTX_ROWS, ATT_Q_W), BF16),
        compiler_params=_params(1),
        name="ctx_attention",
    )(sink, qkv, qkv, qkv)


def _lat_attention(qkv, sink, cache_k, cache_v, layer_j):
    kcol = ATT_Q_W // ATT_KV_W
    qrows = LAT_Q_BLOCKS * ATT_BLOCK
    nb = DEC_SEQ // qrows
    q0 = CTX_ROWS // qrows
    s0 = CTX_ROWS // DEC_SEQ
    cache_spec = pl.BlockSpec((1, 1, ATT_KV_HEADS, PAST_LEN, HEAD_DIM), lambda b, n, s: (b, layer_j, 0, 0, 0))
    return pl.pallas_call(
        _lat_att_kernel,
        grid_spec=pltpu.PrefetchScalarGridSpec(
            num_scalar_prefetch=1,
            grid=(DEC_BATCH, nb),
            in_specs=[
                pl.BlockSpec((qrows, ATT_Q_W), lambda b, n, s: (q0 + b * nb + n, 0)),
                pl.BlockSpec((DEC_SEQ, ATT_KV_W), lambda b, n, s: (s0 + b, kcol)),
                pl.BlockSpec((DEC_SEQ, ATT_KV_W), lambda b, n, s: (s0 + b, kcol + 1)),
                cache_spec,
                cache_spec,
            ],
            out_specs=pl.BlockSpec((qrows, ATT_Q_W), lambda b, n, s: (b * nb + n, 0)),
        ),
        out_shape=jax.ShapeDtypeStruct((LAT_ROWS, ATT_Q_W), BF16),
        compiler_params=_params(2),
        name="lat_attention",
    )(sink, qkv, qkv, qkv, cache_k, cache_v)


ROW_ALIGN = 16
S_SLOTS = 576
BUF_ROWS = S_SLOTS + TM_MOE
XW = D_MODEL + LANES
SLOT_LANE = 6
GATE_LANE0 = 8
REGION_TILES = 28
REGION_ROWS = REGION_TILES * TM_MOE
TAB_W = 16
HALF_WIN = TM_MOE // 2
N_TOK_TILES = T_ROWS // TM


def _router_kernel(ac_ref, al_ref, wo_ref, x_ref, ga_ref, gain_ref, sh_ref, sc_ref, wr_ref, bias_ref,
                   xo_ref, xs_hbm, row_ref, tab_ref, wl_ref, wobf_ref, buf_ref, tri_ref, off_ref, wide_ref, sem):
    i = pl.program_id(0)
    ng = N_GROUPS
    last = pl.num_programs(0) - 1

    @pl.when(i == 0)
    def _():
        wobf_ref[...] = wo_ref[0].astype(BF16)

    mixed = jnp.where(i < N_CTX_TILES, ac_ref[...], al_ref[...])
    x_new = x_ref[...] + ga_ref[0] * jnp.dot(mixed, wobf_ref[...], preferred_element_type=F32)
    xo_ref[...] = x_new

    @pl.when(i == 0)
    def _():
        buf_ref[...] = jnp.zeros_like(buf_ref)
        r = lax.broadcasted_iota(jnp.int32, (TM, TM), 0)
        cc = lax.broadcasted_iota(jnp.int32, (TM, TM), 1)
        tri_ref[...] = jnp.where(r < cc, 1.0, 0.0).astype(BF16)
        for g in range(ng):
            off_ref[g] = 0

        def clear(r, carry):
            for c in range(TAB_W):
                tab_ref[r * TAB_W + c] = 0
            return carry

        lax.fori_loop(0, N_TOK_TILES, clear, 0)

    hf = _norm_mod(x_new, gain_ref[...], sh_ref[0], sc_ref[0])
    hb = hf.astype(BF16)
    hl = (hf - hb.astype(F32)).astype(BF16)
    both = jnp.dot(hb, wr_ref[...], preferred_element_type=F32)
    logits = (both[:, :LANES] + both[:, LANES:]
              + jnp.dot(hl, wr_ref[:, :LANES], preferred_element_type=F32))
    lt = logits.T[0:N_EXPERTS, :]
    scores = jax.nn.sigmoid(lt)
    sel = scores + bias_ref[...]
    xs = [sel[k * ng:(k + 1) * ng, :] for k in range(EXPERTS_PER_GROUP)]
    sc = [scores[k * ng:(k + 1) * ng, :] for k in range(EXPERTS_PER_GROUP)]
    a, b, c, d = xs
    gs = jnp.maximum(jnp.maximum(jnp.maximum(a + b, a + c), jnp.maximum(a + d, b + c)),
                     jnp.maximum(b + d, c + d))
    bv = gs[0:1, :]
    bg = jnp.zeros(bv.shape, jnp.int32)
    for g in range(1, ng):
        better = gs[g:g + 1, :] > bv
        bg = jnp.where(better, g, bg)
        bv = jnp.where(better, gs[g:g + 1, :], bv)
    giota = lax.broadcasted_iota(jnp.int32, (ng, TM), 0)
    onehot = giota == bg
    wk = []
    for k in range(EXPERTS_PER_GROUP):
        rank = jnp.zeros((ng, TM), F32)
        for j in range(EXPERTS_PER_GROUP):
            if j < k:
                rank = rank + (xs[j] >= xs[k]).astype(F32)
            elif j > k:
                rank = rank + (xs[j] > xs[k]).astype(F32)
        chosen = (rank < 2.0) & onehot
        wk.append(jnp.sum(jnp.where(chosen, sc[k], 0.0), axis=0, keepdims=True))
    den = wk[0] + wk[1] + wk[2] + wk[3]
    gates = [w / den for w in wk]

    oh = jnp.where(onehot, 1.0, 0.0)
    before = jnp.dot(oh.astype(BF16), tri_ref[...], preferred_element_type=F32)
    rank_local = jnp.sum(jnp.where(onehot, before, 0.0), axis=0, keepdims=True).astype(jnp.int32)
    counts = [jnp.sum(jnp.where(bg == g, 1, 0)) for g in range(ng)]
    pads = [((cnt + ROW_ALIGN - 1) // ROW_ALIGN) * ROW_ALIGN for cnt in counts]
    segs = [0]
    for g in range(1, ng):
        segs.append(segs[-1] + pads[g - 1])
    total = segs[-1] + pads[-1]
    seg_of = jnp.zeros(bg.shape, jnp.int32)
    for g in range(1, ng):
        seg_of = jnp.where(bg == g, segs[g], seg_of)
    slot = seg_of + rank_local

    parts = []
    for gt in gates:
        hi = gt.astype(BF16).astype(F32)
        rest = gt - hi
        mid = rest.astype(BF16).astype(F32)
        parts += [hi, mid, (rest - mid).astype(BF16).astype(F32)]
    zero_row = jnp.zeros((1, TM), F32)
    info = jnp.concatenate(gates + [bg.astype(F32), zero_row, slot.astype(F32), zero_row] + parts
                           + [jnp.zeros((LANES - GATE_LANE0 - len(parts), TM), F32)], axis=0)
    rowinfo = info.T
    row_ref[...] = rowinfo

    h_ext = jnp.concatenate([hb, rowinfo.astype(BF16)], axis=1)
    srow = lax.broadcasted_iota(jnp.int32, (S_SLOTS, TM), 0)
    pick = jnp.where(srow == slot, 1.0, 0.0).astype(BF16)
    par = i % 2
    buf_ref[par, 0:S_SLOTS, :] = jnp.dot(pick, h_ext, preferred_element_type=F32).astype(BF16)

    def window_copy(g, src_row, dst_row, rows):
        return pltpu.make_async_copy(
            buf_ref.at[par, pl.ds(pl.multiple_of(src_row, ROW_ALIGN), rows), :],
            xs_hbm.at[pl.ds(pl.multiple_of(dst_row, ROW_ALIGN), rows), :],
            sem.at[g])

    def wait_windows(rows):
        for g in range(ng):
            window_copy(g, 0, 0, rows).wait()

    def wait_previous():
        @pl.when(wide_ref[0] == 1)
        def _():
            wait_windows(TM_MOE)

        @pl.when(wide_ref[0] == 0)
        def _():
            wait_windows(HALF_WIN)

    @pl.when(i > 0)
    def _():
        wait_previous()

    wide = jnp.maximum(jnp.maximum(pads[0], pads[1]), jnp.maximum(pads[2], pads[3])) > HALF_WIN
    offs = [off_ref[g] for g in range(ng)]

    def issue(rows):
        for g in range(ng):
            window_copy(g, segs[g], g * REGION_ROWS + offs[g], rows).start()

    @pl.when(wide)
    def _():
        issue(TM_MOE)

    @pl.when(jnp.logical_not(wide))
    def _():
        issue(HALF_WIN)

    wide_ref[0] = wide.astype(jnp.int32)
    for g in range(ng):
        tab_ref[i * TAB_W + g] = segs[g]
        tab_ref[i * TAB_W + ng + g] = offs[g]
        tab_ref[i * TAB_W + 2 * ng + 1 + g] = pads[g]
        off_ref[g] = offs[g] + pads[g]
    tab_ref[i * TAB_W + 2 * ng] = total

    @pl.when(i == last)
    def _():
        wait_previous()
        for g in range(ng):
            window_copy(g, S_SLOTS, g * REGION_ROWS + off_ref[g], TM_MOE).start()
        wait_windows(TM_MOE)

        entry = 0
        full = 0
        for g in range(ng):
            full = (off_ref[g] + TM_MOE - 1) // TM_MOE

            def add(j, e, g=g, full=full):
                wl_ref[e] = g * REGION_TILES + j
                wl_ref[N_MOE_TILES + e] = g
                wl_ref[2 * N_MOE_TILES + e] = (j < full).astype(jnp.int32)
                return e + 1

            entry = lax.fori_loop(0, full + 1, add, entry)

        def pad(e, carry, full=full):
            wl_ref[e] = (ng - 1) * REGION_TILES + full
            wl_ref[N_MOE_TILES + e] = ng - 1
            wl_ref[2 * N_MOE_TILES + e] = 0
            return carry

        lax.fori_loop(entry, N_MOE_TILES, pad, 0)


def _mix_out_router(a_ctx, a_lat, w_out, layer_j, x, gate_a, gain, shift, scale, wr_parts, bias_col):
    k = a_ctx.shape[1]
    return pl.pallas_call(
        _router_kernel,
        grid=(N_TOK_TILES,),
        in_specs=[
            pl.BlockSpec((TM, k), lambda i: (jnp.minimum(i, N_CTX_TILES - 1), 0)),
            pl.BlockSpec((TM, k), lambda i: (jnp.maximum(i - N_CTX_TILES, 0), 0)),
            pl.BlockSpec((1, k, D_MODEL), lambda i: (layer_j, 0, 0), pipeline_mode=pl.Buffered(1)),
            pl.BlockSpec((TM, D_MODEL), lambda i: (i, 0)),
            pl.BlockSpec((1, 1, D_MODEL), lambda i: (_cond_row(i), 0, 0)),
            pl.BlockSpec((1, D_MODEL), lambda i: (0, 0)),
            pl.BlockSpec((1, 1, D_MODEL), lambda i: (_cond_row(i), 0, 0)),
            pl.BlockSpec((1, 1, D_MODEL), lambda i: (_cond_row(i), 0, 0)),
            pl.BlockSpec((D_MODEL, 2 * LANES), lambda i: (0, 0)),
            pl.BlockSpec((N_EXPERTS, 1), lambda i: (0, 0)),
        ],
        out_specs=[
            pl.BlockSpec((TM, D_MODEL), lambda i: (i, 0)),
            pl.BlockSpec(memory_space=pl.ANY),
            pl.BlockSpec((TM, LANES), lambda i: (i, 0)),
            pl.BlockSpec(memory_space=pltpu.SMEM),
            pl.BlockSpec(memory_space=pltpu.SMEM),
        ],
        out_shape=[
            jax.ShapeDtypeStruct((T_ROWS, D_MODEL), F32),
            jax.ShapeDtypeStruct((N_GROUPS * REGION_ROWS, XW), BF16),
            jax.ShapeDtypeStruct((T_ROWS, LANES), F32),
            jax.ShapeDtypeStruct((N_TOK_TILES * TAB_W,), jnp.int32),
            jax.ShapeDtypeStruct((3 * N_MOE_TILES,), jnp.int32),
        ],
        scratch_shapes=[
            pltpu.VMEM((k, D_MODEL), BF16),
            pltpu.VMEM((2, BUF_ROWS, XW), BF16),
            pltpu.VMEM((TM, TM), BF16),
            pltpu.SMEM((N_GROUPS,), jnp.int32),
            pltpu.SMEM((1,), jnp.int32),
            pltpu.SemaphoreType.DMA((N_GROUPS,)),
        ],
        compiler_params=_params(1),
        name="mix_out_router",
    )(a_ctx, a_lat, w_out, x, gate_a, gain, shift, scale, wr_parts, bias_col)


UP_EXPERTS = 4


def _wl_block(wl, i):
    return wl[i]


def _wl_group(wl, i):
    return wl[N_MOE_TILES + i]


def _wl_valid(wl, i):
    return wl[2 * N_MOE_TILES + i]


def _moe_up_kernel(wl_ref, xs_ref, w_ref, o_ref, wbf_ref):
    kk = pl.program_id(0)
    i = pl.program_id(1)
    new_w = (i == 0) | (_wl_group(wl_ref, i) != _wl_group(wl_ref, jnp.maximum(i - 1, 0)))
    valid = _wl_valid(wl_ref, i)

    @pl.when(new_w)
    def _():
        wbf_ref[...] = w_ref[...].astype(BF16)

    @pl.when(valid == 1)
    def _():
        xs = xs_ref[:, :D_MODEL]
        extra = xs_ref[:, D_MODEL:].astype(F32)
        lane = lax.broadcasted_iota(jnp.int32, extra.shape, 1)
        for e in range(UP_EXPERTS):
            hu = jnp.dot(xs, wbf_ref[e], preferred_element_type=F32)
            lane0 = GATE_LANE0 + 3 * (kk * UP_EXPERTS + e)
            gcol = jnp.sum(jnp.where((lane >= lane0) & (lane < lane0 + 3), extra, 0.0), axis=1, keepdims=True)
            a = _silu(hu[:, :D_EXPERT]) * hu[:, D_EXPERT:] * gcol
            o_ref[:, e * D_EXPERT:(e + 1) * D_EXPERT] = a.astype(BF16)

    @pl.when(valid != 1)
    def _():
        o_ref[...] = jnp.zeros_like(o_ref)


def _moe_up(work_list, xs, w_up, layer):
    steps = EXPERTS_PER_GROUP // UP_EXPERTS
    return pl.pallas_call(
        _moe_up_kernel,
        grid_spec=pltpu.PrefetchScalarGridSpec(
            num_scalar_prefetch=1,
            grid=(steps, N_MOE_TILES),
            in_specs=[
                pl.BlockSpec((TM_MOE, XW), lambda k, i, wl: (_wl_block(wl, i), 0)),
                pl.BlockSpec((UP_EXPERTS, D_MODEL, 2 * D_EXPERT),
                             lambda k, i, wl: ((layer * N_GROUPS + _wl_group(wl, i)) * steps + k, 0, 0)),
            ],
            out_specs=pl.BlockSpec((TM_MOE, UP_EXPERTS * D_EXPERT), lambda k, i, wl: (_wl_block(wl, i), k)),
            scratch_shapes=[pltpu.VMEM((UP_EXPERTS, D_MODEL, 2 * D_EXPERT), BF16)],
        ),
        out_shape=jax.ShapeDtypeStruct((N_GROUPS * REGION_ROWS, EXPERTS_PER_GROUP * D_EXPERT), BF16),
        compiler_params=_params(2),
        name="moe_up",
    )(work_list, xs, w_up)


def _moe_down_kernel(wl_ref, a_ref, w_ref, o_ref, wbf_ref):
    i = pl.program_id(0)
    new_w = (i == 0) | (_wl_group(wl_ref, i) != _wl_group(wl_ref, jnp.maximum(i - 1, 0)))
    valid = _wl_valid(wl_ref, i)

    @pl.when(new_w)
    def _():
        wbf_ref[...] = w_ref[0].astype(BF16)

    @pl.when(valid == 1)
    def _():
        o_ref[...] = jnp.dot(a_ref[...], wbf_ref[...], preferred_element_type=F32).astype(BF16)

    @pl.when(valid != 1)
    def _():
        o_ref[...] = jnp.zeros_like(o_ref)


def _moe_down(work_list, a, w_down_grouped, layer):
    kdim = EXPERTS_PER_GROUP * D_EXPERT
    return pl.pallas_call(
        _moe_down_kernel,
        grid_spec=pltpu.PrefetchScalarGridSpec(
            num_scalar_prefetch=1,
            grid=(N_MOE_TILES,),
            in_specs=[
                pl.BlockSpec((TM_MOE, kdim), lambda i, wl: (_wl_block(wl, i), 0)),
                pl.BlockSpec((1, kdim, D_MODEL), lambda i, wl: (layer * N_GROUPS + _wl_group(wl, i), 0, 0)),
            ],
            out_specs=pl.BlockSpec((TM_MOE, D_MODEL), lambda i, wl: (_wl_block(wl, i), 0)),
            scratch_shapes=[pltpu.VMEM((kdim, D_MODEL), BF16)],
        ),
        out_shape=jax.ShapeDtypeStruct((N_GROUPS * REGION_ROWS, D_MODEL), BF16),
        compiler_params=_params(1),
        name="moe_down",
    )(work_list, a, w_down_grouped)


N_WINDOWS = 2 * N_GROUPS


class PendingMoe(NamedTuple):
    tab_flat: jax.Array
    ys: jax.Array
    rowinfo: jax.Array
    x: jax.Array
    gate: jax.Array


def _combine_tile(i, tab_ref, windows, row_ref, x_ref, g_ref, buf_ref):
    first = windows[:N_GROUPS]
    second = windows[N_GROUPS:]

    @pl.when(i == 0)
    def _():
        buf_ref[...] = jnp.zeros_like(buf_ref)

    for g in range(N_GROUPS):
        start = pl.multiple_of(tab_ref[i * TAB_W + g], ROW_ALIGN)
        buf_ref[pl.ds(start, HALF_WIN), :] = first[g][...]

        @pl.when(tab_ref[i * TAB_W + 2 * N_GROUPS + 1 + g] > HALF_WIN)
        def _(g=g, start=start):
            buf_ref[pl.ds(start + HALF_WIN, HALF_WIN), :] = second[g][...]

    slot = row_ref[:, SLOT_LANE:SLOT_LANE + 1].astype(jnp.int32)
    scol = lax.broadcasted_iota(jnp.int32, (TM, S_SLOTS), 1)
    pick = jnp.where(scol == slot, 1.0, 0.0).astype(BF16)
    y = jnp.dot(pick, buf_ref[0:S_SLOTS, :], preferred_element_type=F32)
    return x_ref[...] + g_ref[0] * y


def _pending_specs(p):
    def window_spec(g, second):
        def index(i, tab):
            off = tab[i * TAB_W + N_GROUPS + g]
            if second:
                off = jnp.where(tab[i * TAB_W + 2 * N_GROUPS + 1 + g] > HALF_WIN, off + HALF_WIN, 0)
            return pl.multiple_of(off + g * REGION_ROWS, ROW_ALIGN), 0

        return pl.BlockSpec((pl.Element(HALF_WIN), pl.Element(D_MODEL)), index)

    specs = [window_spec(g, False) for g in range(N_GROUPS)] + [window_spec(g, True) for g in range(N_GROUPS)]
    specs += [
        pl.BlockSpec((TM, LANES), lambda i, tab: (i, 0)),
        pl.BlockSpec((TM, D_MODEL), lambda i, tab: (i, 0)),
        pl.BlockSpec((1, 1, D_MODEL), lambda i, tab: (_cond_row(i), 0, 0)),
    ]
    args = [p.tab_flat] + [p.ys] * N_WINDOWS + [p.rowinfo, p.x, p.gate]
    return specs, args, [pltpu.VMEM((BUF_ROWS, D_MODEL), BF16)]


def _moe_combine_kernel(tab_ref, *refs):
    row_ref, x_ref, g_ref, ctx_ref, lat_ref, buf_ref = refs[N_WINDOWS:]
    i = pl.program_id(0)
    new_x = _combine_tile(i, tab_ref, refs[:N_WINDOWS], row_ref, x_ref, g_ref, buf_ref)

    @pl.when(i < N_CTX_TILES)
    def _():
        ctx_ref[...] = new_x

    @pl.when(i >= N_CTX_TILES)
    def _():
        lat_ref[...] = new_x


def _moe_combine(pending):
    specs, args, scratch = _pending_specs(pending)
    return pl.pallas_call(
        _moe_combine_kernel,
        grid_spec=pltpu.PrefetchScalarGridSpec(
            num_scalar_prefetch=1,
            grid=(N_TOK_TILES,),
            in_specs=specs,
            out_specs=[pl.BlockSpec((TM, D_MODEL), lambda i, tab: (jnp.minimum(i, N_CTX_TILES - 1), 0)),
                       pl.BlockSpec((TM, D_MODEL), lambda i, tab: (jnp.maximum(i - N_CTX_TILES, 0), 0))],
            scratch_shapes=scratch,
        ),
        out_shape=[jax.ShapeDtypeStruct((CTX_ROWS, D_MODEL), F32), jax.ShapeDtypeStruct((LAT_ROWS, D_MODEL), F32)],
        compiler_params=_params(1),
        name="moe_combine",
    )(*args)


def _mix_out_and_moe(mix, x, gate_a, gain, shift, scale, gate, router_w, w_up, w_down_grouped, layer):
    wr_parts, bias_col = router_w
    x, xs, rowinfo, tab, work_list = _mix_out_router(*mix, x, gate_a, gain, shift, scale, wr_parts, bias_col)
    a = _moe_up(work_list, xs, w_up, layer)
    ys = _moe_down(work_list, a, w_down_grouped, layer)
    return PendingMoe(tab, ys, rowinfo, x, gate)


def kernel(x_prompt, x_sample, state_ret, cache_k, cache_v, c, c_ctx, w_mod, b_mod, norm_mix,
           norm_moe, ret_w_in, ret_decay, ret_w_out, att_w_in, att_q_norm, att_k_norm, att_sink,
           att_w_out, w_router, router_bias, moe_w_up, moe_w_down):
    cond8 = jnp.zeros((8, D_MODEL), F32).at[0].set(c_ctx).at[1:N_COND].set(c)
    mods = _modulation_all(cond8, w_mod, b_mod)
    mods = mods[:, :N_COND].reshape(DEPTH, N_COND, N_MOD, 1, D_MODEL)

    ret_cos, ret_sin = _rope_tables(RET_DK)
    att_cos, att_sin = _rope_tables(HEAD_DIM)

    perm = jnp.arange(N_EXPERTS).reshape(N_GROUPS, EXPERTS_PER_GROUP).T.reshape(-1)
    wr = jnp.zeros((D_MODEL, LANES), F32).at[:, :N_EXPERTS].set(w_router[:, perm])
    wr_hi = wr.astype(BF16)
    wr_lo = (wr - wr_hi.astype(F32)).astype(BF16)
    bias_col = router_bias[perm].astype(F32).reshape(N_EXPERTS, 1)
    router_w = (jnp.concatenate([wr_hi, wr_lo], axis=1), bias_col)

    w_up_all = moe_w_up.reshape(DEPTH * N_EXPERTS, D_MODEL, 2 * D_EXPERT)
    w_down_all = moe_w_down.reshape(DEPTH * N_GROUPS, EXPERTS_PER_GROUP * D_EXPERT, D_MODEL)
    ret_w_in_bf16 = ret_w_in.astype(BF16)
    new_state = jnp.zeros((BATCH, (DEPTH + 1) // 2, 2, RET_HEADS, RET_DK, RET_DV), F32)
    cache_shape = (BATCH, DEPTH // 2, ATT_KV_HEADS, SEQ, HEAD_DIM)
    new_kv = (jnp.zeros(cache_shape, F32), jnp.zeros(cache_shape, F32))
    source = (x_prompt.reshape(CTX_ROWS, D_MODEL), x_sample.reshape(LAT_ROWS, D_MODEL))
    for layer in range(DEPTH):
        sh_a, sc_a, g_a, sh_m, sc_m, g_m = [mods[layer, :, t] for t in range(N_MOD)]
        gain_mix = norm_mix[layer].reshape(1, D_MODEL)
        gain_moe = norm_moe[layer].reshape(1, D_MODEL)
        j = layer // 2
        if layer % 2 == 0:
            proj, x = _ret_proj(source, gain_mix, sh_a, sc_a, ret_w_in_bf16, j, ret_cos, ret_sin)
            o_ctx, new_state = _ret_core(proj, ret_decay[j], state_ret, j, latent=False, new_state=new_state)
            (o_lat,) = _ret_core(proj, ret_decay[j], state_ret, j, latent=True)
            mix = (o_ctx, o_lat, ret_w_out, j)
        else:
            q_gain = jnp.tile(att_q_norm[j], LANES // HEAD_DIM).reshape(1, LANES)
            k_gain = jnp.tile(att_k_norm[j], LANES // HEAD_DIM).reshape(1, LANES)
            qkv, x, new_k, new_v = _att_proj(source, gain_mix, sh_a, sc_a, att_w_in, j, q_gain, k_gain,
                                             att_cos, att_sin, new_kv)
            new_kv = (new_k, new_v)
            sink = att_sink[j].astype(F32)
            o_ctx = _ctx_attention(qkv, sink)
            o_lat = _lat_attention(qkv, sink, cache_k, cache_v, j)
            mix = (o_ctx, o_lat, att_w_out, j)
        source = _mix_out_and_moe(mix, x, g_a, gain_moe, sh_m, sc_m, g_m, router_w, w_up_all, w_down_all, layer)

    x_ctx, x_lat = _moe_combine(source)
    y_prompt = x_ctx.reshape(BATCH, SEQ, D_MODEL)
    y_sample = x_lat.reshape(DEC_BATCH, DEC_SEQ, D_MODEL)
    return (y_prompt, y_sample, new_state, new_kv[0], new_kv[1])
```

```python
import functools
from typing import NamedTuple

import jax
import jax.numpy as jnp
import numpy as np
from jax import lax
from jax.experimental import pallas as pl
from jax.experimental.pallas import tpu as pltpu

F32 = jnp.float32
BF16 = jnp.bfloat16

D_MODEL = 1024
BATCH = 16
SEQ = 256
DEPTH = 4
DEC_BATCH = 2
DEC_SEQ = 4096
PAST_LEN = 512
GRID_W = 64
N_MOD = 6
EPS = 1e-6
ROPE_BASE = 10000.0
RET_HEADS = 8
RET_DK = 128
RET_DV = 256
RET_CHUNK = 256
RET_IN = 2 * RET_HEADS * RET_DK + 2 * RET_HEADS * RET_DV
ATT_HEADS = 16
ATT_KV_HEADS = 4
ATT_GROUP = 4
HEAD_DIM = 64
WINDOW = 128
ATT_BLOCK = 128
ATT_IN = (ATT_HEADS + 2 * ATT_KV_HEADS) * HEAD_DIM
ATT_SCALE = HEAD_DIM ** -0.5
N_EXPERTS = 16
N_GROUPS = 4
EXPERTS_PER_GROUP = 4
D_EXPERT = 512

CTX_ROWS = BATCH * SEQ
LAT_ROWS = DEC_BATCH * DEC_SEQ
T_ROWS = CTX_ROWS + LAT_ROWS
N_COND = 1 + DEC_BATCH

LANES = 128
TM = 512
N_CTX_TILES = CTX_ROWS // TM
N_LAT_TILES = DEC_SEQ // TM
TM_MOE = 512
N_MOE_TILES = 36
VMEM_LIMIT = 52 * 1024 * 1024

NT_DIMS = (((1,), (1,)), ((), ()))


def _params(n_axes, vmem=VMEM_LIMIT):
    return pltpu.CompilerParams(dimension_semantics=("arbitrary",) * n_axes, vmem_limit_bytes=vmem)


def _cond_row(i):
    return (i * TM) // DEC_SEQ


def _rope_row(i):
    return jnp.where(i < N_CTX_TILES, 0, 1 + (i - N_CTX_TILES) % N_LAT_TILES)


def _norm_mod(x, gain, shift, scale):
    ms = jnp.mean(x * x, axis=-1, keepdims=True)
    y = x * lax.rsqrt(ms + EPS) * gain
    return y * (1.0 + scale) + shift


def _silu(x):
    return x * jax.nn.sigmoid(x)


def _mod_kernel(c_ref, w_ref, b_ref, o_ref):
    c = c_ref[...]
    s = _silu(c).astype(BF16)
    o_ref[0] = jnp.dot(s, w_ref[0].astype(BF16), preferred_element_type=F32) + b_ref[0]


def _modulation_all(cond8, w_mod, b_mod):
    tn = 1536
    n = N_MOD * D_MODEL
    return pl.pallas_call(
        _mod_kernel,
        grid=(DEPTH, n // tn),
        in_specs=[
            pl.BlockSpec((8, D_MODEL), lambda l, j: (0, 0)),
            pl.BlockSpec((1, D_MODEL, tn), lambda l, j: (l, 0, j)),
            pl.BlockSpec((1, 1, tn), lambda l, j: (l, 0, j)),
        ],
        out_specs=pl.BlockSpec((1, 8, tn), lambda l, j: (l, 0, j)),
        out_shape=jax.ShapeDtypeStruct((DEPTH, 8, n), F32),
        compiler_params=_params(2),
        name="modulation",
    )(cond8, w_mod, b_mod.reshape(DEPTH, 1, n))


def _rope_tables(head_dim):
    half = head_dim // 2
    quarter = half // 2
    t = np.arange(DEC_SEQ)
    row = (t // GRID_W).astype(np.float64)
    col = (t % GRID_W).astype(np.float64)
    inv_freq = ROPE_BASE ** (-np.arange(quarter, dtype=np.float64) / quarter)
    lane = np.arange(LANES)
    d = lane % head_dim
    w = d % half
    f = w % quarter
    pos = np.where((d // half)[None, :] == 0, row[:, None], col[:, None])
    ang = pos * inv_freq[f][None, :]
    cos = np.cos(ang)
    sin = np.where((w < quarter)[None, :], -np.sin(ang), np.sin(ang))
    cos = np.concatenate([np.ones((TM, LANES)), cos], axis=0).astype(np.float32)
    sin = np.concatenate([np.zeros((TM, LANES)), sin], axis=0).astype(np.float32)
    return jnp.asarray(cos), jnp.asarray(sin)


def _rope_chunk(xc, cos, sin, quarter):
    lane = lax.broadcasted_iota(jnp.int32, xc.shape, 1)
    first = (lane % (2 * quarter)) < quarter
    partner = jnp.where(first, pltpu.roll(xc, LANES - quarter, 1), pltpu.roll(xc, quarter, 1))
    return xc * cos + partner * sin


RET_TN = 1024
RET_Q_TILES = RET_HEADS * RET_DK // RET_TN
RET_QK_TILES = 2 * RET_Q_TILES


def _ret_proj_kernel(*refs, from_moe):
    i = pl.program_id(0)
    if from_moe:
        n_src = 1 + N_WINDOWS + 3
        x = _combine_tile(i, refs[0], refs[1:1 + N_WINDOWS], *refs[1 + N_WINDOWS:n_src], refs[-1])
    else:
        n_src = 2
        x = jnp.where(i < N_CTX_TILES, refs[0][...], refs[1][...])
    gain_ref, sh_ref, sc_ref, w_ref, cos_ref, sin_ref, o_ref, xo_ref = refs[n_src:n_src + 8]
    xo_ref[...] = x
    h = _norm_mod(x, gain_ref[...], sh_ref[0], sc_ref[0]).astype(BF16)
    cos = cos_ref[...]
    sin = sin_ref[...]
    for j in range(RET_IN // RET_TN):
        cols = slice(j * RET_TN, (j + 1) * RET_TN)
        acc = jnp.dot(h, w_ref[0, :, cols], preferred_element_type=F32)
        if j < RET_QK_TILES:
            scale = 1.0 if j < RET_Q_TILES else RET_DK ** -0.5
            for c in range(RET_TN // LANES):
                xc = acc[:, c * LANES:(c + 1) * LANES]
                lanes = slice(j * RET_TN + c * LANES, j * RET_TN + (c + 1) * LANES)
                o_ref[:, lanes] = (_rope_chunk(xc, cos, sin, RET_DK // 4) * scale).astype(BF16)
        else:
            o_ref[:, cols] = acc.astype(BF16)


def _ret_proj(source, gain, shift, scale, w_in_bf16, layer_j, cos, sin):
    from_moe = isinstance(source, PendingMoe)
    if from_moe:
        src_specs, src_args, scratch = _pending_specs(source)
        n_prefetch = 1
    else:
        src_specs = [pl.BlockSpec((TM, D_MODEL), lambda i, *_: (jnp.minimum(i, N_CTX_TILES - 1), 0)),
                     pl.BlockSpec((TM, D_MODEL), lambda i, *_: (jnp.maximum(i - N_CTX_TILES, 0), 0))]
        src_args, scratch, n_prefetch = list(source), [], 0
    return pl.pallas_call(
        functools.partial(_ret_proj_kernel, from_moe=from_moe),
        grid_spec=pltpu.PrefetchScalarGridSpec(
            num_scalar_prefetch=n_prefetch,
            grid=(T_ROWS // TM,),
            in_specs=src_specs + [
                pl.BlockSpec((1, D_MODEL), lambda i, *_: (0, 0)),
                pl.BlockSpec((1, 1, D_MODEL), lambda i, *_: (_cond_row(i), 0, 0)),
                pl.BlockSpec((1, 1, D_MODEL), lambda i, *_: (_cond_row(i), 0, 0)),
                pl.BlockSpec((1, D_MODEL, RET_IN), lambda i, *_: (layer_j, 0, 0), pipeline_mode=pl.Buffered(1)),
                pl.BlockSpec((TM, LANES), lambda i, *_: (_rope_row(i), 0)),
                pl.BlockSpec((TM, LANES), lambda i, *_: (_rope_row(i), 0)),
            ],
            out_specs=[pl.BlockSpec((TM, RET_IN), lambda i, *_: (i, 0)),
                       pl.BlockSpec((TM, D_MODEL), lambda i, *_: (i, 0))],
            scratch_shapes=scratch,
        ),
        out_shape=[jax.ShapeDtypeStruct((T_ROWS, RET_IN), BF16), jax.ShapeDtypeStruct((T_ROWS, D_MODEL), F32)],
        compiler_params=_params(1),
        name="ret_proj",
    )(*src_args, gain, shift, scale, w_in_bf16, cos, sin)


def _log_sigmoid(x):
    return -(jnp.maximum(-x, 0.0) + jnp.log(1.0 + jnp.exp(-jnp.abs(x))))


def _ret_core_kernel(*refs, seq_len, n_seq, chunk, has_init, emit_state, n_alias, state_layers=1, state_slot=0):
    it = iter(refs)
    decay_ref = next(it)
    q_ref = next(it)
    k_ref = next(it)
    v_ref = next(it)
    g_ref = next(it)
    s0_ref = next(it) if has_init else None
    for _ in range(n_alias):
        next(it)
    o_ref = next(it)
    st_ref = next(it) if emit_state else None
    acc_ref = next(it)
    kv_ref = next(it)

    c = chunk
    nc = seq_len // c
    n_chunks = n_seq * nc
    unroll = min(RET_UNROLL, n_chunks)
    head = pl.program_id(1)

    lg_all = _log_sigmoid(decay_ref[...])
    pick = lax.broadcasted_iota(jnp.int32, lg_all.shape, 1) == head
    lg = jnp.sum(jnp.where(pick, lg_all, 0.0), axis=1, keepdims=True)
    lg_f = lg[0:1, :]
    lg_b = lg[1:2, :]

    ri = lax.broadcasted_iota(jnp.int32, (c, c), 0).astype(F32)
    ci = lax.broadcasted_iota(jnp.int32, (c, c), 1).astype(F32)
    diff = ri - ci
    decay = (jnp.where(diff >= 0, jnp.exp(jnp.where(diff >= 0, diff, 0.0) * lg_f), 0.0)
             + jnp.where(diff <= 0, jnp.exp(jnp.where(diff <= 0, -diff, 0.0) * lg_b), 0.0))
    pos_col = lax.broadcasted_iota(jnp.int32, (c, 1), 0).astype(F32)
    pos_row = lax.broadcasted_iota(jnp.int32, (1, c), 1).astype(F32)
    xi_f = jnp.exp((pos_col + 1.0) * lg_f)
    xi_b = jnp.exp((c - pos_col) * lg_b)
    zeta_f = jnp.exp((c - 1.0 - pos_row) * lg_f)
    zeta_b = jnp.exp(pos_row * lg_b)
    cd_f = jnp.exp(c * lg_f)
    cd_b = jnp.exp(c * lg_b)

    def intra(n):
        r0 = pl.multiple_of(n * c, c)
        qn = q_ref[pl.ds(r0, c), :]
        kn = k_ref[pl.ds(r0, c), :]
        vn = v_ref[pl.ds(r0, c), :]
        s = lax.dot_general(qn, kn, NT_DIMS, preferred_element_type=F32) * decay
        acc_ref[pl.ds(r0, c), :] = jnp.dot(s.astype(BF16), vn, preferred_element_type=F32)
        kt = kn.astype(F32).T
        kz = jnp.concatenate([(kt * zeta_f).astype(BF16), (kt * zeta_b).astype(BF16)], axis=0)
        kv_ref[n] = jnp.dot(kz, vn, preferred_element_type=F32)

    def cross(n):
        r0 = pl.multiple_of(n * c, c)
        qn = q_ref[pl.ds(r0, c), :].astype(F32)
        qx = jnp.concatenate([(qn * xi_f).astype(BF16), (qn * xi_b).astype(BF16)], axis=1)
        o = acc_ref[pl.ds(r0, c), :] + jnp.dot(qx, kv_ref[n].astype(BF16), preferred_element_type=F32)
        mu = jnp.mean(o, axis=-1, keepdims=True)
        var = jnp.mean(jnp.square(o - mu), axis=-1, keepdims=True)
        on = (o - mu) * lax.rsqrt(var + EPS)
        gate = g_ref[pl.ds(r0, c), :].astype(F32)
        o_ref[pl.ds(r0, c), :] = (_silu(gate) * on).astype(BF16)

    def over_chunks(fn):
        def body(step, carry):
            for u in range(unroll):
                fn(step * unroll + u)
            return carry

        lax.fori_loop(0, n_chunks // unroll, body, 0)

    over_chunks(intra)

    for s in range(n_seq):
        if has_init:
            init_f = s0_ref[s, 0, 0, 0]
            init_b = s0_ref[s, 0, 1, 0]
        else:
            init_f = jnp.zeros((RET_DK, RET_DV), F32)
            init_b = init_f

        def fwd(n, state, s=s):
            kv = kv_ref[s * nc + n, 0:RET_DK, :]
            kv_ref[s * nc + n, 0:RET_DK, :] = state
            return cd_f * state + kv

        def bwd(n, state, s=s):
            m = s * nc + nc - 1 - n
            kv = kv_ref[m, RET_DK:, :]
            kv_ref[m, RET_DK:, :] = state
            return cd_b * state + kv

        final_f = lax.fori_loop(0, nc, fwd, init_f)
        final_b = lax.fori_loop(0, nc, bwd, init_b)
        if emit_state:
            st_ref[s, state_slot, 0, 0] = final_f
            st_ref[s, state_slot, 1, 0] = final_b
            for other in range(state_layers):
                if other != state_slot:
                    st_ref[s, other] = jnp.zeros((2, 1, RET_DK, RET_DV), F32)

    over_chunks(cross)


RET_UNROLL = 16
RET_CTX_SEQS = 8


def _ret_core(proj, ret_decay_j, state_ret, layer_j, *, latent, new_state=None):
    if latent:
        nb, seq_len, n_seq, row0 = DEC_BATCH, DEC_SEQ, 1, CTX_ROWS // DEC_SEQ
    else:
        nb, seq_len, n_seq, row0 = BATCH // RET_CTX_SEQS, SEQ, RET_CTX_SEQS, 0
    rows = n_seq * seq_len
    chunk = min(RET_CHUNK, seq_len)
    kcol = RET_HEADS * RET_DK // RET_DK
    vcol = 2 * RET_HEADS * RET_DK // RET_DV
    gcol = vcol + RET_HEADS
    in_specs = [
        pl.BlockSpec((2, RET_HEADS), lambda b, h: (0, 0)),
        pl.BlockSpec((rows, RET_DK), lambda b, h: (row0 + b, h)),
        pl.BlockSpec((rows, RET_DK), lambda b, h: (row0 + b, kcol + h)),
        pl.BlockSpec((rows, RET_DV), lambda b, h: (row0 + b, vcol + h)),
        pl.BlockSpec((rows, RET_DV), lambda b, h: (row0 + b, gcol + h)),
    ]
    args = [ret_decay_j, proj, proj, proj, proj]
    if latent:
        in_specs.append(pl.BlockSpec((1, 1, 2, 1, RET_DK, RET_DV), lambda b, h: (b, layer_j, 0, h, 0, 0)))
        args.append(state_ret)
    aliases = {}
    if new_state is not None:
        aliases = {len(args): 1}
        in_specs.append(pl.BlockSpec(memory_space=pl.ANY))
        args.append(new_state)
    out_specs = [pl.BlockSpec((rows, RET_DV), lambda b, h: (b, h))]
    out_shape = [jax.ShapeDtypeStruct((nb * rows, RET_HEADS * RET_DV), BF16)]
    n_ret = (DEPTH + 1) // 2
    state_layers, state_slot = (1, 0) if new_state is not None else (n_ret, layer_j)
    if not latent:
        first_layer = layer_j if new_state is not None else 0
        out_specs.append(pl.BlockSpec((n_seq, state_layers, 2, 1, RET_DK, RET_DV),
                                      lambda b, h: (b, first_layer, 0, h, 0, 0)))
        out_shape.append(jax.ShapeDtypeStruct((BATCH, n_ret, 2, RET_HEADS, RET_DK, RET_DV), F32))
    return pl.pallas_call(
        functools.partial(_ret_core_kernel, seq_len=seq_len, n_seq=n_seq, chunk=chunk, has_init=latent,
                          emit_state=not latent, n_alias=len(aliases), state_layers=state_layers,
                          state_slot=state_slot),
        grid=(nb, RET_HEADS),
        in_specs=in_specs,
        out_specs=out_specs,
        out_shape=out_shape,
        input_output_aliases=aliases,
        scratch_shapes=[
            pltpu.VMEM((rows, RET_DV), F32),
            pltpu.VMEM((rows // chunk, 2 * RET_DK, RET_DV), F32),
        ],
        compiler_params=_params(2),
        name="ret_core_lat" if latent else "ret_core_ctx",
    )(*args)


ATT_Q_W = ATT_HEADS * HEAD_DIM
ATT_KV_W = ATT_KV_HEADS * HEAD_DIM


def _group_mean_sq(xc, gmat):
    sq = xc * xc
    hi = sq.astype(BF16)
    lo = (sq - hi.astype(F32)).astype(BF16)
    return jnp.dot(hi, gmat, preferred_element_type=F32) + jnp.dot(lo, gmat, preferred_element_type=F32)


SEQS_PER_TILE = TM // SEQ


def _att_proj_kernel(*refs, n_alias):
    n_src = 1 + N_WINDOWS + 3
    (gain_ref, sh_ref, sc_ref, w_ref, qg_ref, kg_ref, cos_ref, sin_ref) = refs[n_src:n_src + 8]
    o_ref, xo_ref, kc_ref, vc_ref, wbf_ref, kv_ref, buf_ref = refs[n_src + 8 + n_alias:]
    i = pl.program_id(0)

    @pl.when(i == 0)
    def _():
        wbf_ref[...] = w_ref[0].astype(BF16)

    x = _combine_tile(i, refs[0], refs[1:1 + N_WINDOWS], *refs[1 + N_WINDOWS:n_src], buf_ref)
    xo_ref[...] = x
    h = _norm_mod(x, gain_ref[...], sh_ref[0], sc_ref[0]).astype(BF16)
    acc = jnp.dot(h, wbf_ref[...], preferred_element_type=F32)

    r = lax.broadcasted_iota(jnp.int32, (LANES, LANES), 0) // HEAD_DIM
    c = lax.broadcasted_iota(jnp.int32, (LANES, LANES), 1) // HEAD_DIM
    gmat = jnp.where(r == c, 1.0 / HEAD_DIM, 0.0).astype(BF16)
    cos = cos_ref[...]
    sin = sin_ref[...]
    n_q = ATT_Q_W // LANES
    n_kv = ATT_KV_W // LANES
    for cidx in range(n_q + n_kv):
        xc = acc[:, cidx * LANES:(cidx + 1) * LANES]
        gain = qg_ref[...] if cidx < n_q else kg_ref[...]
        normed = xc * lax.rsqrt(_group_mean_sq(xc, gmat) + EPS) * gain
        if cidx >= n_q:
            kv_ref[:, (cidx - n_q) * LANES:(cidx - n_q + 1) * LANES] = normed
        o_ref[:, cidx * LANES:(cidx + 1) * LANES] = _rope_chunk(normed, cos, sin, HEAD_DIM // 4).astype(BF16)
    v = acc[:, ATT_Q_W + ATT_KV_W:]
    kv_ref[:, ATT_KV_W:] = v
    o_ref[:, ATT_Q_W + ATT_KV_W:] = v.astype(BF16)

    @pl.when(i < N_CTX_TILES)
    def _():
        for s in range(SEQS_PER_TILE):
            for hd in range(ATT_KV_HEADS):
                rows = slice(s * SEQ, (s + 1) * SEQ)
                kc_ref[s, 0, hd] = kv_ref[rows, hd * HEAD_DIM:(hd + 1) * HEAD_DIM]
                vc_ref[s, 0, hd] = kv_ref[rows, ATT_KV_W + hd * HEAD_DIM:ATT_KV_W + (hd + 1) * HEAD_DIM]


def _att_proj(pending, gain, shift, scale, w_in, layer_j, q_gain, k_gain, cos, sin, caches):
    n_alias = len(caches)
    cache_shape = caches[0].shape
    cache_spec = pl.BlockSpec((SEQS_PER_TILE, 1, ATT_KV_HEADS, SEQ, HEAD_DIM),
                              lambda i, *_: (jnp.minimum(i, N_CTX_TILES - 1), layer_j, 0, 0, 0))
    src_specs, src_args, scratch = _pending_specs(pending)
    in_specs = src_specs + [
        pl.BlockSpec((1, D_MODEL), lambda i, *_: (0, 0)),
        pl.BlockSpec((1, 1, D_MODEL), lambda i, *_: (_cond_row(i), 0, 0)),
        pl.BlockSpec((1, 1, D_MODEL), lambda i, *_: (_cond_row(i), 0, 0)),
        pl.BlockSpec((1, D_MODEL, ATT_IN), lambda i, *_: (layer_j, 0, 0)),
        pl.BlockSpec((1, LANES), lambda i, *_: (0, 0)),
        pl.BlockSpec((1, LANES), lambda i, *_: (0, 0)),
        pl.BlockSpec((TM, LANES), lambda i, *_: (_rope_row(i), 0)),
        pl.BlockSpec((TM, LANES), lambda i, *_: (_rope_row(i), 0)),
    ] + [pl.BlockSpec(memory_space=pl.ANY)] * n_alias
    args = src_args + [gain, shift, scale, w_in, q_gain, k_gain, cos, sin] + list(caches)
    first_cache = len(args) - n_alias
    return pl.pallas_call(
        functools.partial(_att_proj_kernel, n_alias=n_alias),
        grid_spec=pltpu.PrefetchScalarGridSpec(
            num_scalar_prefetch=1,
            grid=(T_ROWS // TM,),
            in_specs=in_specs,
            out_specs=[pl.BlockSpec((TM, ATT_IN), lambda i, *_: (i, 0)),
                       pl.BlockSpec((TM, D_MODEL), lambda i, *_: (i, 0)), cache_spec, cache_spec],
            scratch_shapes=[pltpu.VMEM((D_MODEL, ATT_IN), BF16), pltpu.VMEM((TM, 2 * ATT_KV_W), F32)] + scratch,
        ),
        out_shape=[
            jax.ShapeDtypeStruct((T_ROWS, ATT_IN), BF16),
            jax.ShapeDtypeStruct((T_ROWS, D_MODEL), F32),
            jax.ShapeDtypeStruct(cache_shape, F32),
            jax.ShapeDtypeStruct(cache_shape, F32),
        ],
        input_output_aliases={first_cache: 2, first_cache + 1: 3},
        compiler_params=_params(1),
        name="att_proj",
    )(*args)


SINK_ROWS = 16
TN_DIMS = (((0,), (0,)), ((), ()))


def _ones_column(n):
    lane = lax.broadcasted_iota(jnp.int32, (n, HEAD_DIM), 1)
    return jnp.where(lane == 0, 1.0, 0.0).astype(BF16)


def _sink_softmax_pv(qs, sink_row, blocks):
    r = qs.shape[0]
    q_scaled = qs * jnp.asarray(ATT_SCALE, BF16)
    scores = []
    values = []
    for k, v, bias in blocks:
        s = lax.dot_general(k, q_scaled, NT_DIMS, preferred_element_type=F32)
        scores.append(s if bias is None else s + bias)
        values.append(jnp.concatenate([v, _ones_column(v.shape[0])], axis=1))
    row = lax.broadcasted_iota(jnp.int32, (SINK_ROWS, r), 0)
    scores.append(jnp.where(row == 0, sink_row, -jnp.inf))
    values.append(jnp.concatenate([jnp.zeros((SINK_ROWS, HEAD_DIM), BF16), _ones_column(SINK_ROWS)], axis=1))
    st = jnp.concatenate(scores, axis=0)
    m = jnp.max(st, axis=0, keepdims=True)
    pt = jnp.exp(st - m).astype(BF16)
    ov = lax.dot_general(pt, jnp.concatenate(values, axis=0), TN_DIMS, preferred_element_type=F32)
    return ov[:, :HEAD_DIM] / ov[:, HEAD_DIM:HEAD_DIM + 1]


def _stack_heads(q, kvh, rows):
    parts = []
    for g in range(ATT_GROUP):
        hd = kvh * ATT_GROUP + g
        parts.append(q[:, hd * HEAD_DIM:(hd + 1) * HEAD_DIM])
    return jnp.concatenate(parts, axis=0)


def _sink_row(sink_ref, kvh, rows):
    head = lax.broadcasted_iota(jnp.int32, (1, ATT_GROUP * rows), 1) // rows
    out = jnp.full((1, ATT_GROUP * rows), sink_ref[kvh * ATT_GROUP], F32)
    for g in range(1, ATT_GROUP):
        out = jnp.where(head == g, sink_ref[kvh * ATT_GROUP + g], out)
    return out


def _store_heads(o_ref, o, kvh, rows):
    for g in range(ATT_GROUP):
        hd = kvh * ATT_GROUP + g
        o_ref[:, hd * HEAD_DIM:(hd + 1) * HEAD_DIM] = o[g * rows:(g + 1) * rows, :].astype(BF16)


CTX_ATT_SEQS = 1


def _ctx_att_kernel(sink_ref, q_ref, k_ref, v_ref, o_ref):
    for s in range(CTX_ATT_SEQS):
        seq = slice(s * SEQ, (s + 1) * SEQ)
        q = q_ref[seq, :]
        for kvh in range(ATT_KV_HEADS):
            cols = slice(kvh * HEAD_DIM, (kvh + 1) * HEAD_DIM)
            qs = _stack_heads(q, kvh, SEQ)
            o = _sink_softmax_pv(qs, _sink_row(sink_ref, kvh, SEQ), [(k_ref[seq, cols], v_ref[seq, cols], None)])
            _store_heads(o_ref.at[seq, :], o, kvh, SEQ)


LAT_Q_BLOCKS = 4


def _lat_att_kernel(sink_ref, q_ref, k_ref, v_ref, ck_ref, cv_ref, o_ref):
    nb = DEC_SEQ // ATT_BLOCK
    blk = ATT_BLOCK
    rows = ATT_GROUP * blk
    kj = lax.broadcasted_iota(jnp.int32, (3 * blk, rows), 0) - blk
    qi = lax.broadcasted_iota(jnp.int32, (3 * blk, rows), 1) % blk
    in_window = jnp.abs(qi - kj) <= WINDOW
    for sub in range(LAT_Q_BLOCKS):
        n = pl.program_id(1) * LAT_Q_BLOCKS + sub
        prev0 = pl.multiple_of(jnp.maximum(n - 1, 0) * blk, blk)
        cur0 = pl.multiple_of(n * blk, blk)
        next0 = pl.multiple_of(jnp.minimum(n + 1, nb - 1) * blk, blk)
        kabs = n * blk + kj
        bias = jnp.where(in_window & (kabs >= 0) & (kabs < DEC_SEQ), 0.0, -jnp.inf)
        q = q_ref[sub * blk:(sub + 1) * blk, :]
        for kvh in range(ATT_KV_HEADS):
            cols = slice(kvh * HEAD_DIM, (kvh + 1) * HEAD_DIM)
            qs = _stack_heads(q, kvh, blk)
            k_loc = jnp.concatenate([k_ref[pl.ds(prev0, blk), cols], k_ref[pl.ds(cur0, blk), cols],
                                     k_ref[pl.ds(next0, blk), cols]], axis=0)
            v_loc = jnp.concatenate([v_ref[pl.ds(prev0, blk), cols], v_ref[pl.ds(cur0, blk), cols],
                                     v_ref[pl.ds(next0, blk), cols]], axis=0)
            k_ctx = ck_ref[0, 0, kvh].astype(BF16)
            v_ctx = cv_ref[0, 0, kvh].astype(BF16)
            o = _sink_softmax_pv(qs, _sink_row(sink_ref, kvh, blk),
                                 [(k_loc, v_loc, bias), (k_ctx, v_ctx, None)])
            _store_heads(o_ref.at[sub * blk:(sub + 1) * blk, :], o, kvh, blk)


def _ctx_attention(qkv, sink):
    kcol = ATT_Q_W // ATT_KV_W
    rows = CTX_ATT_SEQS * SEQ
    return pl.pallas_call(
        _ctx_att_kernel,
        grid_spec=pltpu.PrefetchScalarGridSpec(
            num_scalar_prefetch=1,
            grid=(BATCH // CTX_ATT_SEQS,),
            in_specs=[
                pl.BlockSpec((rows, ATT_Q_W), lambda b, s: (b, 0)),
                pl.BlockSpec((rows, ATT_KV_W), lambda b, s: (b, kcol)),
                pl.BlockSpec((rows, ATT_KV_W), lambda b, s: (b, kcol + 1)),
            ],
            out_specs=pl.BlockSpec((rows, ATT_Q_W), lambda b, s: (b, 0)),
        ),
        out_shape=jax.ShapeDtypeStruct((CTX_ROWS, ATT_Q_W), BF16),
        compiler_params=_params(1),
        name="ctx_attention",
    )(sink, qkv, qkv, qkv)


def _lat_attention(qkv, sink, cache_k, cache_v, layer_j):
    kcol = ATT_Q_W // ATT_KV_W
    qrows = LAT_Q_BLOCKS * ATT_BLOCK
    nb = DEC_SEQ // qrows
    q0 = CTX_ROWS // qrows
    s0 = CTX_ROWS // DEC_SEQ
    cache_spec = pl.BlockSpec((1, 1, ATT_KV_HEADS, PAST_LEN, HEAD_DIM), lambda b, n, s: (b, layer_j, 0, 0, 0))
    return pl.pallas_call(
        _lat_att_kernel,
        grid_spec=pltpu.PrefetchScalarGridSpec(
            num_scalar_prefetch=1,
            grid=(DEC_BATCH, nb),
            in_specs=[
                pl.BlockSpec((qrows, ATT_Q_W), lambda b, n, s: (q0 + b * nb + n, 0)),
                pl.BlockSpec((DEC_SEQ, ATT_KV_W), lambda b, n, s: (s0 + b, kcol)),
                pl.BlockSpec((DEC_SEQ, ATT_KV_W), lambda b, n, s: (s0 + b, kcol + 1)),
                cache_spec,
                cache_spec,
            ],
            out_specs=pl.BlockSpec((qrows, ATT_Q_W), lambda b, n, s: (b * nb + n, 0)),
        ),
        out_shape=jax.ShapeDtypeStruct((LAT_ROWS, ATT_Q_W), BF16),
        compiler_params=_params(2),
        name="lat_attention",
    )(sink, qkv, qkv, qkv, cache_k, cache_v)


ROW_ALIGN = 16
S_SLOTS = 576
BUF_ROWS = S_SLOTS + TM_MOE
XW = D_MODEL + LANES
SLOT_LANE = 6
GATE_LANE0 = 8
REGION_TILES = 28
REGION_ROWS = REGION_TILES * TM_MOE
TAB_W = 16
HALF_WIN = TM_MOE // 2
N_TOK_TILES = T_ROWS // TM


def _router_kernel(ac_ref, al_ref, wo_ref, x_ref, ga_ref, gain_ref, sh_ref, sc_ref, wr_ref, bias_ref,
                   xo_ref, xs_hbm, row_ref, tab_ref, wl_ref, wobf_ref, buf_ref, tri_ref, off_ref, wide_ref, sem):
    i = pl.program_id(0)
    ng = N_GROUPS
    last = pl.num_programs(0) - 1

    @pl.when(i == 0)
    def _():
        wobf_ref[...] = wo_ref[0].astype(BF16)

    mixed = jnp.where(i < N_CTX_TILES, ac_ref[...], al_ref[...])
    x_new = x_ref[...] + ga_ref[0] * jnp.dot(mixed, wobf_ref[...], preferred_element_type=F32)
    xo_ref[...] = x_new

    @pl.when(i == 0)
    def _():
        buf_ref[...] = jnp.zeros_like(buf_ref)
        r = lax.broadcasted_iota(jnp.int32, (TM, TM), 0)
        cc = lax.broadcasted_iota(jnp.int32, (TM, TM), 1)
        tri_ref[...] = jnp.where(r < cc, 1.0, 0.0).astype(BF16)
        for g in range(ng):
            off_ref[g] = 0

        def clear(r, carry):
            for c in range(TAB_W):
                tab_ref[r * TAB_W + c] = 0
            return carry

        lax.fori_loop(0, N_TOK_TILES, clear, 0)

    hf = _norm_mod(x_new, gain_ref[...], sh_ref[0], sc_ref[0])
    hb = hf.astype(BF16)
    hl = (hf - hb.astype(F32)).astype(BF16)
    both = jnp.dot(hb, wr_ref[...], preferred_element_type=F32)
    logits = (both[:, :LANES] + both[:, LANES:]
              + jnp.dot(hl, wr_ref[:, :LANES], preferred_element_type=F32))
    lt = logits.T[0:N_EXPERTS, :]
    scores = jax.nn.sigmoid(lt)
    sel = scores + bias_ref[...]
    xs = [sel[k * ng:(k + 1) * ng, :] for k in range(EXPERTS_PER_GROUP)]
    sc = [scores[k * ng:(k + 1) * ng, :] for k in range(EXPERTS_PER_GROUP)]
    a, b, c, d = xs
    gs = jnp.maximum(jnp.maximum(jnp.maximum(a + b, a + c), jnp.maximum(a + d, b + c)),
                     jnp.maximum(b + d, c + d))
    bv = gs[0:1, :]
    bg = jnp.zeros(bv.shape, jnp.int32)
    for g in range(1, ng):
        better = gs[g:g + 1, :] > bv
        bg = jnp.where(better, g, bg)
        bv = jnp.where(better, gs[g:g + 1, :], bv)
    giota = lax.broadcasted_iota(jnp.int32, (ng, TM), 0)
    onehot = giota == bg
    wk = []
    for k in range(EXPERTS_PER_GROUP):
        rank = jnp.zeros((ng, TM), F32)
        for j in range(EXPERTS_PER_GROUP):
            if j < k:
                rank = rank + (xs[j] >= xs[k]).astype(F32)
            elif j > k:
                rank = rank + (xs[j] > xs[k]).astype(F32)
        chosen = (rank < 2.0) & onehot
        wk.append(jnp.sum(jnp.where(chosen, sc[k], 0.0), axis=0, keepdims=True))
    den = wk[0] + wk[1] + wk[2] + wk[3]
    gates = [w / den for w in wk]

    oh = jnp.where(onehot, 1.0, 0.0)
    before = jnp.dot(oh.astype(BF16), tri_ref[...], preferred_element_type=F32)
    rank_local = jnp.sum(jnp.where(onehot, before, 0.0), axis=0, keepdims=True).astype(jnp.int32)
    counts = [jnp.sum(jnp.where(bg == g, 1, 0)) for g in range(ng)]
    pads = [((cnt + ROW_ALIGN - 1) // ROW_ALIGN) * ROW_ALIGN for cnt in counts]
    segs = [0]
    for g in range(1, ng):
        segs.append(segs[-1] + pads[g - 1])
    total = segs[-1] + pads[-1]
    seg_of = jnp.zeros(bg.shape, jnp.int32)
    for g in range(1, ng):
        seg_of = jnp.where(bg == g, segs[g], seg_of)
    slot = seg_of + rank_local

    parts = []
    for gt in gates:
        hi = gt.astype(BF16).astype(F32)
        rest = gt - hi
        mid = rest.astype(BF16).astype(F32)
        parts += [hi, mid, (rest - mid).astype(BF16).astype(F32)]
    zero_row = jnp.zeros((1, TM), F32)
    info = jnp.concatenate(gates + [bg.astype(F32), zero_row, slot.astype(F32), zero_row] + parts
                           + [jnp.zeros((LANES - GATE_LANE0 - len(parts), TM), F32)], axis=0)
    rowinfo = info.T
    row_ref[...] = rowinfo

    h_ext = jnp.concatenate([hb, rowinfo.astype(BF16)], axis=1)
    srow = lax.broadcasted_iota(jnp.int32, (S_SLOTS, TM), 0)
    pick = jnp.where(srow == slot, 1.0, 0.0).astype(BF16)
    par = i % 2
    buf_ref[par, 0:S_SLOTS, :] = jnp.dot(pick, h_ext, preferred_element_type=F32).astype(BF16)

    def window_copy(g, src_row, dst_row, rows):
        return pltpu.make_async_copy(
            buf_ref.at[par, pl.ds(pl.multiple_of(src_row, ROW_ALIGN), rows), :],
            xs_hbm.at[pl.ds(pl.multiple_of(dst_row, ROW_ALIGN), rows), :],
            sem.at[g])

    def wait_windows(rows):
        for g in range(ng):
            window_copy(g, 0, 0, rows).wait()

    def wait_previous():
        @pl.when(wide_ref[0] == 1)
        def _():
            wait_windows(TM_MOE)

        @pl.when(wide_ref[0] == 0)
        def _():
            wait_windows(HALF_WIN)

    @pl.when(i > 0)
    def _():
        wait_previous()

    wide = jnp.maximum(jnp.maximum(pads[0], pads[1]), jnp.maximum(pads[2], pads[3])) > HALF_WIN
    offs = [off_ref[g] for g in range(ng)]

    def issue(rows):
        for g in range(ng):
            window_copy(g, segs[g], g * REGION_ROWS + offs[g], rows).start()

    @pl.when(wide)
    def _():
        issue(TM_MOE)

    @pl.when(jnp.logical_not(wide))
    def _():
        issue(HALF_WIN)

    wide_ref[0] = wide.astype(jnp.int32)
    for g in range(ng):
        tab_ref[i * TAB_W + g] = segs[g]
        tab_ref[i * TAB_W + ng + g] = offs[g]
        tab_ref[i * TAB_W + 2 * ng + 1 + g] = pads[g]
        off_ref[g] = offs[g] + pads[g]
    tab_ref[i * TAB_W + 2 * ng] = total

    @pl.when(i == last)
    def _():
        wait_previous()
        for g in range(ng):
            window_copy(g, S_SLOTS, g * REGION_ROWS + off_ref[g], TM_MOE).start()
        wait_windows(TM_MOE)

        entry = 0
        full = 0
        for g in range(ng):
            full = (off_ref[g] + TM_MOE - 1) // TM_MOE

            def add(j, e, g=g, full=full):
                wl_ref[e] = g * REGION_TILES + j
                wl_ref[N_MOE_TILES + e] = g
                wl_ref[2 * N_MOE_TILES + e] = (j < full).astype(jnp.int32)
                return e + 1

            entry = lax.fori_loop(0, full + 1, add, entry)

        def pad(e, carry, full=full):
            wl_ref[e] = (ng - 1) * REGION_TILES + full
            wl_ref[N_MOE_TILES + e] = ng - 1
            wl_ref[2 * N_MOE_TILES + e] = 0
            return carry

        lax.fori_loop(entry, N_MOE_TILES, pad, 0)


def _mix_out_router(a_ctx, a_lat, w_out, layer_j, x, gate_a, gain, shift, scale, wr_parts, bias_col):
    k = a_ctx.shape[1]
    return pl.pallas_call(
        _router_kernel,
        grid=(N_TOK_TILES,),
        in_specs=[
            pl.BlockSpec((TM, k), lambda i: (jnp.minimum(i, N_CTX_TILES - 1), 0)),
            pl.BlockSpec((TM, k), lambda i: (jnp.maximum(i - N_CTX_TILES, 0), 0)),
            pl.BlockSpec((1, k, D_MODEL), lambda i: (layer_j, 0, 0), pipeline_mode=pl.Buffered(1)),
            pl.BlockSpec((TM, D_MODEL), lambda i: (i, 0)),
            pl.BlockSpec((1, 1, D_MODEL), lambda i: (_cond_row(i), 0, 0)),
            pl.BlockSpec((1, D_MODEL), lambda i: (0, 0)),
            pl.BlockSpec((1, 1, D_MODEL), lambda i: (_cond_row(i), 0, 0)),
            pl.BlockSpec((1, 1, D_MODEL), lambda i: (_cond_row(i), 0, 0)),
            pl.BlockSpec((D_MODEL, 2 * LANES), lambda i: (0, 0)),
            pl.BlockSpec((N_EXPERTS, 1), lambda i: (0, 0)),
        ],
        out_specs=[
            pl.BlockSpec((TM, D_MODEL), lambda i: (i, 0)),
            pl.BlockSpec(memory_space=pl.ANY),
            pl.BlockSpec((TM, LANES), lambda i: (i, 0)),
            pl.BlockSpec(memory_space=pltpu.SMEM),
            pl.BlockSpec(memory_space=pltpu.SMEM),
        ],
        out_shape=[
            jax.ShapeDtypeStruct((T_ROWS, D_MODEL), F32),
            jax.ShapeDtypeStruct((N_GROUPS * REGION_ROWS, XW), BF16),
            jax.ShapeDtypeStruct((T_ROWS, LANES), F32),
            jax.ShapeDtypeStruct((N_TOK_TILES * TAB_W,), jnp.int32),
            jax.ShapeDtypeStruct((3 * N_MOE_TILES,), jnp.int32),
        ],
        scratch_shapes=[
            pltpu.VMEM((k, D_MODEL), BF16),
            pltpu.VMEM((2, BUF_ROWS, XW), BF16),
            pltpu.VMEM((TM, TM), BF16),
            pltpu.SMEM((N_GROUPS,), jnp.int32),
            pltpu.SMEM((1,), jnp.int32),
            pltpu.SemaphoreType.DMA((N_GROUPS,)),
        ],
        compiler_params=_params(1),
        name="mix_out_router",
    )(a_ctx, a_lat, w_out, x, gate_a, gain, shift, scale, wr_parts, bias_col)


UP_EXPERTS = 4


def _wl_block(wl, i):
    return wl[i]


def _wl_group(wl, i):
    return wl[N_MOE_TILES + i]


def _wl_valid(wl, i):
    return wl[2 * N_MOE_TILES + i]


def _moe_up_kernel(wl_ref, xs_ref, w_ref, o_ref, wbf_ref):
    kk = pl.program_id(0)
    i = pl.program_id(1)
    new_w = (i == 0) | (_wl_group(wl_ref, i) != _wl_group(wl_ref, jnp.maximum(i - 1, 0)))
    valid = _wl_valid(wl_ref, i)

    @pl.when(new_w)
    def _():
        wbf_ref[...] = w_ref[...].astype(BF16)

    @pl.when(valid == 1)
    def _():
        xs = xs_ref[:, :D_MODEL]
        extra = xs_ref[:, D_MODEL:].astype(F32)
        lane = lax.broadcasted_iota(jnp.int32, extra.shape, 1)
        for e in range(UP_EXPERTS):
            hu = jnp.dot(xs, wbf_ref[e], preferred_element_type=F32)
            lane0 = GATE_LANE0 + 3 * (kk * UP_EXPERTS + e)
            gcol = jnp.sum(jnp.where((lane >= lane0) & (lane < lane0 + 3), extra, 0.0), axis=1, keepdims=True)
            a = _silu(hu[:, :D_EXPERT]) * hu[:, D_EXPERT:] * gcol
            o_ref[:, e * D_EXPERT:(e + 1) * D_EXPERT] = a.astype(BF16)

    @pl.when(valid != 1)
    def _():
        o_ref[...] = jnp.zeros_like(o_ref)


def _moe_up(work_list, xs, w_up, layer):
    steps = EXPERTS_PER_GROUP // UP_EXPERTS
    return pl.pallas_call(
        _moe_up_kernel,
        grid_spec=pltpu.PrefetchScalarGridSpec(
            num_scalar_prefetch=1,
            grid=(steps, N_MOE_TILES),
            in_specs=[
                pl.BlockSpec((TM_MOE, XW), lambda k, i, wl: (_wl_block(wl, i), 0)),
                pl.BlockSpec((UP_EXPERTS, D_MODEL, 2 * D_EXPERT),
                             lambda k, i, wl: ((layer * N_GROUPS + _wl_group(wl, i)) * steps + k, 0, 0)),
            ],
            out_specs=pl.BlockSpec((TM_MOE, UP_EXPERTS * D_EXPERT), lambda k, i, wl: (_wl_block(wl, i), k)),
            scratch_shapes=[pltpu.VMEM((UP_EXPERTS, D_MODEL, 2 * D_EXPERT), BF16)],
        ),
        out_shape=jax.ShapeDtypeStruct((N_GROUPS * REGION_ROWS, EXPERTS_PER_GROUP * D_EXPERT), BF16),
        compiler_params=_params(2),
        name="moe_up",
    )(work_list, xs, w_up)


def _moe_down_kernel(wl_ref, a_ref, w_ref, o_ref, wbf_ref):
    i = pl.program_id(0)
    new_w = (i == 0) | (_wl_group(wl_ref, i) != _wl_group(wl_ref, jnp.maximum(i - 1, 0)))
    valid = _wl_valid(wl_ref, i)

    @pl.when(new_w)
    def _():
        wbf_ref[...] = w_ref[0].astype(BF16)

    @pl.when(valid == 1)
    def _():
        o_ref[...] = jnp.dot(a_ref[...], wbf_ref[...], preferred_element_type=F32).astype(BF16)

    @pl.when(valid != 1)
    def _():
        o_ref[...] = jnp.zeros_like(o_ref)


def _moe_down(work_list, a, w_down_grouped, layer):
    kdim = EXPERTS_PER_GROUP * D_EXPERT
    return pl.pallas_call(
        _moe_down_kernel,
        grid_spec=pltpu.PrefetchScalarGridSpec(
            num_scalar_prefetch=1,
            grid=(N_MOE_TILES,),
            in_specs=[
                pl.BlockSpec((TM_MOE, kdim), lambda i, wl: (_wl_block(wl, i), 0)),
                pl.BlockSpec((1, kdim, D_MODEL), lambda i, wl: (layer * N_GROUPS + _wl_group(wl, i), 0, 0)),
            ],
            out_specs=pl.BlockSpec((TM_MOE, D_MODEL), lambda i, wl: (_wl_block(wl, i), 0)),
            scratch_shapes=[pltpu.VMEM((kdim, D_MODEL), BF16)],
        ),
        out_shape=jax.ShapeDtypeStruct((N_GROUPS * REGION_ROWS, D_MODEL), BF16),
        compiler_params=_params(1),
        name="moe_down",
    )(work_list, a, w_down_grouped)


N_WINDOWS = 2 * N_GROUPS


class PendingMoe(NamedTuple):
    tab_flat: jax.Array
    ys: jax.Array
    rowinfo: jax.Array
    x: jax.Array
    gate: jax.Array


def _combine_tile(i, tab_ref, windows, row_ref, x_ref, g_ref, buf_ref):
    first = windows[:N_GROUPS]
    second = windows[N_GROUPS:]

    @pl.when(i == 0)
    def _():
        buf_ref[...] = jnp.zeros_like(buf_ref)

    for g in range(N_GROUPS):
        start = pl.multiple_of(tab_ref[i * TAB_W + g], ROW_ALIGN)
        buf_ref[pl.ds(start, HALF_WIN), :] = first[g][...]

        @pl.when(tab_ref[i * TAB_W + 2 * N_GROUPS + 1 + g] > HALF_WIN)
        def _(g=g, start=start):
            buf_ref[pl.ds(start + HALF_WIN, HALF_WIN), :] = second[g][...]

    slot = row_ref[:, SLOT_LANE:SLOT_LANE + 1].astype(jnp.int32)
    scol = lax.broadcasted_iota(jnp.int32, (TM, S_SLOTS), 1)
    pick = jnp.where(scol == slot, 1.0, 0.0).astype(BF16)
    y = jnp.dot(pick, buf_ref[0:S_SLOTS, :], preferred_element_type=F32)
    return x_ref[...] + g_ref[0] * y


def _pending_specs(p):
    def window_spec(g, second):
        def index(i, tab):
            off = tab[i * TAB_W + N_GROUPS + g]
            if second:
                off = jnp.where(tab[i * TAB_W + 2 * N_GROUPS + 1 + g] > HALF_WIN, off + HALF_WIN, 0)
            return pl.multiple_of(off + g * REGION_ROWS, ROW_ALIGN), 0

        return pl.BlockSpec((pl.Element(HALF_WIN), pl.Element(D_MODEL)), index)

    specs = [window_spec(g, False) for g in range(N_GROUPS)] + [window_spec(g, True) for g in range(N_GROUPS)]
    specs += [
        pl.BlockSpec((TM, LANES), lambda i, tab: (i, 0)),
        pl.BlockSpec((TM, D_MODEL), lambda i, tab: (i, 0)),
        pl.BlockSpec((1, 1, D_MODEL), lambda i, tab: (_cond_row(i), 0, 0)),
    ]
    args = [p.tab_flat] + [p.ys] * N_WINDOWS + [p.rowinfo, p.x, p.gate]
    return specs, args, [pltpu.VMEM((BUF_ROWS, D_MODEL), BF16)]


def _moe_combine_kernel(tab_ref, *refs):
    row_ref, x_ref, g_ref, ctx_ref, lat_ref, buf_ref = refs[N_WINDOWS:]
    i = pl.program_id(0)
    new_x = _combine_tile(i, tab_ref, refs[:N_WINDOWS], row_ref, x_ref, g_ref, buf_ref)

    @pl.when(i < N_CTX_TILES)
    def _():
        ctx_ref[...] = new_x

    @pl.when(i >= N_CTX_TILES)
    def _():
        lat_ref[...] = new_x


def _moe_combine(pending):
    specs, args, scratch = _pending_specs(pending)
    return pl.pallas_call(
        _moe_combine_kernel,
        grid_spec=pltpu.PrefetchScalarGridSpec(
            num_scalar_prefetch=1,
            grid=(N_TOK_TILES,),
            in_specs=specs,
            out_specs=[pl.BlockSpec((TM, D_MODEL), lambda i, tab: (jnp.minimum(i, N_CTX_TILES - 1), 0)),
                       pl.BlockSpec((TM, D_MODEL), lambda i, tab: (jnp.maximum(i - N_CTX_TILES, 0), 0))],
            scratch_shapes=scratch,
        ),
        out_shape=[jax.ShapeDtypeStruct((CTX_ROWS, D_MODEL), F32), jax.ShapeDtypeStruct((LAT_ROWS, D_MODEL), F32)],
        compiler_params=_params(1),
        name="moe_combine",
    )(*args)


def _mix_out_and_moe(mix, x, gate_a, gain, shift, scale, gate, router_w, w_up, w_down_grouped, layer):
    wr_parts, bias_col = router_w
    x, xs, rowinfo, tab, work_list = _mix_out_router(*mix, x, gate_a, gain, shift, scale, wr_parts, bias_col)
    a = _moe_up(work_list, xs, w_up, layer)
    ys = _moe_down(work_list, a, w_down_grouped, layer)
    return PendingMoe(tab, ys, rowinfo, x, gate)


def kernel(x_prompt, x_sample, state_ret, cache_k, cache_v, c, c_ctx, w_mod, b_mod, norm_mix,
           norm_moe, ret_w_in, ret_decay, ret_w_out, att_w_in, att_q_norm, att_k_norm, att_sink,
           att_w_out, w_router, router_bias, moe_w_up, moe_w_down):
    cond8 = jnp.zeros((8, D_MODEL), F32).at[0].set(c_ctx).at[1:N_COND].set(c)
    mods = _modulation_all(cond8, w_mod, b_mod)
    mods = mods[:, :N_COND].reshape(DEPTH, N_COND, N_MOD, 1, D_MODEL)

    ret_cos, ret_sin = _rope_tables(RET_DK)
    att_cos, att_sin = _rope_tables(HEAD_DIM)

    perm = jnp.arange(N_EXPERTS).reshape(N_GROUPS, EXPERTS_PER_GROUP).T.reshape(-1)
    wr = jnp.zeros((D_MODEL, LANES), F32).at[:, :N_EXPERTS].set(w_router[:, perm])
    wr_hi = wr.astype(BF16)
    wr_lo = (wr - wr_hi.astype(F32)).astype(BF16)
    bias_col = router_bias[perm].astype(F32).reshape(N_EXPERTS, 1)
    router_w = (jnp.concatenate([wr_hi, wr_lo], axis=1), bias_col)

    w_up_all = moe_w_up.reshape(DEPTH * N_EXPERTS, D_MODEL, 2 * D_EXPERT)
    w_down_all = moe_w_down.reshape(DEPTH * N_GROUPS, EXPERTS_PER_GROUP * D_EXPERT, D_MODEL)
    ret_w_in_bf16 = ret_w_in.astype(BF16)
    new_state = None
    cache_shape = (BATCH, DEPTH // 2, ATT_KV_HEADS, SEQ, HEAD_DIM)
    new_kv = (jnp.zeros(cache_shape, F32), jnp.zeros(cache_shape, F32))
    source = (x_prompt.reshape(CTX_ROWS, D_MODEL), x_sample.reshape(LAT_ROWS, D_MODEL))
    for layer in range(DEPTH):
        sh_a, sc_a, g_a, sh_m, sc_m, g_m = [mods[layer, :, t] for t in range(N_MOD)]
        gain_mix = norm_mix[layer].reshape(1, D_MODEL)
        gain_moe = norm_moe[layer].reshape(1, D_MODEL)
        j = layer // 2
        if layer % 2 == 0:
            proj, x = _ret_proj(source, gain_mix, sh_a, sc_a, ret_w_in_bf16, j, ret_cos, ret_sin)
            o_ctx, new_state = _ret_core(proj, ret_decay[j], state_ret, j, latent=False, new_state=new_state)
            (o_lat,) = _ret_core(proj, ret_decay[j], state_ret, j, latent=True)
            mix = (o_ctx, o_lat, ret_w_out, j)
        else:
            q_gain = jnp.tile(att_q_norm[j], LANES // HEAD_DIM).reshape(1, LANES)
            k_gain = jnp.tile(att_k_norm[j], LANES // HEAD_DIM).reshape(1, LANES)
            qkv, x, new_k, new_v = _att_proj(source, gain_mix, sh_a, sc_a, att_w_in, j, q_gain, k_gain,
                                             att_cos, att_sin, new_kv)
            new_kv = (new_k, new_v)
            sink = att_sink[j].astype(F32)
            o_ctx = _ctx_attention(qkv, sink)
            o_lat = _lat_attention(qkv, sink, cache_k, cache_v, j)
            mix = (o_ctx, o_lat, att_w_out, j)
        source = _mix_out_and_moe(mix, x, g_a, gain_moe, sh_m, sc_m, g_m, router_w, w_up_all, w_down_all, layer)

    x_ctx, x_lat = _moe_combine(source)
    y_prompt = x_ctx.reshape(BATCH, SEQ, D_MODEL)
    y_sample = x_lat.reshape(DEC_BATCH, DEC_SEQ, D_MODEL)
    return (y_prompt, y_sample, new_state, new_kv[0], new_kv[1])
```
